```python
import jax, jax.numpy as jnp
from jax import lax
import numpy as np

D_MODEL = 1024
BATCH = 2
SEQ = 8192
DEPTH = 1

SWA_HEADS = 8
SWA_KV_HEADS = 2
SWA_HEAD_DIM = 64
WINDOW = 128
ATTN_BLOCK = 128
MLA_HEADS = 8
MLA_Q_RANK = 384
MLA_KV_RANK = 256
MLA_NOPE_DIM = 64
MLA_ROPE_DIM = 32
MLA_V_DIM = 64
ROPE_THETA = 10000.0
N_EXPERTS = 32
TOP_K = 4
D_EXPERT = 1024
SWIGLU_LIMIT = 7.0
SWIGLU_ALPHA = 1.702
MOE_BLOCK = 128
NORM_EPS = 1e-6

SWA_Q_DIM = SWA_HEADS * SWA_HEAD_DIM
SWA_KV_DIM = SWA_KV_HEADS * SWA_HEAD_DIM
MLA_QK_DIM = MLA_NOPE_DIM + MLA_ROPE_DIM
MLA_OUT_DIM = MLA_HEADS * MLA_V_DIM
IN_WIDTHS = (SWA_Q_DIM, SWA_KV_DIM, SWA_KV_DIM, MLA_Q_RANK, MLA_KV_RANK, MLA_ROPE_DIM, D_MODEL, D_MODEL)
IN_DIM = 3488

kernel_name = 'hybrid_swa_mla_moe_adaln_block'


def rmsnorm(x, g):
    x32 = x.astype(jnp.float32)
    r = x32 * lax.rsqrt(jnp.mean(x32 * x32, axis=-1, keepdims=True) + NORM_EPS)
    return (r * g.astype(jnp.float32)).astype(x.dtype)


def alibi_slopes(n_heads):
    return jnp.asarray(np.exp2(-8.0 * np.arange(1, n_heads + 1) / n_heads), dtype=jnp.float32)


def rope(t, cos, sin):
    half = t.shape[-1] // 2
    t1, t2 = t[..., :half], t[..., half:]
    return jnp.concatenate([t1 * cos - t2 * sin, t2 * cos + t1 * sin], axis=-1)


def sliding_window_attention(q, k, v, sinks):
    B, S = q.shape[:2]
    nb = S // ATTN_BLOCK
    G = SWA_HEADS // SWA_KV_HEADS
    qb = q.reshape(B, nb, ATTN_BLOCK, SWA_KV_HEADS, G, SWA_HEAD_DIM)

    def band(t):
        tp = jnp.pad(t, ((0, 0), (ATTN_BLOCK, 0), (0, 0), (0, 0)))
        tp = tp.reshape(B, nb + 1, ATTN_BLOCK, SWA_KV_HEADS, SWA_HEAD_DIM)
        return jnp.concatenate([tp[:, :-1], tp[:, 1:]], axis=2)

    kw, vw = band(k), band(v)
    scores = jnp.einsum('bnqhgd,bnkhd->bnhgqk', qb, kw).astype(jnp.float32) * (SWA_HEAD_DIM ** -0.5)
    qi = jnp.arange(ATTN_BLOCK)[:, None]
    kj = jnp.arange(2 * ATTN_BLOCK)[None, :]
    dist = qi - kj + ATTN_BLOCK
    key_abs = jnp.arange(nb)[:, None, None] * ATTN_BLOCK - ATTN_BLOCK + kj[None]
    valid = (dist >= 0) & (dist < WINDOW) & (key_abs >= 0)
    slopes = alibi_slopes(SWA_HEADS).reshape(SWA_KV_HEADS, G)
    scores = scores - slopes[:, :, None, None] * dist.astype(jnp.float32)
    scores = jnp.where(valid[None, :, None, None], scores, -jnp.inf)
    sink = jnp.broadcast_to(sinks.astype(jnp.float32).reshape(SWA_KV_HEADS, G)[:, :, None, None],
                            scores.shape[:-1] + (1,))
    probs = jax.nn.softmax(jnp.concatenate([scores, sink], axis=-1), axis=-1)[..., :-1]
    out = jnp.einsum('bnhgqk,bnkhd->bnqhgd', probs.astype(v.dtype), vw)
    return out.reshape(B, S, SWA_Q_DIM)


def mla_attention(q_nope, q_rope, k_nope, k_rope, v):
    B, S = q_nope.shape[:2]
    nb = S // ATTN_BLOCK
    scale = MLA_QK_DIM ** -0.5
    key_pos = jnp.arange(S)

    def blocks(t):
        return jnp.moveaxis(t.reshape((B, nb, ATTN_BLOCK) + t.shape[2:]), 1, 0)

    def attend(args):
        qn, qr, n = args
        s = (jnp.einsum('bqhd,bkhd->bhqk', qn, k_nope).astype(jnp.float32)
             + jnp.einsum('bqhd,bkd->bhqk', qr, k_rope).astype(jnp.float32)) * scale
        q_pos = n * ATTN_BLOCK + jnp.arange(ATTN_BLOCK)
        s = jnp.where(key_pos[None, :] <= q_pos[:, None], s, -jnp.inf)
        p = jax.nn.softmax(s, axis=-1)
        return jnp.einsum('bhqk,bkhd->bqhd', p.astype(v.dtype), v)

    out = lax.map(attend, (blocks(q_nope), blocks(q_rope), jnp.arange(nb)))
    return jnp.moveaxis(out, 0, 1).reshape(B, S, MLA_OUT_DIM)


def clamped_swiglu(hb):
    x_glu = jnp.minimum(hb[..., :D_EXPERT], SWIGLU_LIMIT)
    x_lin = jnp.clip(hb[..., D_EXPERT:], -SWIGLU_LIMIT, SWIGLU_LIMIT)
    return x_glu * jax.nn.sigmoid(SWIGLU_ALPHA * x_glu) * (x_lin + 1.0)


def moe(h, w_router, b_router, w1, b1, w2, b2):
    B, S, D = h.shape
    xt = h.reshape(-1, D)
    N = xt.shape[0]
    logits = (xt @ w_router + b_router).astype(jnp.float32)
    top_vals, top_idx = lax.top_k(logits, TOP_K)
    top_w = jax.nn.softmax(top_vals, axis=-1)
    NK = N * TOP_K
    flat_e = top_idx.reshape(-1)
    flat_tok = jnp.repeat(jnp.arange(N, dtype=jnp.int32), TOP_K)
    flat_w = top_w.reshape(-1)
    order = jnp.argsort(flat_e)
    sorted_e = flat_e[order]
    counts = jnp.bincount(flat_e, length=N_EXPERTS)
    padded = ((counts + MOE_BLOCK - 1) // MOE_BLOCK) * MOE_BLOCK
    starts = jnp.cumsum(counts) - counts
    pends = jnp.cumsum(padded)
    pstarts = pends - padded
    dest = pstarts[sorted_e] + (jnp.arange(NK) - starts[sorted_e])
    P = NK + N_EXPERTS * MOE_BLOCK
    n_blocks = P // MOE_BLOCK
    tok_buf = jnp.zeros((P,), jnp.int32).at[dest].set(flat_tok[order])
    w_buf = jnp.zeros((P,), h.dtype).at[dest].set(flat_w[order].astype(h.dtype))
    block_e = jnp.minimum(jnp.searchsorted(pends, jnp.arange(n_blocks) * MOE_BLOCK, side='right'),
                          N_EXPERTS - 1)

    def expert_block(args):
        e, toks = args
        xb = xt[toks]
        hb = clamped_swiglu(xb @ w1[e] + b1[e])
        return hb @ w2[e] + b2[e]

    ys = lax.map(expert_block, (block_e, tok_buf.reshape(n_blocks, MOE_BLOCK)))
    y = jnp.zeros_like(xt).at[tok_buf].add(ys.reshape(P, D) * w_buf[:, None])
    return y.reshape(B, S, D)


def setup_inputs(seed: int = 0) -> dict:
    key = jax.random.key(seed)
    ks = jax.random.split(key, 26)
    f32 = jnp.float32

    def nrm(k, shape, fan_in, mult=1.0):
        return jax.random.normal(k, shape, f32) * (mult * fan_in ** -0.5)

    L, D, E, F = DEPTH, D_MODEL, N_EXPERTS, D_EXPERT
    return {
        'x': jax.random.normal(ks[0], (BATCH, SEQ, D), f32),
        'c': jax.random.normal(ks[1], (BATCH, D), f32),
        'positions': jax.random.randint(ks[2], (BATCH, 1), 0, 1024, jnp.int32)
                     + jnp.arange(SEQ, dtype=jnp.int32)[None, :],
        'w_ada': nrm(ks[3], (L, D, 6 * D), D, 0.5),
        'b_ada': 0.01 * jax.random.normal(ks[4], (L, 6 * D), f32),
        'norm_mix': 1.0 + 0.05 * jax.random.normal(ks[5], (L, D), f32),
        'norm_ffn': 1.0 + 0.05 * jax.random.normal(ks[6], (L, D), f32),
        'w_in': nrm(ks[7], (L, D, IN_DIM), D),
        'sinks': 0.5 * jax.random.normal(ks[8], (L, SWA_HEADS), f32),
        'q_norm': 1.0 + 0.05 * jax.random.normal(ks[9], (L, MLA_Q_RANK), f32),
        'kv_norm': 1.0 + 0.05 * jax.random.normal(ks[10], (L, MLA_KV_RANK), f32),
        'w_uq': nrm(ks[11], (L, MLA_Q_RANK, MLA_HEADS * MLA_QK_DIM), MLA_Q_RANK),
        'w_uk': nrm(ks[12], (L, MLA_KV_RANK, MLA_HEADS * MLA_NOPE_DIM), MLA_KV_RANK),
        'w_uv': nrm(ks[13], (L, MLA_KV_RANK, MLA_OUT_DIM), MLA_KV_RANK),
        'w_branch_a': nrm(ks[14], (L, SWA_Q_DIM, D), SWA_Q_DIM),
        'w_branch_b': nrm(ks[15], (L, MLA_OUT_DIM, D), MLA_OUT_DIM),
        'w_out': nrm(ks[16], (L, D, D), D),
        'w_router': nrm(ks[17], (L, D, E), D),
        'b_router': 0.01 * jax.random.normal(ks[18], (L, E), f32),
        'w_moe1': nrm(ks[19], (L, E, D, 2 * F), D),
        'b_moe1': 0.01 * jax.random.normal(ks[20], (L, E, 2 * F), f32),
        'w_moe2': nrm(ks[21], (L, E, F, D), F),
        'b_moe2': 0.01 * jax.random.normal(ks[22], (L, E, D), f32),
        'final_norm': 1.0 + 0.05 * jax.random.normal(ks[23], (D,), f32),
    }


def reference(x, c, positions, w_ada, b_ada, norm_mix, norm_ffn, w_in, sinks, q_norm, kv_norm,
              w_uq, w_uk, w_uv, w_branch_a, w_branch_b, w_out, w_router, b_router,
              w_moe1, b_moe1, w_moe2, b_moe2, final_norm):
    B, S, D = x.shape
    freqs = ROPE_THETA ** (-jnp.arange(0, MLA_ROPE_DIM, 2, dtype=jnp.float32) / MLA_ROPE_DIM)
    ang = positions.astype(jnp.float32)[..., None] * freqs
    cos, sin = jnp.cos(ang).astype(x.dtype), jnp.sin(ang).astype(x.dtype)
    split_at = list(np.cumsum(IN_WIDTHS)[:-1])
    c_act = jax.nn.silu(c)

    for l in range(DEPTH):
        mod = c_act @ w_ada[l] + b_ada[l]
        sh1, sc1, g1, sh2, sc2, g2 = [m[:, None, :] for m in jnp.split(mod, 6, axis=-1)]

        h = rmsnorm(x, norm_mix[l]) * (1.0 + sc1) + sh1
        proj = h @ w_in[l]
        qa, ka, va, cq, ckv, kr, gate_a, gate_b = jnp.split(proj, split_at, axis=-1)

        ya = sliding_window_attention(qa.reshape(B, S, SWA_HEADS, SWA_HEAD_DIM),
                                      ka.reshape(B, S, SWA_KV_HEADS, SWA_HEAD_DIM),
                                      va.reshape(B, S, SWA_KV_HEADS, SWA_HEAD_DIM),
                                      sinks[l])
        ya = ya @ w_branch_a[l]

        q = (rmsnorm(cq, q_norm[l]) @ w_uq[l]).reshape(B, S, MLA_HEADS, MLA_QK_DIM)
        q_nope = q[..., :MLA_NOPE_DIM]
        q_rope = rope(q[..., MLA_NOPE_DIM:], cos[:, :, None, :], sin[:, :, None, :])
        ckv_n = rmsnorm(ckv, kv_norm[l])
        k_nope = (ckv_n @ w_uk[l]).reshape(B, S, MLA_HEADS, MLA_NOPE_DIM)
        v_b = (ckv_n @ w_uv[l]).reshape(B, S, MLA_HEADS, MLA_V_DIM)
        k_rope = rope(kr, cos, sin)
        yb = mla_attention(q_nope, q_rope, k_nope, k_rope, v_b) @ w_branch_b[l]

        mixed = jax.nn.sigmoid(gate_a) * ya + jax.nn.sigmoid(gate_b) * yb
        x = x + g1 * (mixed @ w_out[l])

        h2 = rmsnorm(x, norm_ffn[l]) * (1.0 + sc2) + sh2
        x = x + g2 * moe(h2, w_router[l], b_router[l], w_moe1[l], b_moe1[l], w_moe2[l], b_moe2[l])

    return rmsnorm(x, final_norm)
```

```python
import functools

import numpy as np
import jax
import jax.numpy as jnp
from jax import lax
from jax.experimental import pallas as pl
from jax.experimental.pallas import tpu as pltpu

D_MODEL = 1024
SWA_HEADS = 8
SWA_KV_HEADS = 2
SWA_HEAD_DIM = 64
ATTN_BLOCK = 128
MLA_HEADS = 8
MLA_Q_RANK = 384
MLA_KV_RANK = 256
MLA_NOPE_DIM = 64
MLA_ROPE_DIM = 32
MLA_V_DIM = 64
ROPE_THETA = 10000.0
N_EXPERTS = 32
TOP_K = 4
D_EXPERT = 1024
SWIGLU_LIMIT = 7.0
SWIGLU_ALPHA = 1.702
NORM_EPS = 1e-6

LANE = 128
LOG2E = 1.4426950408889634
NEG = -1e30
SWA_QSCALE = SWA_HEAD_DIM ** -0.5 * LOG2E
MLA_QSCALE = (MLA_NOPE_DIM + MLA_ROPE_DIM) ** -0.5 * LOG2E
HALF_ROPE = MLA_ROPE_DIM // 2

BF16 = jnp.bfloat16
F32 = jnp.float32

TM_PROJ = 256
TQ_MLA = 512
SWA_QB = 4
MOE_ROWS = 256
ROW_TILE = 128
VMEM_LIMIT = 56 * 1024 * 1024


def _params(sem, vmem=VMEM_LIMIT):
    return pltpu.CompilerParams(dimension_semantics=sem, vmem_limit_bytes=vmem)


def _rms(x):
    return x * lax.rsqrt(jnp.mean(x * x, axis=-1, keepdims=True) + NORM_EPS)


def _ada_kernel(c_ref, w_ref, b_ref, o_ref):
    c = c_ref[...]
    a = (c * jax.nn.sigmoid(c)).astype(BF16)
    o_ref[...] = jnp.dot(a, w_ref[...].astype(BF16), preferred_element_type=F32) + b_ref[...]


def _ada(c8, w_ada, b_ada):
    n_out = w_ada.shape[1]
    return pl.pallas_call(
        _ada_kernel,
        grid=(n_out // D_MODEL,),
        in_specs=[pl.BlockSpec((8, D_MODEL), lambda j: (0, 0)),
                  pl.BlockSpec((D_MODEL, D_MODEL), lambda j: (0, j)),
                  pl.BlockSpec((1, D_MODEL), lambda j: (0, j))],
        out_specs=pl.BlockSpec((8, D_MODEL), lambda j: (0, j)),
        out_shape=jax.ShapeDtypeStruct((8, n_out), F32),
        compiler_params=_params(("arbitrary",)),
        name="ada",
    )(c8, w_ada, b_ada)


_C_QA = (0, 512)
_C_KA = (512, 768)
_C_VA = (768, 1024)
_C_CQ = (1024, 1408)
_C_CKV = (1408, 1664)
_C_KRA = (1664, 1792)
_C_KRB = (1792, 1920)
_C_GA = (1920, 2944)
_C_GB = (2944, 3968)
_W_IN_COLS = 3968


def _inproj_kernel(x_ref, pos_ref, sh_ref, sc_ref, nm_ref, win_ref, qn_ref, kvn_ref, wuq_ref, wuk_ref,
                   wuv_ref, frq_ref, sgn_ref, one_ref,
                   qa_ref, ka_ref, va_ref, qm_ref, km_ref, vm_ref, ga_ref, gb_ref):
    x = x_ref[...]
    h = (_rms(x) * nm_ref[...] * (1.0 + sc_ref[0]) + sh_ref[0]).astype(BF16)

    def proj(c):
        return jnp.dot(h, win_ref[:, c[0]:c[1]], preferred_element_type=F32)

    qa_ref[...] = (proj(_C_QA) * SWA_QSCALE).astype(BF16)
    ka_ref[...] = proj(_C_KA).astype(BF16)
    va_ref[...] = proj(_C_VA).astype(BF16)
    ga_ref[...] = jax.nn.sigmoid(proj(_C_GA)).astype(BF16)
    gb_ref[...] = jax.nn.sigmoid(proj(_C_GB)).astype(BF16)

    ang = pos_ref[...].astype(F32) * frq_ref[...]
    cs = jnp.cos(ang)
    sn = jnp.sin(ang) * sgn_ref[...]

    cqn = (_rms(proj(_C_CQ)) * qn_ref[...]).astype(BF16)
    q2 = jnp.dot(cqn, wuq_ref[...], preferred_element_type=F32)
    hw = MLA_HEADS * LANE
    for hh in range(MLA_HEADS):
        sl = slice(hh * LANE, (hh + 1) * LANE)
        sw = slice(hw + hh * LANE, hw + (hh + 1) * LANE)
        qm_ref[:, sl] = ((q2[:, sl] * cs + q2[:, sw] * sn) * MLA_QSCALE).astype(BF16)

    ckvn = (_rms(proj(_C_CKV)) * kvn_ref[...]).astype(BF16)
    krr = proj(_C_KRA) * cs + proj(_C_KRB) * sn
    kn = jnp.dot(ckvn, wuk_ref[...], preferred_element_type=F32)
    for hh in range(MLA_HEADS):
        sl = slice(hh * LANE, (hh + 1) * LANE)
        km_ref[:, sl] = (kn[:, sl] + krr).astype(BF16)
    vm_ref[...] = (jnp.dot(ckvn, wuv_ref[...], preferred_element_type=F32) + one_ref[...]).astype(BF16)


def _inproj(x2, pos2, sh1, sc1, nm, w_in_p, qn, kvn, wuq_p, wuk_p, wuv_p, frq, sgn, one_row, seq):
    n = x2.shape[0]
    tm = TM_PROJ
    per_b = seq // tm
    hw = MLA_HEADS * LANE
    row = lambda i: (i, 0)
    fix = lambda i: (0, 0)
    bsel = lambda i: (i // per_b, 0, 0)
    widths = (512, 256, 256, hw, hw, hw, D_MODEL, D_MODEL)
    return pl.pallas_call(
        _inproj_kernel,
        grid=(n // tm,),
        in_specs=[pl.BlockSpec((tm, D_MODEL), row),
                  pl.BlockSpec((tm, 1), row),
                  pl.BlockSpec((1, 1, D_MODEL), bsel),
                  pl.BlockSpec((1, 1, D_MODEL), bsel),
                  pl.BlockSpec((1, D_MODEL), fix),
                  pl.BlockSpec((D_MODEL, _W_IN_COLS), fix),
                  pl.BlockSpec((1, MLA_Q_RANK), fix),
                  pl.BlockSpec((1, MLA_KV_RANK), fix),
                  pl.BlockSpec((MLA_Q_RANK, 2 * hw), fix),
                  pl.BlockSpec((MLA_KV_RANK, hw), fix),
                  pl.BlockSpec((MLA_KV_RANK, hw), fix),
                  pl.BlockSpec((1, LANE), fix),
                  pl.BlockSpec((1, LANE), fix),
                  pl.BlockSpec((1, hw), fix)],
        out_specs=[pl.BlockSpec((tm, w), row) for w in widths],
        out_shape=[jax.ShapeDtypeStruct((n, w), BF16) for w in widths],
        compiler_params=_params(("arbitrary",)),
        name="inproj",
    )(x2, pos2, sh1, sc1, nm, w_in_p, qn, kvn, wuq_p, wuk_p, wuv_p, frq, sgn, one_row)


def _mla_kernel(q_ref, k_ref, v_ref, o_ref, m_scr, acc_scr):
    i = pl.program_id(2)
    t = TQ_MLA
    m_scr[...] = jnp.full(m_scr.shape, NEG, F32)
    acc_scr[...] = jnp.zeros(acc_scr.shape, F32)

    def step(j, masked):
        start = pl.multiple_of(j * t, t)
        k = k_ref[pl.ds(start, t), :]
        v = v_ref[pl.ds(start, t), :]
        s = lax.dot_general(q_ref[...], k, (((1,), (1,)), ((), ())), preferred_element_type=F32)
        if masked:
            qi = lax.broadcasted_iota(jnp.int32, (t, t), 0)
            kj = lax.broadcasted_iota(jnp.int32, (t, t), 1)
            s = jnp.where(kj <= qi, s, NEG)
        m_old = m_scr[...]
        m_new = jnp.maximum(m_old, jnp.max(s, axis=1, keepdims=True))
        p = jnp.exp2(s - m_new)
        alpha = jnp.exp2(m_old - m_new)
        acc_scr[...] = acc_scr[...] * alpha + jnp.dot(p.astype(BF16), v, preferred_element_type=F32)
        m_scr[...] = m_new

    def body(j, carry):
        step(j, False)
        return carry

    lax.fori_loop(0, i, body, 0)
    step(i, True)
    acc = acc_scr[...]
    o_ref[...] = (acc / acc[:, MLA_V_DIM:MLA_V_DIM + 1]).astype(BF16)


def _mla(qm, km, vm, batch, seq):
    n = qm.shape[0]
    t = TQ_MLA
    nq = seq // t
    qmap = lambda b, h, i: (b * nq + i, h)
    kmap = lambda b, h, i: (b, h)
    return pl.pallas_call(
        _mla_kernel,
        grid=(batch, MLA_HEADS, nq),
        in_specs=[pl.BlockSpec((t, LANE), qmap),
                  pl.BlockSpec((seq, LANE), kmap),
                  pl.BlockSpec((seq, LANE), kmap)],
        out_specs=pl.BlockSpec((t, LANE), qmap),
        out_shape=jax.ShapeDtypeStruct((n, MLA_HEADS * LANE), BF16),
        scratch_shapes=[pltpu.VMEM((t, 1), F32), pltpu.VMEM((t, LANE), F32)],
        compiler_params=_params(("arbitrary", "arbitrary", "arbitrary")),
        name="mla",
    )(qm, km, vm)


def _swa_kernel(sink_ref, slope_ref, q_ref, kc_ref, kp_ref, vc_ref, vp_ref, o_ref, bias_scr):
    b = pl.program_id(0)
    i = pl.program_id(1)
    blk = ATTN_BLOCK

    @pl.when((b == 0) & (i == 0))
    def _():
        qi = lax.broadcasted_iota(jnp.int32, (blk, 2 * blk), 0)
        kj = lax.broadcasted_iota(jnp.int32, (blk, 2 * blk), 1)
        dist = qi - kj + blk
        valid = (dist >= 0) & (dist < blk)
        distf = dist.astype(F32)
        for hd in range(SWA_HEADS):
            bias_scr[hd] = jnp.where(valid, -(slope_ref[hd] * LOG2E) * distf, NEG)

    lane = lax.broadcasted_iota(jnp.int32, (2 * blk, LANE), 1)
    lo = lane < SWA_HEAD_DIM
    kcol = lax.broadcasted_iota(jnp.int32, (blk, 2 * blk), 1)
    first_pen = jnp.where(kcol < blk, jnp.where(i == 0, NEG, 0.0), 0.0)
    zero = jnp.zeros((2 * blk, LANE), BF16)

    for qb in range(SWA_QB):
        if qb == 0:
            kprev, vprev = kp_ref[...], vp_ref[...]
        else:
            kprev = kc_ref[(qb - 1) * blk:qb * blk, :]
            vprev = vc_ref[(qb - 1) * blk:qb * blk, :]
        kcat = jnp.concatenate([kprev, kc_ref[qb * blk:(qb + 1) * blk, :]], axis=0)
        vcat = jnp.concatenate([vprev, vc_ref[qb * blk:(qb + 1) * blk, :]], axis=0)
        for g in range(SWA_KV_HEADS):
            kg = kcat[:, g * LANE:(g + 1) * LANE]
            vg = vcat[:, g * LANE:(g + 1) * LANE]
            halves = ((jnp.where(lo, kg, zero), jnp.where(lo, vg, zero)),
                      (jnp.where(lo, zero, kg), jnp.where(lo, zero, vg)))
            for pp in range(2):
                pr = g * 2 + pp
                qpair = q_ref[qb * blk:(qb + 1) * blk, pr * LANE:(pr + 1) * LANE]
                out = None
                for half in range(2):
                    hd = pr * 2 + half
                    kx, vx = halves[half]
                    s = lax.dot_general(qpair, kx, (((1,), (1,)), ((), ())), preferred_element_type=F32)
                    s = s + bias_scr[hd]
                    if qb == 0:
                        s = s + first_pen
                    sink = sink_ref[hd] * LOG2E
                    m = jnp.maximum(jnp.max(s, axis=1, keepdims=True), sink)
                    p = jnp.exp2(s - m)
                    denom = jnp.sum(p, axis=1, keepdims=True) + jnp.exp2(sink - m)
                    o = jnp.dot(p.astype(BF16), vx, preferred_element_type=F32) * (1.0 / denom)
                    out = o if out is None else out + o
                o_ref[qb * blk:(qb + 1) * blk, pr * LANE:(pr + 1) * LANE] = out.astype(BF16)


def _swa(sinks, slopes, qa, ka2, va2, batch, seq):
    n = qa.shape[0]
    blk = ATTN_BLOCK
    rows = SWA_QB * blk
    steps = seq // rows
    cur = lambda b, i: (b * steps + i, 0)
    prev = lambda b, i: (b * (seq // blk) + jnp.maximum(i * SWA_QB - 1, 0), 0)
    smem = pl.BlockSpec(memory_space=pltpu.SMEM)
    return pl.pallas_call(
        _swa_kernel,
        grid=(batch, steps),
        in_specs=[smem, smem,
                  pl.BlockSpec((rows, 512), cur),
                  pl.BlockSpec((rows, 256), cur),
                  pl.BlockSpec((blk, 256), prev),
                  pl.BlockSpec((rows, 256), cur),
                  pl.BlockSpec((blk, 256), prev)],
        out_specs=pl.BlockSpec((rows, 512), cur),
        out_shape=jax.ShapeDtypeStruct((n, 512), BF16),
        scratch_shapes=[pltpu.VMEM((SWA_HEADS, blk, 2 * blk), F32)],
        compiler_params=_params(("arbitrary", "arbitrary")),
        name="swa",
    )(sinks, slopes, qa, ka2, ka2, va2, va2)


def _outproj_kernel(ya_ref, yb_ref, ga_ref, gb_ref, x_ref, g1_ref, sh_ref, sc_ref, nm_ref, wa_ref, wb_ref,
                    wo_ref, wrh_ref, wrl_ref, br_ref,
                    x1_ref, h2_ref, eidx_ref, rank_ref, topw_ref, cnt_ref, carry_scr):
    tm = x_ref.shape[0]

    @pl.when(pl.program_id(0) == 0)
    def _():
        carry_scr[...] = jnp.zeros(carry_scr.shape, F32)

    a = jnp.dot(ya_ref[...], wa_ref[...], preferred_element_type=F32)
    bm = jnp.dot(yb_ref[...], wb_ref[...], preferred_element_type=F32)
    mixed = (ga_ref[...].astype(F32) * a + gb_ref[...].astype(F32) * bm).astype(BF16)
    x1 = x_ref[...] + g1_ref[0] * jnp.dot(mixed, wo_ref[...], preferred_element_type=F32)
    x1_ref[...] = x1
    h2 = _rms(x1) * nm_ref[...] * (1.0 + sc_ref[0]) + sh_ref[0]
    h2_ref[...] = h2

    hi = h2.astype(BF16)
    lo = (h2 - hi.astype(F32)).astype(BF16)
    logits = (jnp.dot(hi, wrh_ref[...], preferred_element_type=F32)
              + jnp.dot(lo, wrh_ref[...], preferred_element_type=F32)
              + jnp.dot(hi, wrl_ref[...], preferred_element_type=F32)) + br_ref[...]

    lane = lax.broadcasted_iota(jnp.int32, (tm, LANE), 1).astype(F32)
    work = jnp.where(lane < N_EXPERTS, logits, -jnp.inf)
    vals, idxs = [], []
    for _ in range(TOP_K):
        m = jnp.max(work, axis=1, keepdims=True)
        idx = jnp.min(jnp.where(work == m, lane, float(LANE)), axis=1, keepdims=True)
        vals.append(m)
        idxs.append(idx)
        work = jnp.where(lane == idx, -jnp.inf, work)
    exps = [jnp.exp(v - vals[0]) for v in vals]
    tot = exps[0] + exps[1] + exps[2] + exps[3]

    onehot = jnp.zeros((tm, LANE), F32)
    for idx in idxs:
        onehot = onehot + (lane == idx).astype(F32)
    r = lax.broadcasted_iota(jnp.int32, (tm, tm), 0)
    c = lax.broadcasted_iota(jnp.int32, (tm, tm), 1)
    tri = (c < r).astype(BF16)
    rank_e = carry_scr[0:1, :] + jnp.dot(tri, onehot.astype(BF16), preferred_element_type=F32)

    eidx = jnp.zeros((tm, LANE), F32)
    rank = jnp.zeros((tm, LANE), F32)
    topw = jnp.zeros((tm, LANE), F32)
    for k in range(TOP_K):
        rk = jnp.sum(jnp.where(lane == idxs[k], rank_e, 0.0), axis=1, keepdims=True)
        sel = lane == float(k)
        eidx = jnp.where(sel, idxs[k], eidx)
        rank = jnp.where(sel, rk, rank)
        topw = jnp.where(sel, exps[k] / tot, topw)
    eidx_ref[...] = eidx.astype(jnp.int32)
    rank_ref[...] = rank.astype(jnp.int32)
    topw_ref[...] = topw
    carry = carry_scr[...] + jnp.sum(onehot, axis=0, keepdims=True)
    carry_scr[...] = carry
    cnt_ref[...] = carry


def _outproj(ya, yb, ga, gb, x2, g1, sh2, sc2, nm, wa, wb_p, wo, wr_hi, wr_lo, br_p, seq):
    n = x2.shape[0]
    tm = TM_PROJ
    per_b = seq // tm
    hw = MLA_HEADS * LANE
    row = lambda i: (i, 0)
    fix = lambda i: (0, 0)
    bsel = lambda i: (i // per_b, 0, 0)
    return pl.pallas_call(
        _outproj_kernel,
        grid=(n // tm,),
        in_specs=[pl.BlockSpec((tm, 512), row),
                  pl.BlockSpec((tm, hw), row),
                  pl.BlockSpec((tm, D_MODEL), row),
                  pl.BlockSpec((tm, D_MODEL), row),
                  pl.BlockSpec((tm, D_MODEL), row),
                  pl.BlockSpec((1, 1, D_MODEL), bsel),
                  pl.BlockSpec((1, 1, D_MODEL), bsel),
                  pl.BlockSpec((1, 1, D_MODEL), bsel),
                  pl.BlockSpec((1, D_MODEL), fix),
                  pl.BlockSpec((512, D_MODEL), fix),
                  pl.BlockSpec((hw, D_MODEL), fix),
                  pl.BlockSpec((D_MODEL, D_MODEL), fix),
                  pl.BlockSpec((D_MODEL, LANE), fix),
                  pl.BlockSpec((D_MODEL, LANE), fix),
                  pl.BlockSpec((1, LANE), fix)],
        out_specs=[pl.BlockSpec((tm, D_MODEL), row),
                   pl.BlockSpec((tm, D_MODEL), row),
                   pl.BlockSpec((tm, LANE), row),
                   pl.BlockSpec((tm, LANE), row),
                   pl.BlockSpec((tm, LANE), row),
                   pl.BlockSpec((8, LANE), fix)],
        out_shape=[jax.ShapeDtypeStruct((n, D_MODEL), F32),
                   jax.ShapeDtypeStruct((n, D_MODEL), F32),
                   jax.ShapeDtypeStruct((n, LANE), jnp.int32),
                   jax.ShapeDtypeStruct((n, LANE), jnp.int32),
                   jax.ShapeDtypeStruct((n, LANE), F32),
                   jax.ShapeDtypeStruct((8, LANE), F32)],
        scratch_shapes=[pltpu.VMEM((8, LANE), F32)],
        compiler_params=_params(("arbitrary",)),
        name="outproj",
    )(ya, yb, ga, gb, x2, g1, sh2, sc2, nm, wa, wb_p, wo, wr_hi, wr_lo, br_p)


def _dest_kernel(pstart_ref, eidx_ref, rank_ref, o_ref):
    e = eidx_ref[...]
    acc = rank_ref[...]
    for ex in range(N_EXPERTS):
        acc = acc + jnp.where(e == ex, pstart_ref[ex], 0)
    o_ref[...] = acc


def _dest(pstarts, eidx_c, rank_c):
    return pl.pallas_call(
        _dest_kernel,
        in_specs=[pl.BlockSpec(memory_space=pltpu.SMEM),
                  pl.BlockSpec(memory_space=pltpu.VMEM),
                  pl.BlockSpec(memory_space=pltpu.VMEM)],
        out_specs=pl.BlockSpec(memory_space=pltpu.VMEM),
        out_shape=jax.ShapeDtypeStruct(eidx_c.shape, jnp.int32),
        name="dest",
    )(pstarts, eidx_c, rank_c)


def _dispatch_kernel(dest_ref, h_ref, xs_in_ref, xs_ref, sem):
    del xs_in_ref
    rows = h_ref.shape[0]

    def issue(r, carry):
        for k in range(TOP_K):
            d = dest_ref[0, 0, r * TOP_K + k]
            pltpu.make_async_copy(h_ref.at[pl.ds(r, 1), :], xs_ref.at[pl.ds(d, 1), :], sem).start()
        return carry

    lax.fori_loop(0, rows, issue, 0)
    for _ in range(TOP_K):
        pltpu.make_async_copy(h_ref, xs_ref.at[pl.ds(0, rows), :], sem).wait()


def _dispatch(dest3, h2, xs0):
    n = h2.shape[0]
    rt = ROW_TILE
    return pl.pallas_call(
        _dispatch_kernel,
        grid=(n // rt,),
        in_specs=[pl.BlockSpec((1, 1, rt * TOP_K), lambda i: (i, 0, 0), memory_space=pltpu.SMEM),
                  pl.BlockSpec((rt, D_MODEL), lambda i: (i, 0)),
                  pl.BlockSpec(memory_space=pl.ANY)],
        out_specs=pl.BlockSpec(memory_space=pl.ANY),
        out_shape=jax.ShapeDtypeStruct(xs0.shape, xs0.dtype),
        scratch_shapes=[pltpu.SemaphoreType.DMA(())],
        input_output_aliases={2: 0},
        compiler_params=_params(("arbitrary",)),
        name="dispatch",
    )(dest3, h2, xs0)


def _experts_kernel(be_ref, nu_ref, xs_ref, w1_ref, b1_ref, w2_ref, b2_ref, ys_ref):
    del be_ref

    @pl.when(pl.program_id(0) < nu_ref[0])
    def _():
        xb = xs_ref[...].astype(BF16)
        hcat = jnp.dot(xb, w1_ref[0], preferred_element_type=F32) + b1_ref[0]
        x_glu = jnp.minimum(hcat[:, :D_EXPERT], SWIGLU_LIMIT)
        x_lin = jnp.clip(hcat[:, D_EXPERT:], -SWIGLU_LIMIT, SWIGLU_LIMIT)
        act = (x_glu * jax.nn.sigmoid(SWIGLU_ALPHA * x_glu) * (x_lin + 1.0)).astype(BF16)
        ys_ref[...] = jnp.dot(act, w2_ref[0], preferred_element_type=F32) + b2_ref[0]

    @pl.when(pl.program_id(0) >= nu_ref[0])
    def _():
        ys_ref[...] = jnp.zeros(ys_ref.shape, F32)


def _experts(block_e, n_used, xs, w1, b1, w2, b2):
    p = xs.shape[0]
    mb = MOE_ROWS
    last = lambda i, nu: jnp.minimum(i, nu[0] - 1)
    rowmap = lambda i, be, nu: (last(i, nu), 0)
    wmap = lambda i, be, nu: (be[last(i, nu)], 0, 0)
    return pl.pallas_call(
        _experts_kernel,
        grid_spec=pltpu.PrefetchScalarGridSpec(
            num_scalar_prefetch=2,
            grid=(p // mb,),
            in_specs=[pl.BlockSpec((mb, D_MODEL), rowmap),
                      pl.BlockSpec((1, D_MODEL, 2 * D_EXPERT), wmap),
                      pl.BlockSpec((1, 1, 2 * D_EXPERT), wmap),
                      pl.BlockSpec((1, D_EXPERT, D_MODEL), wmap),
                      pl.BlockSpec((1, 1, D_MODEL), wmap)],
            out_specs=pl.BlockSpec((mb, D_MODEL), lambda i, be, nu: (i, 0))),
        out_shape=jax.ShapeDtypeStruct((p, D_MODEL), F32),
        compiler_params=_params(("arbitrary",)),
        name="experts",
    )(block_e, n_used, xs, w1, b1, w2, b2)


def _combine_kernel(dest_ref, ys_ref, topw_ref, x1_ref, g2_ref, fn_ref, o_ref, buf, sem):
    rows = x1_ref.shape[0]

    def issue(r, carry):
        for k in range(TOP_K):
            d = dest_ref[0, 0, r * TOP_K + k]
            pltpu.make_async_copy(ys_ref.at[pl.ds(d, 1), :], buf.at[k, pl.ds(r, 1), :], sem).start()
        return carry

    lax.fori_loop(0, rows, issue, 0)
    for k in range(TOP_K):
        pltpu.make_async_copy(ys_ref.at[pl.ds(0, rows), :], buf.at[k], sem).wait()
    tw = topw_ref[...]
    y = jnp.zeros((rows, D_MODEL), F32)
    for k in range(TOP_K):
        y = y + tw[:, k:k + 1] * buf[k]
    x = x1_ref[...] + g2_ref[0] * y
    o_ref[...] = _rms(x) * fn_ref[...]


def _combine(dest3, ys, topw, x1, g2, fn, seq):
    n = x1.shape[0]
    rt = ROW_TILE
    per_b = seq // rt
    return pl.pallas_call(
        _combine_kernel,
        grid=(n // rt,),
        in_specs=[pl.BlockSpec((1, 1, rt * TOP_K), lambda i: (i, 0, 0), memory_space=pltpu.SMEM),
                  pl.BlockSpec(memory_space=pl.ANY),
                  pl.BlockSpec((rt, LANE), lambda i: (i, 0)),
                  pl.BlockSpec((rt, D_MODEL), lambda i: (i, 0)),
                  pl.BlockSpec((1, 1, D_MODEL), lambda i: (i // per_b, 0, 0)),
                  pl.BlockSpec((1, D_MODEL), lambda i: (0, 0))],
        out_specs=pl.BlockSpec((rt, D_MODEL), lambda i: (i, 0)),
        out_shape=jax.ShapeDtypeStruct((n, D_MODEL), F32),
        scratch_shapes=[pltpu.VMEM((TOP_K, rt, D_MODEL), F32), pltpu.SemaphoreType.DMA(())],
        compiler_params=_params(("arbitrary",)),
        name="combine",
    )(dest3, ys, topw, x1, g2, fn)


def _pack_w_in(w_in):
    sq, skv = 512, 128
    o_ka, o_va, o_cq = sq, sq + skv, sq + 2 * skv
    o_ckv = o_cq + MLA_Q_RANK
    o_kr = o_ckv + MLA_KV_RANK
    o_ga = o_kr + MLA_ROPE_DIM
    o_gb = o_ga + D_MODEL
    z = lambda w: jnp.zeros((D_MODEL, w), w_in.dtype)
    ka, va = w_in[:, o_ka:o_va], w_in[:, o_va:o_cq]
    dup = lambda t: jnp.concatenate([t[:, :64], t[:, :64], t[:, 64:], t[:, 64:]], axis=1)
    kr1, kr2 = w_in[:, o_kr:o_kr + HALF_ROPE], w_in[:, o_kr + HALF_ROPE:o_ga]
    packed = jnp.concatenate([
        w_in[:, :sq], dup(ka), dup(va), w_in[:, o_cq:o_ckv], w_in[:, o_ckv:o_kr],
        z(64), kr1, kr2, z(32),
        z(64), kr2, kr1, z(32),
        w_in[:, o_ga:o_gb], w_in[:, o_gb:o_gb + D_MODEL]], axis=1)
    return packed.astype(BF16)


def _pack_heads(w, width, pieces):
    rows = w.shape[0]
    w3 = w.reshape(rows, MLA_HEADS, width)
    cols = [w3[:, :, a:b] for a, b in pieces]
    used = sum(b - a for a, b in pieces)
    cols.append(jnp.zeros((rows, MLA_HEADS, LANE - used), w.dtype))
    return jnp.concatenate(cols, axis=2).reshape(rows, MLA_HEADS * LANE)


def kernel(x, c, positions, w_ada, b_ada, norm_mix, norm_ffn, w_in, sinks, q_norm, kv_norm, w_uq, w_uk, w_uv,
           w_branch_a, w_branch_b, w_out, w_router, b_router, w_moe1, b_moe1, w_moe2, b_moe2, final_norm):
    batch, seq, d = x.shape
    n = batch * seq
    assert d == D_MODEL and w_ada.shape[0] == 1
    assert seq % TQ_MLA == 0 and seq % (SWA_QB * ATTN_BLOCK) == 0 and seq % TM_PROJ == 0
    hw = MLA_HEADS * LANE
    qk = MLA_NOPE_DIM + MLA_ROPE_DIM

    w_in_p = _pack_w_in(w_in[0])
    zq = jnp.zeros((MLA_Q_RANK, MLA_HEADS, MLA_NOPE_DIM), F32).reshape(MLA_Q_RANK, -1)
    wuq_main = _pack_heads(w_uq[0], qk, [(0, qk)])
    w3 = w_uq[0].reshape(MLA_Q_RANK, MLA_HEADS, qk)
    wuq_swap = jnp.concatenate([
        jnp.zeros((MLA_Q_RANK, MLA_HEADS, MLA_NOPE_DIM), F32),
        w3[:, :, MLA_NOPE_DIM + HALF_ROPE:], w3[:, :, MLA_NOPE_DIM:MLA_NOPE_DIM + HALF_ROPE],
        jnp.zeros((MLA_Q_RANK, MLA_HEADS, LANE - qk), F32)], axis=2).reshape(MLA_Q_RANK, hw)
    del zq
    wuq_p = jnp.concatenate([wuq_main, wuq_swap], axis=1).astype(BF16)
    wuk_p = _pack_heads(w_uk[0], MLA_NOPE_DIM, [(0, MLA_NOPE_DIM)]).astype(BF16)
    wuv_p = _pack_heads(w_uv[0], MLA_V_DIM, [(0, MLA_V_DIM)]).astype(BF16)
    one_row = jnp.tile((jnp.arange(LANE) == MLA_V_DIM).astype(F32), MLA_HEADS)[None, :]
    freqs = ROPE_THETA ** (-jnp.arange(0, MLA_ROPE_DIM, 2, dtype=F32) / MLA_ROPE_DIM)
    frq = jnp.concatenate([jnp.zeros((MLA_NOPE_DIM,), F32), freqs, freqs,
                           jnp.zeros((LANE - qk,), F32)])[None, :]
    sgn = jnp.concatenate([jnp.zeros((MLA_NOPE_DIM,), F32), -jnp.ones((HALF_ROPE,), F32),
                           jnp.ones((HALF_ROPE,), F32), jnp.zeros((LANE - qk,), F32)])[None, :]
    wb3 = w_branch_b[0].reshape(MLA_HEADS, MLA_V_DIM, D_MODEL)
    wb_p = jnp.concatenate([wb3, jnp.zeros((MLA_HEADS, LANE - MLA_V_DIM, D_MODEL), F32)],
                           axis=1).reshape(hw, D_MODEL).astype(BF16)
    wa = w_branch_a[0].astype(BF16)
    wo = w_out[0].astype(BF16)
    wr = jnp.pad(w_router[0], ((0, 0), (0, LANE - N_EXPERTS)))
    wr_hi = wr.astype(BF16)
    wr_lo = (wr - wr_hi.astype(F32)).astype(BF16)
    br_p = jnp.pad(b_router[0], (0, LANE - N_EXPERTS))[None, :]
    w1 = w_moe1[0].astype(BF16)
    w2 = w_moe2[0].astype(BF16)
    b1 = b_moe1[0][:, None, :]
    b2 = b_moe2[0][:, None, :]
    slopes = jnp.asarray(np.exp2(-8.0 * np.arange(1, SWA_HEADS + 1) / SWA_HEADS), dtype=F32)

    c8 = jnp.pad(c, ((0, 8 - batch), (0, 0)))
    mod = _ada(c8, w_ada[0], b_ada[0][None, :])[:batch]
    sh1, sc1, g1, sh2, sc2, g2 = [m[:, None, :] for m in jnp.split(mod, 6, axis=-1)]

    x2 = x.reshape(n, D_MODEL)
    pos2 = positions.reshape(n, 1).astype(jnp.int32)
    qa, ka2, va2, qm, km, vm, ga, gb = _inproj(
        x2, pos2, sh1, sc1, norm_mix, w_in_p, q_norm, kv_norm, wuq_p, wuk_p, wuv_p, frq, sgn, one_row, seq)
    yb = _mla(qm, km, vm, batch, seq)
    ya = _swa(sinks[0], slopes, qa, ka2, va2, batch, seq)

    x1, h2, eidx, rank, topw, cnt = _outproj(
        ya, yb, ga, gb, x2, g1, sh2, sc2, norm_ffn, wa, wb_p, wo, wr_hi, wr_lo, br_p, seq)

    counts = cnt[0, :N_EXPERTS].astype(jnp.int32)
    padded = ((counts + MOE_ROWS - 1) // MOE_ROWS) * MOE_ROWS
    pends = jnp.cumsum(padded)
    pstarts = (pends - padded).astype(jnp.int32)
    p_rows = n * TOP_K + N_EXPERTS * MOE_ROWS
    n_blocks = p_rows // MOE_ROWS
    block_e = jnp.minimum(jnp.searchsorted(pends, jnp.arange(n_blocks) * MOE_ROWS, side='right'),
                          N_EXPERTS - 1).astype(jnp.int32)
    n_used = (pends[-1:] // MOE_ROWS).astype(jnp.int32)

    compact = (n * TOP_K // LANE, LANE)
    dest_c = _dest(pstarts, eidx[:, :TOP_K].reshape(compact), rank[:, :TOP_K].reshape(compact))
    dest3 = dest_c.reshape(n // ROW_TILE, 1, ROW_TILE * TOP_K)

    xs = _dispatch(dest3, h2, jnp.zeros((p_rows, D_MODEL), F32))
    ys = _experts(block_e, n_used, xs, w1, b1, w2, b2)
    out = _combine(dest3, ys, topw, x1, g2, final_norm[None, :], seq)
    return out.reshape(batch, seq, D_MODEL)
```

```python
import functools

import numpy as np
import jax
import jax.numpy as jnp
from jax import lax
from jax.experimental import pallas as pl
from jax.experimental.pallas import tpu as pltpu

D_MODEL = 1024
SWA_HEADS = 8
SWA_KV_HEADS = 2
SWA_HEAD_DIM = 64
ATTN_BLOCK = 128
MLA_HEADS = 8
MLA_Q_RANK = 384
MLA_KV_RANK = 256
MLA_NOPE_DIM = 64
MLA_ROPE_DIM = 32
MLA_V_DIM = 64
ROPE_THETA = 10000.0
N_EXPERTS = 32
TOP_K = 4
D_EXPERT = 1024
SWIGLU_LIMIT = 7.0
SWIGLU_ALPHA = 1.702
NORM_EPS = 1e-6

LANE = 128
LOG2E = 1.4426950408889634
NEG = -1e30
SWA_QSCALE = SWA_HEAD_DIM ** -0.5 * LOG2E
MLA_QSCALE = (MLA_NOPE_DIM + MLA_ROPE_DIM) ** -0.5 * LOG2E
HALF_ROPE = MLA_ROPE_DIM // 2
MLA_PLAIN_BOUND = 60.0
MLA_WIDE = 4

BF16 = jnp.bfloat16
F32 = jnp.float32

TM_PROJ = 256
TQ_MLA = 512
SWA_QB = 4
MOE_ROWS = 256
ROW_TILE = 128
VMEM_LIMIT = 56 * 1024 * 1024


def _params(sem, vmem=VMEM_LIMIT):
    return pltpu.CompilerParams(dimension_semantics=sem, vmem_limit_bytes=vmem)


def _rms(x):
    return x * lax.rsqrt(jnp.mean(x * x, axis=-1, keepdims=True) + NORM_EPS)


def _ada_kernel(c_ref, w_ref, b_ref, o_ref):
    c = c_ref[...]
    a = (c * jax.nn.sigmoid(c)).astype(BF16)
    o_ref[...] = jnp.dot(a, w_ref[...].astype(BF16), preferred_element_type=F32) + b_ref[...]


def _ada(c8, w_ada, b_ada):
    n_out = w_ada.shape[1]
    return pl.pallas_call(
        _ada_kernel,
        grid=(n_out // D_MODEL,),
        in_specs=[pl.BlockSpec((8, D_MODEL), lambda j: (0, 0)),
                  pl.BlockSpec((D_MODEL, D_MODEL), lambda j: (0, j)),
                  pl.BlockSpec((1, D_MODEL), lambda j: (0, j))],
        out_specs=pl.BlockSpec((8, D_MODEL), lambda j: (0, j)),
        out_shape=jax.ShapeDtypeStruct((8, n_out), F32),
        compiler_params=_params(("arbitrary",)),
        name="ada",
    )(c8, w_ada, b_ada)


_C_QA = (0, 512)
_C_KA = (512, 768)
_C_VA = (768, 1024)
_C_CQ = (1024, 1408)
_C_CKV = (1408, 1664)
_C_KRA = (1664, 1792)
_C_KRB = (1792, 1920)
_C_GA = (1920, 2944)
_C_GB = (2944, 3968)
_W_IN_COLS = 3968


def _inproj_kernel(x_ref, pos_ref, sh_ref, sc_ref, nm_ref, win_ref, qn_ref, kvn_ref, wuq_ref, wuk_ref,
                   wuv_ref, frq_ref, sgn_ref, one_ref,
                   qa_ref, ka_ref, va_ref, qm_ref, km_ref, vm_ref, ga_ref, gb_ref):
    x = x_ref[...]
    h = (_rms(x) * nm_ref[...] * (1.0 + sc_ref[0]) + sh_ref[0]).astype(BF16)

    def proj(c):
        return jnp.dot(h, win_ref[:, c[0]:c[1]], preferred_element_type=F32)

    qa_ref[...] = (proj(_C_QA) * SWA_QSCALE).astype(BF16)
    ka_ref[...] = proj(_C_KA).astype(BF16)
    va_ref[...] = proj(_C_VA).astype(BF16)
    ga_ref[...] = jax.nn.sigmoid(proj(_C_GA)).astype(BF16)
    gb_ref[...] = jax.nn.sigmoid(proj(_C_GB)).astype(BF16)

    ang = pos_ref[...].astype(F32) * frq_ref[...]
    cs = jnp.cos(ang)
    sn = jnp.sin(ang) * sgn_ref[...]

    cqn = (_rms(proj(_C_CQ)) * qn_ref[...]).astype(BF16)
    q2 = jnp.dot(cqn, wuq_ref[...], preferred_element_type=F32)
    hw = MLA_HEADS * LANE
    for hh in range(MLA_HEADS):
        sl = slice(hh * LANE, (hh + 1) * LANE)
        sw = slice(hw + hh * LANE, hw + (hh + 1) * LANE)
        qm_ref[:, sl] = ((q2[:, sl] * cs + q2[:, sw] * sn) * MLA_QSCALE).astype(BF16)

    ckvn = (_rms(proj(_C_CKV)) * kvn_ref[...]).astype(BF16)
    krr = proj(_C_KRA) * cs + proj(_C_KRB) * sn
    kn = jnp.dot(ckvn, wuk_ref[...], preferred_element_type=F32)
    for hh in range(MLA_HEADS):
        sl = slice(hh * LANE, (hh + 1) * LANE)
        km_ref[:, sl] = (kn[:, sl] + krr).astype(BF16)
    vm_ref[...] = (jnp.dot(ckvn, wuv_ref[...], preferred_element_type=F32) + one_ref[...]).astype(BF16)


def _inproj(x2, pos2, sh1, sc1, nm, w_in_p, qn, kvn, wuq_p, wuk_p, wuv_p, frq, sgn, one_row, seq):
    n = x2.shape[0]
    tm = TM_PROJ
    per_b = seq // tm
    hw = MLA_HEADS * LANE
    row = lambda i: (i, 0)
    fix = lambda i: (0, 0)
    bsel = lambda i: (i // per_b, 0, 0)
    widths = (512, 256, 256, hw, hw, hw, D_MODEL, D_MODEL)
    return pl.pallas_call(
        _inproj_kernel,
        grid=(n // tm,),
        in_specs=[pl.BlockSpec((tm, D_MODEL), row),
                  pl.BlockSpec((tm, 1), row),
                  pl.BlockSpec((1, 1, D_MODEL), bsel),
                  pl.BlockSpec((1, 1, D_MODEL), bsel),
                  pl.BlockSpec((1, D_MODEL), fix),
                  pl.BlockSpec((D_MODEL, _W_IN_COLS), fix),
                  pl.BlockSpec((1, MLA_Q_RANK), fix),
                  pl.BlockSpec((1, MLA_KV_RANK), fix),
                  pl.BlockSpec((MLA_Q_RANK, 2 * hw), fix),
                  pl.BlockSpec((MLA_KV_RANK, hw), fix),
                  pl.BlockSpec((MLA_KV_RANK, hw), fix),
                  pl.BlockSpec((1, LANE), fix),
                  pl.BlockSpec((1, LANE), fix),
                  pl.BlockSpec((1, hw), fix)],
        out_specs=[pl.BlockSpec((tm, w), row) for w in widths],
        out_shape=[jax.ShapeDtypeStruct((n, w), BF16) for w in widths],
        compiler_params=_params(("arbitrary",)),
        name="inproj",
    )(x2, pos2, sh1, sc1, nm, w_in_p, qn, kvn, wuq_p, wuk_p, wuv_p, frq, sgn, one_row)


def _mla_kernel(q_ref, k_ref, v_ref, o_ref, m_scr, acc_scr, kn_scr):
    i = pl.program_id(2)
    t = TQ_MLA
    acc_scr[...] = jnp.zeros(acc_scr.shape, F32)

    @pl.when(i == 0)
    def _():
        kk = k_ref[...].astype(F32)
        kn_scr[...] = jnp.max(jnp.sum(kk * kk, axis=1, keepdims=True), axis=0, keepdims=True)

    def scores(j, nblk, masked):
        start = pl.multiple_of(j * t, t)
        k = k_ref[pl.ds(start, nblk * t), :]
        v = v_ref[pl.ds(start, nblk * t), :]
        s = lax.dot_general(q_ref[...], k, (((1,), (1,)), ((), ())), preferred_element_type=F32)
        if masked:
            qi = lax.broadcasted_iota(jnp.int32, s.shape, 0)
            kj = lax.broadcasted_iota(jnp.int32, s.shape, 1)
            s = jnp.where(kj <= qi, s, NEG)
        return s, v

    def plain_step(j, nblk, masked):
        s, v = scores(j, nblk, masked)
        acc_scr[...] += jnp.dot(jnp.exp2(s).astype(BF16), v, preferred_element_type=F32)

    def online_step(j, nblk, masked):
        s, v = scores(j, nblk, masked)
        m_old = m_scr[...]
        m_new = jnp.maximum(m_old, jnp.max(s, axis=1, keepdims=True))
        p = jnp.exp2(s - m_new)
        alpha = jnp.exp2(m_old - m_new)
        acc_scr[...] = acc_scr[...] * alpha + jnp.dot(p.astype(BF16), v, preferred_element_type=F32)
        m_scr[...] = m_new

    def run(step, wide):
        nw = lax.shift_right_logical(i, wide.bit_length() - 1)

        def wide_body(jj, carry):
            step(jj * wide, wide, False)
            return carry

        def body(j, carry):
            step(j, 1, False)
            return carry

        lax.fori_loop(0, nw, wide_body, 0)
        lax.fori_loop(nw * wide, i, body, 0)
        step(i, 1, True)

    qf = q_ref[...].astype(F32)
    qn = jnp.max(jnp.sum(qf * qf, axis=1, keepdims=True), axis=0, keepdims=True)
    small = jnp.max(qn * kn_scr[...]) <= MLA_PLAIN_BOUND ** 2

    @pl.when(small)
    def _():
        run(plain_step, MLA_WIDE)

    @pl.when(jnp.logical_not(small))
    def _():
        m_scr[...] = jnp.full(m_scr.shape, NEG, F32)
        run(online_step, 1)

    acc = acc_scr[...]
    o_ref[...] = (acc / acc[:, MLA_V_DIM:MLA_V_DIM + 1]).astype(BF16)


def _mla(qm, km, vm, batch, seq):
    n = qm.shape[0]
    t = TQ_MLA
    nq = seq // t
    qmap = lambda b, h, i: (b * nq + i, h)
    kmap = lambda b, h, i: (b, h)
    return pl.pallas_call(
        _mla_kernel,
        grid=(batch, MLA_HEADS, nq),
        in_specs=[pl.BlockSpec((t, LANE), qmap),
                  pl.BlockSpec((seq, LANE), kmap),
                  pl.BlockSpec((seq, LANE), kmap)],
        out_specs=pl.BlockSpec((t, LANE), qmap),
        out_shape=jax.ShapeDtypeStruct((n, MLA_HEADS * LANE), BF16),
        scratch_shapes=[pltpu.VMEM((t, 1), F32), pltpu.VMEM((t, LANE), F32), pltpu.VMEM((1, 1), F32)],
        compiler_params=_params(("arbitrary", "arbitrary", "arbitrary")),
        name="mla",
    )(qm, km, vm)


def _swa_kernel(sink_ref, slope_ref, q_ref, kc_ref, kp_ref, vc_ref, vp_ref, o_ref, bias_scr):
    b = pl.program_id(0)
    i = pl.program_id(1)
    blk = ATTN_BLOCK

    @pl.when((b == 0) & (i == 0))
    def _():
        qi = lax.broadcasted_iota(jnp.int32, (blk, 2 * blk), 0)
        kj = lax.broadcasted_iota(jnp.int32, (blk, 2 * blk), 1)
        dist = qi - kj + blk
        valid = (dist >= 0) & (dist < blk)
        distf = dist.astype(F32)
        for hd in range(SWA_HEADS):
            bias_scr[hd] = jnp.where(valid, -(slope_ref[hd] * LOG2E) * distf, NEG)

    lane = lax.broadcasted_iota(jnp.int32, (2 * blk, LANE), 1)
    lo = lane < SWA_HEAD_DIM
    kcol = lax.broadcasted_iota(jnp.int32, (blk, 2 * blk), 1)
    first_pen = jnp.where(kcol < blk, jnp.where(i == 0, NEG, 0.0), 0.0)
    zero = jnp.zeros((2 * blk, LANE), BF16)

    for qb in range(SWA_QB):
        if qb == 0:
            kprev, vprev = kp_ref[...], vp_ref[...]
        else:
            kprev = kc_ref[(qb - 1) * blk:qb * blk, :]
            vprev = vc_ref[(qb - 1) * blk:qb * blk, :]
        kcat = jnp.concatenate([kprev, kc_ref[qb * blk:(qb + 1) * blk, :]], axis=0)
        vcat = jnp.concatenate([vprev, vc_ref[qb * blk:(qb + 1) * blk, :]], axis=0)
        for g in range(SWA_KV_HEADS):
            kg = kcat[:, g * LANE:(g + 1) * LANE]
            vg = vcat[:, g * LANE:(g + 1) * LANE]
            halves = ((jnp.where(lo, kg, zero), jnp.where(lo, vg, zero)),
                      (jnp.where(lo, zero, kg), jnp.where(lo, zero, vg)))
            for pp in range(2):
                pr = g * 2 + pp
                qpair = q_ref[qb * blk:(qb + 1) * blk, pr * LANE:(pr + 1) * LANE]
                out = None
                for half in range(2):
                    hd = pr * 2 + half
                    kx, vx = halves[half]
                    s = lax.dot_general(qpair, kx, (((1,), (1,)), ((), ())), preferred_element_type=F32)
                    s = s + bias_scr[hd]
                    if qb == 0:
                        s = s + first_pen
                    sink = sink_ref[hd] * LOG2E
                    m = jnp.maximum(jnp.max(s, axis=1, keepdims=True), sink)
                    p = jnp.exp2(s - m)
                    denom = jnp.sum(p, axis=1, keepdims=True) + jnp.exp2(sink - m)
                    o = jnp.dot(p.astype(BF16), vx, preferred_element_type=F32) * (1.0 / denom)
                    out = o if out is None else out + o
                o_ref[qb * blk:(qb + 1) * blk, pr * LANE:(pr + 1) * LANE] = out.astype(BF16)


def _swa(sinks, slopes, qa, ka2, va2, batch, seq):
    n = qa.shape[0]
    blk = ATTN_BLOCK
    rows = SWA_QB * blk
    steps = seq // rows
    cur = lambda b, i: (b * steps + i, 0)
    prev = lambda b, i: (b * (seq // blk) + jnp.maximum(i * SWA_QB - 1, 0), 0)
    smem = pl.BlockSpec(memory_space=pltpu.SMEM)
    return pl.pallas_call(
        _swa_kernel,
        grid=(batch, steps),
        in_specs=[smem, smem,
                  pl.BlockSpec((rows, 512), cur),
                  pl.BlockSpec((rows, 256), cur),
                  pl.BlockSpec((blk, 256), prev),
                  pl.BlockSpec((rows, 256), cur),
                  pl.BlockSpec((blk, 256), prev)],
        out_specs=pl.BlockSpec((rows, 512), cur),
        out_shape=jax.ShapeDtypeStruct((n, 512), BF16),
        scratch_shapes=[pltpu.VMEM((SWA_HEADS, blk, 2 * blk), F32)],
        compiler_params=_params(("arbitrary", "arbitrary")),
        name="swa",
    )(sinks, slopes, qa, ka2, ka2, va2, va2)


def _outproj_kernel(ya_ref, yb_ref, ga_ref, gb_ref, x_ref, g1_ref, sh_ref, sc_ref, nm_ref, wa_ref, wb_ref,
                    wo_ref, wrh_ref, wrl_ref, br_ref,
                    x1_ref, h2_ref, eidx_ref, rank_ref, topw_ref, cnt_ref, carry_scr):
    tm = x_ref.shape[0]

    @pl.when(pl.program_id(0) == 0)
    def _():
        carry_scr[...] = jnp.zeros(carry_scr.shape, F32)

    a = jnp.dot(ya_ref[...], wa_ref[...], preferred_element_type=F32)
    bm = jnp.dot(yb_ref[...], wb_ref[...], preferred_element_type=F32)
    mixed = (ga_ref[...].astype(F32) * a + gb_ref[...].astype(F32) * bm).astype(BF16)
    x1 = x_ref[...] + g1_ref[0] * jnp.dot(mixed, wo_ref[...], preferred_element_type=F32)
    x1_ref[...] = x1
    h2 = _rms(x1) * nm_ref[...] * (1.0 + sc_ref[0]) + sh_ref[0]
    h2_ref[...] = h2

    hi = h2.astype(BF16)
    lo = (h2 - hi.astype(F32)).astype(BF16)
    logits = (jnp.dot(hi, wrh_ref[...], preferred_element_type=F32)
              + jnp.dot(lo, wrh_ref[...], preferred_element_type=F32)
              + jnp.dot(hi, wrl_ref[...], preferred_element_type=F32)) + br_ref[...]

    lane = lax.broadcasted_iota(jnp.int32, (tm, LANE), 1).astype(F32)
    work = jnp.where(lane < N_EXPERTS, logits, -jnp.inf)
    vals, idxs = [], []
    for _ in range(TOP_K):
        m = jnp.max(work, axis=1, keepdims=True)
        idx = jnp.min(jnp.where(work == m, lane, float(LANE)), axis=1, keepdims=True)
        vals.append(m)
        idxs.append(idx)
        work = jnp.where(lane == idx, -jnp.inf, work)
    exps = [jnp.exp(v - vals[0]) for v in vals]
    tot = exps[0] + exps[1] + exps[2] + exps[3]

    onehot = jnp.zeros((tm, LANE), F32)
    for idx in idxs:
        onehot = onehot + (lane == idx).astype(F32)
    r = lax.broadcasted_iota(jnp.int32, (tm, tm), 0)
    c = lax.broadcasted_iota(jnp.int32, (tm, tm), 1)
    tri = (c < r).astype(BF16)
    rank_e = carry_scr[0:1, :] + jnp.dot(tri, onehot.astype(BF16), preferred_element_type=F32)

    eidx = jnp.zeros((tm, LANE), F32)
    rank = jnp.zeros((tm, LANE), F32)
    topw = jnp.zeros((tm, LANE), F32)
    for k in range(TOP_K):
        rk = jnp.sum(jnp.where(lane == idxs[k], rank_e, 0.0), axis=1, keepdims=True)
        sel = lane == float(k)
        eidx = jnp.where(sel, idxs[k], eidx)
        rank = jnp.where(sel, rk, rank)
        topw = jnp.where(sel, exps[k] / tot, topw)
    eidx_ref[...] = eidx.astype(jnp.int32)
    rank_ref[...] = rank.astype(jnp.int32)
    topw_ref[...] = topw
    carry = carry_scr[...] + jnp.sum(onehot, axis=0, keepdims=True)
    carry_scr[...] = carry
    cnt_ref[...] = carry


def _outproj(ya, yb, ga, gb, x2, g1, sh2, sc2, nm, wa, wb_p, wo, wr_hi, wr_lo, br_p, seq):
    n = x2.shape[0]
    tm = TM_PROJ
    per_b = seq // tm
    hw = MLA_HEADS * LANE
    row = lambda i: (i, 0)
    fix = lambda i: (0, 0)
    bsel = lambda i: (i // per_b, 0, 0)
    return pl.pallas_call(
        _outproj_kernel,
        grid=(n // tm,),
        in_specs=[pl.BlockSpec((tm, 512), row),
                  pl.BlockSpec((tm, hw), row),
                  pl.BlockSpec((tm, D_MODEL), row),
                  pl.BlockSpec((tm, D_MODEL), row),
                  pl.BlockSpec((tm, D_MODEL), row),
                  pl.BlockSpec((1, 1, D_MODEL), bsel),
                  pl.BlockSpec((1, 1, D_MODEL), bsel),
                  pl.BlockSpec((1, 1, D_MODEL), bsel),
                  pl.BlockSpec((1, D_MODEL), fix),
                  pl.BlockSpec((512, D_MODEL), fix),
                  pl.BlockSpec((hw, D_MODEL), fix),
                  pl.BlockSpec((D_MODEL, D_MODEL), fix),
                  pl.BlockSpec((D_MODEL, LANE), fix),
                  pl.BlockSpec((D_MODEL, LANE), fix),
                  pl.BlockSpec((1, LANE), fix)],
        out_specs=[pl.BlockSpec((tm, D_MODEL), row),
                   pl.BlockSpec((tm, D_MODEL), row),
                   pl.BlockSpec((tm, LANE), row),
                   pl.BlockSpec((tm, LANE), row),
                   pl.BlockSpec((tm, LANE), row),
                   pl.BlockSpec((8, LANE), fix)],
        out_shape=[jax.ShapeDtypeStruct((n, D_MODEL), F32),
                   jax.ShapeDtypeStruct((n, D_MODEL), F32),
                   jax.ShapeDtypeStruct((n, LANE), jnp.int32),
                   jax.ShapeDtypeStruct((n, LANE), jnp.int32),
                   jax.ShapeDtypeStruct((n, LANE), F32),
                   jax.ShapeDtypeStruct((8, LANE), F32)],
        scratch_shapes=[pltpu.VMEM((8, LANE), F32)],
        compiler_params=_params(("arbitrary",)),
        name="outproj",
    )(ya, yb, ga, gb, x2, g1, sh2, sc2, nm, wa, wb_p, wo, wr_hi, wr_lo, br_p)


def _dest_kernel(pstart_ref, eidx_ref, rank_ref, o_ref):
    e = eidx_ref[...]
    acc = rank_ref[...]
    for ex in range(N_EXPERTS):
        acc = acc + jnp.where(e == ex, pstart_ref[ex], 0)
    o_ref[...] = acc


def _dest(pstarts, eidx_c, rank_c):
    return pl.pallas_call(
        _dest_kernel,
        in_specs=[pl.BlockSpec(memory_space=pltpu.SMEM),
                  pl.BlockSpec(memory_space=pltpu.VMEM),
                  pl.BlockSpec(memory_space=pltpu.VMEM)],
        out_specs=pl.BlockSpec(memory_space=pltpu.VMEM),
        out_shape=jax.ShapeDtypeStruct(eidx_c.shape, jnp.int32),
        name="dest",
    )(pstarts, eidx_c, rank_c)


def _dispatch_kernel(pends_ref, padded_ref, dest_ref, h_ref, xs_ref, zbuf, sem, zsem):
    rows = h_ref.shape[0]

    @pl.when(pl.program_id(0) == 0)
    def _():
        zbuf[...] = jnp.zeros(zbuf.shape, zbuf.dtype)

        def fill(e):
            start = pl.multiple_of(pends_ref[e] - MOE_ROWS, MOE_ROWS)
            return pltpu.make_async_copy(zbuf, xs_ref.at[pl.ds(start, MOE_ROWS), :], zsem)

        def tail(b):
            start = pl.multiple_of(b * MOE_ROWS, MOE_ROWS)
            return pltpu.make_async_copy(zbuf, xs_ref.at[pl.ds(start, MOE_ROWS), :], zsem)

        used = lax.shift_right_logical(pends_ref[N_EXPERTS - 1], MOE_ROWS.bit_length() - 1)
        n_blocks = xs_ref.shape[0] // MOE_ROWS
        for e in range(N_EXPERTS):
            pl.when(padded_ref[e] > 0)(lambda e=e: fill(e).start())
        lax.fori_loop(used, n_blocks, lambda b, c: (tail(b).start(), c)[1], 0)
        for e in range(N_EXPERTS):
            pl.when(padded_ref[e] > 0)(lambda e=e: fill(e).wait())
        lax.fori_loop(used, n_blocks, lambda b, c: (tail(b).wait(), c)[1], 0)

    def issue(r, carry):
        for k in range(TOP_K):
            d = dest_ref[0, 0, r * TOP_K + k]
            pltpu.make_async_copy(h_ref.at[pl.ds(r, 1), :], xs_ref.at[pl.ds(d, 1), :], sem).start(priority=k % 2)
        return carry

    lax.fori_loop(0, rows, issue, 0)
    for _ in range(TOP_K):
        pltpu.make_async_copy(h_ref, xs_ref.at[pl.ds(0, rows), :], sem).wait()


def _dispatch(pends, padded, dest3, h2, p_rows):
    n = h2.shape[0]
    rt = ROW_TILE
    smem = pl.BlockSpec(memory_space=pltpu.SMEM)
    return pl.pallas_call(
        _dispatch_kernel,
        grid=(n // rt,),
        in_specs=[smem, smem,
                  pl.BlockSpec((1, 1, rt * TOP_K), lambda i: (i, 0, 0), memory_space=pltpu.SMEM),
                  pl.BlockSpec((rt, D_MODEL), lambda i: (i, 0))],
        out_specs=pl.BlockSpec(memory_space=pl.ANY),
        out_shape=jax.ShapeDtypeStruct((p_rows, D_MODEL), h2.dtype),
        scratch_shapes=[pltpu.VMEM((MOE_ROWS, D_MODEL), h2.dtype),
                        pltpu.SemaphoreType.DMA(()), pltpu.SemaphoreType.DMA(())],
        compiler_params=_params(("arbitrary",)),
        name="dispatch",
    )(pends, padded, dest3, h2)


def _experts_kernel(be_ref, nu_ref, xs_ref, w1_ref, b1_ref, w2_ref, b2_ref, ys_ref):
    del be_ref

    @pl.when(pl.program_id(0) < nu_ref[0])
    def _():
        xb = xs_ref[...].astype(BF16)
        hcat = jnp.dot(xb, w1_ref[0].astype(BF16), preferred_element_type=F32) + b1_ref[0]
        x_glu = jnp.minimum(hcat[:, :D_EXPERT], SWIGLU_LIMIT)
        x_lin = jnp.clip(hcat[:, D_EXPERT:], -SWIGLU_LIMIT, SWIGLU_LIMIT)
        act = (x_glu * jax.nn.sigmoid(SWIGLU_ALPHA * x_glu) * (x_lin + 1.0)).astype(BF16)
        ys_ref[...] = jnp.dot(act, w2_ref[0].astype(BF16), preferred_element_type=F32) + b2_ref[0]

    @pl.when(pl.program_id(0) >= nu_ref[0])
    def _():
        ys_ref[...] = jnp.zeros(ys_ref.shape, F32)


def _experts(block_e, n_used, xs, w1, b1, w2, b2):
    p = xs.shape[0]
    mb = MOE_ROWS
    last = lambda i, nu: jnp.minimum(i, nu[0] - 1)
    rowmap = lambda i, be, nu: (last(i, nu), 0)
    wmap = lambda i, be, nu: (be[last(i, nu)], 0, 0)
    return pl.pallas_call(
        _experts_kernel,
        grid_spec=pltpu.PrefetchScalarGridSpec(
            num_scalar_prefetch=2,
            grid=(p // mb,),
            in_specs=[pl.BlockSpec((mb, D_MODEL), rowmap),
                      pl.BlockSpec((1, D_MODEL, 2 * D_EXPERT), wmap),
                      pl.BlockSpec((1, 1, 2 * D_EXPERT), wmap),
                      pl.BlockSpec((1, D_EXPERT, D_MODEL), wmap),
                      pl.BlockSpec((1, 1, D_MODEL), wmap)],
            out_specs=pl.BlockSpec((mb, D_MODEL), lambda i, be, nu: (i, 0))),
        out_shape=jax.ShapeDtypeStruct((p, D_MODEL), F32),
        compiler_params=_params(("arbitrary",)),
        name="experts",
    )(block_e, n_used, xs, w1, b1, w2, b2)


def _combine_kernel(dest_ref, ys_ref, topw_ref, x1_ref, g2_ref, fn_ref, o_ref, buf, sem):
    rows = x1_ref.shape[0]

    def issue(r, carry):
        for k in range(TOP_K):
            d = dest_ref[0, 0, r * TOP_K + k]
            pltpu.make_async_copy(ys_ref.at[pl.ds(d, 1), :], buf.at[k, pl.ds(r, 1), :], sem).start(priority=k % 2)
        return carry

    lax.fori_loop(0, rows, issue, 0)
    for k in range(TOP_K):
        pltpu.make_async_copy(ys_ref.at[pl.ds(0, rows), :], buf.at[k], sem).wait()
    tw = topw_ref[...]
    y = jnp.zeros((rows, D_MODEL), F32)
    for k in range(TOP_K):
        y = y + tw[:, k:k + 1] * buf[k]
    x = x1_ref[...] + g2_ref[0] * y
    o_ref[...] = _rms(x) * fn_ref[...]


def _combine(dest3, ys, topw, x1, g2, fn, seq):
    n = x1.shape[0]
    rt = ROW_TILE
    per_b = seq // rt
    return pl.pallas_call(
        _combine_kernel,
        grid=(n // rt,),
        in_specs=[pl.BlockSpec((1, 1, rt * TOP_K), lambda i: (i, 0, 0), memory_space=pltpu.SMEM),
                  pl.BlockSpec(memory_space=pl.ANY),
                  pl.BlockSpec((rt, LANE), lambda i: (i, 0)),
                  pl.BlockSpec((rt, D_MODEL), lambda i: (i, 0)),
                  pl.BlockSpec((1, 1, D_MODEL), lambda i: (i // per_b, 0, 0)),
                  pl.BlockSpec((1, D_MODEL), lambda i: (0, 0))],
        out_specs=pl.BlockSpec((rt, D_MODEL), lambda i: (i, 0)),
        out_shape=jax.ShapeDtypeStruct((n, D_MODEL), F32),
        scratch_shapes=[pltpu.VMEM((TOP_K, rt, D_MODEL), F32), pltpu.SemaphoreType.DMA(())],
        compiler_params=_params(("arbitrary",)),
        name="combine",
    )(dest3, ys, topw, x1, g2, fn)


def _pack_w_in(w_in):
    sq, skv = 512, 128
    o_ka, o_va, o_cq = sq, sq + skv, sq + 2 * skv
    o_ckv = o_cq + MLA_Q_RANK
    o_kr = o_ckv + MLA_KV_RANK
    o_ga = o_kr + MLA_ROPE_DIM
    o_gb = o_ga + D_MODEL
    z = lambda w: jnp.zeros((D_MODEL, w), w_in.dtype)
    ka, va = w_in[:, o_ka:o_va], w_in[:, o_va:o_cq]
    dup = lambda t: jnp.concatenate([t[:, :64], t[:, :64], t[:, 64:], t[:, 64:]], axis=1)
    kr1, kr2 = w_in[:, o_kr:o_kr + HALF_ROPE], w_in[:, o_kr + HALF_ROPE:o_ga]
    packed = jnp.concatenate([
        w_in[:, :sq], dup(ka), dup(va), w_in[:, o_cq:o_ckv], w_in[:, o_ckv:o_kr],
        z(64), kr1, kr2, z(32),
        z(64), kr2, kr1, z(32),
        w_in[:, o_ga:o_gb], w_in[:, o_gb:o_gb + D_MODEL]], axis=1)
    return packed.astype(BF16)


def _pack_heads(w, width, pieces):
    rows = w.shape[0]
    w3 = w.reshape(rows, MLA_HEADS, width)
    cols = [w3[:, :, a:b] for a, b in pieces]
    used = sum(b - a for a, b in pieces)
    cols.append(jnp.zeros((rows, MLA_HEADS, LANE - used), w.dtype))
    return jnp.concatenate(cols, axis=2).reshape(rows, MLA_HEADS * LANE)


def kernel(x, c, positions, w_ada, b_ada, norm_mix, norm_ffn, w_in, sinks, q_norm, kv_norm, w_uq, w_uk, w_uv,
           w_branch_a, w_branch_b, w_out, w_router, b_router, w_moe1, b_moe1, w_moe2, b_moe2, final_norm):
    batch, seq, d = x.shape
    n = batch * seq
    assert d == D_MODEL and w_ada.shape[0] == 1
    assert seq % TQ_MLA == 0 and seq % (SWA_QB * ATTN_BLOCK) == 0 and seq % TM_PROJ == 0
    hw = MLA_HEADS * LANE
    qk = MLA_NOPE_DIM + MLA_ROPE_DIM

    w_in_p = _pack_w_in(w_in[0])
    zq = jnp.zeros((MLA_Q_RANK, MLA_HEADS, MLA_NOPE_DIM), F32).reshape(MLA_Q_RANK, -1)
    wuq_main = _pack_heads(w_uq[0], qk, [(0, qk)])
    w3 = w_uq[0].reshape(MLA_Q_RANK, MLA_HEADS, qk)
    wuq_swap = jnp.concatenate([
        jnp.zeros((MLA_Q_RANK, MLA_HEADS, MLA_NOPE_DIM), F32),
        w3[:, :, MLA_NOPE_DIM + HALF_ROPE:], w3[:, :, MLA_NOPE_DIM:MLA_NOPE_DIM + HALF_ROPE],
        jnp.zeros((MLA_Q_RANK, MLA_HEADS, LANE - qk), F32)], axis=2).reshape(MLA_Q_RANK, hw)
    del zq
    wuq_p = jnp.concatenate([wuq_main, wuq_swap], axis=1).astype(BF16)
    wuk_p = _pack_heads(w_uk[0], MLA_NOPE_DIM, [(0, MLA_NOPE_DIM)]).astype(BF16)
    wuv_p = _pack_heads(w_uv[0], MLA_V_DIM, [(0, MLA_V_DIM)]).astype(BF16)
    one_row = jnp.tile((jnp.arange(LANE) == MLA_V_DIM).astype(F32), MLA_HEADS)[None, :]
    freqs = ROPE_THETA ** (-jnp.arange(0, MLA_ROPE_DIM, 2, dtype=F32) / MLA_ROPE_DIM)
    frq = jnp.concatenate([jnp.zeros((MLA_NOPE_DIM,), F32), freqs, freqs,
                           jnp.zeros((LANE - qk,), F32)])[None, :]
    sgn = jnp.concatenate([jnp.zeros((MLA_NOPE_DIM,), F32), -jnp.ones((HALF_ROPE,), F32),
                           jnp.ones((HALF_ROPE,), F32), jnp.zeros((LANE - qk,), F32)])[None, :]
    wb3 = w_branch_b[0].reshape(MLA_HEADS, MLA_V_DIM, D_MODEL)
    wb_p = jnp.concatenate([wb3, jnp.zeros((MLA_HEADS, LANE - MLA_V_DIM, D_MODEL), F32)],
                           axis=1).reshape(hw, D_MODEL).astype(BF16)
    wa = w_branch_a[0].astype(BF16)
    wo = w_out[0].astype(BF16)
    wr = jnp.pad(w_router[0], ((0, 0), (0, LANE - N_EXPERTS)))
    wr_hi = wr.astype(BF16)
    wr_lo = (wr - wr_hi.astype(F32)).astype(BF16)
    br_p = jnp.pad(b_router[0], (0, LANE - N_EXPERTS))[None, :]
    b1 = b_moe1[0][:, None, :]
    b2 = b_moe2[0][:, None, :]
    slopes = jnp.asarray(np.exp2(-8.0 * np.arange(1, SWA_HEADS + 1) / SWA_HEADS), dtype=F32)

    c8 = jnp.pad(c, ((0, 8 - batch), (0, 0)))
    mod = _ada(c8, w_ada[0], b_ada[0][None, :])[:batch]
    sh1, sc1, g1, sh2, sc2, g2 = [m[:, None, :] for m in jnp.split(mod, 6, axis=-1)]

    x2 = x.reshape(n, D_MODEL)
    pos2 = positions.reshape(n, 1).astype(jnp.int32)
    qa, ka2, va2, qm, km, vm, ga, gb = _inproj(
        x2, pos2, sh1, sc1, norm_mix, w_in_p, q_norm, kv_norm, wuq_p, wuk_p, wuv_p, frq, sgn, one_row, seq)
    yb = _mla(qm, km, vm, batch, seq)
    ya = _swa(sinks[0], slopes, qa, ka2, va2, batch, seq)

    x1, h2, eidx, rank, topw, cnt = _outproj(
        ya, yb, ga, gb, x2, g1, sh2, sc2, norm_ffn, wa, wb_p, wo, wr_hi, wr_lo, br_p, seq)

    counts = cnt[0, :N_EXPERTS].astype(jnp.int32)
    padded = ((counts + MOE_ROWS - 1) // MOE_ROWS) * MOE_ROWS
    pends = jnp.cumsum(padded)
    pstarts = (pends - padded).astype(jnp.int32)
    p_rows = n * TOP_K + N_EXPERTS * MOE_ROWS
    n_blocks = p_rows // MOE_ROWS
    block_start = jnp.arange(n_blocks, dtype=jnp.int32) * MOE_ROWS
    block_e = jnp.minimum(jnp.sum((pends[None, :] <= block_start[:, None]).astype(jnp.int32), axis=1),
                          N_EXPERTS - 1).astype(jnp.int32)
    n_used = (pends[-1:] // MOE_ROWS).astype(jnp.int32)

    compact = (n * TOP_K // LANE, LANE)
    dest_c = _dest(pstarts, eidx[:, :TOP_K].reshape(compact), rank[:, :TOP_K].reshape(compact))
    dest3 = dest_c.reshape(n // ROW_TILE, 1, ROW_TILE * TOP_K)

    xs = _dispatch(pends.astype(jnp.int32), padded.astype(jnp.int32), dest3, h2, p_rows)
    ys = _experts(block_e, n_used, xs, w_moe1[0], b1, w_moe2[0], b2)
    out = _combine(dest3, ys, topw, x1, g2, final_norm[None, :], seq)
    return out.reshape(batch, seq, D_MODEL)
```

```python
import functools

import numpy as np
import jax
import jax.numpy as jnp
from jax import lax
from jax.experimental import pallas as pl
from jax.experimental.pallas import tpu as pltpu

D_MODEL = 1024
SWA_HEADS = 8
SWA_KV_HEADS = 2
SWA_HEAD_DIM = 64
ATTN_BLOCK = 128
MLA_HEADS = 8
MLA_Q_RANK = 384
MLA_KV_RANK = 256
MLA_NOPE_DIM = 64
MLA_ROPE_DIM = 32
MLA_V_DIM = 64
ROPE_THETA = 10000.0
N_EXPERTS = 32
TOP_K = 4
D_EXPERT = 1024
SWIGLU_LIMIT = 7.0
SWIGLU_ALPHA = 1.702
NORM_EPS = 1e-6

LANE = 128
LOG2E = 1.4426950408889634
NEG = -1e30
SWA_QSCALE = SWA_HEAD_DIM ** -0.5 * LOG2E
MLA_QSCALE = (MLA_NOPE_DIM + MLA_ROPE_DIM) ** -0.5 * LOG2E
HALF_ROPE = MLA_ROPE_DIM // 2
MLA_PLAIN_BOUND = 60.0
MLA_WIDE = 4

BF16 = jnp.bfloat16
F32 = jnp.float32

TM_PROJ = 256
TQ_MLA = 512
SWA_QB = 4
MOE_ROWS = 256
ROUTE_TILE = 512
GROUP_ALIGN = 8
SLOTS_PER_TILE = ROUTE_TILE * TOP_K + N_EXPERTS * GROUP_ALIGN
SLOT_CHUNK = 768
XS_COLS = D_MODEL + LANE
VMEM_LIMIT = 56 * 1024 * 1024


def _params(sem, vmem=VMEM_LIMIT):
    return pltpu.CompilerParams(dimension_semantics=sem, vmem_limit_bytes=vmem)


def _rms(x):
    return x * lax.rsqrt(jnp.mean(x * x, axis=-1, keepdims=True) + NORM_EPS)


def _ada_kernel(c_ref, w_ref, b_ref, o_ref):
    c = c_ref[...]
    a = (c * jax.nn.sigmoid(c)).astype(BF16)
    o_ref[...] = jnp.dot(a, w_ref[...].astype(BF16), preferred_element_type=F32) + b_ref[...]


def _ada(c8, w_ada, b_ada):
    n_out = w_ada.shape[1]
    return pl.pallas_call(
        _ada_kernel,
        grid=(n_out // D_MODEL,),
        in_specs=[pl.BlockSpec((8, D_MODEL), lambda j: (0, 0)),
                  pl.BlockSpec((D_MODEL, D_MODEL), lambda j: (0, j)),
                  pl.BlockSpec((1, D_MODEL), lambda j: (0, j))],
        out_specs=pl.BlockSpec((8, D_MODEL), lambda j: (0, j)),
        out_shape=jax.ShapeDtypeStruct((8, n_out), F32),
        compiler_params=_params(("arbitrary",)),
        name="ada",
    )(c8, w_ada, b_ada)


_C_QA = (0, 512)
_C_KA = (512, 768)
_C_VA = (768, 1024)
_C_CQ = (1024, 1408)
_C_CKV = (1408, 1664)
_C_KRA = (1664, 1792)
_C_KRB = (1792, 1920)
_C_GA = (1920, 2944)
_C_GB = (2944, 3968)
_W_IN_COLS = 3968


def _inproj_kernel(x_ref, pos_ref, sh_ref, sc_ref, nm_ref, win_ref, qn_ref, kvn_ref, wuq_ref, wuk_ref,
                   wuv_ref, frq_ref, sgn_ref, one_ref,
                   qa_ref, ka_ref, va_ref, qm_ref, km_ref, vm_ref, ga_ref, gb_ref):
    x = x_ref[...]
    h = (_rms(x) * nm_ref[...] * (1.0 + sc_ref[0]) + sh_ref[0]).astype(BF16)

    def proj(c):
        return jnp.dot(h, win_ref[:, c[0]:c[1]], preferred_element_type=F32)

    qa_ref[...] = (proj(_C_QA) * SWA_QSCALE).astype(BF16)
    ka_ref[...] = proj(_C_KA).astype(BF16)
    va_ref[...] = proj(_C_VA).astype(BF16)
    ga_ref[...] = jax.nn.sigmoid(proj(_C_GA)).astype(BF16)
    gb_ref[...] = jax.nn.sigmoid(proj(_C_GB)).astype(BF16)

    ang = pos_ref[...].astype(F32) * frq_ref[...]
    cs = jnp.cos(ang)
    sn = jnp.sin(ang) * sgn_ref[...]

    cqn = (_rms(proj(_C_CQ)) * qn_ref[...]).astype(BF16)
    q2 = jnp.dot(cqn, wuq_ref[...], preferred_element_type=F32)
    hw = MLA_HEADS * LANE
    for hh in range(MLA_HEADS):
        sl = slice(hh * LANE, (hh + 1) * LANE)
        sw = slice(hw + hh * LANE, hw + (hh + 1) * LANE)
        qm_ref[:, sl] = ((q2[:, sl] * cs + q2[:, sw] * sn) * MLA_QSCALE).astype(BF16)

    ckvn = (_rms(proj(_C_CKV)) * kvn_ref[...]).astype(BF16)
    krr = proj(_C_KRA) * cs + proj(_C_KRB) * sn
    kn = jnp.dot(ckvn, wuk_ref[...], preferred_element_type=F32)
    for hh in range(MLA_HEADS):
        sl = slice(hh * LANE, (hh + 1) * LANE)
        km_ref[:, sl] = (kn[:, sl] + krr).astype(BF16)
    vm_ref[...] = (jnp.dot(ckvn, wuv_ref[...], preferred_element_type=F32) + one_ref[...]).astype(BF16)


def _inproj(x2, pos2, sh1, sc1, nm, w_in_p, qn, kvn, wuq_p, wuk_p, wuv_p, frq, sgn, one_row, seq):
    n = x2.shape[0]
    tm = TM_PROJ
    per_b = seq // tm
    hw = MLA_HEADS * LANE
    row = lambda i: (i, 0)
    fix = lambda i: (0, 0)
    bsel = lambda i: (i // per_b, 0, 0)
    widths = (512, 256, 256, hw, hw, hw, D_MODEL, D_MODEL)
    return pl.pallas_call(
        _inproj_kernel,
        grid=(n // tm,),
        in_specs=[pl.BlockSpec((tm, D_MODEL), row),
                  pl.BlockSpec((tm, 1), row),
                  pl.BlockSpec((1, 1, D_MODEL), bsel),
                  pl.BlockSpec((1, 1, D_MODEL), bsel),
                  pl.BlockSpec((1, D_MODEL), fix),
                  pl.BlockSpec((D_MODEL, _W_IN_COLS), fix),
                  pl.BlockSpec((1, MLA_Q_RANK), fix),
                  pl.BlockSpec((1, MLA_KV_RANK), fix),
                  pl.BlockSpec((MLA_Q_RANK, 2 * hw), fix),
                  pl.BlockSpec((MLA_KV_RANK, hw), fix),
                  pl.BlockSpec((MLA_KV_RANK, hw), fix),
                  pl.BlockSpec((1, LANE), fix),
                  pl.BlockSpec((1, LANE), fix),
                  pl.BlockSpec((1, hw), fix)],
        out_specs=[pl.BlockSpec((tm, w), row) for w in widths],
        out_shape=[jax.ShapeDtypeStruct((n, w), BF16) for w in widths],
        compiler_params=_params(("arbitrary",)),
        name="inproj",
    )(x2, pos2, sh1, sc1, nm, w_in_p, qn, kvn, wuq_p, wuk_p, wuv_p, frq, sgn, one_row)


def _mla_kernel(q_ref, k_ref, v_ref, o_ref, m_scr, acc_scr, kn_scr):
    i = pl.program_id(2)
    t = TQ_MLA
    acc_scr[...] = jnp.zeros(acc_scr.shape, F32)

    @pl.when(i == 0)
    def _():
        kk = k_ref[...].astype(F32)
        kn_scr[...] = jnp.max(jnp.sum(kk * kk, axis=1, keepdims=True), axis=0, keepdims=True)

    def scores(j, nblk, masked):
        start = pl.multiple_of(j * t, t)
        k = k_ref[pl.ds(start, nblk * t), :]
        v = v_ref[pl.ds(start, nblk * t), :]
        s = lax.dot_general(q_ref[...], k, (((1,), (1,)), ((), ())), preferred_element_type=F32)
        if masked:
            qi = lax.broadcasted_iota(jnp.int32, s.shape, 0)
            kj = lax.broadcasted_iota(jnp.int32, s.shape, 1)
            s = jnp.where(kj <= qi, s, NEG)
        return s, v

    def plain_step(j, nblk, masked):
        s, v = scores(j, nblk, masked)
        acc_scr[...] += jnp.dot(jnp.exp2(s).astype(BF16), v, preferred_element_type=F32)

    def online_step(j, nblk, masked):
        s, v = scores(j, nblk, masked)
        m_old = m_scr[...]
        m_new = jnp.maximum(m_old, jnp.max(s, axis=1, keepdims=True))
        p = jnp.exp2(s - m_new)
        alpha = jnp.exp2(m_old - m_new)
        acc_scr[...] = acc_scr[...] * alpha + jnp.dot(p.astype(BF16), v, preferred_element_type=F32)
        m_scr[...] = m_new

    def run(step, wide):
        nw = lax.shift_right_logical(i, wide.bit_length() - 1)

        def wide_body(jj, carry):
            step(jj * wide, wide, False)
            return carry

        def body(j, carry):
            step(j, 1, False)
            return carry

        lax.fori_loop(0, nw, wide_body, 0)
        lax.fori_loop(nw * wide, i, body, 0)
        step(i, 1, True)

    qf = q_ref[...].astype(F32)
    qn = jnp.max(jnp.sum(qf * qf, axis=1, keepdims=True), axis=0, keepdims=True)
    small = jnp.max(qn * kn_scr[...]) <= MLA_PLAIN_BOUND ** 2

    @pl.when(small)
    def _():
        run(plain_step, MLA_WIDE)

    @pl.when(jnp.logical_not(small))
    def _():
        m_scr[...] = jnp.full(m_scr.shape, NEG, F32)
        run(online_step, 1)

    acc = acc_scr[...]
    o_ref[...] = (acc / acc[:, MLA_V_DIM:MLA_V_DIM + 1]).astype(BF16)


def _mla(qm, km, vm, batch, seq):
    n = qm.shape[0]
    t = TQ_MLA
    nq = seq // t
    qmap = lambda b, h, i: (b * nq + i, h)
    kmap = lambda b, h, i: (b, h)
    return pl.pallas_call(
        _mla_kernel,
        grid=(batch, MLA_HEADS, nq),
        in_specs=[pl.BlockSpec((t, LANE), qmap),
                  pl.BlockSpec((seq, LANE), kmap),
                  pl.BlockSpec((seq, LANE), kmap)],
        out_specs=pl.BlockSpec((t, LANE), qmap),
        out_shape=jax.ShapeDtypeStruct((n, MLA_HEADS * LANE), BF16),
        scratch_shapes=[pltpu.VMEM((t, 1), F32), pltpu.VMEM((t, LANE), F32), pltpu.VMEM((1, 1), F32)],
        compiler_params=_params(("arbitrary", "arbitrary", "arbitrary")),
        name="mla",
    )(qm, km, vm)


def _swa_kernel(sink_ref, slope_ref, q_ref, kc_ref, kp_ref, vc_ref, vp_ref, o_ref, bias_scr):
    b = pl.program_id(0)
    i = pl.program_id(1)
    blk = ATTN_BLOCK

    @pl.when((b == 0) & (i == 0))
    def _():
        qi = lax.broadcasted_iota(jnp.int32, (blk, 2 * blk), 0)
        kj = lax.broadcasted_iota(jnp.int32, (blk, 2 * blk), 1)
        dist = qi - kj + blk
        valid = (dist >= 0) & (dist < blk)
        distf = dist.astype(F32)
        for hd in range(SWA_HEADS):
            bias_scr[hd] = jnp.where(valid, -(slope_ref[hd] * LOG2E) * distf, NEG)

    lane = lax.broadcasted_iota(jnp.int32, (2 * blk, LANE), 1)
    lo = lane < SWA_HEAD_DIM
    kcol = lax.broadcasted_iota(jnp.int32, (blk, 2 * blk), 1)
    first_pen = jnp.where(kcol < blk, jnp.where(i == 0, NEG, 0.0), 0.0)
    zero = jnp.zeros((2 * blk, LANE), BF16)

    for qb in range(SWA_QB):
        if qb == 0:
            kprev, vprev = kp_ref[...], vp_ref[...]
        else:
            kprev = kc_ref[(qb - 1) * blk:qb * blk, :]
            vprev = vc_ref[(qb - 1) * blk:qb * blk, :]
        kcat = jnp.concatenate([kprev, kc_ref[qb * blk:(qb + 1) * blk, :]], axis=0)
        vcat = jnp.concatenate([vprev, vc_ref[qb * blk:(qb + 1) * blk, :]], axis=0)
        for g in range(SWA_KV_HEADS):
            kg = kcat[:, g * LANE:(g + 1) * LANE]
            vg = vcat[:, g * LANE:(g + 1) * LANE]
            halves = ((jnp.where(lo, kg, zero), jnp.where(lo, vg, zero)),
                      (jnp.where(lo, zero, kg), jnp.where(lo, zero, vg)))
            for pp in range(2):
                pr = g * 2 + pp
                qpair = q_ref[qb * blk:(qb + 1) * blk, pr * LANE:(pr + 1) * LANE]
                out = None
                for half in range(2):
                    hd = pr * 2 + half
                    kx, vx = halves[half]
                    s = lax.dot_general(qpair, kx, (((1,), (1,)), ((), ())), preferred_element_type=F32)
                    s = s + bias_scr[hd]
                    if qb == 0:
                        s = s + first_pen
                    sink = sink_ref[hd] * LOG2E
                    m = jnp.maximum(jnp.max(s, axis=1, keepdims=True), sink)
                    p = jnp.exp2(s - m)
                    denom = jnp.sum(p, axis=1, keepdims=True) + jnp.exp2(sink - m)
                    o = jnp.dot(p.astype(BF16), vx, preferred_element_type=F32) * (1.0 / denom)
                    out = o if out is None else out + o
                o_ref[qb * blk:(qb + 1) * blk, pr * LANE:(pr + 1) * LANE] = out.astype(BF16)


def _swa(sinks, slopes, qa, ka2, va2, batch, seq):
    n = qa.shape[0]
    blk = ATTN_BLOCK
    rows = SWA_QB * blk
    steps = seq // rows
    cur = lambda b, i: (b * steps + i, 0)
    prev = lambda b, i: (b * (seq // blk) + jnp.maximum(i * SWA_QB - 1, 0), 0)
    smem = pl.BlockSpec(memory_space=pltpu.SMEM)
    return pl.pallas_call(
        _swa_kernel,
        grid=(batch, steps),
        in_specs=[smem, smem,
                  pl.BlockSpec((rows, 512), cur),
                  pl.BlockSpec((rows, 256), cur),
                  pl.BlockSpec((blk, 256), prev),
                  pl.BlockSpec((rows, 256), cur),
                  pl.BlockSpec((blk, 256), prev)],
        out_specs=pl.BlockSpec((rows, 512), cur),
        out_shape=jax.ShapeDtypeStruct((n, 512), BF16),
        scratch_shapes=[pltpu.VMEM((SWA_HEADS, blk, 2 * blk), F32)],
        compiler_params=_params(("arbitrary", "arbitrary")),
        name="swa",
    )(sinks, slopes, qa, ka2, ka2, va2, va2)


def _outproj_kernel(ya_ref, yb_ref, ga_ref, gb_ref, x_ref, g1_ref, sh_ref, sc_ref, nm_ref, wa_ref, wb_ref,
                    wo_ref, wrh_ref, wrl_ref, br_ref,
                    x1_ref, h2_ref, route_ref, routet_ref, g8_ref):
    tm = x_ref.shape[0]
    a = jnp.dot(ya_ref[...], wa_ref[...], preferred_element_type=F32)
    bm = jnp.dot(yb_ref[...], wb_ref[...], preferred_element_type=F32)
    mixed = (ga_ref[...].astype(F32) * a + gb_ref[...].astype(F32) * bm).astype(BF16)
    x1 = x_ref[...] + g1_ref[0] * jnp.dot(mixed, wo_ref[...], preferred_element_type=F32)
    x1_ref[...] = x1
    h2 = _rms(x1) * nm_ref[...] * (1.0 + sc_ref[0]) + sh_ref[0]
    hi = h2.astype(BF16)
    h2_ref[...] = hi

    lo = (h2 - hi.astype(F32)).astype(BF16)
    logits = (jnp.dot(hi, wrh_ref[...], preferred_element_type=F32)
              + jnp.dot(lo, wrh_ref[...], preferred_element_type=F32)
              + jnp.dot(hi, wrl_ref[...], preferred_element_type=F32)) + br_ref[...]

    lane = lax.broadcasted_iota(jnp.int32, (tm, LANE), 1).astype(F32)
    work = jnp.where(lane < N_EXPERTS, logits, -jnp.inf)
    vals, idxs = [], []
    for _ in range(TOP_K):
        m = jnp.max(work, axis=1, keepdims=True)
        idx = jnp.min(jnp.where(work == m, lane, float(LANE)), axis=1, keepdims=True)
        vals.append(m)
        idxs.append(idx)
        work = jnp.where(lane == idx, -jnp.inf, work)
    exps = [jnp.exp(v - vals[0]) for v in vals]
    tot = exps[0] + exps[1] + exps[2] + exps[3]

    onehot = jnp.zeros((tm, LANE), F32)
    for idx in idxs:
        onehot = onehot + (lane == idx).astype(F32)
    r = lax.broadcasted_iota(jnp.int32, (tm, tm), 0)
    c = lax.broadcasted_iota(jnp.int32, (tm, tm), 1)
    prefix = jnp.dot((c < r).astype(BF16), onehot.astype(BF16), preferred_element_type=F32)
    groups = jnp.ceil(jnp.sum(onehot, axis=0, keepdims=True) * (1.0 / GROUP_ALIGN))
    er = lax.broadcasted_iota(jnp.int32, (LANE, LANE), 0)
    ec = lax.broadcasted_iota(jnp.int32, (LANE, LANE), 1)
    before = jnp.dot(jnp.broadcast_to(groups, (8, LANE)).astype(BF16), (er < ec).astype(BF16),
                     preferred_element_type=F32)[0:1, :]
    slot_e = before * GROUP_ALIGN + prefix

    route = jnp.zeros((tm, LANE), F32)
    for k in range(TOP_K):
        sk = jnp.sum(jnp.where(lane == idxs[k], slot_e, 0.0), axis=1, keepdims=True)
        route = jnp.where(lane == float(k), sk, route)
        route = jnp.where(lane == float(TOP_K + k), exps[k] / tot, route)
    route_ref[...] = route
    routet_ref[...] = route.T[0:8, :]
    g8_ref[...] = jnp.broadcast_to(groups * GROUP_ALIGN, (8, LANE))


def _outproj(ya, yb, ga, gb, x2, g1, sh2, sc2, nm, wa, wb_p, wo, wr_hi, wr_lo, br_p, seq):
    n = x2.shape[0]
    tm = ROUTE_TILE
    per_b = seq // tm
    hw = MLA_HEADS * LANE
    row = lambda i: (i, 0)
    fix = lambda i: (0, 0)
    bsel = lambda i: (i // per_b, 0, 0)
    return pl.pallas_call(
        _outproj_kernel,
        grid=(n // tm,),
        in_specs=[pl.BlockSpec((tm, 512), row),
                  pl.BlockSpec((tm, hw), row),
                  pl.BlockSpec((tm, D_MODEL), row),
                  pl.BlockSpec((tm, D_MODEL), row),
                  pl.BlockSpec((tm, D_MODEL), row),
                  pl.BlockSpec((1, 1, D_MODEL), bsel),
                  pl.BlockSpec((1, 1, D_MODEL), bsel),
                  pl.BlockSpec((1, 1, D_MODEL), bsel),
                  pl.BlockSpec((1, D_MODEL), fix),
                  pl.BlockSpec((512, D_MODEL), fix),
                  pl.BlockSpec((hw, D_MODEL), fix),
                  pl.BlockSpec((D_MODEL, D_MODEL), fix),
                  pl.BlockSpec((D_MODEL, LANE), fix),
                  pl.BlockSpec((D_MODEL, LANE), fix),
                  pl.BlockSpec((1, LANE), fix)],
        out_specs=[pl.BlockSpec((tm, D_MODEL), row),
                   pl.BlockSpec((tm, D_MODEL), row),
                   pl.BlockSpec((tm, LANE), row),
                   pl.BlockSpec((8, tm), lambda i: (0, i)),
                   pl.BlockSpec((8, LANE), row)],
        out_shape=[jax.ShapeDtypeStruct((n, D_MODEL), F32),
                   jax.ShapeDtypeStruct((n, D_MODEL), BF16),
                   jax.ShapeDtypeStruct((n, LANE), F32),
                   jax.ShapeDtypeStruct((8, n), F32),
                   jax.ShapeDtypeStruct((n // tm * 8, LANE), F32)],
        compiler_params=_params(("arbitrary",)),
        name="outproj",
    )(ya, yb, ga, gb, x2, g1, sh2, sc2, nm, wa, wb_p, wo, wr_hi, wr_lo, br_p)


_RUN_BITS = tuple(1 << b for b in range((ROUTE_TILE).bit_length() - 1, GROUP_ALIGN.bit_length() - 2, -1))


def _start_runs(tile, g8_ref, loff_ref, goff_ref, make_copy):
    for e in range(N_EXPERTS):
        g = g8_ref[tile * N_EXPERTS + e]
        lo = loff_ref[tile * N_EXPERTS + e]
        go = goff_ref[tile * N_EXPERTS + e]
        for bit in _RUN_BITS:
            sh = bit.bit_length()
            done = lax.shift_left(lax.shift_right_logical(g, sh), sh)

            def go_copy(lo=lo, go=go, done=done, bit=bit):
                make_copy(pl.multiple_of(lo + done, GROUP_ALIGN), pl.multiple_of(go + done, GROUP_ALIGN), bit).start()

            pl.when((g & bit) != 0)(go_copy)


def _dispatch_kernel(g8_ref, loff_ref, goff_ref, tot_ref, ends_ref, h_ref, rt_ref, xs_ref, buf, zbuf, sem, zsem):
    t = pl.program_id(0)
    nt = pl.num_programs(0)
    tm = h_ref.shape[0]
    slot = lax.rem(t, 2)

    @pl.when(t == 0)
    def _():
        zbuf[...] = jnp.zeros(zbuf.shape, zbuf.dtype)

        def fill(start):
            return pltpu.make_async_copy(zbuf, xs_ref.at[pl.ds(pl.multiple_of(start, MOE_ROWS), MOE_ROWS), :], zsem)

        def nonempty(e):
            return ends_ref[e] > (ends_ref[e - 1] if e else 0)

        used = lax.shift_right_logical(ends_ref[N_EXPERTS - 1], MOE_ROWS.bit_length() - 1)
        n_blocks = xs_ref.shape[0] // MOE_ROWS
        for e in range(N_EXPERTS):
            pl.when(nonempty(e))(lambda e=e: fill(ends_ref[e] - MOE_ROWS).start())
        lax.fori_loop(used, n_blocks, lambda b, c: (fill(b * MOE_ROWS).start(), c)[1], 0)
        for e in range(N_EXPERTS):
            pl.when(nonempty(e))(lambda e=e: fill(ends_ref[e] - MOE_ROWS).wait())
        lax.fori_loop(used, n_blocks, lambda b, c: (fill(b * MOE_ROWS).wait(), c)[1], 0)

    hb = h_ref[...]
    ls = [rt_ref[k:k + 1, :] for k in range(TOP_K)]
    ws = [rt_ref[TOP_K + k:TOP_K + k + 1, :] for k in range(TOP_K)]

    def chunk(ci, carry):
        base = pl.multiple_of(ci * SLOT_CHUNK, SLOT_CHUNK)
        sidx = (lax.broadcasted_iota(jnp.int32, (SLOT_CHUNK, tm), 0) + base).astype(F32)
        own = jnp.zeros((SLOT_CHUNK, tm), F32)
        wsl = jnp.zeros((SLOT_CHUNK, tm), F32)
        for k in range(TOP_K):
            hit = sidx == ls[k]
            own = jnp.where(hit, 1.0, own)
            wsl = jnp.where(hit, ws[k], wsl)
        rows = jnp.dot(own.astype(BF16), hb, preferred_element_type=F32)
        wcol = jnp.sum(wsl, axis=1, keepdims=True)
        buf[slot, pl.ds(base, SLOT_CHUNK), 0:D_MODEL] = rows
        buf[slot, pl.ds(base, SLOT_CHUNK), D_MODEL:XS_COLS] = jnp.broadcast_to(wcol, (SLOT_CHUNK, LANE))
        return carry

    lax.fori_loop(0, SLOTS_PER_TILE // SLOT_CHUNK, chunk, 0)

    def copy(s, lo, go, rows):
        return pltpu.make_async_copy(buf.at[s, pl.ds(lo, rows), :], xs_ref.at[pl.ds(go, rows), :], sem.at[s])

    _start_runs(t, g8_ref, loff_ref, goff_ref, functools.partial(copy, slot))

    def drain(tile, s):
        rows = pl.multiple_of(tot_ref[tile], GROUP_ALIGN)
        copy(s, 0, 0, rows).wait()

    pl.when(t > 0)(lambda: drain(t - 1, 1 - slot))
    pl.when(t == nt - 1)(lambda: drain(t, slot))


def _dispatch(g8f, lofff, gofff, totf, ends, h2, routet, p_rows):
    n = h2.shape[0]
    tm = ROUTE_TILE
    return pl.pallas_call(
        _dispatch_kernel,
        grid_spec=pltpu.PrefetchScalarGridSpec(
            num_scalar_prefetch=5,
            grid=(n // tm,),
            in_specs=[pl.BlockSpec((tm, D_MODEL), lambda i, *_: (i, 0)),
                      pl.BlockSpec((8, tm), lambda i, *_: (0, i))],
            out_specs=pl.BlockSpec(memory_space=pl.ANY),
            scratch_shapes=[pltpu.VMEM((2, SLOTS_PER_TILE, XS_COLS), F32),
                            pltpu.VMEM((MOE_ROWS, XS_COLS), F32),
                            pltpu.SemaphoreType.DMA((2,)), pltpu.SemaphoreType.DMA(())]),
        out_shape=jax.ShapeDtypeStruct((p_rows, XS_COLS), F32),
        compiler_params=_params(("arbitrary",)),
        name="dispatch",
    )(g8f, lofff, gofff, totf, ends, h2, routet)


def _experts_kernel(be_ref, nu_ref, xs_ref, w1_ref, b1_ref, w2_ref, b2_ref, ys_ref):
    del be_ref

    @pl.when(pl.program_id(0) < nu_ref[0])
    def _():
        xb = xs_ref[:, 0:D_MODEL].astype(BF16)
        wcol = xs_ref[:, D_MODEL:D_MODEL + 1]
        hcat = jnp.dot(xb, w1_ref[0].astype(BF16), preferred_element_type=F32) + b1_ref[0]
        x_glu = jnp.minimum(hcat[:, :D_EXPERT], SWIGLU_LIMIT)
        x_lin = jnp.clip(hcat[:, D_EXPERT:], -SWIGLU_LIMIT, SWIGLU_LIMIT)
        act = (x_glu * jax.nn.sigmoid(SWIGLU_ALPHA * x_glu) * (x_lin + 1.0)).astype(BF16)
        ys_ref[...] = (jnp.dot(act, w2_ref[0].astype(BF16), preferred_element_type=F32) + b2_ref[0]) * wcol

    @pl.when(pl.program_id(0) >= nu_ref[0])
    def _():
        ys_ref[...] = jnp.zeros(ys_ref.shape, F32)


def _experts(block_e, n_used, xs, w1, b1, w2, b2):
    p = xs.shape[0]
    mb = MOE_ROWS
    last = lambda i, nu: jnp.minimum(i, nu[0] - 1)
    rowmap = lambda i, be, nu: (last(i, nu), 0)
    wmap = lambda i, be, nu: (be[last(i, nu)], 0, 0)
    return pl.pallas_call(
        _experts_kernel,
        grid_spec=pltpu.PrefetchScalarGridSpec(
            num_scalar_prefetch=2,
            grid=(p // mb,),
            in_specs=[pl.BlockSpec((mb, XS_COLS), rowmap),
                      pl.BlockSpec((1, D_MODEL, 2 * D_EXPERT), wmap),
                      pl.BlockSpec((1, 1, 2 * D_EXPERT), wmap),
                      pl.BlockSpec((1, D_EXPERT, D_MODEL), wmap),
                      pl.BlockSpec((1, 1, D_MODEL), wmap)],
            out_specs=pl.BlockSpec((mb, D_MODEL), lambda i, be, nu: (i, 0))),
        out_shape=jax.ShapeDtypeStruct((p, D_MODEL), F32),
        compiler_params=_params(("arbitrary",)),
        name="experts",
    )(block_e, n_used, xs, w1, b1, w2, b2)


def _combine_kernel(g8_ref, loff_ref, goff_ref, tot_ref, ys_ref, route_ref, x1_ref, g2_ref, fn_ref, o_ref, stg, sem):
    t = pl.program_id(0)
    nt = pl.num_programs(0)
    tm = x1_ref.shape[0]
    slot = lax.rem(t, 2)

    def copy(s, lo, go, rows):
        return pltpu.make_async_copy(ys_ref.at[pl.ds(go, rows), :], stg.at[s, pl.ds(lo, rows), :], sem.at[s])

    @pl.when(t == 0)
    def _():
        stg[...] = jnp.zeros(stg.shape, F32)
        _start_runs(t, g8_ref, loff_ref, goff_ref, functools.partial(copy, 0))

    pl.when(t + 1 < nt)(lambda: _start_runs(t + 1, g8_ref, loff_ref, goff_ref, functools.partial(copy, 1 - slot)))
    copy(slot, 0, 0, pl.multiple_of(tot_ref[t], GROUP_ALIGN)).wait()

    route = route_ref[...]
    ls = [route[:, k:k + 1] for k in range(TOP_K)]

    def chunk(ci, y):
        base = pl.multiple_of(ci * SLOT_CHUNK, SLOT_CHUNK)
        sidx = (lax.broadcasted_iota(jnp.int32, (tm, SLOT_CHUNK), 1) + base).astype(F32)
        own = jnp.zeros((tm, SLOT_CHUNK), F32)
        for k in range(TOP_K):
            own = jnp.where(sidx == ls[k], 1.0, own)
        rows = stg[slot, pl.ds(base, SLOT_CHUNK), :].astype(BF16)
        return y + jnp.dot(own.astype(BF16), rows, preferred_element_type=F32)

    y = lax.fori_loop(0, SLOTS_PER_TILE // SLOT_CHUNK, chunk, jnp.zeros((tm, D_MODEL), F32))
    x = x1_ref[...] + g2_ref[0] * y
    o_ref[...] = _rms(x) * fn_ref[...]


def _combine(g8f, lofff, gofff, totf, ys, route, x1, g2, fn, seq):
    n = x1.shape[0]
    tm = ROUTE_TILE
    per_b = seq // tm
    return pl.pallas_call(
        _combine_kernel,
        grid_spec=pltpu.PrefetchScalarGridSpec(
            num_scalar_prefetch=4,
            grid=(n // tm,),
            in_specs=[pl.BlockSpec(memory_space=pl.ANY),
                      pl.BlockSpec((tm, LANE), lambda i, *_: (i, 0)),
                      pl.BlockSpec((tm, D_MODEL), lambda i, *_: (i, 0)),
                      pl.BlockSpec((1, 1, D_MODEL), lambda i, *_: (i // per_b, 0, 0)),
                      pl.BlockSpec((1, D_MODEL), lambda i, *_: (0, 0))],
            out_specs=pl.BlockSpec((tm, D_MODEL), lambda i, *_: (i, 0)),
            scratch_shapes=[pltpu.VMEM((2, SLOTS_PER_TILE, D_MODEL), F32), pltpu.SemaphoreType.DMA((2,))]),
        out_shape=jax.ShapeDtypeStruct((n, D_MODEL), F32),
        compiler_params=_params(("arbitrary",)),
        name="combine",
    )(g8f, lofff, gofff, totf, ys, route, x1, g2, fn)


def _pack_w_in(w_in):
    sq, skv = 512, 128
    o_ka, o_va, o_cq = sq, sq + skv, sq + 2 * skv
    o_ckv = o_cq + MLA_Q_RANK
    o_kr = o_ckv + MLA_KV_RANK
    o_ga = o_kr + MLA_ROPE_DIM
    o_gb = o_ga + D_MODEL
    z = lambda w: jnp.zeros((D_MODEL, w), w_in.dtype)
    ka, va = w_in[:, o_ka:o_va], w_in[:, o_va:o_cq]
    dup = lambda t: jnp.concatenate([t[:, :64], t[:, :64], t[:, 64:], t[:, 64:]], axis=1)
    kr1, kr2 = w_in[:, o_kr:o_kr + HALF_ROPE], w_in[:, o_kr + HALF_ROPE:o_ga]
    packed = jnp.concatenate([
        w_in[:, :sq], dup(ka), dup(va), w_in[:, o_cq:o_ckv], w_in[:, o_ckv:o_kr],
        z(64), kr1, kr2, z(32),
        z(64), kr2, kr1, z(32),
        w_in[:, o_ga:o_gb], w_in[:, o_gb:o_gb + D_MODEL]], axis=1)
    return packed.astype(BF16)


def _pack_heads(w, width, pieces):
    rows = w.shape[0]
    w3 = w.reshape(rows, MLA_HEADS, width)
    cols = [w3[:, :, a:b] for a, b in pieces]
    used = sum(b - a for a, b in pieces)
    cols.append(jnp.zeros((rows, MLA_HEADS, LANE - used), w.dtype))
    return jnp.concatenate(cols, axis=2).reshape(rows, MLA_HEADS * LANE)


def kernel(x, c, positions, w_ada, b_ada, norm_mix, norm_ffn, w_in, sinks, q_norm, kv_norm, w_uq, w_uk, w_uv,
           w_branch_a, w_branch_b, w_out, w_router, b_router, w_moe1, b_moe1, w_moe2, b_moe2, final_norm):
    batch, seq, d = x.shape
    n = batch * seq
    assert d == D_MODEL and w_ada.shape[0] == 1
    assert seq % TQ_MLA == 0 and seq % (SWA_QB * ATTN_BLOCK) == 0 and seq % TM_PROJ == 0 and seq % ROUTE_TILE == 0
    hw = MLA_HEADS * LANE
    qk = MLA_NOPE_DIM + MLA_ROPE_DIM

    w_in_p = _pack_w_in(w_in[0])
    wuq_main = _pack_heads(w_uq[0], qk, [(0, qk)])
    w3 = w_uq[0].reshape(MLA_Q_RANK, MLA_HEADS, qk)
    wuq_swap = jnp.concatenate([
        jnp.zeros((MLA_Q_RANK, MLA_HEADS, MLA_NOPE_DIM), F32),
        w3[:, :, MLA_NOPE_DIM + HALF_ROPE:], w3[:, :, MLA_NOPE_DIM:MLA_NOPE_DIM + HALF_ROPE],
        jnp.zeros((MLA_Q_RANK, MLA_HEADS, LANE - qk), F32)], axis=2).reshape(MLA_Q_RANK, hw)
    wuq_p = jnp.concatenate([wuq_main, wuq_swap], axis=1).astype(BF16)
    wuk_p = _pack_heads(w_uk[0], MLA_NOPE_DIM, [(0, MLA_NOPE_DIM)]).astype(BF16)
    wuv_p = _pack_heads(w_uv[0], MLA_V_DIM, [(0, MLA_V_DIM)]).astype(BF16)
    one_row = jnp.tile((jnp.arange(LANE) == MLA_V_DIM).astype(F32), MLA_HEADS)[None, :]
    freqs = ROPE_THETA ** (-jnp.arange(0, MLA_ROPE_DIM, 2, dtype=F32) / MLA_ROPE_DIM)
    frq = jnp.concatenate([jnp.zeros((MLA_NOPE_DIM,), F32), freqs, freqs,
                           jnp.zeros((LANE - qk,), F32)])[None, :]
    sgn = jnp.concatenate([jnp.zeros((MLA_NOPE_DIM,), F32), -jnp.ones((HALF_ROPE,), F32),
                           jnp.ones((HALF_ROPE,), F32), jnp.zeros((LANE - qk,), F32)])[None, :]
    wb3 = w_branch_b[0].reshape(MLA_HEADS, MLA_V_DIM, D_MODEL)
    wb_p = jnp.concatenate([wb3, jnp.zeros((MLA_HEADS, LANE - MLA_V_DIM, D_MODEL), F32)],
                           axis=1).reshape(hw, D_MODEL).astype(BF16)
    wa = w_branch_a[0].astype(BF16)
    wo = w_out[0].astype(BF16)
    wr = jnp.pad(w_router[0], ((0, 0), (0, LANE - N_EXPERTS)))
    wr_hi = wr.astype(BF16)
    wr_lo = (wr - wr_hi.astype(F32)).astype(BF16)
    br_p = jnp.pad(b_router[0], (0, LANE - N_EXPERTS))[None, :]
    b1 = b_moe1[0][:, None, :]
    b2 = b_moe2[0][:, None, :]
    slopes = jnp.asarray(np.exp2(-8.0 * np.arange(1, SWA_HEADS + 1) / SWA_HEADS), dtype=F32)

    c8 = jnp.pad(c, ((0, 8 - batch), (0, 0)))
    mod = _ada(c8, w_ada[0], b_ada[0][None, :])[:batch]
    sh1, sc1, g1, sh2, sc2, g2 = [m[:, None, :] for m in jnp.split(mod, 6, axis=-1)]

    x2 = x.reshape(n, D_MODEL)
    pos2 = positions.reshape(n, 1).astype(jnp.int32)
    qa, ka2, va2, qm, km, vm, ga, gb = _inproj(
        x2, pos2, sh1, sc1, norm_mix, w_in_p, q_norm, kv_norm, wuq_p, wuk_p, wuv_p, frq, sgn, one_row, seq)
    yb = _mla(qm, km, vm, batch, seq)
    ya = _swa(sinks[0], slopes, qa, ka2, va2, batch, seq)

    x1, h2, route, routet, g8 = _outproj(
        ya, yb, ga, gb, x2, g1, sh2, sc2, norm_ffn, wa, wb_p, wo, wr_hi, wr_lo, br_p, seq)

    n_tiles = n // ROUTE_TILE
    g8t = g8.reshape(n_tiles, 8, LANE)[:, 0, :N_EXPERTS].astype(jnp.int32)
    loff = jnp.cumsum(g8t, axis=1) - g8t
    padded = ((jnp.sum(g8t, axis=0) + MOE_ROWS - 1) // MOE_ROWS) * MOE_ROWS
    pends = jnp.cumsum(padded)
    goff = (pends - padded)[None, :] + jnp.cumsum(g8t, axis=0) - g8t
    tot = jnp.sum(g8t, axis=1).astype(jnp.int32)
    p_rows = n * TOP_K + n_tiles * N_EXPERTS * (GROUP_ALIGN - 1) + N_EXPERTS * (MOE_ROWS - 1)
    p_rows = -(-p_rows // MOE_ROWS) * MOE_ROWS
    n_blocks = p_rows // MOE_ROWS
    block_start = jnp.arange(n_blocks, dtype=jnp.int32) * MOE_ROWS
    block_e = jnp.minimum(jnp.sum((pends[None, :] <= block_start[:, None]).astype(jnp.int32), axis=1),
                          N_EXPERTS - 1).astype(jnp.int32)
    n_used = (pends[-1:] // MOE_ROWS).astype(jnp.int32)
    tabs = (g8t.reshape(-1), loff.reshape(-1).astype(jnp.int32), goff.reshape(-1).astype(jnp.int32), tot)

    xs = _dispatch(*tabs, pends.astype(jnp.int32), h2, routet, p_rows)
    ys = _experts(block_e, n_used, xs, w_moe1[0], b1, w_moe2[0], b2)
    out = _combine(*tabs, ys, route, x1, g2, final_norm[None, :], seq)
    return out.reshape(batch, seq, D_MODEL)
```

```python
import functools

import numpy as np
import jax
import jax.numpy as jnp
from jax import lax
from jax.experimental import pallas as pl
from jax.experimental.pallas import tpu as pltpu

D_MODEL = 1024
SWA_HEADS = 8
SWA_KV_HEADS = 2
SWA_HEAD_DIM = 64
ATTN_BLOCK = 128
MLA_HEADS = 8
MLA_Q_RANK = 384
MLA_KV_RANK = 256
MLA_NOPE_DIM = 64
MLA_ROPE_DIM = 32
MLA_V_DIM = 64
ROPE_THETA = 10000.0
N_EXPERTS = 32
TOP_K = 4
D_EXPERT = 1024
SWIGLU_LIMIT = 7.0
SWIGLU_ALPHA = 1.702
NORM_EPS = 1e-6

LANE = 128
LOG2E = 1.4426950408889634
NEG = -1e30
SWA_QSCALE = SWA_HEAD_DIM ** -0.5 * LOG2E
MLA_QSCALE = (MLA_NOPE_DIM + MLA_ROPE_DIM) ** -0.5 * LOG2E
HALF_ROPE = MLA_ROPE_DIM // 2
MLA_PLAIN_BOUND = 60.0
MLA_WIDE = 4

BF16 = jnp.bfloat16
F32 = jnp.float32

TM_PROJ = 512
TQ_MLA = 512
SWA_QB = 4
MOE_ROWS = 512
ROUTE_TILE = 512
GROUP_ALIGN = 8
SLOTS_PER_TILE = ROUTE_TILE * TOP_K + N_EXPERTS * GROUP_ALIGN
SLOT_CHUNK = 768
XS_COLS = D_MODEL + LANE
VMEM_LIMIT = 56 * 1024 * 1024


def _params(sem, vmem=VMEM_LIMIT):
    return pltpu.CompilerParams(dimension_semantics=sem, vmem_limit_bytes=vmem)


def _rms(x):
    return x * lax.rsqrt(jnp.mean(x * x, axis=-1, keepdims=True) + NORM_EPS)


def _ada_kernel(c_ref, w_ref, b_ref, o_ref):
    c = c_ref[...]
    a = (c * jax.nn.sigmoid(c)).astype(BF16)
    o_ref[...] = jnp.dot(a, w_ref[...].astype(BF16), preferred_element_type=F32) + b_ref[...]


def _ada(c8, w_ada, b_ada):
    n_out = w_ada.shape[1]
    return pl.pallas_call(
        _ada_kernel,
        grid=(n_out // D_MODEL,),
        in_specs=[pl.BlockSpec((8, D_MODEL), lambda j: (0, 0)),
                  pl.BlockSpec((D_MODEL, D_MODEL), lambda j: (0, j)),
                  pl.BlockSpec((1, D_MODEL), lambda j: (0, j))],
        out_specs=pl.BlockSpec((8, D_MODEL), lambda j: (0, j)),
        out_shape=jax.ShapeDtypeStruct((8, n_out), F32),
        compiler_params=_params(("arbitrary",)),
        name="ada",
    )(c8, w_ada, b_ada)


_C_QA = (0, 512)
_C_KA = (512, 768)
_C_VA = (768, 1024)
_C_CQ = (1024, 1408)
_C_CKV = (1408, 1664)
_C_KRA = (1664, 1792)
_C_KRB = (1792, 1920)
_C_GA = (1920, 2944)
_C_GB = (2944, 3968)
_W_IN_COLS = 3968


def _inproj_kernel(x_ref, pos_ref, sh_ref, sc_ref, nm_ref, win_ref, qn_ref, kvn_ref, wuq_ref, wuk_ref,
                   wuv_ref, frq_ref, sgn_ref, one_ref,
                   qa_ref, ka_ref, va_ref, qm_ref, km_ref, vm_ref, ga_ref, gb_ref):
    x = x_ref[...]
    h = (_rms(x) * nm_ref[...] * (1.0 + sc_ref[0]) + sh_ref[0]).astype(BF16)

    def proj(c):
        return jnp.dot(h, win_ref[:, c[0]:c[1]], preferred_element_type=F32)

    qa_ref[...] = (proj(_C_QA) * SWA_QSCALE).astype(BF16)
    ka_ref[...] = proj(_C_KA).astype(BF16)
    va_ref[...] = proj(_C_VA).astype(BF16)
    ga_ref[...] = jax.nn.sigmoid(proj(_C_GA)).astype(BF16)
    gb_ref[...] = jax.nn.sigmoid(proj(_C_GB)).astype(BF16)

    ang = pos_ref[...].astype(F32) * frq_ref[...]
    cs = jnp.cos(ang)
    sn = jnp.sin(ang) * sgn_ref[...]

    cqn = (_rms(proj(_C_CQ)) * qn_ref[...]).astype(BF16)
    q2 = jnp.dot(cqn, wuq_ref[...], preferred_element_type=F32)
    hw = MLA_HEADS * LANE
    for hh in range(MLA_HEADS):
        sl = slice(hh * LANE, (hh + 1) * LANE)
        sw = slice(hw + hh * LANE, hw + (hh + 1) * LANE)
        qm_ref[:, sl] = ((q2[:, sl] * cs + q2[:, sw] * sn) * MLA_QSCALE).astype(BF16)

    ckvn = (_rms(proj(_C_CKV)) * kvn_ref[...]).astype(BF16)
    krr = proj(_C_KRA) * cs + proj(_C_KRB) * sn
    kn = jnp.dot(ckvn, wuk_ref[...], preferred_element_type=F32)
    for hh in range(MLA_HEADS):
        sl = slice(hh * LANE, (hh + 1) * LANE)
        km_ref[:, sl] = (kn[:, sl] + krr).astype(BF16)
    vm_ref[...] = (jnp.dot(ckvn, wuv_ref[...], preferred_element_type=F32) + one_ref[...]).astype(BF16)


def _inproj(x2, pos2, sh1, sc1, nm, w_in_p, qn, kvn, wuq_p, wuk_p, wuv_p, frq, sgn, one_row, seq):
    n = x2.shape[0]
    tm = TM_PROJ
    per_b = seq // tm
    hw = MLA_HEADS * LANE
    row = lambda i: (i, 0)
    fix = lambda i: (0, 0)
    bsel = lambda i: (i // per_b, 0, 0)
    widths = (512, 256, 256, hw, hw, hw, D_MODEL, D_MODEL)
    return pl.pallas_call(
        _inproj_kernel,
        grid=(n // tm,),
        in_specs=[pl.BlockSpec((tm, D_MODEL), row),
                  pl.BlockSpec((tm, 1), row),
                  pl.BlockSpec((1, 1, D_MODEL), bsel),
                  pl.BlockSpec((1, 1, D_MODEL), bsel),
                  pl.BlockSpec((1, D_MODEL), fix),
                  pl.BlockSpec((D_MODEL, _W_IN_COLS), fix),
                  pl.BlockSpec((1, MLA_Q_RANK), fix),
                  pl.BlockSpec((1, MLA_KV_RANK), fix),
                  pl.BlockSpec((MLA_Q_RANK, 2 * hw), fix),
                  pl.BlockSpec((MLA_KV_RANK, hw), fix),
                  pl.BlockSpec((MLA_KV_RANK, hw), fix),
                  pl.BlockSpec((1, LANE), fix),
                  pl.BlockSpec((1, LANE), fix),
                  pl.BlockSpec((1, hw), fix)],
        out_specs=[pl.BlockSpec((tm, w), row) for w in widths],
        out_shape=[jax.ShapeDtypeStruct((n, w), BF16) for w in widths],
        compiler_params=_params(("arbitrary",)),
        name="inproj",
    )(x2, pos2, sh1, sc1, nm, w_in_p, qn, kvn, wuq_p, wuk_p, wuv_p, frq, sgn, one_row)


def _mla_kernel(q_ref, k_ref, v_ref, o_ref, m_scr, acc_scr, kn_scr):
    i = pl.program_id(2)
    t = TQ_MLA
    acc_scr[...] = jnp.zeros(acc_scr.shape, F32)

    @pl.when(i == 0)
    def _():
        kk = k_ref[...].astype(F32)
        kn_scr[...] = jnp.max(jnp.sum(kk * kk, axis=1, keepdims=True), axis=0, keepdims=True)

    def scores(j, nblk, diag_at):
        start = pl.multiple_of(j * t, t)
        k = k_ref[pl.ds(start, nblk * t), :]
        v = v_ref[pl.ds(start, nblk * t), :]
        s = lax.dot_general(q_ref[...], k, (((1,), (1,)), ((), ())), preferred_element_type=F32)
        if diag_at is not None:
            qi = lax.broadcasted_iota(jnp.int32, s.shape, 0)
            kj = lax.broadcasted_iota(jnp.int32, s.shape, 1)
            s = jnp.where(kj <= qi + diag_at * t, s, NEG)
        return s, v

    def plain_step(j, nblk, diag_at):
        s, v = scores(j, nblk, diag_at)
        acc_scr[...] += jnp.dot(jnp.exp2(s).astype(BF16), v, preferred_element_type=F32)

    def online_step(j, nblk, diag_at):
        s, v = scores(j, nblk, diag_at)
        m_old = m_scr[...]
        m_new = jnp.maximum(m_old, jnp.max(s, axis=1, keepdims=True))
        p = jnp.exp2(s - m_new)
        alpha = jnp.exp2(m_old - m_new)
        acc_scr[...] = acc_scr[...] * alpha + jnp.dot(p.astype(BF16), v, preferred_element_type=F32)
        m_scr[...] = m_new

    def run(step, wide):
        nw = lax.shift_right_logical(i, wide.bit_length() - 1)
        left = i - nw * wide

        def wide_body(jj, carry):
            step(jj * wide, wide, None)
            return carry

        lax.fori_loop(0, nw, wide_body, 0)
        for r in range(wide):
            pl.when(left == r)(lambda r=r: step(nw * wide, r + 1, r))

    qf = q_ref[...].astype(F32)
    qn = jnp.max(jnp.sum(qf * qf, axis=1, keepdims=True), axis=0, keepdims=True)
    small = jnp.max(qn * kn_scr[...]) <= MLA_PLAIN_BOUND ** 2

    @pl.when(small)
    def _():
        run(plain_step, MLA_WIDE)

    @pl.when(jnp.logical_not(small))
    def _():
        m_scr[...] = jnp.full(m_scr.shape, NEG, F32)
        run(online_step, 1)

    acc = acc_scr[...]
    o_ref[...] = (acc / acc[:, MLA_V_DIM:MLA_V_DIM + 1]).astype(BF16)


def _mla(qm, km, vm, batch, seq):
    n = qm.shape[0]
    t = TQ_MLA
    nq = seq // t
    qmap = lambda b, h, i: (b * nq + i, h)
    kmap = lambda b, h, i: (b, h)
    return pl.pallas_call(
        _mla_kernel,
        grid=(batch, MLA_HEADS, nq),
        in_specs=[pl.BlockSpec((t, LANE), qmap),
                  pl.BlockSpec((seq, LANE), kmap),
                  pl.BlockSpec((seq, LANE), kmap)],
        out_specs=pl.BlockSpec((t, LANE), qmap),
        out_shape=jax.ShapeDtypeStruct((n, MLA_HEADS * LANE), BF16),
        scratch_shapes=[pltpu.VMEM((t, 1), F32), pltpu.VMEM((t, LANE), F32), pltpu.VMEM((1, 1), F32)],
        compiler_params=_params(("arbitrary", "arbitrary", "arbitrary")),
        name="mla",
    )(qm, km, vm)


def _swa_kernel(sink_ref, slope_ref, q_ref, kc_ref, kp_ref, vc_ref, vp_ref, o_ref, bias_scr):
    b = pl.program_id(0)
    i = pl.program_id(1)
    blk = ATTN_BLOCK

    @pl.when((b == 0) & (i == 0))
    def _():
        qi = lax.broadcasted_iota(jnp.int32, (blk, 2 * blk), 0)
        kj = lax.broadcasted_iota(jnp.int32, (blk, 2 * blk), 1)
        dist = qi - kj + blk
        valid = (dist >= 0) & (dist < blk)
        distf = dist.astype(F32)
        for hd in range(SWA_HEADS):
            bias_scr[hd] = jnp.where(valid, -(slope_ref[hd] * LOG2E) * distf, NEG)

    lane = lax.broadcasted_iota(jnp.int32, (2 * blk, LANE), 1)
    lo = lane < SWA_HEAD_DIM
    kcol = lax.broadcasted_iota(jnp.int32, (blk, 2 * blk), 1)
    first_pen = jnp.where(kcol < blk, jnp.where(i == 0, NEG, 0.0), 0.0)
    zero = jnp.zeros((2 * blk, LANE), BF16)

    for qb in range(SWA_QB):
        if qb == 0:
            kprev, vprev = kp_ref[...], vp_ref[...]
        else:
            kprev = kc_ref[(qb - 1) * blk:qb * blk, :]
            vprev = vc_ref[(qb - 1) * blk:qb * blk, :]
        kcat = jnp.concatenate([kprev, kc_ref[qb * blk:(qb + 1) * blk, :]], axis=0)
        vcat = jnp.concatenate([vprev, vc_ref[qb * blk:(qb + 1) * blk, :]], axis=0)
        for g in range(SWA_KV_HEADS):
            kg = kcat[:, g * LANE:(g + 1) * LANE]
            vg = vcat[:, g * LANE:(g + 1) * LANE]
            halves = ((jnp.where(lo, kg, zero), jnp.where(lo, vg, zero)),
                      (jnp.where(lo, zero, kg), jnp.where(lo, zero, vg)))
            for pp in range(2):
                pr = g * 2 + pp
                qpair = q_ref[qb * blk:(qb + 1) * blk, pr * LANE:(pr + 1) * LANE]
                out = None
                for half in range(2):
                    hd = pr * 2 + half
                    kx, vx = halves[half]
                    s = lax.dot_general(qpair, kx, (((1,), (1,)), ((), ())), preferred_element_type=F32)
                    s = s + bias_scr[hd]
                    if qb == 0:
                        s = s + first_pen
                    sink = sink_ref[hd] * LOG2E
                    m = jnp.maximum(jnp.max(s, axis=1, keepdims=True), sink)
                    p = jnp.exp2(s - m)
                    denom = jnp.sum(p, axis=1, keepdims=True) + jnp.exp2(sink - m)
                    o = jnp.dot(p.astype(BF16), vx, preferred_element_type=F32) * (1.0 / denom)
                    out = o if out is None else out + o
                o_ref[qb * blk:(qb + 1) * blk, pr * LANE:(pr + 1) * LANE] = out.astype(BF16)


def _swa(sinks, slopes, qa, ka2, va2, batch, seq):
    n = qa.shape[0]
    blk = ATTN_BLOCK
    rows = SWA_QB * blk
    steps = seq // rows
    cur = lambda b, i: (b * steps + i, 0)
    prev = lambda b, i: (b * (seq // blk) + jnp.maximum(i * SWA_QB - 1, 0), 0)
    smem = pl.BlockSpec(memory_space=pltpu.SMEM)
    return pl.pallas_call(
        _swa_kernel,
        grid=(batch, steps),
        in_specs=[smem, smem,
                  pl.BlockSpec((rows, 512), cur),
                  pl.BlockSpec((rows, 256), cur),
                  pl.BlockSpec((blk, 256), prev),
                  pl.BlockSpec((rows, 256), cur),
                  pl.BlockSpec((blk, 256), prev)],
        out_specs=pl.BlockSpec((rows, 512), cur),
        out_shape=jax.ShapeDtypeStruct((n, 512), BF16),
        scratch_shapes=[pltpu.VMEM((SWA_HEADS, blk, 2 * blk), F32)],
        compiler_params=_params(("arbitrary", "arbitrary")),
        name="swa",
    )(sinks, slopes, qa, ka2, ka2, va2, va2)


def _outproj_kernel(ya_ref, yb_ref, ga_ref, gb_ref, x_ref, g1_ref, sh_ref, sc_ref, nm_ref, wa_ref, wb_ref,
                    wo_ref, wrh_ref, wrl_ref, br_ref,
                    x1_ref, h2_ref, route_ref, routet_ref, g8_ref):
    tm = x_ref.shape[0]
    a = jnp.dot(ya_ref[...], wa_ref[...], preferred_element_type=F32)
    bm = jnp.dot(yb_ref[...], wb_ref[...], preferred_element_type=F32)
    mixed = (ga_ref[...].astype(F32) * a + gb_ref[...].astype(F32) * bm).astype(BF16)
    x1 = x_ref[...] + g1_ref[0] * jnp.dot(mixed, wo_ref[...], preferred_element_type=F32)
    x1_ref[...] = x1
    h2 = _rms(x1) * nm_ref[...] * (1.0 + sc_ref[0]) + sh_ref[0]
    hi = h2.astype(BF16)
    h2_ref[...] = hi

    lo = (h2 - hi.astype(F32)).astype(BF16)
    logits = (jnp.dot(hi, wrh_ref[...], preferred_element_type=F32)
              + jnp.dot(lo, wrh_ref[...], preferred_element_type=F32)
              + jnp.dot(hi, wrl_ref[...], preferred_element_type=F32)) + br_ref[...]

    lane = lax.broadcasted_iota(jnp.int32, (tm, LANE), 1).astype(F32)
    work = jnp.where(lane < N_EXPERTS, logits, -jnp.inf)
    vals, idxs = [], []
    for _ in range(TOP_K):
        m = jnp.max(work, axis=1, keepdims=True)
        idx = jnp.min(jnp.where(work == m, lane, float(LANE)), axis=1, keepdims=True)
        vals.append(m)
        idxs.append(idx)
        work = jnp.where(lane == idx, -jnp.inf, work)
    exps = [jnp.exp(v - vals[0]) for v in vals]
    tot = exps[0] + exps[1] + exps[2] + exps[3]

    onehot = jnp.zeros((tm, LANE), F32)
    for idx in idxs:
        onehot = onehot + (lane == idx).astype(F32)
    r = lax.broadcasted_iota(jnp.int32, (tm, tm), 0)
    c = lax.broadcasted_iota(jnp.int32, (tm, tm), 1)
    prefix = jnp.dot((c < r).astype(BF16), onehot.astype(BF16), preferred_element_type=F32)
    groups = jnp.ceil(jnp.sum(onehot, axis=0, keepdims=True) * (1.0 / GROUP_ALIGN))
    er = lax.broadcasted_iota(jnp.int32, (LANE, LANE), 0)
    ec = lax.broadcasted_iota(jnp.int32, (LANE, LANE), 1)
    before = jnp.dot(jnp.broadcast_to(groups, (8, LANE)).astype(BF16), (er < ec).astype(BF16),
                     preferred_element_type=F32)[0:1, :]
    slot_e = before * GROUP_ALIGN + prefix

    route = jnp.zeros((tm, LANE), F32)
    for k in range(TOP_K):
        sk = jnp.sum(jnp.where(lane == idxs[k], slot_e, 0.0), axis=1, keepdims=True)
        route = jnp.where(lane == float(k), sk, route)
        route = jnp.where(lane == float(TOP_K + k), exps[k] / tot, route)
    route_ref[...] = route
    routet_ref[...] = route.T[0:8, :]
    g8_ref[...] = jnp.broadcast_to(groups * GROUP_ALIGN, (8, LANE))


def _outproj(ya, yb, ga, gb, x2, g1, sh2, sc2, nm, wa, wb_p, wo, wr_hi, wr_lo, br_p, seq):
    n = x2.shape[0]
    tm = ROUTE_TILE
    per_b = seq // tm
    hw = MLA_HEADS * LANE
    row = lambda i: (i, 0)
    fix = lambda i: (0, 0)
    bsel = lambda i: (i // per_b, 0, 0)
    return pl.pallas_call(
        _outproj_kernel,
        grid=(n // tm,),
        in_specs=[pl.BlockSpec((tm, 512), row),
                  pl.BlockSpec((tm, hw), row),
                  pl.BlockSpec((tm, D_MODEL), row),
                  pl.BlockSpec((tm, D_MODEL), row),
                  pl.BlockSpec((tm, D_MODEL), row),
                  pl.BlockSpec((1, 1, D_MODEL), bsel),
                  pl.BlockSpec((1, 1, D_MODEL), bsel),
                  pl.BlockSpec((1, 1, D_MODEL), bsel),
                  pl.BlockSpec((1, D_MODEL), fix),
                  pl.BlockSpec((512, D_MODEL), fix),
                  pl.BlockSpec((hw, D_MODEL), fix),
                  pl.BlockSpec((D_MODEL, D_MODEL), fix),
                  pl.BlockSpec((D_MODEL, LANE), fix),
                  pl.BlockSpec((D_MODEL, LANE), fix),
                  pl.BlockSpec((1, LANE), fix)],
        out_specs=[pl.BlockSpec((tm, D_MODEL), row),
                   pl.BlockSpec((tm, D_MODEL), row),
                   pl.BlockSpec((tm, LANE), row),
                   pl.BlockSpec((8, tm), lambda i: (0, i)),
                   pl.BlockSpec((8, LANE), row)],
        out_shape=[jax.ShapeDtypeStruct((n, D_MODEL), F32),
                   jax.ShapeDtypeStruct((n, D_MODEL), BF16),
                   jax.ShapeDtypeStruct((n, LANE), F32),
                   jax.ShapeDtypeStruct((8, n), F32),
                   jax.ShapeDtypeStruct((n // tm * 8, LANE), F32)],
        compiler_params=_params(("arbitrary",)),
        name="outproj",
    )(ya, yb, ga, gb, x2, g1, sh2, sc2, nm, wa, wb_p, wo, wr_hi, wr_lo, br_p)


_RUN_BITS = tuple(1 << b for b in range((ROUTE_TILE).bit_length() - 1, GROUP_ALIGN.bit_length() - 2, -1))


def _start_runs(tile, g8_ref, loff_ref, goff_ref, make_copy):
    for e in range(N_EXPERTS):
        g = g8_ref[tile * N_EXPERTS + e]
        lo = loff_ref[tile * N_EXPERTS + e]
        go = goff_ref[tile * N_EXPERTS + e]
        for bit in _RUN_BITS:
            sh = bit.bit_length()
            done = lax.shift_left(lax.shift_right_logical(g, sh), sh)

            def go_copy(lo=lo, go=go, done=done, bit=bit):
                make_copy(pl.multiple_of(lo + done, GROUP_ALIGN), pl.multiple_of(go + done, GROUP_ALIGN), bit).start()

            pl.when((g & bit) != 0)(go_copy)


def _dispatch_kernel(g8_ref, loff_ref, goff_ref, tot_ref, ends_ref, h_ref, rt_ref, xs_ref, buf, zbuf, sem, zsem):
    t = pl.program_id(0)
    nt = pl.num_programs(0)
    tm = h_ref.shape[0]
    slot = lax.rem(t, 2)

    @pl.when(t == 0)
    def _():
        zbuf[...] = jnp.zeros(zbuf.shape, zbuf.dtype)

        def fill(start):
            return pltpu.make_async_copy(zbuf, xs_ref.at[pl.ds(pl.multiple_of(start, MOE_ROWS), MOE_ROWS), :], zsem)

        def nonempty(e):
            return ends_ref[e] > (ends_ref[e - 1] if e else 0)

        used = lax.shift_right_logical(ends_ref[N_EXPERTS - 1], MOE_ROWS.bit_length() - 1)
        n_blocks = xs_ref.shape[0] // MOE_ROWS
        for e in range(N_EXPERTS):
            pl.when(nonempty(e))(lambda e=e: fill(ends_ref[e] - MOE_ROWS).start())
        lax.fori_loop(used, n_blocks, lambda b, c: (fill(b * MOE_ROWS).start(), c)[1], 0)
        for e in range(N_EXPERTS):
            pl.when(nonempty(e))(lambda e=e: fill(ends_ref[e] - MOE_ROWS).wait())
        lax.fori_loop(used, n_blocks, lambda b, c: (fill(b * MOE_ROWS).wait(), c)[1], 0)

    hb = h_ref[...]
    ls = [rt_ref[k:k + 1, :] for k in range(TOP_K)]
    ws = [rt_ref[TOP_K + k:TOP_K + k + 1, :] for k in range(TOP_K)]

    def chunk(ci, carry):
        base = pl.multiple_of(ci * SLOT_CHUNK, SLOT_CHUNK)
        sidx = (lax.broadcasted_iota(jnp.int32, (SLOT_CHUNK, tm), 0) + base).astype(F32)
        own = jnp.zeros((SLOT_CHUNK, tm), F32)
        wsl = jnp.zeros((SLOT_CHUNK, tm), F32)
        for k in range(TOP_K):
            hit = sidx == ls[k]
            own = jnp.where(hit, 1.0, own)
            wsl = jnp.where(hit, ws[k], wsl)
        rows = jnp.dot(own.astype(BF16), hb, preferred_element_type=F32)
        wcol = jnp.sum(wsl, axis=1, keepdims=True)
        buf[slot, pl.ds(base, SLOT_CHUNK), 0:D_MODEL] = rows
        buf[slot, pl.ds(base, SLOT_CHUNK), D_MODEL:XS_COLS] = jnp.broadcast_to(wcol, (SLOT_CHUNK, LANE))
        return carry

    lax.fori_loop(0, SLOTS_PER_TILE // SLOT_CHUNK, chunk, 0)

    def copy(s, lo, go, rows):
        return pltpu.make_async_copy(buf.at[s, pl.ds(lo, rows), :], xs_ref.at[pl.ds(go, rows), :], sem.at[s])

    _start_runs(t, g8_ref, loff_ref, goff_ref, functools.partial(copy, slot))

    def drain(tile, s):
        rows = pl.multiple_of(tot_ref[tile], GROUP_ALIGN)
        copy(s, 0, 0, rows).wait()

    pl.when(t > 0)(lambda: drain(t - 1, 1 - slot))
    pl.when(t == nt - 1)(lambda: drain(t, slot))


def _dispatch(g8f, lofff, gofff, totf, ends, h2, routet, p_rows):
    n = h2.shape[0]
    tm = ROUTE_TILE
    return pl.pallas_call(
        _dispatch_kernel,
        grid_spec=pltpu.PrefetchScalarGridSpec(
            num_scalar_prefetch=5,
            grid=(n // tm,),
            in_specs=[pl.BlockSpec((tm, D_MODEL), lambda i, *_: (i, 0)),
                      pl.BlockSpec((8, tm), lambda i, *_: (0, i))],
            out_specs=pl.BlockSpec(memory_space=pl.ANY),
            scratch_shapes=[pltpu.VMEM((2, SLOTS_PER_TILE, XS_COLS), F32),
                            pltpu.VMEM((MOE_ROWS, XS_COLS), F32),
                            pltpu.SemaphoreType.DMA((2,)), pltpu.SemaphoreType.DMA(())]),
        out_shape=jax.ShapeDtypeStruct((p_rows, XS_COLS), F32),
        compiler_params=_params(("arbitrary",)),
        name="dispatch",
    )(g8f, lofff, gofff, totf, ends, h2, routet)


def _experts_kernel(be_ref, nu_ref, xs_ref, w1_ref, b1_ref, w2_ref, b2_ref, ys_ref):
    del be_ref

    @pl.when(pl.program_id(0) < nu_ref[0])
    def _():
        xb = xs_ref[:, 0:D_MODEL].astype(BF16)
        wcol = xs_ref[:, D_MODEL:D_MODEL + 1]
        hcat = jnp.dot(xb, w1_ref[0].astype(BF16), preferred_element_type=F32) + b1_ref[0]
        x_glu = jnp.minimum(hcat[:, :D_EXPERT], SWIGLU_LIMIT)
        x_lin = jnp.clip(hcat[:, D_EXPERT:], -SWIGLU_LIMIT, SWIGLU_LIMIT)
        act = (x_glu * jax.nn.sigmoid(SWIGLU_ALPHA * x_glu) * (x_lin + 1.0)).astype(BF16)
        ys_ref[...] = (jnp.dot(act, w2_ref[0].astype(BF16), preferred_element_type=F32) + b2_ref[0]) * wcol

    @pl.when(pl.program_id(0) >= nu_ref[0])
    def _():
        ys_ref[...] = jnp.zeros(ys_ref.shape, F32)


def _experts(block_e, n_used, xs, w1, b1, w2, b2):
    p = xs.shape[0]
    mb = MOE_ROWS
    last = lambda i, nu: jnp.minimum(i, nu[0] - 1)
    rowmap = lambda i, be, nu: (last(i, nu), 0)
    wmap = lambda i, be, nu: (be[last(i, nu)], 0, 0)
    return pl.pallas_call(
        _experts_kernel,
        grid_spec=pltpu.PrefetchScalarGridSpec(
            num_scalar_prefetch=2,
            grid=(p // mb,),
            in_specs=[pl.BlockSpec((mb, XS_COLS), rowmap),
                      pl.BlockSpec((1, D_MODEL, 2 * D_EXPERT), wmap),
                      pl.BlockSpec((1, 1, 2 * D_EXPERT), wmap),
                      pl.BlockSpec((1, D_EXPERT, D_MODEL), wmap),
                      pl.BlockSpec((1, 1, D_MODEL), wmap)],
            out_specs=pl.BlockSpec((mb, D_MODEL), lambda i, be, nu: (i, 0))),
        out_shape=jax.ShapeDtypeStruct((p, D_MODEL), F32),
        compiler_params=_params(("arbitrary",)),
        name="experts",
    )(block_e, n_used, xs, w1, b1, w2, b2)


def _combine_kernel(g8_ref, loff_ref, goff_ref, tot_ref, ys_ref, route_ref, x1_ref, g2_ref, fn_ref, o_ref, stg, sem):
    t = pl.program_id(0)
    nt = pl.num_programs(0)
    tm = x1_ref.shape[0]
    slot = lax.rem(t, 2)

    def copy(s, lo, go, rows):
        return pltpu.make_async_copy(ys_ref.at[pl.ds(go, rows), :], stg.at[s, pl.ds(lo, rows), :], sem.at[s])

    @pl.when(t == 0)
    def _():
        stg[...] = jnp.zeros(stg.shape, F32)
        _start_runs(t, g8_ref, loff_ref, goff_ref, functools.partial(copy, 0))

    pl.when(t + 1 < nt)(lambda: _start_runs(t + 1, g8_ref, loff_ref, goff_ref, functools.partial(copy, 1 - slot)))
    copy(slot, 0, 0, pl.multiple_of(tot_ref[t], GROUP_ALIGN)).wait()

    route = route_ref[...]
    ls = [route[:, k:k + 1] for k in range(TOP_K)]

    def chunk(ci, y):
        base = pl.multiple_of(ci * SLOT_CHUNK, SLOT_CHUNK)
        sidx = (lax.broadcasted_iota(jnp.int32, (tm, SLOT_CHUNK), 1) + base).astype(F32)
        own = jnp.zeros((tm, SLOT_CHUNK), F32)
        for k in range(TOP_K):
            own = jnp.where(sidx == ls[k], 1.0, own)
        rows = stg[slot, pl.ds(base, SLOT_CHUNK), :].astype(BF16)
        return y + jnp.dot(own.astype(BF16), rows, preferred_element_type=F32)

    y = lax.fori_loop(0, SLOTS_PER_TILE // SLOT_CHUNK, chunk, jnp.zeros((tm, D_MODEL), F32))
    x = x1_ref[...] + g2_ref[0] * y
    o_ref[...] = _rms(x) * fn_ref[...]


def _combine(g8f, lofff, gofff, totf, ys, route, x1, g2, fn, seq):
    n = x1.shape[0]
    tm = ROUTE_TILE
    per_b = seq // tm
    return pl.pallas_call(
        _combine_kernel,
        grid_spec=pltpu.PrefetchScalarGridSpec(
            num_scalar_prefetch=4,
            grid=(n // tm,),
            in_specs=[pl.BlockSpec(memory_space=pl.ANY),
                      pl.BlockSpec((tm, LANE), lambda i, *_: (i, 0)),
                      pl.BlockSpec((tm, D_MODEL), lambda i, *_: (i, 0)),
                      pl.BlockSpec((1, 1, D_MODEL), lambda i, *_: (i // per_b, 0, 0)),
                      pl.BlockSpec((1, D_MODEL), lambda i, *_: (0, 0))],
            out_specs=pl.BlockSpec((tm, D_MODEL), lambda i, *_: (i, 0)),
            scratch_shapes=[pltpu.VMEM((2, SLOTS_PER_TILE, D_MODEL), F32), pltpu.SemaphoreType.DMA((2,))]),
        out_shape=jax.ShapeDtypeStruct((n, D_MODEL), F32),
        compiler_params=_params(("arbitrary",)),
        name="combine",
    )(g8f, lofff, gofff, totf, ys, route, x1, g2, fn)


def _pack_w_in(w_in):
    sq, skv = 512, 128
    o_ka, o_va, o_cq = sq, sq + skv, sq + 2 * skv
    o_ckv = o_cq + MLA_Q_RANK
    o_kr = o_ckv + MLA_KV_RANK
    o_ga = o_kr + MLA_ROPE_DIM
    o_gb = o_ga + D_MODEL
    z = lambda w: jnp.zeros((D_MODEL, w), w_in.dtype)
    ka, va = w_in[:, o_ka:o_va], w_in[:, o_va:o_cq]
    dup = lambda t: jnp.concatenate([t[:, :64], t[:, :64], t[:, 64:], t[:, 64:]], axis=1)
    kr1, kr2 = w_in[:, o_kr:o_kr + HALF_ROPE], w_in[:, o_kr + HALF_ROPE:o_ga]
    packed = jnp.concatenate([
        w_in[:, :sq], dup(ka), dup(va), w_in[:, o_cq:o_ckv], w_in[:, o_ckv:o_kr],
        z(64), kr1, kr2, z(32),
        z(64), kr2, kr1, z(32),
        w_in[:, o_ga:o_gb], w_in[:, o_gb:o_gb + D_MODEL]], axis=1)
    return packed.astype(BF16)


def _pack_heads(w, width, pieces):
    rows = w.shape[0]
    w3 = w.reshape(rows, MLA_HEADS, width)
    cols = [w3[:, :, a:b] for a, b in pieces]
    used = sum(b - a for a, b in pieces)
    cols.append(jnp.zeros((rows, MLA_HEADS, LANE - used), w.dtype))
    return jnp.concatenate(cols, axis=2).reshape(rows, MLA_HEADS * LANE)


def kernel(x, c, positions, w_ada, b_ada, norm_mix, norm_ffn, w_in, sinks, q_norm, kv_norm, w_uq, w_uk, w_uv,
           w_branch_a, w_branch_b, w_out, w_router, b_router, w_moe1, b_moe1, w_moe2, b_moe2, final_norm):
    batch, seq, d = x.shape
    n = batch * seq
    assert d == D_MODEL and w_ada.shape[0] == 1
    assert seq % TQ_MLA == 0 and seq % (SWA_QB * ATTN_BLOCK) == 0 and seq % TM_PROJ == 0 and seq % ROUTE_TILE == 0
    hw = MLA_HEADS * LANE
    qk = MLA_NOPE_DIM + MLA_ROPE_DIM

    w_in_p = _pack_w_in(w_in[0])
    wuq_main = _pack_heads(w_uq[0], qk, [(0, qk)])
    w3 = w_uq[0].reshape(MLA_Q_RANK, MLA_HEADS, qk)
    wuq_swap = jnp.concatenate([
        jnp.zeros((MLA_Q_RANK, MLA_HEADS, MLA_NOPE_DIM), F32),
        w3[:, :, MLA_NOPE_DIM + HALF_ROPE:], w3[:, :, MLA_NOPE_DIM:MLA_NOPE_DIM + HALF_ROPE],
        jnp.zeros((MLA_Q_RANK, MLA_HEADS, LANE - qk), F32)], axis=2).reshape(MLA_Q_RANK, hw)
    wuq_p = jnp.concatenate([wuq_main, wuq_swap], axis=1).astype(BF16)
    wuk_p = _pack_heads(w_uk[0], MLA_NOPE_DIM, [(0, MLA_NOPE_DIM)]).astype(BF16)
    wuv_p = _pack_heads(w_uv[0], MLA_V_DIM, [(0, MLA_V_DIM)]).astype(BF16)
    one_row = jnp.tile((jnp.arange(LANE) == MLA_V_DIM).astype(F32), MLA_HEADS)[None, :]
    freqs = ROPE_THETA ** (-jnp.arange(0, MLA_ROPE_DIM, 2, dtype=F32) / MLA_ROPE_DIM)
    frq = jnp.concatenate([jnp.zeros((MLA_NOPE_DIM,), F32), freqs, freqs,
                           jnp.zeros((LANE - qk,), F32)])[None, :]
    sgn = jnp.concatenate([jnp.zeros((MLA_NOPE_DIM,), F32), -jnp.ones((HALF_ROPE,), F32),
                           jnp.ones((HALF_ROPE,), F32), jnp.zeros((LANE - qk,), F32)])[None, :]
    wb3 = w_branch_b[0].reshape(MLA_HEADS, MLA_V_DIM, D_MODEL)
    wb_p = jnp.concatenate([wb3, jnp.zeros((MLA_HEADS, LANE - MLA_V_DIM, D_MODEL), F32)],
                           axis=1).reshape(hw, D_MODEL).astype(BF16)
    wa = w_branch_a[0].astype(BF16)
    wo = w_out[0].astype(BF16)
    wr = jnp.pad(w_router[0], ((0, 0), (0, LANE - N_EXPERTS)))
    wr_hi = wr.astype(BF16)
    wr_lo = (wr - wr_hi.astype(F32)).astype(BF16)
    br_p = jnp.pad(b_router[0], (0, LANE - N_EXPERTS))[None, :]
    b1 = b_moe1[0][:, None, :]
    b2 = b_moe2[0][:, None, :]
    slopes = jnp.asarray(np.exp2(-8.0 * np.arange(1, SWA_HEADS + 1) / SWA_HEADS), dtype=F32)

    c8 = jnp.pad(c, ((0, 8 - batch), (0, 0)))
    mod = _ada(c8, w_ada[0], b_ada[0][None, :])[:batch]
    sh1, sc1, g1, sh2, sc2, g2 = [m[:, None, :] for m in jnp.split(mod, 6, axis=-1)]

    x2 = x.reshape(n, D_MODEL)
    pos2 = positions.reshape(n, 1).astype(jnp.int32)
    qa, ka2, va2, qm, km, vm, ga, gb = _inproj(
        x2, pos2, sh1, sc1, norm_mix, w_in_p, q_norm, kv_norm, wuq_p, wuk_p, wuv_p, frq, sgn, one_row, seq)
    yb = _mla(qm, km, vm, batch, seq)
    ya = _swa(sinks[0], slopes, qa, ka2, va2, batch, seq)

    x1, h2, route, routet, g8 = _outproj(
        ya, yb, ga, gb, x2, g1, sh2, sc2, norm_ffn, wa, wb_p, wo, wr_hi, wr_lo, br_p, seq)

    n_tiles = n // ROUTE_TILE
    g8t = g8.reshape(n_tiles, 8, LANE)[:, 0, :N_EXPERTS].astype(jnp.int32)
    loff = jnp.cumsum(g8t, axis=1) - g8t
    padded = ((jnp.sum(g8t, axis=0) + MOE_ROWS - 1) // MOE_ROWS) * MOE_ROWS
    pends = jnp.cumsum(padded)
    goff = (pends - padded)[None, :] + jnp.cumsum(g8t, axis=0) - g8t
    tot = jnp.sum(g8t, axis=1).astype(jnp.int32)
    p_rows = n * TOP_K + n_tiles * N_EXPERTS * (GROUP_ALIGN - 1) + N_EXPERTS * (MOE_ROWS - 1)
    p_rows = -(-p_rows // MOE_ROWS) * MOE_ROWS
    n_blocks = p_rows // MOE_ROWS
    block_start = jnp.arange(n_blocks, dtype=jnp.int32) * MOE_ROWS
    block_e = jnp.minimum(jnp.sum((pends[None, :] <= block_start[:, None]).astype(jnp.int32), axis=1),
                          N_EXPERTS - 1).astype(jnp.int32)
    n_used = (pends[-1:] // MOE_ROWS).astype(jnp.int32)
    tabs = (g8t.reshape(-1), loff.reshape(-1).astype(jnp.int32), goff.reshape(-1).astype(jnp.int32), tot)

    xs = _dispatch(*tabs, pends.astype(jnp.int32), h2, routet, p_rows)
    ys = _experts(block_e, n_used, xs, w_moe1[0], b1, w_moe2[0], b2)
    out = _combine(*tabs, ys, route, x1, g2, final_norm[None, :], seq)
    return out.reshape(batch, seq, D_MODEL)
```

```python
import functools

import numpy as np
import jax
import jax.numpy as jnp
from jax import lax
from jax.experimental import pallas as pl
from jax.experimental.pallas import tpu as pltpu

D_MODEL = 1024
SWA_HEADS = 8
SWA_KV_HEADS = 2
SWA_HEAD_DIM = 64
ATTN_BLOCK = 128
MLA_HEADS = 8
MLA_Q_RANK = 384
MLA_KV_RANK = 256
MLA_NOPE_DIM = 64
MLA_ROPE_DIM = 32
MLA_V_DIM = 64
ROPE_THETA = 10000.0
N_EXPERTS = 32
TOP_K = 4
D_EXPERT = 1024
SWIGLU_LIMIT = 7.0
SWIGLU_ALPHA = 1.702
NORM_EPS = 1e-6

LANE = 128
LOG2E = 1.4426950408889634
NEG = -1e30
SWA_QSCALE = SWA_HEAD_DIM ** -0.5 * LOG2E
MLA_QSCALE = (MLA_NOPE_DIM + MLA_ROPE_DIM) ** -0.5 * LOG2E
HALF_ROPE = MLA_ROPE_DIM // 2
MLA_PLAIN_BOUND = 60.0
MLA_WIDE = 4

BF16 = jnp.bfloat16
F32 = jnp.float32

TM_PROJ = 512
TQ_MLA = 512
SWA_QB = 4
MOE_ROWS = 512
ROUTE_TILE = 512
GROUP_ALIGN = 8
SLOTS_PER_TILE = ROUTE_TILE * TOP_K + N_EXPERTS * GROUP_ALIGN
SLOT_SUB = 256
SLOT_CHUNK = 3 * SLOT_SUB
XS_COLS = D_MODEL + LANE
VMEM_LIMIT = 56 * 1024 * 1024


def _params(sem, vmem=VMEM_LIMIT):
    return pltpu.CompilerParams(dimension_semantics=sem, vmem_limit_bytes=vmem)


def _rms(x):
    return x * lax.rsqrt(jnp.mean(x * x, axis=-1, keepdims=True) + NORM_EPS)


def _ada_kernel(c_ref, w_ref, b_ref, o_ref):
    c = c_ref[...]
    a = (c * jax.nn.sigmoid(c)).astype(BF16)
    o_ref[...] = jnp.dot(a, w_ref[...].astype(BF16), preferred_element_type=F32) + b_ref[...]


def _ada(c8, w_ada, b_ada):
    n_out = w_ada.shape[1]
    return pl.pallas_call(
        _ada_kernel,
        grid=(n_out // D_MODEL,),
        in_specs=[pl.BlockSpec((8, D_MODEL), lambda j: (0, 0)),
                  pl.BlockSpec((D_MODEL, D_MODEL), lambda j: (0, j)),
                  pl.BlockSpec((1, D_MODEL), lambda j: (0, j))],
        out_specs=pl.BlockSpec((8, D_MODEL), lambda j: (0, j)),
        out_shape=jax.ShapeDtypeStruct((8, n_out), F32),
        compiler_params=_params(("arbitrary",)),
        name="ada",
    )(c8, w_ada, b_ada)


_C_QA = (0, 512)
_C_KA = (512, 768)
_C_VA = (768, 1024)
_C_CQ = (1024, 1408)
_C_CKV = (1408, 1664)
_C_KRA = (1664, 1792)
_C_KRB = (1792, 1920)
_C_GA = (1920, 2944)
_C_GB = (2944, 3968)
_W_IN_COLS = 3968


def _inproj_kernel(x_ref, pos_ref, sh_ref, sc_ref, nm_ref, win_ref, qn_ref, kvn_ref, wuq_ref, wuk_ref,
                   wuv_ref, frq_ref, sgn_ref, one_ref,
                   qa_ref, ka_ref, va_ref, qm_ref, km_ref, vm_ref, ga_ref, gb_ref):
    x = x_ref[...]
    h = (_rms(x) * nm_ref[...] * (1.0 + sc_ref[0]) + sh_ref[0]).astype(BF16)

    def proj(c):
        return jnp.dot(h, win_ref[:, c[0]:c[1]], preferred_element_type=F32)

    qa_ref[...] = (proj(_C_QA) * SWA_QSCALE).astype(BF16)
    ka_ref[...] = proj(_C_KA).astype(BF16)
    va_ref[...] = proj(_C_VA).astype(BF16)
    ga_ref[...] = jax.nn.sigmoid(proj(_C_GA)).astype(BF16)
    gb_ref[...] = jax.nn.sigmoid(proj(_C_GB)).astype(BF16)

    ang = pos_ref[...].astype(F32) * frq_ref[...]
    cs = jnp.cos(ang)
    sn = jnp.sin(ang) * sgn_ref[...]

    cqn = (_rms(proj(_C_CQ)) * qn_ref[...]).astype(BF16)
    q2 = jnp.dot(cqn, wuq_ref[...], preferred_element_type=F32)
    hw = MLA_HEADS * LANE
    for hh in range(MLA_HEADS):
        sl = slice(hh * LANE, (hh + 1) * LANE)
        sw = slice(hw + hh * LANE, hw + (hh + 1) * LANE)
        qm_ref[:, sl] = ((q2[:, sl] * cs + q2[:, sw] * sn) * MLA_QSCALE).astype(BF16)

    ckvn = (_rms(proj(_C_CKV)) * kvn_ref[...]).astype(BF16)
    krr = proj(_C_KRA) * cs + proj(_C_KRB) * sn
    kn = jnp.dot(ckvn, wuk_ref[...], preferred_element_type=F32)
    for hh in range(MLA_HEADS):
        sl = slice(hh * LANE, (hh + 1) * LANE)
        km_ref[:, sl] = (kn[:, sl] + krr).astype(BF16)
    vm_ref[...] = (jnp.dot(ckvn, wuv_ref[...], preferred_element_type=F32) + one_ref[...]).astype(BF16)


def _inproj(x2, pos2, sh1, sc1, nm, w_in_p, qn, kvn, wuq_p, wuk_p, wuv_p, frq, sgn, one_row, seq):
    n = x2.shape[0]
    tm = TM_PROJ
    per_b = seq // tm
    hw = MLA_HEADS * LANE
    row = lambda i: (i, 0)
    fix = lambda i: (0, 0)
    bsel = lambda i: (i // per_b, 0, 0)
    widths = (512, 256, 256, hw, hw, hw, D_MODEL, D_MODEL)
    return pl.pallas_call(
        _inproj_kernel,
        grid=(n // tm,),
        in_specs=[pl.BlockSpec((tm, D_MODEL), row),
                  pl.BlockSpec((tm, 1), row),
                  pl.BlockSpec((1, 1, D_MODEL), bsel),
                  pl.BlockSpec((1, 1, D_MODEL), bsel),
                  pl.BlockSpec((1, D_MODEL), fix),
                  pl.BlockSpec((D_MODEL, _W_IN_COLS), fix),
                  pl.BlockSpec((1, MLA_Q_RANK), fix),
                  pl.BlockSpec((1, MLA_KV_RANK), fix),
                  pl.BlockSpec((MLA_Q_RANK, 2 * hw), fix),
                  pl.BlockSpec((MLA_KV_RANK, hw), fix),
                  pl.BlockSpec((MLA_KV_RANK, hw), fix),
                  pl.BlockSpec((1, LANE), fix),
                  pl.BlockSpec((1, LANE), fix),
                  pl.BlockSpec((1, hw), fix)],
        out_specs=[pl.BlockSpec((tm, w), row) for w in widths],
        out_shape=[jax.ShapeDtypeStruct((n, w), BF16) for w in widths],
        compiler_params=_params(("arbitrary",)),
        name="inproj",
    )(x2, pos2, sh1, sc1, nm, w_in_p, qn, kvn, wuq_p, wuk_p, wuv_p, frq, sgn, one_row)


def _mla_kernel(q_ref, k_ref, v_ref, o_ref, m_scr, acc_scr, kn_scr):
    i = pl.program_id(2)
    t = TQ_MLA
    acc_scr[...] = jnp.zeros(acc_scr.shape, F32)

    @pl.when(i == 0)
    def _():
        kk = k_ref[...].astype(F32)
        kn_scr[...] = jnp.max(jnp.sum(kk * kk, axis=1, keepdims=True), axis=0, keepdims=True)

    def scores(j, nblk, diag_at):
        start = pl.multiple_of(j * t, t)
        k = k_ref[pl.ds(start, nblk * t), :]
        v = v_ref[pl.ds(start, nblk * t), :]
        s = lax.dot_general(q_ref[...], k, (((1,), (1,)), ((), ())), preferred_element_type=F32)
        if diag_at is not None:
            qi = lax.broadcasted_iota(jnp.int32, s.shape, 0)
            kj = lax.broadcasted_iota(jnp.int32, s.shape, 1)
            s = jnp.where(kj <= qi + diag_at * t, s, NEG)
        return s, v

    def plain_step(j, nblk, diag_at):
        s, v = scores(j, nblk, diag_at)
        acc_scr[...] += jnp.dot(jnp.exp2(s).astype(BF16), v, preferred_element_type=F32)

    def online_step(j, nblk, diag_at):
        s, v = scores(j, nblk, diag_at)
        m_old = m_scr[...]
        m_new = jnp.maximum(m_old, jnp.max(s, axis=1, keepdims=True))
        p = jnp.exp2(s - m_new)
        alpha = jnp.exp2(m_old - m_new)
        acc_scr[...] = acc_scr[...] * alpha + jnp.dot(p.astype(BF16), v, preferred_element_type=F32)
        m_scr[...] = m_new

    def run(step, wide):
        nw = lax.shift_right_logical(i, wide.bit_length() - 1)
        left = i - nw * wide

        def wide_body(jj, carry):
            step(jj * wide, wide, None)
            return carry

        lax.fori_loop(0, nw, wide_body, 0)
        for r in range(wide):
            pl.when(left == r)(lambda r=r: step(nw * wide, r + 1, r))

    qf = q_ref[...].astype(F32)
    qn = jnp.max(jnp.sum(qf * qf, axis=1, keepdims=True), axis=0, keepdims=True)
    small = jnp.max(qn * kn_scr[...]) <= MLA_PLAIN_BOUND ** 2

    @pl.when(small)
    def _():
        run(plain_step, MLA_WIDE)

    @pl.when(jnp.logical_not(small))
    def _():
        m_scr[...] = jnp.full(m_scr.shape, NEG, F32)
        run(online_step, 1)

    acc = acc_scr[...]
    o_ref[...] = (acc / acc[:, MLA_V_DIM:MLA_V_DIM + 1]).astype(BF16)


def _mla(qm, km, vm, batch, seq):
    n = qm.shape[0]
    t = TQ_MLA
    nq = seq // t
    qmap = lambda b, h, i: (b * nq + i, h)
    kmap = lambda b, h, i: (b, h)
    return pl.pallas_call(
        _mla_kernel,
        grid=(batch, MLA_HEADS, nq),
        in_specs=[pl.BlockSpec((t, LANE), qmap),
                  pl.BlockSpec((seq, LANE), kmap),
                  pl.BlockSpec((seq, LANE), kmap)],
        out_specs=pl.BlockSpec((t, LANE), qmap),
        out_shape=jax.ShapeDtypeStruct((n, MLA_HEADS * LANE), BF16),
        scratch_shapes=[pltpu.VMEM((t, 1), F32), pltpu.VMEM((t, LANE), F32), pltpu.VMEM((1, 1), F32)],
        compiler_params=_params(("arbitrary", "arbitrary", "arbitrary")),
        name="mla",
    )(qm, km, vm)


def _swa_kernel(sink_ref, slope_ref, q_ref, kc_ref, kp_ref, vc_ref, vp_ref, o_ref, bias_scr):
    b = pl.program_id(0)
    i = pl.program_id(1)
    blk = ATTN_BLOCK

    @pl.when((b == 0) & (i == 0))
    def _():
        qi = lax.broadcasted_iota(jnp.int32, (blk, 2 * blk), 0)
        kj = lax.broadcasted_iota(jnp.int32, (blk, 2 * blk), 1)
        dist = qi - kj + blk
        valid = (dist >= 0) & (dist < blk)
        distf = dist.astype(F32)
        for hd in range(SWA_HEADS):
            bias_scr[hd] = jnp.where(valid, -(slope_ref[hd] * LOG2E) * distf, NEG)

    lane = lax.broadcasted_iota(jnp.int32, (2 * blk, LANE), 1)
    lo = lane < SWA_HEAD_DIM
    kcol = lax.broadcasted_iota(jnp.int32, (blk, 2 * blk), 1)
    first_pen = jnp.where(kcol < blk, jnp.where(i == 0, NEG, 0.0), 0.0)
    zero = jnp.zeros((2 * blk, LANE), BF16)

    for qb in range(SWA_QB):
        if qb == 0:
            kprev, vprev = kp_ref[...], vp_ref[...]
        else:
            kprev = kc_ref[(qb - 1) * blk:qb * blk, :]
            vprev = vc_ref[(qb - 1) * blk:qb * blk, :]
        kcat = jnp.concatenate([kprev, kc_ref[qb * blk:(qb + 1) * blk, :]], axis=0)
        vcat = jnp.concatenate([vprev, vc_ref[qb * blk:(qb + 1) * blk, :]], axis=0)
        for g in range(SWA_KV_HEADS):
            kg = kcat[:, g * LANE:(g + 1) * LANE]
            vg = vcat[:, g * LANE:(g + 1) * LANE]
            halves = ((jnp.where(lo, kg, zero), jnp.where(lo, vg, zero)),
                      (jnp.where(lo, zero, kg), jnp.where(lo, zero, vg)))
            for pp in range(2):
                pr = g * 2 + pp
                qpair = q_ref[qb * blk:(qb + 1) * blk, pr * LANE:(pr + 1) * LANE]
                out = None
                for half in range(2):
                    hd = pr * 2 + half
                    kx, vx = halves[half]
                    s = lax.dot_general(qpair, kx, (((1,), (1,)), ((), ())), preferred_element_type=F32)
                    s = s + bias_scr[hd]
                    if qb == 0:
                        s = s + first_pen
                    sink = sink_ref[hd] * LOG2E
                    m = jnp.maximum(jnp.max(s, axis=1, keepdims=True), sink)
                    p = jnp.exp2(s - m)
                    denom = jnp.sum(p, axis=1, keepdims=True) + jnp.exp2(sink - m)
                    o = jnp.dot(p.astype(BF16), vx, preferred_element_type=F32) * (1.0 / denom)
                    out = o if out is None else out + o
                o_ref[qb * blk:(qb + 1) * blk, pr * LANE:(pr + 1) * LANE] = out.astype(BF16)


def _swa(sinks, slopes, qa, ka2, va2, batch, seq):
    n = qa.shape[0]
    blk = ATTN_BLOCK
    rows = SWA_QB * blk
    steps = seq // rows
    cur = lambda b, i: (b * steps + i, 0)
    prev = lambda b, i: (b * (seq // blk) + jnp.maximum(i * SWA_QB - 1, 0), 0)
    smem = pl.BlockSpec(memory_space=pltpu.SMEM)
    return pl.pallas_call(
        _swa_kernel,
        grid=(batch, steps),
        in_specs=[smem, smem,
                  pl.BlockSpec((rows, 512), cur),
                  pl.BlockSpec((rows, 256), cur),
                  pl.BlockSpec((blk, 256), prev),
                  pl.BlockSpec((rows, 256), cur),
                  pl.BlockSpec((blk, 256), prev)],
        out_specs=pl.BlockSpec((rows, 512), cur),
        out_shape=jax.ShapeDtypeStruct((n, 512), BF16),
        scratch_shapes=[pltpu.VMEM((SWA_HEADS, blk, 2 * blk), F32)],
        compiler_params=_params(("arbitrary", "arbitrary")),
        name="swa",
    )(sinks, slopes, qa, ka2, ka2, va2, va2)


def _outproj_kernel(ya_ref, yb_ref, ga_ref, gb_ref, x_ref, g1_ref, sh_ref, sc_ref, nm_ref, wa_ref, wb_ref,
                    wo_ref, wrh_ref, wrl_ref, br_ref,
                    x1_ref, h2_ref, route_ref, routet_ref, g8_ref):
    tm = x_ref.shape[0]
    a = jnp.dot(ya_ref[...], wa_ref[...], preferred_element_type=F32)
    bm = jnp.dot(yb_ref[...], wb_ref[...], preferred_element_type=F32)
    mixed = (ga_ref[...].astype(F32) * a + gb_ref[...].astype(F32) * bm).astype(BF16)
    x1 = x_ref[...] + g1_ref[0] * jnp.dot(mixed, wo_ref[...], preferred_element_type=F32)
    x1_ref[...] = x1
    h2 = _rms(x1) * nm_ref[...] * (1.0 + sc_ref[0]) + sh_ref[0]
    hi = h2.astype(BF16)
    h2_ref[:, 0:D_MODEL] = hi

    lo = (h2 - hi.astype(F32)).astype(BF16)
    logits = (jnp.dot(hi, wrh_ref[...], preferred_element_type=F32)
              + jnp.dot(lo, wrh_ref[...], preferred_element_type=F32)
              + jnp.dot(hi, wrl_ref[...], preferred_element_type=F32)) + br_ref[...]

    lane = lax.broadcasted_iota(jnp.int32, (tm, LANE), 1).astype(F32)
    work = jnp.where(lane < N_EXPERTS, logits, -jnp.inf)
    vals, idxs = [], []
    for _ in range(TOP_K):
        m = jnp.max(work, axis=1, keepdims=True)
        idx = jnp.min(jnp.where(work == m, lane, float(LANE)), axis=1, keepdims=True)
        vals.append(m)
        idxs.append(idx)
        work = jnp.where(lane == idx, -jnp.inf, work)
    exps = [jnp.exp(v - vals[0]) for v in vals]
    tot = exps[0] + exps[1] + exps[2] + exps[3]

    onehot = jnp.zeros((tm, LANE), F32)
    for idx in idxs:
        onehot = onehot + (lane == idx).astype(F32)
    r = lax.broadcasted_iota(jnp.int32, (tm, tm), 0)
    c = lax.broadcasted_iota(jnp.int32, (tm, tm), 1)
    prefix = jnp.dot((c < r).astype(BF16), onehot.astype(BF16), preferred_element_type=F32)
    groups = jnp.ceil(jnp.sum(onehot, axis=0, keepdims=True) * (1.0 / GROUP_ALIGN))
    er = lax.broadcasted_iota(jnp.int32, (LANE, LANE), 0)
    ec = lax.broadcasted_iota(jnp.int32, (LANE, LANE), 1)
    before = jnp.dot(jnp.broadcast_to(groups, (8, LANE)).astype(BF16), (er < ec).astype(BF16),
                     preferred_element_type=F32)[0:1, :]
    slot_e = before * GROUP_ALIGN + prefix

    route = jnp.zeros((tm, LANE), F32)
    wext = jnp.zeros((tm, LANE), F32)
    for k in range(TOP_K):
        sk = jnp.sum(jnp.where(lane == idxs[k], slot_e, 0.0), axis=1, keepdims=True)
        sub = jnp.floor(sk * (1.0 / SLOT_SUB))
        route = jnp.where(lane == float(k), sk - sub * SLOT_SUB, route)
        route = jnp.where(lane == float(TOP_K + k), sub, route)
        wk = exps[k] / tot
        wk_hi = wk.astype(BF16).astype(F32)
        wext = jnp.where(lane == idxs[k], wk_hi, wext)
        wext = jnp.where(lane == idxs[k] + float(N_EXPERTS), wk - wk_hi, wext)
    h2_ref[:, D_MODEL:XS_COLS] = wext.astype(BF16)
    route_ref[...] = route
    routet_ref[...] = route.T[0:8, :]
    g8_ref[...] = jnp.broadcast_to(groups * GROUP_ALIGN, (8, LANE))


def _outproj(ya, yb, ga, gb, x2, g1, sh2, sc2, nm, wa, wb_p, wo, wr_hi, wr_lo, br_p, seq):
    n = x2.shape[0]
    tm = ROUTE_TILE
    per_b = seq // tm
    hw = MLA_HEADS * LANE
    row = lambda i: (i, 0)
    fix = lambda i: (0, 0)
    bsel = lambda i: (i // per_b, 0, 0)
    return pl.pallas_call(
        _outproj_kernel,
        grid=(n // tm,),
        in_specs=[pl.BlockSpec((tm, 512), row),
                  pl.BlockSpec((tm, hw), row),
                  pl.BlockSpec((tm, D_MODEL), row),
                  pl.BlockSpec((tm, D_MODEL), row),
                  pl.BlockSpec((tm, D_MODEL), row),
                  pl.BlockSpec((1, 1, D_MODEL), bsel),
                  pl.BlockSpec((1, 1, D_MODEL), bsel),
                  pl.BlockSpec((1, 1, D_MODEL), bsel),
                  pl.BlockSpec((1, D_MODEL), fix),
                  pl.BlockSpec((512, D_MODEL), fix),
                  pl.BlockSpec((hw, D_MODEL), fix),
                  pl.BlockSpec((D_MODEL, D_MODEL), fix),
                  pl.BlockSpec((D_MODEL, LANE), fix),
                  pl.BlockSpec((D_MODEL, LANE), fix),
                  pl.BlockSpec((1, LANE), fix)],
        out_specs=[pl.BlockSpec((tm, D_MODEL), row),
                   pl.BlockSpec((tm, XS_COLS), row),
                   pl.BlockSpec((tm, LANE), row),
                   pl.BlockSpec((8, tm), lambda i: (0, i)),
                   pl.BlockSpec((8, LANE), row)],
        out_shape=[jax.ShapeDtypeStruct((n, D_MODEL), F32),
                   jax.ShapeDtypeStruct((n, XS_COLS), BF16),
                   jax.ShapeDtypeStruct((n, LANE), F32),
                   jax.ShapeDtypeStruct((8, n), F32),
                   jax.ShapeDtypeStruct((n // tm * 8, LANE), F32)],
        compiler_params=_params(("arbitrary",)),
        name="outproj",
    )(ya, yb, ga, gb, x2, g1, sh2, sc2, nm, wa, wb_p, wo, wr_hi, wr_lo, br_p)


def _start_runs(tile, g8_ref, loff_ref, goff_ref, make_copy):
    for e in range(N_EXPERTS):
        g = g8_ref[tile * N_EXPERTS + e]
        lo = loff_ref[tile * N_EXPERTS + e]
        go = goff_ref[tile * N_EXPERTS + e]
        def go_copy(lo=lo, go=go, g=g):
            make_copy(pl.multiple_of(lo, GROUP_ALIGN), pl.multiple_of(go, GROUP_ALIGN),
                      pl.multiple_of(g, GROUP_ALIGN)).start()

        pl.when(g > 0)(go_copy)


def _dispatch_kernel(g8_ref, loff_ref, goff_ref, tot_ref, ends_ref, h_ref, rt_ref, xs_ref, buf, zbuf, sem, zsem):
    t = pl.program_id(0)
    nt = pl.num_programs(0)
    tm = h_ref.shape[0]
    slot = lax.rem(t, 2)

    @pl.when(t == 0)
    def _():
        zbuf[...] = jnp.zeros(zbuf.shape, zbuf.dtype)

        def fill(start):
            return pltpu.make_async_copy(zbuf, xs_ref.at[pl.ds(pl.multiple_of(start, MOE_ROWS), MOE_ROWS), :], zsem)

        def nonempty(e):
            return ends_ref[e] > (ends_ref[e - 1] if e else 0)

        used = lax.shift_right_logical(ends_ref[N_EXPERTS - 1], MOE_ROWS.bit_length() - 1)
        n_blocks = xs_ref.shape[0] // MOE_ROWS
        for e in range(N_EXPERTS):
            pl.when(nonempty(e))(lambda e=e: fill(ends_ref[e] - MOE_ROWS).start())
        lax.fori_loop(used, n_blocks, lambda b, c: (fill(b * MOE_ROWS).start(), c)[1], 0)
        for e in range(N_EXPERTS):
            pl.when(nonempty(e))(lambda e=e: fill(ends_ref[e] - MOE_ROWS).wait())
        lax.fori_loop(used, n_blocks, lambda b, c: (fill(b * MOE_ROWS).wait(), c)[1], 0)

    hb = h_ref[...]
    low = [rt_ref[k:k + 1, :] for k in range(TOP_K)]
    sub = [rt_ref[TOP_K + k:TOP_K + k + 1, :] for k in range(TOP_K)]
    srow = lax.broadcasted_iota(jnp.int32, (SLOT_SUB, tm), 0).astype(F32).astype(BF16)
    one = jnp.ones((SLOT_SUB, tm), BF16)

    for ci in range(SLOTS_PER_TILE // SLOT_CHUNK):
        parts = []
        for b in range(SLOT_CHUNK // SLOT_SUB):
            blk = float(ci * (SLOT_CHUNK // SLOT_SUB) + b)
            own = jnp.zeros((SLOT_SUB, tm), BF16)
            for k in range(TOP_K):
                here = jnp.where(sub[k] == blk, low[k], -1.0).astype(BF16)
                own = jnp.where(srow == here, one, own)
            parts.append(own)
        own = jnp.concatenate(parts, axis=0)
        buf[slot, ci * SLOT_CHUNK:(ci + 1) * SLOT_CHUNK, :] = jnp.dot(own, hb, preferred_element_type=F32)

    def copy(s, lo, go, rows):
        return pltpu.make_async_copy(buf.at[s, pl.ds(lo, rows), :], xs_ref.at[pl.ds(go, rows), :], sem.at[s])

    _start_runs(t, g8_ref, loff_ref, goff_ref, functools.partial(copy, slot))

    def drain(tile, s):
        rows = pl.multiple_of(tot_ref[tile], GROUP_ALIGN)
        copy(s, 0, 0, rows).wait()

    pl.when(t > 0)(lambda: drain(t - 1, 1 - slot))
    pl.when(t == nt - 1)(lambda: drain(t, slot))


def _dispatch(g8f, lofff, gofff, totf, ends, h2, routet, p_rows):
    n = h2.shape[0]
    tm = ROUTE_TILE
    return pl.pallas_call(
        _dispatch_kernel,
        grid_spec=pltpu.PrefetchScalarGridSpec(
            num_scalar_prefetch=5,
            grid=(n // tm,),
            in_specs=[pl.BlockSpec((tm, XS_COLS), lambda i, *_: (i, 0)),
                      pl.BlockSpec((8, tm), lambda i, *_: (0, i))],
            out_specs=pl.BlockSpec(memory_space=pl.ANY),
            scratch_shapes=[pltpu.VMEM((2, SLOTS_PER_TILE, XS_COLS), F32),
                            pltpu.VMEM((MOE_ROWS, XS_COLS), F32),
                            pltpu.SemaphoreType.DMA((2,)), pltpu.SemaphoreType.DMA(())]),
        out_shape=jax.ShapeDtypeStruct((p_rows, XS_COLS), F32),
        compiler_params=_params(("arbitrary",)),
        name="dispatch",
    )(g8f, lofff, gofff, totf, ends, h2, routet)


def _experts_kernel(be_ref, nu_ref, xs_ref, w1_ref, b1_ref, w2_ref, b2_ref, ys_ref):
    @pl.when(pl.program_id(0) < nu_ref[0])
    def _():
        xb = xs_ref[:, 0:D_MODEL].astype(BF16)
        e = be_ref[pl.program_id(0)]
        wext = xs_ref[:, D_MODEL:XS_COLS]
        lane = lax.broadcasted_iota(jnp.int32, wext.shape, 1)
        wcol = jnp.sum(jnp.where((lane == e) | (lane == e + N_EXPERTS), wext, 0.0), axis=1, keepdims=True)
        hcat = jnp.dot(xb, w1_ref[0].astype(BF16), preferred_element_type=F32) + b1_ref[0]
        x_glu = jnp.minimum(hcat[:, :D_EXPERT], SWIGLU_LIMIT)
        x_lin = jnp.clip(hcat[:, D_EXPERT:], -SWIGLU_LIMIT, SWIGLU_LIMIT)
        act = (x_glu * jax.nn.sigmoid(SWIGLU_ALPHA * x_glu) * (x_lin + 1.0)).astype(BF16)
        ys_ref[...] = (jnp.dot(act, w2_ref[0].astype(BF16), preferred_element_type=F32) + b2_ref[0]) * wcol

    @pl.when(pl.program_id(0) >= nu_ref[0])
    def _():
        ys_ref[...] = jnp.zeros(ys_ref.shape, F32)


def _experts(block_e, n_used, xs, w1, b1, w2, b2):
    p = xs.shape[0]
    mb = MOE_ROWS
    last = lambda i, nu: jnp.minimum(i, nu[0] - 1)
    rowmap = lambda i, be, nu: (last(i, nu), 0)
    wmap = lambda i, be, nu: (be[last(i, nu)], 0, 0)
    return pl.pallas_call(
        _experts_kernel,
        grid_spec=pltpu.PrefetchScalarGridSpec(
            num_scalar_prefetch=2,
            grid=(p // mb,),
            in_specs=[pl.BlockSpec((mb, XS_COLS), rowmap),
                      pl.BlockSpec((1, D_MODEL, 2 * D_EXPERT), wmap),
                      pl.BlockSpec((1, 1, 2 * D_EXPERT), wmap),
                      pl.BlockSpec((1, D_EXPERT, D_MODEL), wmap),
                      pl.BlockSpec((1, 1, D_MODEL), wmap)],
            out_specs=pl.BlockSpec((mb, D_MODEL), lambda i, be, nu: (i, 0))),
        out_shape=jax.ShapeDtypeStruct((p, D_MODEL), F32),
        compiler_params=_params(("arbitrary",)),
        name="experts",
    )(block_e, n_used, xs, w1, b1, w2, b2)


def _combine_kernel(g8_ref, loff_ref, goff_ref, tot_ref, ys_ref, route_ref, x1_ref, g2_ref, fn_ref, o_ref, stg, sem):
    t = pl.program_id(0)
    nt = pl.num_programs(0)
    tm = x1_ref.shape[0]
    slot = lax.rem(t, 2)

    def copy(s, lo, go, rows):
        return pltpu.make_async_copy(ys_ref.at[pl.ds(go, rows), :], stg.at[s, pl.ds(lo, rows), :], sem.at[s])

    @pl.when(t == 0)
    def _():
        stg[...] = jnp.zeros(stg.shape, F32)
        _start_runs(t, g8_ref, loff_ref, goff_ref, functools.partial(copy, 0))

    pl.when(t + 1 < nt)(lambda: _start_runs(t + 1, g8_ref, loff_ref, goff_ref, functools.partial(copy, 1 - slot)))
    copy(slot, 0, 0, pl.multiple_of(tot_ref[t], GROUP_ALIGN)).wait()

    route = route_ref[...]
    low = [jnp.broadcast_to(route[:, k:k + 1], (tm, SLOT_SUB)).astype(BF16) for k in range(TOP_K)]
    sub = [jnp.broadcast_to(route[:, TOP_K + k:TOP_K + k + 1], (tm, SLOT_SUB)).astype(BF16) for k in range(TOP_K)]
    scol = lax.broadcasted_iota(jnp.int32, (tm, SLOT_SUB), 1).astype(F32).astype(BF16)
    one = jnp.ones((tm, SLOT_SUB), BF16)
    never = jnp.full((tm, SLOT_SUB), -1.0, BF16)

    y = None
    for ci in range(SLOTS_PER_TILE // SLOT_CHUNK):
        parts = []
        for b in range(SLOT_CHUNK // SLOT_SUB):
            blk = float(ci * (SLOT_CHUNK // SLOT_SUB) + b)
            own = jnp.zeros((tm, SLOT_SUB), BF16)
            for k in range(TOP_K):
                own = jnp.where(scol == jnp.where(sub[k] == blk, low[k], never), one, own)
            parts.append(own)
        own = jnp.concatenate(parts, axis=1)
        rows = stg[slot, ci * SLOT_CHUNK:(ci + 1) * SLOT_CHUNK, :].astype(BF16)
        part = jnp.dot(own, rows, preferred_element_type=F32)
        y = part if y is None else y + part
    x = x1_ref[...] + g2_ref[0] * y
    o_ref[...] = _rms(x) * fn_ref[...]


def _combine(g8f, lofff, gofff, totf, ys, route, x1, g2, fn, seq):
    n = x1.shape[0]
    tm = ROUTE_TILE
    per_b = seq // tm
    return pl.pallas_call(
        _combine_kernel,
        grid_spec=pltpu.PrefetchScalarGridSpec(
            num_scalar_prefetch=4,
            grid=(n // tm,),
            in_specs=[pl.BlockSpec(memory_space=pl.ANY),
                      pl.BlockSpec((tm, LANE), lambda i, *_: (i, 0)),
                      pl.BlockSpec((tm, D_MODEL), lambda i, *_: (i, 0)),
                      pl.BlockSpec((1, 1, D_MODEL), lambda i, *_: (i // per_b, 0, 0)),
                      pl.BlockSpec((1, D_MODEL), lambda i, *_: (0, 0))],
            out_specs=pl.BlockSpec((tm, D_MODEL), lambda i, *_: (i, 0)),
            scratch_shapes=[pltpu.VMEM((2, SLOTS_PER_TILE, D_MODEL), F32), pltpu.SemaphoreType.DMA((2,))]),
        out_shape=jax.ShapeDtypeStruct((n, D_MODEL), F32),
        compiler_params=_params(("arbitrary",)),
        name="combine",
    )(g8f, lofff, gofff, totf, ys, route, x1, g2, fn)


def _pack_w_in(w_in):
    sq, skv = 512, 128
    o_ka, o_va, o_cq = sq, sq + skv, sq + 2 * skv
    o_ckv = o_cq + MLA_Q_RANK
    o_kr = o_ckv + MLA_KV_RANK
    o_ga = o_kr + MLA_ROPE_DIM
    o_gb = o_ga + D_MODEL
    z = lambda w: jnp.zeros((D_MODEL, w), w_in.dtype)
    ka, va = w_in[:, o_ka:o_va], w_in[:, o_va:o_cq]
    dup = lambda t: jnp.concatenate([t[:, :64], t[:, :64], t[:, 64:], t[:, 64:]], axis=1)
    kr1, kr2 = w_in[:, o_kr:o_kr + HALF_ROPE], w_in[:, o_kr + HALF_ROPE:o_ga]
    packed = jnp.concatenate([
        w_in[:, :sq], dup(ka), dup(va), w_in[:, o_cq:o_ckv], w_in[:, o_ckv:o_kr],
        z(64), kr1, kr2, z(32),
        z(64), kr2, kr1, z(32),
        w_in[:, o_ga:o_gb], w_in[:, o_gb:o_gb + D_MODEL]], axis=1)
    return packed.astype(BF16)


def _pack_heads(w, width, pieces):
    rows = w.shape[0]
    w3 = w.reshape(rows, MLA_HEADS, width)
    cols = [w3[:, :, a:b] for a, b in pieces]
    used = sum(b - a for a, b in pieces)
    cols.append(jnp.zeros((rows, MLA_HEADS, LANE - used), w.dtype))
    return jnp.concatenate(cols, axis=2).reshape(rows, MLA_HEADS * LANE)


def kernel(x, c, positions, w_ada, b_ada, norm_mix, norm_ffn, w_in, sinks, q_norm, kv_norm, w_uq, w_uk, w_uv,
           w_branch_a, w_branch_b, w_out, w_router, b_router, w_moe1, b_moe1, w_moe2, b_moe2, final_norm):
    batch, seq, d = x.shape
    n = batch * seq
    assert d == D_MODEL and w_ada.shape[0] == 1
    assert seq % TQ_MLA == 0 and seq % (SWA_QB * ATTN_BLOCK) == 0 and seq % TM_PROJ == 0 and seq % ROUTE_TILE == 0
    hw = MLA_HEADS * LANE
    qk = MLA_NOPE_DIM + MLA_ROPE_DIM

    w_in_p = _pack_w_in(w_in[0])
    wuq_main = _pack_heads(w_uq[0], qk, [(0, qk)])
    w3 = w_uq[0].reshape(MLA_Q_RANK, MLA_HEADS, qk)
    wuq_swap = jnp.concatenate([
        jnp.zeros((MLA_Q_RANK, MLA_HEADS, MLA_NOPE_DIM), F32),
        w3[:, :, MLA_NOPE_DIM + HALF_ROPE:], w3[:, :, MLA_NOPE_DIM:MLA_NOPE_DIM + HALF_ROPE],
        jnp.zeros((MLA_Q_RANK, MLA_HEADS, LANE - qk), F32)], axis=2).reshape(MLA_Q_RANK, hw)
    wuq_p = jnp.concatenate([wuq_main, wuq_swap], axis=1).astype(BF16)
    wuk_p = _pack_heads(w_uk[0], MLA_NOPE_DIM, [(0, MLA_NOPE_DIM)]).astype(BF16)
    wuv_p = _pack_heads(w_uv[0], MLA_V_DIM, [(0, MLA_V_DIM)]).astype(BF16)
    one_row = jnp.tile((jnp.arange(LANE) == MLA_V_DIM).astype(F32), MLA_HEADS)[None, :]
    freqs = ROPE_THETA ** (-jnp.arange(0, MLA_ROPE_DIM, 2, dtype=F32) / MLA_ROPE_DIM)
    frq = jnp.concatenate([jnp.zeros((MLA_NOPE_DIM,), F32), freqs, freqs,
                           jnp.zeros((LANE - qk,), F32)])[None, :]
    sgn = jnp.concatenate([jnp.zeros((MLA_NOPE_DIM,), F32), -jnp.ones((HALF_ROPE,), F32),
                           jnp.ones((HALF_ROPE,), F32), jnp.zeros((LANE - qk,), F32)])[None, :]
    wb3 = w_branch_b[0].reshape(MLA_HEADS, MLA_V_DIM, D_MODEL)
    wb_p = jnp.concatenate([wb3, jnp.zeros((MLA_HEADS, LANE - MLA_V_DIM, D_MODEL), F32)],
                           axis=1).reshape(hw, D_MODEL).astype(BF16)
    wa = w_branch_a[0].astype(BF16)
    wo = w_out[0].astype(BF16)
    wr = jnp.pad(w_router[0], ((0, 0), (0, LANE - N_EXPERTS)))
    wr_hi = wr.astype(BF16)
    wr_lo = (wr - wr_hi.astype(F32)).astype(BF16)
    br_p = jnp.pad(b_router[0], (0, LANE - N_EXPERTS))[None, :]
    b1 = b_moe1[0][:, None, :]
    b2 = b_moe2[0][:, None, :]
    slopes = jnp.asarray(np.exp2(-8.0 * np.arange(1, SWA_HEADS + 1) / SWA_HEADS), dtype=F32)

    c8 = jnp.pad(c, ((0, 8 - batch), (0, 0)))
    mod = _ada(c8, w_ada[0], b_ada[0][None, :])[:batch]
    sh1, sc1, g1, sh2, sc2, g2 = [m[:, None, :] for m in jnp.split(mod, 6, axis=-1)]

    x2 = x.reshape(n, D_MODEL)
    pos2 = positions.reshape(n, 1).astype(jnp.int32)
    qa, ka2, va2, qm, km, vm, ga, gb = _inproj(
        x2, pos2, sh1, sc1, norm_mix, w_in_p, q_norm, kv_norm, wuq_p, wuk_p, wuv_p, frq, sgn, one_row, seq)
    yb = _mla(qm, km, vm, batch, seq)
    ya = _swa(sinks[0], slopes, qa, ka2, va2, batch, seq)

    x1, h2, route, routet, g8 = _outproj(
        ya, yb, ga, gb, x2, g1, sh2, sc2, norm_ffn, wa, wb_p, wo, wr_hi, wr_lo, br_p, seq)

    n_tiles = n // ROUTE_TILE
    g8t = g8.reshape(n_tiles, 8, LANE)[:, 0, :N_EXPERTS].astype(jnp.int32)
    loff = jnp.cumsum(g8t, axis=1) - g8t
    padded = ((jnp.sum(g8t, axis=0) + MOE_ROWS - 1) // MOE_ROWS) * MOE_ROWS
    pends = jnp.cumsum(padded)
    goff = (pends - padded)[None, :] + jnp.cumsum(g8t, axis=0) - g8t
    tot = jnp.sum(g8t, axis=1).astype(jnp.int32)
    p_rows = n * TOP_K + n_tiles * N_EXPERTS * (GROUP_ALIGN - 1) + N_EXPERTS * (MOE_ROWS - 1)
    p_rows = -(-p_rows // MOE_ROWS) * MOE_ROWS
    n_blocks = p_rows // MOE_ROWS
    block_start = jnp.arange(n_blocks, dtype=jnp.int32) * MOE_ROWS
    block_e = jnp.minimum(jnp.sum((pends[None, :] <= block_start[:, None]).astype(jnp.int32), axis=1),
                          N_EXPERTS - 1).astype(jnp.int32)
    n_used = (pends[-1:] // MOE_ROWS).astype(jnp.int32)
    tabs = (g8t.reshape(-1), loff.reshape(-1).astype(jnp.int32), goff.reshape(-1).astype(jnp.int32), tot)

    xs = _dispatch(*tabs, pends.astype(jnp.int32), h2, routet, p_rows)
    ys = _experts(block_e, n_used, xs, w_moe1[0], b1, w_moe2[0], b2)
    out = _combine(*tabs, ys, route, x1, g2, final_norm[None, :], seq)
    return out.reshape(batch, seq, D_MODEL)
```

```python
import functools

import numpy as np
import jax
import jax.numpy as jnp
from jax import lax
from jax.experimental import pallas as pl
from jax.experimental.pallas import tpu as pltpu

D_MODEL = 1024
SWA_HEADS = 8
SWA_KV_HEADS = 2
SWA_HEAD_DIM = 64
ATTN_BLOCK = 128
MLA_HEADS = 8
MLA_Q_RANK = 384
MLA_KV_RANK = 256
MLA_NOPE_DIM = 64
MLA_ROPE_DIM = 32
MLA_V_DIM = 64
ROPE_THETA = 10000.0
N_EXPERTS = 32
TOP_K = 4
D_EXPERT = 1024
SWIGLU_LIMIT = 7.0
SWIGLU_ALPHA = 1.702
NORM_EPS = 1e-6

LANE = 128
LOG2E = 1.4426950408889634
NEG = -1e30
SWA_QSCALE = SWA_HEAD_DIM ** -0.5 * LOG2E
MLA_QSCALE = (MLA_NOPE_DIM + MLA_ROPE_DIM) ** -0.5 * LOG2E
HALF_ROPE = MLA_ROPE_DIM // 2
MLA_PLAIN_BOUND = 60.0
MLA_WIDE = 4

BF16 = jnp.bfloat16
F32 = jnp.float32

TM_PROJ = 512
TQ_MLA = 512
SWA_QB = 4
MOE_ROWS = 512
ROUTE_TILE = 512
GROUP_ALIGN = 8
SLOTS_PER_TILE = ROUTE_TILE * TOP_K + N_EXPERTS * GROUP_ALIGN
SLOT_SUB = 256
SLOT_CHUNK = 3 * SLOT_SUB
XS_COLS = D_MODEL + LANE
VMEM_LIMIT = 56 * 1024 * 1024


def _params(sem, vmem=VMEM_LIMIT):
    return pltpu.CompilerParams(dimension_semantics=sem, vmem_limit_bytes=vmem)


def _rms(x):
    return x * lax.rsqrt(jnp.mean(x * x, axis=-1, keepdims=True) + NORM_EPS)


def _ada_kernel(c_ref, w_ref, b_ref, o_ref):
    c = c_ref[...]
    a = (c * jax.nn.sigmoid(c)).astype(BF16)
    o_ref[...] = jnp.dot(a, w_ref[...].astype(BF16), preferred_element_type=F32) + b_ref[...]


def _ada(c8, w_ada, b_ada):
    n_out = w_ada.shape[1]
    return pl.pallas_call(
        _ada_kernel,
        grid=(n_out // D_MODEL,),
        in_specs=[pl.BlockSpec((8, D_MODEL), lambda j: (0, 0)),
                  pl.BlockSpec((D_MODEL, D_MODEL), lambda j: (0, j)),
                  pl.BlockSpec((1, D_MODEL), lambda j: (0, j))],
        out_specs=pl.BlockSpec((8, D_MODEL), lambda j: (0, j)),
        out_shape=jax.ShapeDtypeStruct((8, n_out), F32),
        compiler_params=_params(("arbitrary",)),
        name="ada",
    )(c8, w_ada, b_ada)


_C_QA = (0, 512)
_C_KV = (512, 768)
_C_LAT = (768, 1536)
_C_GA = (1536, 2560)
_C_GB = (2560, 3584)
_W_IN_COLS = 3584
_ROPE_LO = (MLA_NOPE_DIM, MLA_NOPE_DIM + HALF_ROPE)
_ROPE_HI = (MLA_NOPE_DIM + HALF_ROPE, MLA_NOPE_DIM + MLA_ROPE_DIM)


def _inproj_kernel(x_ref, pos_ref, sh_ref, sc_ref, nm_ref, win_ref, qn_ref, kvn_ref, wuq_ref, wuk_ref,
                   wuv_ref, frq_ref, one_ref,
                   qa_ref, ka_ref, va_ref, qm_ref, km_ref, vm_ref, ga_ref, gb_ref, st_ref):
    tm = x_ref.shape[0]
    x = x_ref[...]
    h = (_rms(x) * nm_ref[...] * (1.0 + sc_ref[0]) + sh_ref[0]).astype(BF16)

    def proj(c):
        return jnp.dot(h, win_ref[:, c[0]:c[1]], preferred_element_type=F32)

    lane = lax.broadcasted_iota(jnp.int32, (tm, LANE), 1)
    first = lane < SWA_HEAD_DIM

    def twice(t):
        r = pltpu.roll(t, SWA_HEAD_DIM, axis=1)
        return jnp.concatenate([jnp.where(first, t, r), jnp.where(first, r, t)], axis=1)

    kv = proj(_C_KV)
    ka_ref[...] = twice(kv[:, :LANE]).astype(BF16)
    va_ref[...] = twice(kv[:, LANE:]).astype(BF16)

    ang = pos_ref[...].astype(F32) * frq_ref[...]
    cs = jnp.cos(ang)
    sn = jnp.sin(ang)
    sn_lo = jnp.where((lane >= _ROPE_LO[0]) & (lane < _ROPE_LO[1]), -sn, 0.0)
    sn_hi = jnp.where((lane >= _ROPE_HI[0]) & (lane < _ROPE_HI[1]), sn, 0.0)

    def rotary(t):
        return (t * cs + pltpu.roll(t, LANE - HALF_ROPE, axis=1) * sn_lo
                + pltpu.roll(t, HALF_ROPE, axis=1) * sn_hi)

    lat = proj(_C_LAT)
    cqn = (_rms(lat[:, 0:MLA_Q_RANK]) * qn_ref[...]).astype(BF16)
    ckvn = (_rms(lat[:, MLA_Q_RANK:MLA_Q_RANK + MLA_KV_RANK]) * kvn_ref[...]).astype(BF16)
    krr = rotary(lat[:, MLA_Q_RANK + MLA_KV_RANK:])
    q = jnp.dot(cqn, wuq_ref[...], preferred_element_type=F32)
    kn = jnp.dot(ckvn, wuk_ref[...], preferred_element_type=F32)

    def max_sq_norm(t):
        tf = t.astype(F32)
        return jnp.max(jnp.sum(tf * tf, axis=1, keepdims=True), axis=0, keepdims=True)

    rid = lax.broadcasted_iota(jnp.int32, (8, LANE), 0)
    lid = lax.broadcasted_iota(jnp.int32, (8, LANE), 1)
    stats = jnp.zeros((8, LANE), F32)
    for hh in range(MLA_HEADS):
        sl = slice(hh * LANE, (hh + 1) * LANE)
        qb = (rotary(q[:, sl]) * MLA_QSCALE).astype(BF16)
        kb = (kn[:, sl] + krr).astype(BF16)
        qm_ref[:, sl] = qb
        km_ref[:, sl] = kb
        stats = jnp.where((rid == hh) & (lid == 0), max_sq_norm(qb), stats)
        stats = jnp.where((rid == hh) & (lid == 1), max_sq_norm(kb), stats)
    st_ref[...] = stats
    vm_ref[...] = (jnp.dot(ckvn, wuv_ref[...], preferred_element_type=F32) + one_ref[...]).astype(BF16)
    qa_ref[...] = (proj(_C_QA) * SWA_QSCALE).astype(BF16)
    ga_ref[...] = jax.nn.sigmoid(proj(_C_GA)).astype(BF16)
    gb_ref[...] = jax.nn.sigmoid(proj(_C_GB)).astype(BF16)


def _inproj(x2, pos2, sh1, sc1, nm, w_in_p, qn, kvn, wuq_p, wuk_p, wuv_p, frq, one_row, seq):
    n = x2.shape[0]
    tm = TM_PROJ
    per_b = seq // tm
    hw = MLA_HEADS * LANE
    row = lambda i: (i, 0)
    fix = lambda i: (0, 0)
    bsel = lambda i: (i // per_b, 0, 0)
    widths = (512, 256, 256, hw, hw, hw, D_MODEL, D_MODEL)
    return pl.pallas_call(
        _inproj_kernel,
        grid=(n // tm,),
        in_specs=[pl.BlockSpec((tm, D_MODEL), row),
                  pl.BlockSpec((tm, 1), row),
                  pl.BlockSpec((1, 1, D_MODEL), bsel),
                  pl.BlockSpec((1, 1, D_MODEL), bsel),
                  pl.BlockSpec((1, D_MODEL), fix),
                  pl.BlockSpec((D_MODEL, _W_IN_COLS), fix),
                  pl.BlockSpec((1, MLA_Q_RANK), fix),
                  pl.BlockSpec((1, MLA_KV_RANK), fix),
                  pl.BlockSpec((MLA_Q_RANK, hw), fix),
                  pl.BlockSpec((MLA_KV_RANK, hw), fix),
                  pl.BlockSpec((MLA_KV_RANK, hw), fix),
                  pl.BlockSpec((1, LANE), fix),
                  pl.BlockSpec((1, hw), fix)],
        out_specs=[pl.BlockSpec((tm, w), row) for w in widths] + [pl.BlockSpec((8, LANE), row)],
        out_shape=[jax.ShapeDtypeStruct((n, w), BF16) for w in widths]
        + [jax.ShapeDtypeStruct((n // tm * 8, LANE), F32)],
        compiler_params=_params(("arbitrary",)),
        name="inproj",
    )(x2, pos2, sh1, sc1, nm, w_in_p, qn, kvn, wuq_p, wuk_p, wuv_p, frq, one_row)


def _mla_kernel(plain_ref, q_ref, k_ref, v_ref, o_ref, m_scr, acc_scr):
    i = pl.program_id(2)
    t = TQ_MLA
    acc_scr[...] = jnp.zeros(acc_scr.shape, F32)

    def scores(j, nblk, diag_at):
        start = pl.multiple_of(j * t, t)
        k = k_ref[pl.ds(start, nblk * t), :]
        v = v_ref[pl.ds(start, nblk * t), :]
        s = lax.dot_general(q_ref[...], k, (((1,), (1,)), ((), ())), preferred_element_type=F32)
        if diag_at is not None:
            qi = lax.broadcasted_iota(jnp.int32, s.shape, 0)
            kj = lax.broadcasted_iota(jnp.int32, s.shape, 1)
            s = jnp.where(kj <= qi + diag_at * t, s, NEG)
        return s, v

    def plain_step(j, nblk, diag_at):
        s, v = scores(j, nblk, diag_at)
        acc_scr[...] += jnp.dot(jnp.exp2(s).astype(BF16), v, preferred_element_type=F32)

    def online_step(j, nblk, diag_at):
        s, v = scores(j, nblk, diag_at)
        m_old = m_scr[...]
        m_new = jnp.maximum(m_old, jnp.max(s, axis=1, keepdims=True))
        p = jnp.exp2(s - m_new)
        alpha = jnp.exp2(m_old - m_new)
        acc_scr[...] = acc_scr[...] * alpha + jnp.dot(p.astype(BF16), v, preferred_element_type=F32)
        m_scr[...] = m_new

    def run(step, wide):
        nw = lax.shift_right_logical(i, wide.bit_length() - 1)
        left = i - nw * wide

        def wide_body(jj, carry):
            step(jj * wide, wide, None)
            return carry

        lax.fori_loop(0, nw, wide_body, 0)
        for r in range(wide):
            pl.when(left == r)(lambda r=r: step(nw * wide, r + 1, r))

    small = plain_ref[pl.program_id(0) * MLA_HEADS + pl.program_id(1)] != 0

    @pl.when(small)
    def _():
        run(plain_step, MLA_WIDE)

    @pl.when(jnp.logical_not(small))
    def _():
        m_scr[...] = jnp.full(m_scr.shape, NEG, F32)
        run(online_step, 1)

    acc = acc_scr[...]
    o_ref[...] = (acc / acc[:, MLA_V_DIM:MLA_V_DIM + 1]).astype(BF16)


def _mla(plain, qm, km, vm, batch, seq):
    n = qm.shape[0]
    t = TQ_MLA
    nq = seq // t
    qmap = lambda b, h, i, *_: (b * nq + i, h)
    kmap = lambda b, h, i, *_: (b, h)
    return pl.pallas_call(
        _mla_kernel,
        grid_spec=pltpu.PrefetchScalarGridSpec(
            num_scalar_prefetch=1,
            grid=(batch, MLA_HEADS, nq),
            in_specs=[pl.BlockSpec((t, LANE), qmap),
                      pl.BlockSpec((seq, LANE), kmap),
                      pl.BlockSpec((seq, LANE), kmap)],
            out_specs=pl.BlockSpec((t, LANE), qmap),
            scratch_shapes=[pltpu.VMEM((t, 1), F32), pltpu.VMEM((t, LANE), F32)]),
        out_shape=jax.ShapeDtypeStruct((n, MLA_HEADS * LANE), BF16),
        compiler_params=_params(("arbitrary", "arbitrary", "arbitrary")),
        name="mla",
    )(plain, qm, km, vm)


def _swa_kernel(sink_ref, slope_ref, q_ref, kc_ref, kp_ref, vc_ref, vp_ref, o_ref, bias_scr):
    b = pl.program_id(0)
    i = pl.program_id(1)
    blk = ATTN_BLOCK

    @pl.when((b == 0) & (i == 0))
    def _():
        qi = lax.broadcasted_iota(jnp.int32, (blk, 2 * blk), 0)
        kj = lax.broadcasted_iota(jnp.int32, (blk, 2 * blk), 1)
        dist = qi - kj + blk
        valid = (dist >= 0) & (dist < blk)
        distf = dist.astype(F32)
        for hd in range(SWA_HEADS):
            bias_scr[hd] = jnp.where(valid, -(slope_ref[hd] * LOG2E) * distf, NEG)

    lane = lax.broadcasted_iota(jnp.int32, (2 * blk, LANE), 1)
    lo = lane < SWA_HEAD_DIM
    kcol = lax.broadcasted_iota(jnp.int32, (blk, 2 * blk), 1)
    first_pen = jnp.where(kcol < blk, jnp.where(i == 0, NEG, 0.0), 0.0)
    zero = jnp.zeros((2 * blk, LANE), BF16)

    for qb in range(SWA_QB):
        if qb == 0:
            kprev, vprev = kp_ref[...], vp_ref[...]
        else:
            kprev = kc_ref[(qb - 1) * blk:qb * blk, :]
            vprev = vc_ref[(qb - 1) * blk:qb * blk, :]
        kcat = jnp.concatenate([kprev, kc_ref[qb * blk:(qb + 1) * blk, :]], axis=0)
        vcat = jnp.concatenate([vprev, vc_ref[qb * blk:(qb + 1) * blk, :]], axis=0)
        for g in range(SWA_KV_HEADS):
            kg = kcat[:, g * LANE:(g + 1) * LANE]
            vg = vcat[:, g * LANE:(g + 1) * LANE]
            halves = ((jnp.where(lo, kg, zero), jnp.where(lo, vg, zero)),
                      (jnp.where(lo, zero, kg), jnp.where(lo, zero, vg)))
            for pp in range(2):
                pr = g * 2 + pp
                qpair = q_ref[qb * blk:(qb + 1) * blk, pr * LANE:(pr + 1) * LANE]
                out = None
                for half in range(2):
                    hd = pr * 2 + half
                    kx, vx = halves[half]
                    s = lax.dot_general(qpair, kx, (((1,), (1,)), ((), ())), preferred_element_type=F32)
                    s = s + bias_scr[hd]
                    if qb == 0:
                        s = s + first_pen
                    sink = sink_ref[hd] * LOG2E
                    m = jnp.maximum(jnp.max(s, axis=1, keepdims=True), sink)
                    p = jnp.exp2(s - m)
                    denom = jnp.sum(p, axis=1, keepdims=True) + jnp.exp2(sink - m)
                    o = jnp.dot(p.astype(BF16), vx, preferred_element_type=F32) * (1.0 / denom)
                    out = o if out is None else out + o
                o_ref[qb * blk:(qb + 1) * blk, pr * LANE:(pr + 1) * LANE] = out.astype(BF16)


def _swa(sinks, slopes, qa, ka2, va2, batch, seq):
    n = qa.shape[0]
    blk = ATTN_BLOCK
    rows = SWA_QB * blk
    steps = seq // rows
    cur = lambda b, i: (b * steps + i, 0)
    prev = lambda b, i: (b * (seq // blk) + jnp.maximum(i * SWA_QB - 1, 0), 0)
    smem = pl.BlockSpec(memory_space=pltpu.SMEM)
    return pl.pallas_call(
        _swa_kernel,
        grid=(batch, steps),
        in_specs=[smem, smem,
                  pl.BlockSpec((rows, 512), cur),
                  pl.BlockSpec((rows, 256), cur),
                  pl.BlockSpec((blk, 256), prev),
                  pl.BlockSpec((rows, 256), cur),
                  pl.BlockSpec((blk, 256), prev)],
        out_specs=pl.BlockSpec((rows, 512), cur),
        out_shape=jax.ShapeDtypeStruct((n, 512), BF16),
        scratch_shapes=[pltpu.VMEM((SWA_HEADS, blk, 2 * blk), F32)],
        compiler_params=_params(("arbitrary", "arbitrary")),
        name="swa",
    )(sinks, slopes, qa, ka2, ka2, va2, va2)


def _outproj_kernel(ya_ref, yb_ref, ga_ref, gb_ref, x_ref, g1_ref, sh_ref, sc_ref, nm_ref, wa_ref, wb_ref,
                    wo_ref, wrh_ref, wrl_ref, br_ref,
                    x1_ref, h2_ref, route_ref, routet_ref, g8_ref):
    tm = x_ref.shape[0]
    a = jnp.dot(ya_ref[...], wa_ref[...], preferred_element_type=F32)
    bm = jnp.dot(yb_ref[...], wb_ref[...], preferred_element_type=F32)
    mixed = (ga_ref[...].astype(F32) * a + gb_ref[...].astype(F32) * bm).astype(BF16)
    x1 = x_ref[...] + g1_ref[0] * jnp.dot(mixed, wo_ref[...], preferred_element_type=F32)
    x1_ref[...] = x1
    h2 = _rms(x1) * nm_ref[...] * (1.0 + sc_ref[0]) + sh_ref[0]
    hi = h2.astype(BF16)
    h2_ref[:, 0:D_MODEL] = hi

    lo = (h2 - hi.astype(F32)).astype(BF16)
    logits = (jnp.dot(hi, wrh_ref[...], preferred_element_type=F32)
              + jnp.dot(lo, wrh_ref[...], preferred_element_type=F32)
              + jnp.dot(hi, wrl_ref[...], preferred_element_type=F32)) + br_ref[...]

    lane = lax.broadcasted_iota(jnp.int32, (tm, LANE), 1).astype(F32)
    work = jnp.where(lane < N_EXPERTS, logits, -jnp.inf)
    vals, idxs = [], []
    for _ in range(TOP_K):
        m = jnp.max(work, axis=1, keepdims=True)
        idx = jnp.min(jnp.where(work == m, lane, float(LANE)), axis=1, keepdims=True)
        vals.append(m)
        idxs.append(idx)
        work = jnp.where(lane == idx, -jnp.inf, work)
    exps = [jnp.exp(v - vals[0]) for v in vals]
    tot = exps[0] + exps[1] + exps[2] + exps[3]

    onehot = jnp.zeros((tm, LANE), F32)
    for idx in idxs:
        onehot = onehot + (lane == idx).astype(F32)
    r = lax.broadcasted_iota(jnp.int32, (tm, tm), 0)
    c = lax.broadcasted_iota(jnp.int32, (tm, tm), 1)
    prefix = jnp.dot((c < r).astype(BF16), onehot.astype(BF16), preferred_element_type=F32)
    groups = jnp.ceil(jnp.sum(onehot, axis=0, keepdims=True) * (1.0 / GROUP_ALIGN))
    er = lax.broadcasted_iota(jnp.int32, (LANE, LANE), 0)
    ec = lax.broadcasted_iota(jnp.int32, (LANE, LANE), 1)
    before = jnp.dot(jnp.broadcast_to(groups, (8, LANE)).astype(BF16), (er < ec).astype(BF16),
                     preferred_element_type=F32)[0:1, :]
    slot_e = before * GROUP_ALIGN + prefix

    route = jnp.zeros((tm, LANE), F32)
    wext = jnp.zeros((tm, LANE), F32)
    for k in range(TOP_K):
        sk = jnp.sum(jnp.where(lane == idxs[k], slot_e, 0.0), axis=1, keepdims=True)
        sub = jnp.floor(sk * (1.0 / SLOT_SUB))
        route = jnp.where(lane == float(k), sk - sub * SLOT_SUB, route)
        route = jnp.where(lane == float(TOP_K + k), sub, route)
        wk = exps[k] / tot
        wk_hi = wk.astype(BF16).astype(F32)
        wext = jnp.where(lane == idxs[k], wk_hi, wext)
        wext = jnp.where(lane == idxs[k] + float(N_EXPERTS), wk - wk_hi, wext)
    h2_ref[:, D_MODEL:XS_COLS] = wext.astype(BF16)
    route_ref[...] = route
    routet_ref[...] = route.T[0:8, :]
    g8_ref[...] = jnp.broadcast_to(groups * GROUP_ALIGN, (8, LANE))


def _outproj(ya, yb, ga, gb, x2, g1, sh2, sc2, nm, wa, wb_p, wo, wr_hi, wr_lo, br_p, seq):
    n = x2.shape[0]
    tm = ROUTE_TILE
    per_b = seq // tm
    hw = MLA_HEADS * LANE
    row = lambda i: (i, 0)
    fix = lambda i: (0, 0)
    bsel = lambda i: (i // per_b, 0, 0)
    return pl.pallas_call(
        _outproj_kernel,
        grid=(n // tm,),
        in_specs=[pl.BlockSpec((tm, 512), row),
                  pl.BlockSpec((tm, hw), row),
                  pl.BlockSpec((tm, D_MODEL), row),
                  pl.BlockSpec((tm, D_MODEL), row),
                  pl.BlockSpec((tm, D_MODEL), row),
                  pl.BlockSpec((1, 1, D_MODEL), bsel),
                  pl.BlockSpec((1, 1, D_MODEL), bsel),
                  pl.BlockSpec((1, 1, D_MODEL), bsel),
                  pl.BlockSpec((1, D_MODEL), fix),
                  pl.BlockSpec((512, D_MODEL), fix),
                  pl.BlockSpec((hw, D_MODEL), fix),
                  pl.BlockSpec((D_MODEL, D_MODEL), fix),
                  pl.BlockSpec((D_MODEL, LANE), fix),
                  pl.BlockSpec((D_MODEL, LANE), fix),
                  pl.BlockSpec((1, LANE), fix)],
        out_specs=[pl.BlockSpec((tm, D_MODEL), row),
                   pl.BlockSpec((tm, XS_COLS), row),
                   pl.BlockSpec((tm, LANE), row),
                   pl.BlockSpec((8, tm), lambda i: (0, i)),
                   pl.BlockSpec((8, LANE), row)],
        out_shape=[jax.ShapeDtypeStruct((n, D_MODEL), F32),
                   jax.ShapeDtypeStruct((n, XS_COLS), BF16),
                   jax.ShapeDtypeStruct((n, LANE), F32),
                   jax.ShapeDtypeStruct((8, n), F32),
                   jax.ShapeDtypeStruct((n // tm * 8, LANE), F32)],
        compiler_params=_params(("arbitrary",)),
        name="outproj",
    )(ya, yb, ga, gb, x2, g1, sh2, sc2, nm, wa, wb_p, wo, wr_hi, wr_lo, br_p)


def _start_runs(tile, g8_ref, loff_ref, goff_ref, make_copy):
    for e in range(N_EXPERTS):
        g = g8_ref[tile * N_EXPERTS + e]
        lo = loff_ref[tile * N_EXPERTS + e]
        go = goff_ref[tile * N_EXPERTS + e]
        def go_copy(lo=lo, go=go, g=g):
            make_copy(pl.multiple_of(lo, GROUP_ALIGN), pl.multiple_of(go, GROUP_ALIGN),
                      pl.multiple_of(g, GROUP_ALIGN)).start()

        pl.when(g > 0)(go_copy)


def _dispatch_kernel(g8_ref, loff_ref, goff_ref, tot_ref, ends_ref, h_ref, rt_ref, xs_ref, buf, zbuf, sem, zsem):
    t = pl.program_id(0)
    nt = pl.num_programs(0)
    tm = h_ref.shape[0]
    slot = lax.rem(t, 2)

    @pl.when(t == 0)
    def _():
        zbuf[...] = jnp.zeros(zbuf.shape, zbuf.dtype)

        def fill(start):
            return pltpu.make_async_copy(zbuf, xs_ref.at[pl.ds(pl.multiple_of(start, MOE_ROWS), MOE_ROWS), :], zsem)

        def nonempty(e):
            return ends_ref[e] > (ends_ref[e - 1] if e else 0)

        used = lax.shift_right_logical(ends_ref[N_EXPERTS - 1], MOE_ROWS.bit_length() - 1)
        n_blocks = xs_ref.shape[0] // MOE_ROWS
        for e in range(N_EXPERTS):
            pl.when(nonempty(e))(lambda e=e: fill(ends_ref[e] - MOE_ROWS).start())
        lax.fori_loop(used, n_blocks, lambda b, c: (fill(b * MOE_ROWS).start(), c)[1], 0)
        for e in range(N_EXPERTS):
            pl.when(nonempty(e))(lambda e=e: fill(ends_ref[e] - MOE_ROWS).wait())
        lax.fori_loop(used, n_blocks, lambda b, c: (fill(b * MOE_ROWS).wait(), c)[1], 0)

    hb = h_ref[...]
    low = [rt_ref[k:k + 1, :] for k in range(TOP_K)]
    sub = [rt_ref[TOP_K + k:TOP_K + k + 1, :] for k in range(TOP_K)]
    srow = lax.broadcasted_iota(jnp.int32, (SLOT_SUB, tm), 0).astype(F32).astype(BF16)
    one = jnp.ones((SLOT_SUB, tm), BF16)

    for ci in range(SLOTS_PER_TILE // SLOT_CHUNK):
        parts = []
        for b in range(SLOT_CHUNK // SLOT_SUB):
            blk = float(ci * (SLOT_CHUNK // SLOT_SUB) + b)
            own = jnp.zeros((SLOT_SUB, tm), BF16)
            for k in range(TOP_K):
                here = jnp.where(sub[k] == blk, low[k], -1.0).astype(BF16)
                own = jnp.where(srow == here, one, own)
            parts.append(own)
        own = jnp.concatenate(parts, axis=0)
        buf[slot, ci * SLOT_CHUNK:(ci + 1) * SLOT_CHUNK, :] = jnp.dot(own, hb, preferred_element_type=F32)

    def copy(s, lo, go, rows):
        return pltpu.make_async_copy(buf.at[s, pl.ds(lo, rows), :], xs_ref.at[pl.ds(go, rows), :], sem.at[s])

    _start_runs(t, g8_ref, loff_ref, goff_ref, functools.partial(copy, slot))

    def drain(tile, s):
        rows = pl.multiple_of(tot_ref[tile], GROUP_ALIGN)
        copy(s, 0, 0, rows).wait()

    pl.when(t > 0)(lambda: drain(t - 1, 1 - slot))
    pl.when(t == nt - 1)(lambda: drain(t, slot))


def _dispatch(g8f, lofff, gofff, totf, ends, h2, routet, p_rows):
    n = h2.shape[0]
    tm = ROUTE_TILE
    return pl.pallas_call(
        _dispatch_kernel,
        grid_spec=pltpu.PrefetchScalarGridSpec(
            num_scalar_prefetch=5,
            grid=(n // tm,),
            in_specs=[pl.BlockSpec((tm, XS_COLS), lambda i, *_: (i, 0)),
                      pl.BlockSpec((8, tm), lambda i, *_: (0, i))],
            out_specs=pl.BlockSpec(memory_space=pl.ANY),
            scratch_shapes=[pltpu.VMEM((2, SLOTS_PER_TILE, XS_COLS), F32),
                            pltpu.VMEM((MOE_ROWS, XS_COLS), F32),
                            pltpu.SemaphoreType.DMA((2,)), pltpu.SemaphoreType.DMA(())]),
        out_shape=jax.ShapeDtypeStruct((p_rows, XS_COLS), F32),
        compiler_params=_params(("arbitrary",)),
        name="dispatch",
    )(g8f, lofff, gofff, totf, ends, h2, routet)


def _experts_kernel(be_ref, nu_ref, xs_ref, w1_ref, b1_ref, w2_ref, b2_ref, ys_ref):
    @pl.when(pl.program_id(0) < nu_ref[0])
    def _():
        xb = xs_ref[:, 0:D_MODEL].astype(BF16)
        e = be_ref[pl.program_id(0)]
        wext = xs_ref[:, D_MODEL:XS_COLS]
        lane = lax.broadcasted_iota(jnp.int32, wext.shape, 1)
        wcol = jnp.sum(jnp.where((lane == e) | (lane == e + N_EXPERTS), wext, 0.0), axis=1, keepdims=True)
        hcat = jnp.dot(xb, w1_ref[0].astype(BF16), preferred_element_type=F32) + b1_ref[0]
        x_glu = jnp.minimum(hcat[:, :D_EXPERT], SWIGLU_LIMIT)
        x_lin = jnp.clip(hcat[:, D_EXPERT:], -SWIGLU_LIMIT, SWIGLU_LIMIT)
        act = (x_glu * jax.nn.sigmoid(SWIGLU_ALPHA * x_glu) * (x_lin + 1.0)).astype(BF16)
        ys_ref[...] = (jnp.dot(act, w2_ref[0].astype(BF16), preferred_element_type=F32) + b2_ref[0]) * wcol

    @pl.when(pl.program_id(0) >= nu_ref[0])
    def _():
        ys_ref[...] = jnp.zeros(ys_ref.shape, F32)


def _experts(block_e, n_used, xs, w1, b1, w2, b2):
    p = xs.shape[0]
    mb = MOE_ROWS
    last = lambda i, nu: jnp.minimum(i, nu[0] - 1)
    rowmap = lambda i, be, nu: (last(i, nu), 0)
    wmap = lambda i, be, nu: (be[last(i, nu)], 0, 0)
    return pl.pallas_call(
        _experts_kernel,
        grid_spec=pltpu.PrefetchScalarGridSpec(
            num_scalar_prefetch=2,
            grid=(p // mb,),
            in_specs=[pl.BlockSpec((mb, XS_COLS), rowmap),
                      pl.BlockSpec((1, D_MODEL, 2 * D_EXPERT), wmap),
                      pl.BlockSpec((1, 1, 2 * D_EXPERT), wmap),
                      pl.BlockSpec((1, D_EXPERT, D_MODEL), wmap),
                      pl.BlockSpec((1, 1, D_MODEL), wmap)],
            out_specs=pl.BlockSpec((mb, D_MODEL), lambda i, be, nu: (i, 0))),
        out_shape=jax.ShapeDtypeStruct((p, D_MODEL), F32),
        compiler_params=_params(("arbitrary",)),
        name="experts",
    )(block_e, n_used, xs, w1, b1, w2, b2)


def _combine_kernel(g8_ref, loff_ref, goff_ref, tot_ref, ys_ref, route_ref, x1_ref, g2_ref, fn_ref, o_ref, stg, sem):
    t = pl.program_id(0)
    nt = pl.num_programs(0)
    tm = x1_ref.shape[0]
    slot = lax.rem(t, 2)

    def copy(s, lo, go, rows):
        return pltpu.make_async_copy(ys_ref.at[pl.ds(go, rows), :], stg.at[s, pl.ds(lo, rows), :], sem.at[s])

    @pl.when(t == 0)
    def _():
        stg[...] = jnp.zeros(stg.shape, F32)
        _start_runs(t, g8_ref, loff_ref, goff_ref, functools.partial(copy, 0))

    pl.when(t + 1 < nt)(lambda: _start_runs(t + 1, g8_ref, loff_ref, goff_ref, functools.partial(copy, 1 - slot)))
    copy(slot, 0, 0, pl.multiple_of(tot_ref[t], GROUP_ALIGN)).wait()

    route = route_ref[...]
    low = [jnp.broadcast_to(route[:, k:k + 1], (tm, SLOT_SUB)).astype(BF16) for k in range(TOP_K)]
    sub = [jnp.broadcast_to(route[:, TOP_K + k:TOP_K + k + 1], (tm, SLOT_SUB)).astype(BF16) for k in range(TOP_K)]
    scol = lax.broadcasted_iota(jnp.int32, (tm, SLOT_SUB), 1).astype(F32).astype(BF16)
    one = jnp.ones((tm, SLOT_SUB), BF16)
    never = jnp.full((tm, SLOT_SUB), -1.0, BF16)

    y = None
    for ci in range(SLOTS_PER_TILE // SLOT_CHUNK):
        parts = []
        for b in range(SLOT_CHUNK // SLOT_SUB):
            blk = float(ci * (SLOT_CHUNK // SLOT_SUB) + b)
            own = jnp.zeros((tm, SLOT_SUB), BF16)
            for k in range(TOP_K):
                own = jnp.where(scol == jnp.where(sub[k] == blk, low[k], never), one, own)
            parts.append(own)
        own = jnp.concatenate(parts, axis=1)
        rows = stg[slot, ci * SLOT_CHUNK:(ci + 1) * SLOT_CHUNK, :].astype(BF16)
        part = jnp.dot(own, rows, preferred_element_type=F32)
        y = part if y is None else y + part
    x = x1_ref[...] + g2_ref[0] * y
    o_ref[...] = _rms(x) * fn_ref[...]


def _combine(g8f, lofff, gofff, totf, ys, route, x1, g2, fn, seq):
    n = x1.shape[0]
    tm = ROUTE_TILE
    per_b = seq // tm
    return pl.pallas_call(
        _combine_kernel,
        grid_spec=pltpu.PrefetchScalarGridSpec(
            num_scalar_prefetch=4,
            grid=(n // tm,),
            in_specs=[pl.BlockSpec(memory_space=pl.ANY),
                      pl.BlockSpec((tm, LANE), lambda i, *_: (i, 0)),
                      pl.BlockSpec((tm, D_MODEL), lambda i, *_: (i, 0)),
                      pl.BlockSpec((1, 1, D_MODEL), lambda i, *_: (i // per_b, 0, 0)),
                      pl.BlockSpec((1, D_MODEL), lambda i, *_: (0, 0))],
            out_specs=pl.BlockSpec((tm, D_MODEL), lambda i, *_: (i, 0)),
            scratch_shapes=[pltpu.VMEM((2, SLOTS_PER_TILE, D_MODEL), F32), pltpu.SemaphoreType.DMA((2,))]),
        out_shape=jax.ShapeDtypeStruct((n, D_MODEL), F32),
        compiler_params=_params(("arbitrary",)),
        name="combine",
    )(g8f, lofff, gofff, totf, ys, route, x1, g2, fn)


def _pack_w_in(w_in):
    o_kr = 512 + 2 * 128 + MLA_Q_RANK + MLA_KV_RANK
    z = lambda w: jnp.zeros((D_MODEL, w), w_in.dtype)
    packed = jnp.concatenate([
        w_in[:, :o_kr], z(MLA_NOPE_DIM), w_in[:, o_kr:o_kr + MLA_ROPE_DIM],
        z(LANE - MLA_NOPE_DIM - MLA_ROPE_DIM), w_in[:, o_kr + MLA_ROPE_DIM:]], axis=1)
    return packed.astype(BF16)


def _pack_heads(w, width, pieces):
    rows = w.shape[0]
    w3 = w.reshape(rows, MLA_HEADS, width)
    cols = [w3[:, :, a:b] for a, b in pieces]
    used = sum(b - a for a, b in pieces)
    cols.append(jnp.zeros((rows, MLA_HEADS, LANE - used), w.dtype))
    return jnp.concatenate(cols, axis=2).reshape(rows, MLA_HEADS * LANE)


def kernel(x, c, positions, w_ada, b_ada, norm_mix, norm_ffn, w_in, sinks, q_norm, kv_norm, w_uq, w_uk, w_uv,
           w_branch_a, w_branch_b, w_out, w_router, b_router, w_moe1, b_moe1, w_moe2, b_moe2, final_norm):
    batch, seq, d = x.shape
    n = batch * seq
    assert d == D_MODEL and w_ada.shape[0] == 1
    assert seq % TQ_MLA == 0 and seq % (SWA_QB * ATTN_BLOCK) == 0 and seq % TM_PROJ == 0 and seq % ROUTE_TILE == 0
    hw = MLA_HEADS * LANE
    qk = MLA_NOPE_DIM + MLA_ROPE_DIM

    w_in_p = _pack_w_in(w_in[0])
    wuq_p = _pack_heads(w_uq[0], qk, [(0, qk)]).astype(BF16)
    wuk_p = _pack_heads(w_uk[0], MLA_NOPE_DIM, [(0, MLA_NOPE_DIM)]).astype(BF16)
    wuv_p = _pack_heads(w_uv[0], MLA_V_DIM, [(0, MLA_V_DIM)]).astype(BF16)
    one_row = jnp.tile((jnp.arange(LANE) == MLA_V_DIM).astype(F32), MLA_HEADS)[None, :]
    freqs = ROPE_THETA ** (-jnp.arange(0, MLA_ROPE_DIM, 2, dtype=F32) / MLA_ROPE_DIM)
    frq = jnp.concatenate([jnp.zeros((MLA_NOPE_DIM,), F32), freqs, freqs,
                           jnp.zeros((LANE - qk,), F32)])[None, :]
    wb3 = w_branch_b[0].reshape(MLA_HEADS, MLA_V_DIM, D_MODEL)
    wb_p = jnp.concatenate([wb3, jnp.zeros((MLA_HEADS, LANE - MLA_V_DIM, D_MODEL), F32)],
                           axis=1).reshape(hw, D_MODEL).astype(BF16)
    wa = w_branch_a[0].astype(BF16)
    wo = w_out[0].astype(BF16)
    wr = jnp.pad(w_router[0], ((0, 0), (0, LANE - N_EXPERTS)))
    wr_hi = wr.astype(BF16)
    wr_lo = (wr - wr_hi.astype(F32)).astype(BF16)
    br_p = jnp.pad(b_router[0], (0, LANE - N_EXPERTS))[None, :]
    b1 = b_moe1[0][:, None, :]
    b2 = b_moe2[0][:, None, :]
    slopes = jnp.asarray(np.exp2(-8.0 * np.arange(1, SWA_HEADS + 1) / SWA_HEADS), dtype=F32)

    c8 = jnp.pad(c, ((0, 8 - batch), (0, 0)))
    mod = _ada(c8, w_ada[0], b_ada[0][None, :])[:batch]
    sh1, sc1, g1, sh2, sc2, g2 = [m[:, None, :] for m in jnp.split(mod, 6, axis=-1)]

    x2 = x.reshape(n, D_MODEL)
    pos2 = positions.reshape(n, 1).astype(jnp.int32)
    qa, ka2, va2, qm, km, vm, ga, gb, stats = _inproj(
        x2, pos2, sh1, sc1, norm_mix, w_in_p, q_norm, kv_norm, wuq_p, wuk_p, wuv_p, frq, one_row, seq)
    st = stats.reshape(batch, seq // TM_PROJ, 8, LANE)
    bound_sq = jnp.max(st[:, :, :, 0], axis=1) * jnp.max(st[:, :, :, 1], axis=1)
    plain = (bound_sq <= MLA_PLAIN_BOUND ** 2).astype(jnp.int32).reshape(-1)
    yb = _mla(plain, qm, km, vm, batch, seq)
    ya = _swa(sinks[0], slopes, qa, ka2, va2, batch, seq)

    x1, h2, route, routet, g8 = _outproj(
        ya, yb, ga, gb, x2, g1, sh2, sc2, norm_ffn, wa, wb_p, wo, wr_hi, wr_lo, br_p, seq)

    n_tiles = n // ROUTE_TILE
    g8t = g8.reshape(n_tiles, 8, LANE)[:, 0, :N_EXPERTS].astype(jnp.int32)
    loff = jnp.cumsum(g8t, axis=1) - g8t
    padded = ((jnp.sum(g8t, axis=0) + MOE_ROWS - 1) // MOE_ROWS) * MOE_ROWS
    pends = jnp.cumsum(padded)
    goff = (pends - padded)[None, :] + jnp.cumsum(g8t, axis=0) - g8t
    tot = jnp.sum(g8t, axis=1).astype(jnp.int32)
    p_rows = n * TOP_K + n_tiles * N_EXPERTS * (GROUP_ALIGN - 1) + N_EXPERTS * (MOE_ROWS - 1)
    p_rows = -(-p_rows // MOE_ROWS) * MOE_ROWS
    n_blocks = p_rows // MOE_ROWS
    block_start = jnp.arange(n_blocks, dtype=jnp.int32) * MOE_ROWS
    block_e = jnp.minimum(jnp.sum((pends[None, :] <= block_start[:, None]).astype(jnp.int32), axis=1),
                          N_EXPERTS - 1).astype(jnp.int32)
    n_used = (pends[-1:] // MOE_ROWS).astype(jnp.int32)
    tabs = (g8t.reshape(-1), loff.reshape(-1).astype(jnp.int32), goff.reshape(-1).astype(jnp.int32), tot)

    xs = _dispatch(*tabs, pends.astype(jnp.int32), h2, routet, p_rows)
    ys = _experts(block_e, n_used, xs, w_moe1[0], b1, w_moe2[0], b2)
    out = _combine(*tabs, ys, route, x1, g2, final_norm[None, :], seq)
    return out.reshape(batch, seq, D_MODEL)
```

```python
import functools

import numpy as np
import jax
import jax.numpy as jnp
from jax import lax
from jax.experimental import pallas as pl
from jax.experimental.pallas import tpu as pltpu

D_MODEL = 1024
SWA_HEADS = 8
SWA_KV_HEADS = 2
SWA_HEAD_DIM = 64
ATTN_BLOCK = 128
MLA_HEADS = 8
MLA_Q_RANK = 384
MLA_KV_RANK = 256
MLA_NOPE_DIM = 64
MLA_ROPE_DIM = 32
MLA_V_DIM = 64
ROPE_THETA = 10000.0
N_EXPERTS = 32
TOP_K = 4
D_EXPERT = 1024
SWIGLU_LIMIT = 7.0
SWIGLU_ALPHA = 1.702
NORM_EPS = 1e-6

LANE = 128
LOG2E = 1.4426950408889634
NEG = -1e30
SWA_QSCALE = SWA_HEAD_DIM ** -0.5 * LOG2E
MLA_QSCALE = (MLA_NOPE_DIM + MLA_ROPE_DIM) ** -0.5 * LOG2E
HALF_ROPE = MLA_ROPE_DIM // 2
MLA_PLAIN_BOUND = 60.0
MLA_WIDE = 4

BF16 = jnp.bfloat16
F32 = jnp.float32

TM_PROJ = 512
TQ_MLA = 512
SWA_QB = 4
MOE_ROWS = 512
ROUTE_TILE = 512
GROUP_ALIGN = 8
SLOTS_PER_TILE = ROUTE_TILE * TOP_K + N_EXPERTS * GROUP_ALIGN
SLOT_SUB = 256
SLOT_CHUNK = 3 * SLOT_SUB
XS_COLS = D_MODEL + LANE
VMEM_LIMIT = 56 * 1024 * 1024


def _params(sem, vmem=VMEM_LIMIT):
    return pltpu.CompilerParams(dimension_semantics=sem, vmem_limit_bytes=vmem)


def _rms(x):
    return x * lax.rsqrt(jnp.mean(x * x, axis=-1, keepdims=True) + NORM_EPS)


def _ada_kernel(c_ref, w_ref, b_ref, o_ref):
    c = c_ref[...]
    a = (c * jax.nn.sigmoid(c)).astype(BF16)
    o_ref[...] = jnp.dot(a, w_ref[...].astype(BF16), preferred_element_type=F32) + b_ref[...]


def _ada(c8, w_ada, b_ada):
    n_out = w_ada.shape[1]
    return pl.pallas_call(
        _ada_kernel,
        grid=(n_out // D_MODEL,),
        in_specs=[pl.BlockSpec((8, D_MODEL), lambda j: (0, 0)),
                  pl.BlockSpec((D_MODEL, D_MODEL), lambda j: (0, j)),
                  pl.BlockSpec((1, D_MODEL), lambda j: (0, j))],
        out_specs=pl.BlockSpec((8, D_MODEL), lambda j: (0, j)),
        out_shape=jax.ShapeDtypeStruct((8, n_out), F32),
        compiler_params=_params(("arbitrary",)),
        name="ada",
    )(c8, w_ada, b_ada)


_C_QA = (0, 512)
_C_KV = (512, 768)
_C_LAT = (768, 1536)
_C_GA = (1536, 2560)
_C_GB = (2560, 3584)
_W_IN_COLS = 3584
_ROPE_LO = (MLA_NOPE_DIM, MLA_NOPE_DIM + HALF_ROPE)
_ROPE_HI = (MLA_NOPE_DIM + HALF_ROPE, MLA_NOPE_DIM + MLA_ROPE_DIM)


def _inproj_kernel(x_ref, pos_ref, sh_ref, sc_ref, nm_ref, win_ref, qn_ref, kvn_ref, wuq_ref, wuk_ref,
                   wuv_ref, frq_ref, one_ref,
                   qa_ref, ka_ref, va_ref, qm_ref, km_ref, vm_ref, ga_ref, gb_ref, st_ref):
    tm = x_ref.shape[0]
    x = x_ref[...]
    h = (_rms(x) * nm_ref[...] * (1.0 + sc_ref[0]) + sh_ref[0]).astype(BF16)

    def proj(c):
        return jnp.dot(h, win_ref[:, c[0]:c[1]], preferred_element_type=F32)

    lane = lax.broadcasted_iota(jnp.int32, (tm, LANE), 1)
    first = lane < SWA_HEAD_DIM

    def twice(t):
        r = pltpu.roll(t, SWA_HEAD_DIM, axis=1)
        return jnp.concatenate([jnp.where(first, t, r), jnp.where(first, r, t)], axis=1)

    kv = proj(_C_KV)
    ka_ref[...] = twice(kv[:, :LANE]).astype(BF16)
    va_ref[...] = twice(kv[:, LANE:]).astype(BF16)

    ang = pos_ref[...].astype(F32) * frq_ref[...]
    cs = jnp.cos(ang)
    sn = jnp.sin(ang)
    sn_lo = jnp.where((lane >= _ROPE_LO[0]) & (lane < _ROPE_LO[1]), -sn, 0.0)
    sn_hi = jnp.where((lane >= _ROPE_HI[0]) & (lane < _ROPE_HI[1]), sn, 0.0)

    def rotary(t):
        return (t * cs + pltpu.roll(t, LANE - HALF_ROPE, axis=1) * sn_lo
                + pltpu.roll(t, HALF_ROPE, axis=1) * sn_hi)

    lat = proj(_C_LAT)
    cqn = (_rms(lat[:, 0:MLA_Q_RANK]) * qn_ref[...]).astype(BF16)
    ckvn = (_rms(lat[:, MLA_Q_RANK:MLA_Q_RANK + MLA_KV_RANK]) * kvn_ref[...]).astype(BF16)
    krr = rotary(lat[:, MLA_Q_RANK + MLA_KV_RANK:])
    q = jnp.dot(cqn, wuq_ref[...], preferred_element_type=F32)
    kn = jnp.dot(ckvn, wuk_ref[...], preferred_element_type=F32)

    def max_sq_norm(t):
        tf = t.astype(F32)
        return jnp.max(jnp.sum(tf * tf, axis=1, keepdims=True), axis=0, keepdims=True)

    rid = lax.broadcasted_iota(jnp.int32, (8, LANE), 0)
    lid = lax.broadcasted_iota(jnp.int32, (8, LANE), 1)
    stats = jnp.zeros((8, LANE), F32)
    for hh in range(MLA_HEADS):
        sl = slice(hh * LANE, (hh + 1) * LANE)
        qb = (rotary(q[:, sl]) * MLA_QSCALE).astype(BF16)
        kb = (kn[:, sl] + krr).astype(BF16)
        qm_ref[:, sl] = qb
        km_ref[:, sl] = kb
        stats = jnp.where((rid == hh) & (lid == 0), max_sq_norm(qb), stats)
        stats = jnp.where((rid == hh) & (lid == 1), max_sq_norm(kb), stats)
    st_ref[...] = stats
    vm_ref[...] = (jnp.dot(ckvn, wuv_ref[...], preferred_element_type=F32) + one_ref[...]).astype(BF16)
    qa_ref[...] = (proj(_C_QA) * SWA_QSCALE).astype(BF16)
    ga_ref[...] = jax.nn.sigmoid(proj(_C_GA)).astype(BF16)
    gb_ref[...] = jax.nn.sigmoid(proj(_C_GB)).astype(BF16)


def _inproj(x2, pos2, sh1, sc1, nm, w_in_p, qn, kvn, wuq_p, wuk_p, wuv_p, frq, one_row, seq):
    n = x2.shape[0]
    tm = TM_PROJ
    per_b = seq // tm
    hw = MLA_HEADS * LANE
    row = lambda i: (i, 0)
    fix = lambda i: (0, 0)
    bsel = lambda i: (i // per_b, 0, 0)
    widths = (512, 256, 256, hw, hw, hw, D_MODEL, D_MODEL)
    return pl.pallas_call(
        _inproj_kernel,
        grid=(n // tm,),
        in_specs=[pl.BlockSpec((tm, D_MODEL), row),
                  pl.BlockSpec((tm, 1), row),
                  pl.BlockSpec((1, 1, D_MODEL), bsel),
                  pl.BlockSpec((1, 1, D_MODEL), bsel),
                  pl.BlockSpec((1, D_MODEL), fix),
                  pl.BlockSpec((D_MODEL, _W_IN_COLS), fix),
                  pl.BlockSpec((1, MLA_Q_RANK), fix),
                  pl.BlockSpec((1, MLA_KV_RANK), fix),
                  pl.BlockSpec((MLA_Q_RANK, hw), fix),
                  pl.BlockSpec((MLA_KV_RANK, hw), fix),
                  pl.BlockSpec((MLA_KV_RANK, hw), fix),
                  pl.BlockSpec((1, LANE), fix),
                  pl.BlockSpec((1, hw), fix)],
        out_specs=[pl.BlockSpec((tm, w), row) for w in widths] + [pl.BlockSpec((8, LANE), row)],
        out_shape=[jax.ShapeDtypeStruct((n, w), BF16) for w in widths]
        + [jax.ShapeDtypeStruct((n // tm * 8, LANE), F32)],
        compiler_params=_params(("arbitrary",)),
        name="inproj",
    )(x2, pos2, sh1, sc1, nm, w_in_p, qn, kvn, wuq_p, wuk_p, wuv_p, frq, one_row)


def _mla_kernel(plain_ref, q_ref, k_ref, v_ref, o_ref, m_scr, acc_scr):
    t = TQ_MLA
    nq = q_ref.shape[0] // t

    def scores(i, j, nblk, diag_at):
        q = q_ref[pl.ds(pl.multiple_of(i * t, t), t), :]
        start = pl.multiple_of(j * t, t)
        k = k_ref[pl.ds(start, nblk * t), :]
        v = v_ref[pl.ds(start, nblk * t), :]
        s = lax.dot_general(q, k, (((1,), (1,)), ((), ())), preferred_element_type=F32)
        if diag_at is not None:
            qi = lax.broadcasted_iota(jnp.int32, s.shape, 0)
            kj = lax.broadcasted_iota(jnp.int32, s.shape, 1)
            s = jnp.where(kj <= qi + diag_at * t, s, NEG)
        return s, v

    def plain_step(i, j, nblk, diag_at):
        s, v = scores(i, j, nblk, diag_at)
        acc_scr[...] += jnp.dot(jnp.exp2(s).astype(BF16), v, preferred_element_type=F32)

    def online_step(i, j, nblk, diag_at):
        s, v = scores(i, j, nblk, diag_at)
        m_old = m_scr[...]
        m_new = jnp.maximum(m_old, jnp.max(s, axis=1, keepdims=True))
        p = jnp.exp2(s - m_new)
        alpha = jnp.exp2(m_old - m_new)
        acc_scr[...] = acc_scr[...] * alpha + jnp.dot(p.astype(BF16), v, preferred_element_type=F32)
        m_scr[...] = m_new

    def run(step, wide, online):
        def q_tile(i, carry):
            acc_scr[...] = jnp.zeros(acc_scr.shape, F32)
            if online:
                m_scr[...] = jnp.full(m_scr.shape, NEG, F32)
            nw = lax.shift_right_logical(i, wide.bit_length() - 1)
            left = i - nw * wide

            def wide_body(jj, c):
                step(i, jj * wide, wide, None)
                return c

            lax.fori_loop(0, nw, wide_body, 0)
            for r in range(wide):
                pl.when(left == r)(lambda r=r: step(i, nw * wide, r + 1, r))
            acc = acc_scr[...]
            o_ref[pl.ds(pl.multiple_of(i * t, t), t), :] = (acc / acc[:, MLA_V_DIM:MLA_V_DIM + 1]).astype(BF16)
            return carry

        lax.fori_loop(0, nq, q_tile, 0)

    small = plain_ref[pl.program_id(0) * MLA_HEADS + pl.program_id(1)] != 0
    pl.when(small)(lambda: run(plain_step, MLA_WIDE, False))
    pl.when(jnp.logical_not(small))(lambda: run(online_step, 1, True))


def _mla(plain, qm, km, vm, batch, seq):
    n = qm.shape[0]
    t = TQ_MLA
    kmap = lambda b, h, *_: (b, h)
    return pl.pallas_call(
        _mla_kernel,
        grid_spec=pltpu.PrefetchScalarGridSpec(
            num_scalar_prefetch=1,
            grid=(batch, MLA_HEADS),
            in_specs=[pl.BlockSpec((seq, LANE), kmap),
                      pl.BlockSpec((seq, LANE), kmap),
                      pl.BlockSpec((seq, LANE), kmap)],
            out_specs=pl.BlockSpec((seq, LANE), kmap),
            scratch_shapes=[pltpu.VMEM((t, 1), F32), pltpu.VMEM((t, LANE), F32)]),
        out_shape=jax.ShapeDtypeStruct((n, MLA_HEADS * LANE), BF16),
        compiler_params=_params(("arbitrary", "arbitrary")),
        name="mla",
    )(plain, qm, km, vm)


def _swa_kernel(sink_ref, slope_ref, q_ref, kc_ref, kp_ref, vc_ref, vp_ref, o_ref, bias_scr):
    b = pl.program_id(0)
    i = pl.program_id(1)
    blk = ATTN_BLOCK

    @pl.when((b == 0) & (i == 0))
    def _():
        qi = lax.broadcasted_iota(jnp.int32, (blk, 2 * blk), 0)
        kj = lax.broadcasted_iota(jnp.int32, (blk, 2 * blk), 1)
        dist = qi - kj + blk
        valid = (dist >= 0) & (dist < blk)
        distf = dist.astype(F32)
        for hd in range(SWA_HEADS):
            bias_scr[hd] = jnp.where(valid, -(slope_ref[hd] * LOG2E) * distf, NEG)

    lane = lax.broadcasted_iota(jnp.int32, (2 * blk, LANE), 1)
    lo = lane < SWA_HEAD_DIM
    kcol = lax.broadcasted_iota(jnp.int32, (blk, 2 * blk), 1)
    first_pen = jnp.where(kcol < blk, jnp.where(i == 0, NEG, 0.0), 0.0)
    zero = jnp.zeros((2 * blk, LANE), BF16)

    for qb in range(SWA_QB):
        if qb == 0:
            kprev, vprev = kp_ref[...], vp_ref[...]
        else:
            kprev = kc_ref[(qb - 1) * blk:qb * blk, :]
            vprev = vc_ref[(qb - 1) * blk:qb * blk, :]
        kcat = jnp.concatenate([kprev, kc_ref[qb * blk:(qb + 1) * blk, :]], axis=0)
        vcat = jnp.concatenate([vprev, vc_ref[qb * blk:(qb + 1) * blk, :]], axis=0)
        for g in range(SWA_KV_HEADS):
            kg = kcat[:, g * LANE:(g + 1) * LANE]
            vg = vcat[:, g * LANE:(g + 1) * LANE]
            halves = ((jnp.where(lo, kg, zero), jnp.where(lo, vg, zero)),
                      (jnp.where(lo, zero, kg), jnp.where(lo, zero, vg)))
            for pp in range(2):
                pr = g * 2 + pp
                qpair = q_ref[qb * blk:(qb + 1) * blk, pr * LANE:(pr + 1) * LANE]
                out = None
                for half in range(2):
                    hd = pr * 2 + half
                    kx, vx = halves[half]
                    s = lax.dot_general(qpair, kx, (((1,), (1,)), ((), ())), preferred_element_type=F32)
                    s = s + bias_scr[hd]
                    if qb == 0:
                        s = s + first_pen
                    sink = sink_ref[hd] * LOG2E
                    m = jnp.maximum(jnp.max(s, axis=1, keepdims=True), sink)
                    p = jnp.exp2(s - m)
                    denom = jnp.sum(p, axis=1, keepdims=True) + jnp.exp2(sink - m)
                    o = jnp.dot(p.astype(BF16), vx, preferred_element_type=F32) * (1.0 / denom)
                    out = o if out is None else out + o
                o_ref[qb * blk:(qb + 1) * blk, pr * LANE:(pr + 1) * LANE] = out.astype(BF16)


def _swa(sinks, slopes, qa, ka2, va2, batch, seq):
    n = qa.shape[0]
    blk = ATTN_BLOCK
    rows = SWA_QB * blk
    steps = seq // rows
    cur = lambda b, i: (b * steps + i, 0)
    prev = lambda b, i: (b * (seq // blk) + jnp.maximum(i * SWA_QB - 1, 0), 0)
    smem = pl.BlockSpec(memory_space=pltpu.SMEM)
    return pl.pallas_call(
        _swa_kernel,
        grid=(batch, steps),
        in_specs=[smem, smem,
                  pl.BlockSpec((rows, 512), cur),
                  pl.BlockSpec((rows, 256), cur),
                  pl.BlockSpec((blk, 256), prev),
                  pl.BlockSpec((rows, 256), cur),
                  pl.BlockSpec((blk, 256), prev)],
        out_specs=pl.BlockSpec((rows, 512), cur),
        out_shape=jax.ShapeDtypeStruct((n, 512), BF16),
        scratch_shapes=[pltpu.VMEM((SWA_HEADS, blk, 2 * blk), F32)],
        compiler_params=_params(("arbitrary", "arbitrary")),
        name="swa",
    )(sinks, slopes, qa, ka2, ka2, va2, va2)


def _outproj_kernel(ya_ref, yb_ref, ga_ref, gb_ref, x_ref, g1_ref, sh_ref, sc_ref, nm_ref, wa_ref, wb_ref,
                    wo_ref, wrh_ref, wrl_ref, br_ref,
                    x1_ref, h2_ref, route_ref, routet_ref, g8_ref):
    tm = x_ref.shape[0]
    a = jnp.dot(ya_ref[...], wa_ref[...], preferred_element_type=F32)
    bm = jnp.dot(yb_ref[...], wb_ref[...], preferred_element_type=F32)
    mixed = (ga_ref[...].astype(F32) * a + gb_ref[...].astype(F32) * bm).astype(BF16)
    x1 = x_ref[...] + g1_ref[0] * jnp.dot(mixed, wo_ref[...], preferred_element_type=F32)
    x1_ref[...] = x1
    h2 = _rms(x1) * nm_ref[...] * (1.0 + sc_ref[0]) + sh_ref[0]
    hi = h2.astype(BF16)
    h2_ref[:, 0:D_MODEL] = hi

    lo = (h2 - hi.astype(F32)).astype(BF16)
    logits = (jnp.dot(hi, wrh_ref[...], preferred_element_type=F32)
              + jnp.dot(lo, wrh_ref[...], preferred_element_type=F32)
              + jnp.dot(hi, wrl_ref[...], preferred_element_type=F32)) + br_ref[...]

    lane = lax.broadcasted_iota(jnp.int32, (tm, LANE), 1).astype(F32)
    work = jnp.where(lane < N_EXPERTS, logits, -jnp.inf)
    vals, idxs = [], []
    for _ in range(TOP_K):
        m = jnp.max(work, axis=1, keepdims=True)
        idx = jnp.min(jnp.where(work == m, lane, float(LANE)), axis=1, keepdims=True)
        vals.append(m)
        idxs.append(idx)
        work = jnp.where(lane == idx, -jnp.inf, work)
    exps = [jnp.exp(v - vals[0]) for v in vals]
    tot = exps[0] + exps[1] + exps[2] + exps[3]

    onehot = jnp.zeros((tm, LANE), F32)
    for idx in idxs:
        onehot = onehot + (lane == idx).astype(F32)
    r = lax.broadcasted_iota(jnp.int32, (tm, tm), 0)
    c = lax.broadcasted_iota(jnp.int32, (tm, tm), 1)
    prefix = jnp.dot((c < r).astype(BF16), onehot.astype(BF16), preferred_element_type=F32)
    groups = jnp.ceil(jnp.sum(onehot, axis=0, keepdims=True) * (1.0 / GROUP_ALIGN))
    er = lax.broadcasted_iota(jnp.int32, (LANE, LANE), 0)
    ec = lax.broadcasted_iota(jnp.int32, (LANE, LANE), 1)
    before = jnp.dot(jnp.broadcast_to(groups, (8, LANE)).astype(BF16), (er < ec).astype(BF16),
                     preferred_element_type=F32)[0:1, :]
    slot_e = before * GROUP_ALIGN + prefix

    route = jnp.zeros((tm, LANE), F32)
    wext = jnp.zeros((tm, LANE), F32)
    for k in range(TOP_K):
        sk = jnp.sum(jnp.where(lane == idxs[k], slot_e, 0.0), axis=1, keepdims=True)
        sub = jnp.floor(sk * (1.0 / SLOT_SUB))
        route = jnp.where(lane == float(k), sk - sub * SLOT_SUB, route)
        route = jnp.where(lane == float(TOP_K + k), sub, route)
        wk = exps[k] / tot
        wk_hi = wk.astype(BF16).astype(F32)
        wext = jnp.where(lane == idxs[k], wk_hi, wext)
        wext = jnp.where(lane == idxs[k] + float(N_EXPERTS), wk - wk_hi, wext)
    h2_ref[:, D_MODEL:XS_COLS] = wext.astype(BF16)
    route_ref[...] = route
    routet_ref[...] = route.T[0:8, :]
    g8_ref[...] = jnp.broadcast_to(groups * GROUP_ALIGN, (8, LANE))


def _outproj(ya, yb, ga, gb, x2, g1, sh2, sc2, nm, wa, wb_p, wo, wr_hi, wr_lo, br_p, seq):
    n = x2.shape[0]
    tm = ROUTE_TILE
    per_b = seq // tm
    hw = MLA_HEADS * LANE
    row = lambda i: (i, 0)
    fix = lambda i: (0, 0)
    bsel = lambda i: (i // per_b, 0, 0)
    return pl.pallas_call(
        _outproj_kernel,
        grid=(n // tm,),
        in_specs=[pl.BlockSpec((tm, 512), row),
                  pl.BlockSpec((tm, hw), row),
                  pl.BlockSpec((tm, D_MODEL), row),
                  pl.BlockSpec((tm, D_MODEL), row),
                  pl.BlockSpec((tm, D_MODEL), row),
                  pl.BlockSpec((1, 1, D_MODEL), bsel),
                  pl.BlockSpec((1, 1, D_MODEL), bsel),
                  pl.BlockSpec((1, 1, D_MODEL), bsel),
                  pl.BlockSpec((1, D_MODEL), fix),
                  pl.BlockSpec((512, D_MODEL), fix),
                  pl.BlockSpec((hw, D_MODEL), fix),
                  pl.BlockSpec((D_MODEL, D_MODEL), fix),
                  pl.BlockSpec((D_MODEL, LANE), fix),
                  pl.BlockSpec((D_MODEL, LANE), fix),
                  pl.BlockSpec((1, LANE), fix)],
        out_specs=[pl.BlockSpec((tm, D_MODEL), row),
                   pl.BlockSpec((tm, XS_COLS), row),
                   pl.BlockSpec((tm, LANE), row),
                   pl.BlockSpec((8, tm), lambda i: (0, i)),
                   pl.BlockSpec((8, LANE), row)],
        out_shape=[jax.ShapeDtypeStruct((n, D_MODEL), F32),
                   jax.ShapeDtypeStruct((n, XS_COLS), BF16),
                   jax.ShapeDtypeStruct((n, LANE), F32),
                   jax.ShapeDtypeStruct((8, n), F32),
                   jax.ShapeDtypeStruct((n // tm * 8, LANE), F32)],
        compiler_params=_params(("arbitrary",)),
        name="outproj",
    )(ya, yb, ga, gb, x2, g1, sh2, sc2, nm, wa, wb_p, wo, wr_hi, wr_lo, br_p)


def _start_runs(tile, g8_ref, loff_ref, goff_ref, make_copy):
    for e in range(N_EXPERTS):
        g = g8_ref[tile * N_EXPERTS + e]
        lo = loff_ref[tile * N_EXPERTS + e]
        go = goff_ref[tile * N_EXPERTS + e]
        def go_copy(lo=lo, go=go, g=g):
            make_copy(pl.multiple_of(lo, GROUP_ALIGN), pl.multiple_of(go, GROUP_ALIGN),
                      pl.multiple_of(g, GROUP_ALIGN)).start()

        pl.when(g > 0)(go_copy)


def _dispatch_kernel(g8_ref, loff_ref, goff_ref, tot_ref, ends_ref, h_ref, rt_ref, xs_ref, buf, zbuf, sem, zsem):
    t = pl.program_id(0)
    nt = pl.num_programs(0)
    tm = h_ref.shape[0]
    slot = lax.rem(t, 2)

    @pl.when(t == 0)
    def _():
        zbuf[...] = jnp.zeros(zbuf.shape, zbuf.dtype)

        def fill(start):
            return pltpu.make_async_copy(zbuf, xs_ref.at[pl.ds(pl.multiple_of(start, MOE_ROWS), MOE_ROWS), :], zsem)

        def nonempty(e):
            return ends_ref[e] > (ends_ref[e - 1] if e else 0)

        used = lax.shift_right_logical(ends_ref[N_EXPERTS - 1], MOE_ROWS.bit_length() - 1)
        n_blocks = xs_ref.shape[0] // MOE_ROWS
        for e in range(N_EXPERTS):
            pl.when(nonempty(e))(lambda e=e: fill(ends_ref[e] - MOE_ROWS).start())
        lax.fori_loop(used, n_blocks, lambda b, c: (fill(b * MOE_ROWS).start(), c)[1], 0)
        for e in range(N_EXPERTS):
            pl.when(nonempty(e))(lambda e=e: fill(ends_ref[e] - MOE_ROWS).wait())
        lax.fori_loop(used, n_blocks, lambda b, c: (fill(b * MOE_ROWS).wait(), c)[1], 0)

    hb = h_ref[...]
    low = [rt_ref[k:k + 1, :] for k in range(TOP_K)]
    sub = [rt_ref[TOP_K + k:TOP_K + k + 1, :] for k in range(TOP_K)]
    srow = lax.broadcasted_iota(jnp.int32, (SLOT_SUB, tm), 0).astype(F32).astype(BF16)
    one = jnp.ones((SLOT_SUB, tm), BF16)

    for ci in range(SLOTS_PER_TILE // SLOT_CHUNK):
        parts = []
        for b in range(SLOT_CHUNK // SLOT_SUB):
            blk = float(ci * (SLOT_CHUNK // SLOT_SUB) + b)
            own = jnp.zeros((SLOT_SUB, tm), BF16)
            for k in range(TOP_K):
                here = jnp.where(sub[k] == blk, low[k], -1.0).astype(BF16)
                own = jnp.where(srow == here, one, own)
            parts.append(own)
        own = jnp.concatenate(parts, axis=0)
        buf[slot, ci * SLOT_CHUNK:(ci + 1) * SLOT_CHUNK, :] = jnp.dot(own, hb, preferred_element_type=F32)

    def copy(s, lo, go, rows):
        return pltpu.make_async_copy(buf.at[s, pl.ds(lo, rows), :], xs_ref.at[pl.ds(go, rows), :], sem.at[s])

    _start_runs(t, g8_ref, loff_ref, goff_ref, functools.partial(copy, slot))

    def drain(tile, s):
        rows = pl.multiple_of(tot_ref[tile], GROUP_ALIGN)
        copy(s, 0, 0, rows).wait()

    pl.when(t > 0)(lambda: drain(t - 1, 1 - slot))
    pl.when(t == nt - 1)(lambda: drain(t, slot))


def _dispatch(g8f, lofff, gofff, totf, ends, h2, routet, p_rows):
    n = h2.shape[0]
    tm = ROUTE_TILE
    return pl.pallas_call(
        _dispatch_kernel,
        grid_spec=pltpu.PrefetchScalarGridSpec(
            num_scalar_prefetch=5,
            grid=(n // tm,),
            in_specs=[pl.BlockSpec((tm, XS_COLS), lambda i, *_: (i, 0)),
                      pl.BlockSpec((8, tm), lambda i, *_: (0, i))],
            out_specs=pl.BlockSpec(memory_space=pl.ANY),
            scratch_shapes=[pltpu.VMEM((2, SLOTS_PER_TILE, XS_COLS), F32),
                            pltpu.VMEM((MOE_ROWS, XS_COLS), F32),
                            pltpu.SemaphoreType.DMA((2,)), pltpu.SemaphoreType.DMA(())]),
        out_shape=jax.ShapeDtypeStruct((p_rows, XS_COLS), F32),
        compiler_params=_params(("arbitrary",)),
        name="dispatch",
    )(g8f, lofff, gofff, totf, ends, h2, routet)


def _experts_kernel(be_ref, nu_ref, xs_ref, w1_ref, b1_ref, w2_ref, b2_ref, ys_ref):
    @pl.when(pl.program_id(0) < nu_ref[0])
    def _():
        xb = xs_ref[:, 0:D_MODEL].astype(BF16)
        e = be_ref[pl.program_id(0)]
        wext = xs_ref[:, D_MODEL:XS_COLS]
        lane = lax.broadcasted_iota(jnp.int32, wext.shape, 1)
        wcol = jnp.sum(jnp.where((lane == e) | (lane == e + N_EXPERTS), wext, 0.0), axis=1, keepdims=True)
        hcat = jnp.dot(xb, w1_ref[0].astype(BF16), preferred_element_type=F32) + b1_ref[0]
        x_glu = jnp.minimum(hcat[:, :D_EXPERT], SWIGLU_LIMIT)
        x_lin = jnp.clip(hcat[:, D_EXPERT:], -SWIGLU_LIMIT, SWIGLU_LIMIT)
        act = (x_glu * jax.nn.sigmoid(SWIGLU_ALPHA * x_glu) * (x_lin + 1.0)).astype(BF16)
        ys_ref[...] = (jnp.dot(act, w2_ref[0].astype(BF16), preferred_element_type=F32) + b2_ref[0]) * wcol

    @pl.when(pl.program_id(0) >= nu_ref[0])
    def _():
        ys_ref[...] = jnp.zeros(ys_ref.shape, F32)


def _experts(block_e, n_used, xs, w1, b1, w2, b2):
    p = xs.shape[0]
    mb = MOE_ROWS
    last = lambda i, nu: jnp.minimum(i, nu[0] - 1)
    rowmap = lambda i, be, nu: (last(i, nu), 0)
    wmap = lambda i, be, nu: (be[last(i, nu)], 0, 0)
    return pl.pallas_call(
        _experts_kernel,
        grid_spec=pltpu.PrefetchScalarGridSpec(
            num_scalar_prefetch=2,
            grid=(p // mb,),
            in_specs=[pl.BlockSpec((mb, XS_COLS), rowmap),
                      pl.BlockSpec((1, D_MODEL, 2 * D_EXPERT), wmap),
                      pl.BlockSpec((1, 1, 2 * D_EXPERT), wmap),
                      pl.BlockSpec((1, D_EXPERT, D_MODEL), wmap),
                      pl.BlockSpec((1, 1, D_MODEL), wmap)],
            out_specs=pl.BlockSpec((mb, D_MODEL), lambda i, be, nu: (i, 0))),
        out_shape=jax.ShapeDtypeStruct((p, D_MODEL), F32),
        compiler_params=_params(("arbitrary",)),
        name="experts",
    )(block_e, n_used, xs, w1, b1, w2, b2)


def _combine_kernel(g8_ref, loff_ref, goff_ref, tot_ref, ys_ref, route_ref, x1_ref, g2_ref, fn_ref, o_ref, stg, sem):
    t = pl.program_id(0)
    nt = pl.num_programs(0)
    tm = x1_ref.shape[0]
    slot = lax.rem(t, 2)

    def copy(s, lo, go, rows):
        return pltpu.make_async_copy(ys_ref.at[pl.ds(go, rows), :], stg.at[s, pl.ds(lo, rows), :], sem.at[s])

    @pl.when(t == 0)
    def _():
        stg[...] = jnp.zeros(stg.shape, F32)
        _start_runs(t, g8_ref, loff_ref, goff_ref, functools.partial(copy, 0))

    pl.when(t + 1 < nt)(lambda: _start_runs(t + 1, g8_ref, loff_ref, goff_ref, functools.partial(copy, 1 - slot)))
    copy(slot, 0, 0, pl.multiple_of(tot_ref[t], GROUP_ALIGN)).wait()

    route = route_ref[...]
    low = [jnp.broadcast_to(route[:, k:k + 1], (tm, SLOT_SUB)).astype(BF16) for k in range(TOP_K)]
    sub = [jnp.broadcast_to(route[:, TOP_K + k:TOP_K + k + 1], (tm, SLOT_SUB)).astype(BF16) for k in range(TOP_K)]
    scol = lax.broadcasted_iota(jnp.int32, (tm, SLOT_SUB), 1).astype(F32).astype(BF16)
    one = jnp.ones((tm, SLOT_SUB), BF16)
    never = jnp.full((tm, SLOT_SUB), -1.0, BF16)

    y = None
    for ci in range(SLOTS_PER_TILE // SLOT_CHUNK):
        parts = []
        for b in range(SLOT_CHUNK // SLOT_SUB):
            blk = float(ci * (SLOT_CHUNK // SLOT_SUB) + b)
            own = jnp.zeros((tm, SLOT_SUB), BF16)
            for k in range(TOP_K):
                own = jnp.where(scol == jnp.where(sub[k] == blk, low[k], never), one, own)
            parts.append(own)
        own = jnp.concatenate(parts, axis=1)
        rows = stg[slot, ci * SLOT_CHUNK:(ci + 1) * SLOT_CHUNK, :].astype(BF16)
        part = jnp.dot(own, rows, preferred_element_type=F32)
        y = part if y is None else y + part
    x = x1_ref[...] + g2_ref[0] * y
    o_ref[...] = _rms(x) * fn_ref[...]


def _combine(g8f, lofff, gofff, totf, ys, route, x1, g2, fn, seq):
    n = x1.shape[0]
    tm = ROUTE_TILE
    per_b = seq // tm
    return pl.pallas_call(
        _combine_kernel,
        grid_spec=pltpu.PrefetchScalarGridSpec(
            num_scalar_prefetch=4,
            grid=(n // tm,),
            in_specs=[pl.BlockSpec(memory_space=pl.ANY),
                      pl.BlockSpec((tm, LANE), lambda i, *_: (i, 0)),
                      pl.BlockSpec((tm, D_MODEL), lambda i, *_: (i, 0)),
                      pl.BlockSpec((1, 1, D_MODEL), lambda i, *_: (i // per_b, 0, 0)),
                      pl.BlockSpec((1, D_MODEL), lambda i, *_: (0, 0))],
            out_specs=pl.BlockSpec((tm, D_MODEL), lambda i, *_: (i, 0)),
            scratch_shapes=[pltpu.VMEM((2, SLOTS_PER_TILE, D_MODEL), F32), pltpu.SemaphoreType.DMA((2,))]),
        out_shape=jax.ShapeDtypeStruct((n, D_MODEL), F32),
        compiler_params=_params(("arbitrary",)),
        name="combine",
    )(g8f, lofff, gofff, totf, ys, route, x1, g2, fn)


def _pack_w_in(w_in):
    o_kr = 512 + 2 * 128 + MLA_Q_RANK + MLA_KV_RANK
    z = lambda w: jnp.zeros((D_MODEL, w), w_in.dtype)
    packed = jnp.concatenate([
        w_in[:, :o_kr], z(MLA_NOPE_DIM), w_in[:, o_kr:o_kr + MLA_ROPE_DIM],
        z(LANE - MLA_NOPE_DIM - MLA_ROPE_DIM), w_in[:, o_kr + MLA_ROPE_DIM:]], axis=1)
    return packed.astype(BF16)


def _pack_heads(w, width, pieces):
    rows = w.shape[0]
    w3 = w.reshape(rows, MLA_HEADS, width)
    cols = [w3[:, :, a:b] for a, b in pieces]
    used = sum(b - a for a, b in pieces)
    cols.append(jnp.zeros((rows, MLA_HEADS, LANE - used), w.dtype))
    return jnp.concatenate(cols, axis=2).reshape(rows, MLA_HEADS * LANE)


def kernel(x, c, positions, w_ada, b_ada, norm_mix, norm_ffn, w_in, sinks, q_norm, kv_norm, w_uq, w_uk, w_uv,
           w_branch_a, w_branch_b, w_out, w_router, b_router, w_moe1, b_moe1, w_moe2, b_moe2, final_norm):
    batch, seq, d = x.shape
    n = batch * seq
    assert d == D_MODEL and w_ada.shape[0] == 1
    assert seq % TQ_MLA == 0 and seq % (SWA_QB * ATTN_BLOCK) == 0 and seq % TM_PROJ == 0 and seq % ROUTE_TILE == 0
    hw = MLA_HEADS * LANE
    qk = MLA_NOPE_DIM + MLA_ROPE_DIM

    w_in_p = _pack_w_in(w_in[0])
    wuq_p = _pack_heads(w_uq[0], qk, [(0, qk)]).astype(BF16)
    wuk_p = _pack_heads(w_uk[0], MLA_NOPE_DIM, [(0, MLA_NOPE_DIM)]).astype(BF16)
    wuv_p = _pack_heads(w_uv[0], MLA_V_DIM, [(0, MLA_V_DIM)]).astype(BF16)
    one_row = jnp.tile((jnp.arange(LANE) == MLA_V_DIM).astype(F32), MLA_HEADS)[None, :]
    freqs = ROPE_THETA ** (-jnp.arange(0, MLA_ROPE_DIM, 2, dtype=F32) / MLA_ROPE_DIM)
    frq = jnp.concatenate([jnp.zeros((MLA_NOPE_DIM,), F32), freqs, freqs,
                           jnp.zeros((LANE - qk,), F32)])[None, :]
    wb3 = w_branch_b[0].reshape(MLA_HEADS, MLA_V_DIM, D_MODEL)
    wb_p = jnp.concatenate([wb3, jnp.zeros((MLA_HEADS, LANE - MLA_V_DIM, D_MODEL), F32)],
                           axis=1).reshape(hw, D_MODEL).astype(BF16)
    wa = w_branch_a[0].astype(BF16)
    wo = w_out[0].astype(BF16)
    wr = jnp.pad(w_router[0], ((0, 0), (0, LANE - N_EXPERTS)))
    wr_hi = wr.astype(BF16)
    wr_lo = (wr - wr_hi.astype(F32)).astype(BF16)
    br_p = jnp.pad(b_router[0], (0, LANE - N_EXPERTS))[None, :]
    b1 = b_moe1[0][:, None, :]
    b2 = b_moe2[0][:, None, :]
    slopes = jnp.asarray(np.exp2(-8.0 * np.arange(1, SWA_HEADS + 1) / SWA_HEADS), dtype=F32)

    c8 = jnp.pad(c, ((0, 8 - batch), (0, 0)))
    mod = _ada(c8, w_ada[0], b_ada[0][None, :])[:batch]
    sh1, sc1, g1, sh2, sc2, g2 = [m[:, None, :] for m in jnp.split(mod, 6, axis=-1)]

    x2 = x.reshape(n, D_MODEL)
    pos2 = positions.reshape(n, 1).astype(jnp.int32)
    qa, ka2, va2, qm, km, vm, ga, gb, stats = _inproj(
        x2, pos2, sh1, sc1, norm_mix, w_in_p, q_norm, kv_norm, wuq_p, wuk_p, wuv_p, frq, one_row, seq)
    st = stats.reshape(batch, seq // TM_PROJ, 8, LANE)
    bound_sq = jnp.max(st[:, :, :, 0], axis=1) * jnp.max(st[:, :, :, 1], axis=1)
    plain = (bound_sq <= MLA_PLAIN_BOUND ** 2).astype(jnp.int32).reshape(-1)
    yb = _mla(plain, qm, km, vm, batch, seq)
    ya = _swa(sinks[0], slopes, qa, ka2, va2, batch, seq)

    x1, h2, route, routet, g8 = _outproj(
        ya, yb, ga, gb, x2, g1, sh2, sc2, norm_ffn, wa, wb_p, wo, wr_hi, wr_lo, br_p, seq)

    n_tiles = n // ROUTE_TILE
    g8t = g8.reshape(n_tiles, 8, LANE)[:, 0, :N_EXPERTS].astype(jnp.int32)
    loff = jnp.cumsum(g8t, axis=1) - g8t
    padded = ((jnp.sum(g8t, axis=0) + MOE_ROWS - 1) // MOE_ROWS) * MOE_ROWS
    pends = jnp.cumsum(padded)
    goff = (pends - padded)[None, :] + jnp.cumsum(g8t, axis=0) - g8t
    tot = jnp.sum(g8t, axis=1).astype(jnp.int32)
    p_rows = n * TOP_K + n_tiles * N_EXPERTS * (GROUP_ALIGN - 1) + N_EXPERTS * (MOE_ROWS - 1)
    p_rows = -(-p_rows // MOE_ROWS) * MOE_ROWS
    n_blocks = p_rows // MOE_ROWS
    block_start = jnp.arange(n_blocks, dtype=jnp.int32) * MOE_ROWS
    block_e = jnp.minimum(jnp.sum((pends[None, :] <= block_start[:, None]).astype(jnp.int32), axis=1),
                          N_EXPERTS - 1).astype(jnp.int32)
    n_used = (pends[-1:] // MOE_ROWS).astype(jnp.int32)
    tabs = (g8t.reshape(-1), loff.reshape(-1).astype(jnp.int32), goff.reshape(-1).astype(jnp.int32), tot)

    xs = _dispatch(*tabs, pends.astype(jnp.int32), h2, routet, p_rows)
    ys = _experts(block_e, n_used, xs, w_moe1[0], b1, w_moe2[0], b2)
    out = _combine(*tabs, ys, route, x1, g2, final_norm[None, :], seq)
    return out.reshape(batch, seq, D_MODEL)
```

```python
import functools

import numpy as np
import jax
import jax.numpy as jnp
from jax import lax
from jax.experimental import pallas as pl
from jax.experimental.pallas import tpu as pltpu

D_MODEL = 1024
SWA_HEADS = 8
SWA_KV_HEADS = 2
SWA_HEAD_DIM = 64
ATTN_BLOCK = 128
MLA_HEADS = 8
MLA_Q_RANK = 384
MLA_KV_RANK = 256
MLA_NOPE_DIM = 64
MLA_ROPE_DIM = 32
MLA_V_DIM = 64
ROPE_THETA = 10000.0
N_EXPERTS = 32
TOP_K = 4
D_EXPERT = 1024
SWIGLU_LIMIT = 7.0
SWIGLU_ALPHA = 1.702
NORM_EPS = 1e-6

LANE = 128
LOG2E = 1.4426950408889634
NEG = -1e30
SWA_QSCALE = SWA_HEAD_DIM ** -0.5 * LOG2E
MLA_QSCALE = (MLA_NOPE_DIM + MLA_ROPE_DIM) ** -0.5 * LOG2E
HALF_ROPE = MLA_ROPE_DIM // 2
MLA_PLAIN_BOUND = 60.0
MLA_WIDE = 4

BF16 = jnp.bfloat16
F32 = jnp.float32

TM_PROJ = 512
TQ_MLA = 512
SWA_QB = 4
MOE_ROWS = 512
ROUTE_TILE = 512
GROUP_ALIGN = 8
SLOTS_PER_TILE = ROUTE_TILE * TOP_K + N_EXPERTS * GROUP_ALIGN
SLOT_SUB = 256
SLOT_CHUNK = 3 * SLOT_SUB
XS_COLS = D_MODEL + LANE
VMEM_LIMIT = 56 * 1024 * 1024


def _params(sem, vmem=VMEM_LIMIT):
    return pltpu.CompilerParams(dimension_semantics=sem, vmem_limit_bytes=vmem)


def _rms(x):
    return x * lax.rsqrt(jnp.mean(x * x, axis=-1, keepdims=True) + NORM_EPS)


def _ada_kernel(c_ref, w_ref, b_ref, o_ref):
    c = c_ref[...]
    a = (c * jax.nn.sigmoid(c)).astype(BF16)
    o_ref[...] = jnp.dot(a, w_ref[...].astype(BF16), preferred_element_type=F32) + b_ref[...]


def _ada(c8, w_ada, b_ada):
    n_out = w_ada.shape[1]
    return pl.pallas_call(
        _ada_kernel,
        grid=(n_out // D_MODEL,),
        in_specs=[pl.BlockSpec((8, D_MODEL), lambda j: (0, 0)),
                  pl.BlockSpec((D_MODEL, D_MODEL), lambda j: (0, j)),
                  pl.BlockSpec((1, D_MODEL), lambda j: (0, j))],
        out_specs=pl.BlockSpec((8, D_MODEL), lambda j: (0, j)),
        out_shape=jax.ShapeDtypeStruct((8, n_out), F32),
        compiler_params=_params(("arbitrary",)),
        name="ada",
    )(c8, w_ada, b_ada)


_C_QA = (0, 512)
_C_KV = (512, 768)
_C_LAT = (768, 1408)
_W_MIX_COLS = 1408
_ROPE_LO = (MLA_NOPE_DIM, MLA_NOPE_DIM + HALF_ROPE)
_ROPE_HI = (MLA_NOPE_DIM + HALF_ROPE, MLA_NOPE_DIM + MLA_ROPE_DIM)


def _inproj_kernel(x_ref, pos_ref, sh_ref, sc_ref, nm_ref, win_ref, wkr_ref, wg_ref, qn_ref, kvn_ref, wuq_ref, wuk_ref,
                   wuv_ref, frq_ref, one_ref,
                   qa_ref, ka_ref, va_ref, qm_ref, km_ref, vm_ref, ga_ref, gb_ref, st_ref):
    tm = x_ref.shape[0]
    x = x_ref[...]
    h = (_rms(x) * nm_ref[...] * (1.0 + sc_ref[0]) + sh_ref[0]).astype(BF16)

    def proj(c):
        return jnp.dot(h, win_ref[:, c[0]:c[1]], preferred_element_type=F32)

    lane = lax.broadcasted_iota(jnp.int32, (tm, LANE), 1)
    first = lane < SWA_HEAD_DIM

    def twice(t):
        r = pltpu.roll(t, SWA_HEAD_DIM, axis=1)
        return jnp.concatenate([jnp.where(first, t, r), jnp.where(first, r, t)], axis=1)

    kv = proj(_C_KV)
    ka_ref[...] = twice(kv[:, :LANE]).astype(BF16)
    va_ref[...] = twice(kv[:, LANE:]).astype(BF16)

    ang = pos_ref[...].astype(F32) * frq_ref[...]
    cs = jnp.cos(ang)
    sn = jnp.sin(ang)
    sn_lo = jnp.where((lane >= _ROPE_LO[0]) & (lane < _ROPE_LO[1]), -sn, 0.0)
    sn_hi = jnp.where((lane >= _ROPE_HI[0]) & (lane < _ROPE_HI[1]), sn, 0.0)

    def rotary(t):
        return (t * cs + pltpu.roll(t, LANE - HALF_ROPE, axis=1) * sn_lo
                + pltpu.roll(t, HALF_ROPE, axis=1) * sn_hi)

    lat = proj(_C_LAT)
    cqn = (_rms(lat[:, 0:MLA_Q_RANK]) * qn_ref[...]).astype(BF16)
    ckvn = (_rms(lat[:, MLA_Q_RANK:MLA_Q_RANK + MLA_KV_RANK]) * kvn_ref[...]).astype(BF16)
    krr = rotary(jnp.dot(h, wkr_ref[...], preferred_element_type=F32))
    q = jnp.dot(cqn, wuq_ref[...], preferred_element_type=F32)
    kn = jnp.dot(ckvn, wuk_ref[...], preferred_element_type=F32)

    def max_sq_norm(t):
        tf = t.astype(F32)
        return jnp.max(jnp.sum(tf * tf, axis=1, keepdims=True), axis=0, keepdims=True)

    rid = lax.broadcasted_iota(jnp.int32, (8, LANE), 0)
    lid = lax.broadcasted_iota(jnp.int32, (8, LANE), 1)
    stats = jnp.zeros((8, LANE), F32)
    for hh in range(MLA_HEADS):
        sl = slice(hh * LANE, (hh + 1) * LANE)
        qb = (rotary(q[:, sl]) * MLA_QSCALE).astype(BF16)
        kb = (kn[:, sl] + krr).astype(BF16)
        qm_ref[:, sl] = qb
        km_ref[:, sl] = kb
        stats = jnp.where((rid == hh) & (lid == 0), max_sq_norm(qb), stats)
        stats = jnp.where((rid == hh) & (lid == 1), max_sq_norm(kb), stats)
    st_ref[...] = stats
    vm_ref[...] = (jnp.dot(ckvn, wuv_ref[...], preferred_element_type=F32) + one_ref[...]).astype(BF16)
    qa_ref[...] = (proj(_C_QA) * SWA_QSCALE).astype(BF16)
    ga_ref[...] = jax.nn.sigmoid(jnp.dot(h, wg_ref[:, 0:D_MODEL], preferred_element_type=F32)).astype(BF16)
    gb_ref[...] = jax.nn.sigmoid(jnp.dot(h, wg_ref[:, D_MODEL:], preferred_element_type=F32)).astype(BF16)


def _inproj(x2, pos2, sh1, sc1, nm, w_mix, w_kr, w_gates, qn, kvn, wuq_p, wuk_p, wuv_p, frq, one_row, seq):
    n = x2.shape[0]
    tm = TM_PROJ
    per_b = seq // tm
    hw = MLA_HEADS * LANE
    row = lambda i: (i, 0)
    fix = lambda i: (0, 0)
    bsel = lambda i: (i // per_b, 0, 0)
    widths = (512, 256, 256, hw, hw, hw, D_MODEL, D_MODEL)
    return pl.pallas_call(
        _inproj_kernel,
        grid=(n // tm,),
        in_specs=[pl.BlockSpec((tm, D_MODEL), row),
                  pl.BlockSpec((tm, 1), row),
                  pl.BlockSpec((1, 1, D_MODEL), bsel),
                  pl.BlockSpec((1, 1, D_MODEL), bsel),
                  pl.BlockSpec((1, D_MODEL), fix),
                  pl.BlockSpec((D_MODEL, _W_MIX_COLS), fix),
                  pl.BlockSpec((D_MODEL, LANE), fix),
                  pl.BlockSpec((D_MODEL, 2 * D_MODEL), fix),
                  pl.BlockSpec((1, MLA_Q_RANK), fix),
                  pl.BlockSpec((1, MLA_KV_RANK), fix),
                  pl.BlockSpec((MLA_Q_RANK, hw), fix),
                  pl.BlockSpec((MLA_KV_RANK, hw), fix),
                  pl.BlockSpec((MLA_KV_RANK, hw), fix),
                  pl.BlockSpec((1, LANE), fix),
                  pl.BlockSpec((1, hw), fix)],
        out_specs=[pl.BlockSpec((tm, w), row) for w in widths] + [pl.BlockSpec((8, LANE), row)],
        out_shape=[jax.ShapeDtypeStruct((n, w), BF16) for w in widths]
        + [jax.ShapeDtypeStruct((n // tm * 8, LANE), F32)],
        compiler_params=_params(("arbitrary",)),
        name="inproj",
    )(x2, pos2, sh1, sc1, nm, w_mix, w_kr, w_gates, qn, kvn, wuq_p, wuk_p, wuv_p, frq, one_row)


def _mla_kernel(plain_ref, q_ref, k_ref, v_ref, o_ref, m_scr, acc_scr):
    t = TQ_MLA
    nq = q_ref.shape[0] // t

    def scores(i, j, nblk, diag_at):
        q = q_ref[pl.ds(pl.multiple_of(i * t, t), t), :]
        start = pl.multiple_of(j * t, t)
        k = k_ref[pl.ds(start, nblk * t), :]
        v = v_ref[pl.ds(start, nblk * t), :]
        s = lax.dot_general(q, k, (((1,), (1,)), ((), ())), preferred_element_type=F32)
        if diag_at is not None:
            qi = lax.broadcasted_iota(jnp.int32, s.shape, 0)
            kj = lax.broadcasted_iota(jnp.int32, s.shape, 1)
            s = jnp.where(kj <= qi + diag_at * t, s, NEG)
        return s, v

    def plain_step(i, j, nblk, diag_at):
        s, v = scores(i, j, nblk, diag_at)
        acc_scr[...] += jnp.dot(jnp.exp2(s).astype(BF16), v, preferred_element_type=F32)

    def online_step(i, j, nblk, diag_at):
        s, v = scores(i, j, nblk, diag_at)
        m_old = m_scr[...]
        m_new = jnp.maximum(m_old, jnp.max(s, axis=1, keepdims=True))
        p = jnp.exp2(s - m_new)
        alpha = jnp.exp2(m_old - m_new)
        acc_scr[...] = acc_scr[...] * alpha + jnp.dot(p.astype(BF16), v, preferred_element_type=F32)
        m_scr[...] = m_new

    def run(step, wide, online):
        def q_tile(i, carry):
            acc_scr[...] = jnp.zeros(acc_scr.shape, F32)
            if online:
                m_scr[...] = jnp.full(m_scr.shape, NEG, F32)
            big = 2 * wide
            nb = lax.shift_right_logical(i, big.bit_length() - 1)
            done = nb * big
            mid = lax.shift_right_logical(i - done, wide.bit_length() - 1)

            def big_body(jj, c):
                step(i, jj * big, big, None)
                return c

            lax.fori_loop(0, nb, big_body, 0)
            pl.when(mid == 1)(lambda: step(i, done, wide, None))
            done = done + mid * wide
            left = i - done
            for r in range(wide):
                pl.when(left == r)(lambda r=r: step(i, done, r + 1, r))
            acc = acc_scr[...]
            o_ref[pl.ds(pl.multiple_of(i * t, t), t), :] = (acc / acc[:, MLA_V_DIM:MLA_V_DIM + 1]).astype(BF16)
            return carry

        lax.fori_loop(0, nq, q_tile, 0)

    small = plain_ref[pl.program_id(0) * MLA_HEADS + pl.program_id(1)] != 0
    pl.when(small)(lambda: run(plain_step, MLA_WIDE, False))
    pl.when(jnp.logical_not(small))(lambda: run(online_step, 1, True))


def _mla(plain, qm, km, vm, batch, seq):
    n = qm.shape[0]
    t = TQ_MLA
    kmap = lambda b, h, *_: (b, h)
    return pl.pallas_call(
        _mla_kernel,
        grid_spec=pltpu.PrefetchScalarGridSpec(
            num_scalar_prefetch=1,
            grid=(batch, MLA_HEADS),
            in_specs=[pl.BlockSpec((seq, LANE), kmap),
                      pl.BlockSpec((seq, LANE), kmap),
                      pl.BlockSpec((seq, LANE), kmap)],
            out_specs=pl.BlockSpec((seq, LANE), kmap),
            scratch_shapes=[pltpu.VMEM((t, 1), F32), pltpu.VMEM((t, LANE), F32)]),
        out_shape=jax.ShapeDtypeStruct((n, MLA_HEADS * LANE), BF16),
        compiler_params=_params(("arbitrary", "arbitrary")),
        name="mla",
    )(plain, qm, km, vm)


def _swa_kernel(sink_ref, slope_ref, q_ref, kc_ref, kp_ref, vc_ref, vp_ref, o_ref, bias_scr):
    b = pl.program_id(0)
    i = pl.program_id(1)
    blk = ATTN_BLOCK

    @pl.when((b == 0) & (i == 0))
    def _():
        qi = lax.broadcasted_iota(jnp.int32, (blk, 2 * blk), 0)
        kj = lax.broadcasted_iota(jnp.int32, (blk, 2 * blk), 1)
        dist = qi - kj + blk
        valid = (dist >= 0) & (dist < blk)
        distf = dist.astype(F32)
        for hd in range(SWA_HEADS):
            bias_scr[hd] = jnp.where(valid, -(slope_ref[hd] * LOG2E) * distf, NEG)

    lane = lax.broadcasted_iota(jnp.int32, (2 * blk, LANE), 1)
    lo = lane < SWA_HEAD_DIM
    kcol = lax.broadcasted_iota(jnp.int32, (blk, 2 * blk), 1)
    first_pen = jnp.where(kcol < blk, jnp.where(i == 0, NEG, 0.0), 0.0)
    zero = jnp.zeros((2 * blk, LANE), BF16)

    for qb in range(SWA_QB):
        if qb == 0:
            kprev, vprev = kp_ref[...], vp_ref[...]
        else:
            kprev = kc_ref[(qb - 1) * blk:qb * blk, :]
            vprev = vc_ref[(qb - 1) * blk:qb * blk, :]
        kcat = jnp.concatenate([kprev, kc_ref[qb * blk:(qb + 1) * blk, :]], axis=0)
        vcat = jnp.concatenate([vprev, vc_ref[qb * blk:(qb + 1) * blk, :]], axis=0)
        for g in range(SWA_KV_HEADS):
            kg = kcat[:, g * LANE:(g + 1) * LANE]
            vg = vcat[:, g * LANE:(g + 1) * LANE]
            halves = ((jnp.where(lo, kg, zero), jnp.where(lo, vg, zero)),
                      (jnp.where(lo, zero, kg), jnp.where(lo, zero, vg)))
            for pp in range(2):
                pr = g * 2 + pp
                qpair = q_ref[qb * blk:(qb + 1) * blk, pr * LANE:(pr + 1) * LANE]
                out = None
                for half in range(2):
                    hd = pr * 2 + half
                    kx, vx = halves[half]
                    s = lax.dot_general(qpair, kx, (((1,), (1,)), ((), ())), preferred_element_type=F32)
                    s = s + bias_scr[hd]
                    if qb == 0:
                        s = s + first_pen
                    sink = sink_ref[hd] * LOG2E
                    m = jnp.maximum(jnp.max(s, axis=1, keepdims=True), sink)
                    p = jnp.exp2(s - m)
                    denom = jnp.sum(p, axis=1, keepdims=True) + jnp.exp2(sink - m)
                    o = jnp.dot(p.astype(BF16), vx, preferred_element_type=F32) * (1.0 / denom)
                    out = o if out is None else out + o
                o_ref[qb * blk:(qb + 1) * blk, pr * LANE:(pr + 1) * LANE] = out.astype(BF16)


def _swa(sinks, slopes, qa, ka2, va2, batch, seq):
    n = qa.shape[0]
    blk = ATTN_BLOCK
    rows = SWA_QB * blk
    steps = seq // rows
    cur = lambda b, i: (b * steps + i, 0)
    prev = lambda b, i: (b * (seq // blk) + jnp.maximum(i * SWA_QB - 1, 0), 0)
    smem = pl.BlockSpec(memory_space=pltpu.SMEM)
    return pl.pallas_call(
        _swa_kernel,
        grid=(batch, steps),
        in_specs=[smem, smem,
                  pl.BlockSpec((rows, 512), cur),
                  pl.BlockSpec((rows, 256), cur),
                  pl.BlockSpec((blk, 256), prev),
                  pl.BlockSpec((rows, 256), cur),
                  pl.BlockSpec((blk, 256), prev)],
        out_specs=pl.BlockSpec((rows, 512), cur),
        out_shape=jax.ShapeDtypeStruct((n, 512), BF16),
        scratch_shapes=[pltpu.VMEM((SWA_HEADS, blk, 2 * blk), F32)],
        compiler_params=_params(("arbitrary", "arbitrary")),
        name="swa",
    )(sinks, slopes, qa, ka2, ka2, va2, va2)


def _outproj_kernel(ya_ref, yb_ref, ga_ref, gb_ref, x_ref, g1_ref, sh_ref, sc_ref, nm_ref, wa_ref, wb_ref,
                    wo_ref, wrh_ref, wrl_ref, br_ref,
                    x1_ref, h2_ref, route_ref, routet_ref, g8_ref):
    tm = x_ref.shape[0]
    a = jnp.dot(ya_ref[...], wa_ref[...], preferred_element_type=F32)
    bm = jnp.dot(yb_ref[...], wb_ref[...], preferred_element_type=F32)
    mixed = (ga_ref[...].astype(F32) * a + gb_ref[...].astype(F32) * bm).astype(BF16)
    x1 = x_ref[...] + g1_ref[0] * jnp.dot(mixed, wo_ref[...], preferred_element_type=F32)
    x1_ref[...] = x1
    h2 = _rms(x1) * nm_ref[...] * (1.0 + sc_ref[0]) + sh_ref[0]
    hi = h2.astype(BF16)
    h2_ref[:, 0:D_MODEL] = hi

    lo = (h2 - hi.astype(F32)).astype(BF16)
    logits = (jnp.dot(hi, wrh_ref[...], preferred_element_type=F32)
              + jnp.dot(lo, wrh_ref[...], preferred_element_type=F32)
              + jnp.dot(hi, wrl_ref[...], preferred_element_type=F32)) + br_ref[...]

    lane = lax.broadcasted_iota(jnp.int32, (tm, LANE), 1).astype(F32)
    work = jnp.where(lane < N_EXPERTS, logits, -jnp.inf)
    vals, idxs = [], []
    for _ in range(TOP_K):
        m = jnp.max(work, axis=1, keepdims=True)
        idx = jnp.min(jnp.where(work == m, lane, float(LANE)), axis=1, keepdims=True)
        vals.append(m)
        idxs.append(idx)
        work = jnp.where(lane == idx, -jnp.inf, work)
    exps = [jnp.exp(v - vals[0]) for v in vals]
    tot = exps[0] + exps[1] + exps[2] + exps[3]

    onehot = jnp.zeros((tm, LANE), F32)
    for idx in idxs:
        onehot = onehot + (lane == idx).astype(F32)
    r = lax.broadcasted_iota(jnp.int32, (tm, tm), 0)
    c = lax.broadcasted_iota(jnp.int32, (tm, tm), 1)
    prefix = jnp.dot((c < r).astype(BF16), onehot.astype(BF16), preferred_element_type=F32)
    groups = jnp.ceil(jnp.sum(onehot, axis=0, keepdims=True) * (1.0 / GROUP_ALIGN))
    er = lax.broadcasted_iota(jnp.int32, (LANE, LANE), 0)
    ec = lax.broadcasted_iota(jnp.int32, (LANE, LANE), 1)
    before = jnp.dot(jnp.broadcast_to(groups, (8, LANE)).astype(BF16), (er < ec).astype(BF16),
                     preferred_element_type=F32)[0:1, :]
    slot_e = before * GROUP_ALIGN + prefix

    route = jnp.zeros((tm, LANE), F32)
    wext = jnp.zeros((tm, LANE), F32)
    for k in range(TOP_K):
        sk = jnp.sum(jnp.where(lane == idxs[k], slot_e, 0.0), axis=1, keepdims=True)
        sub = jnp.floor(sk * (1.0 / SLOT_SUB))
        route = jnp.where(lane == float(k), sk - sub * SLOT_SUB, route)
        route = jnp.where(lane == float(TOP_K + k), sub, route)
        wk = exps[k] / tot
        wk_hi = wk.astype(BF16).astype(F32)
        wext = jnp.where(lane == idxs[k], wk_hi, wext)
        wext = jnp.where(lane == idxs[k] + float(N_EXPERTS), wk - wk_hi, wext)
    h2_ref[:, D_MODEL:XS_COLS] = wext.astype(BF16)
    route_ref[...] = route
    routet_ref[...] = route.T[0:8, :]
    g8_ref[...] = jnp.broadcast_to(groups * GROUP_ALIGN, (8, LANE))


def _outproj(ya, yb, ga, gb, x2, g1, sh2, sc2, nm, wa, wb_p, wo, wr_hi, wr_lo, br_p, seq):
    n = x2.shape[0]
    tm = ROUTE_TILE
    per_b = seq // tm
    hw = MLA_HEADS * LANE
    row = lambda i: (i, 0)
    fix = lambda i: (0, 0)
    bsel = lambda i: (i // per_b, 0, 0)
    return pl.pallas_call(
        _outproj_kernel,
        grid=(n // tm,),
        in_specs=[pl.BlockSpec((tm, 512), row),
                  pl.BlockSpec((tm, hw), row),
                  pl.BlockSpec((tm, D_MODEL), row),
                  pl.BlockSpec((tm, D_MODEL), row),
                  pl.BlockSpec((tm, D_MODEL), row),
                  pl.BlockSpec((1, 1, D_MODEL), bsel),
                  pl.BlockSpec((1, 1, D_MODEL), bsel),
                  pl.BlockSpec((1, 1, D_MODEL), bsel),
                  pl.BlockSpec((1, D_MODEL), fix),
                  pl.BlockSpec((512, D_MODEL), fix),
                  pl.BlockSpec((hw, D_MODEL), fix),
                  pl.BlockSpec((D_MODEL, D_MODEL), fix),
                  pl.BlockSpec((D_MODEL, LANE), fix),
                  pl.BlockSpec((D_MODEL, LANE), fix),
                  pl.BlockSpec((1, LANE), fix)],
        out_specs=[pl.BlockSpec((tm, D_MODEL), row),
                   pl.BlockSpec((tm, XS_COLS), row),
                   pl.BlockSpec((tm, LANE), row),
                   pl.BlockSpec((8, tm), lambda i: (0, i)),
                   pl.BlockSpec((8, LANE), row)],
        out_shape=[jax.ShapeDtypeStruct((n, D_MODEL), F32),
                   jax.ShapeDtypeStruct((n, XS_COLS), BF16),
                   jax.ShapeDtypeStruct((n, LANE), F32),
                   jax.ShapeDtypeStruct((8, n), F32),
                   jax.ShapeDtypeStruct((n // tm * 8, LANE), F32)],
        compiler_params=_params(("arbitrary",)),
        name="outproj",
    )(ya, yb, ga, gb, x2, g1, sh2, sc2, nm, wa, wb_p, wo, wr_hi, wr_lo, br_p)


def _start_runs(tile, g8_ref, loff_ref, goff_ref, make_copy):
    for e in range(N_EXPERTS):
        g = g8_ref[tile * N_EXPERTS + e]
        lo = loff_ref[tile * N_EXPERTS + e]
        go = goff_ref[tile * N_EXPERTS + e]
        def go_copy(lo=lo, go=go, g=g):
            make_copy(pl.multiple_of(lo, GROUP_ALIGN), pl.multiple_of(go, GROUP_ALIGN),
                      pl.multiple_of(g, GROUP_ALIGN)).start()

        pl.when(g > 0)(go_copy)


def _dispatch_kernel(g8_ref, loff_ref, goff_ref, tot_ref, ends_ref, h_ref, rt_ref, xs_ref, buf, zbuf, sem, zsem):
    t = pl.program_id(0)
    nt = pl.num_programs(0)
    tm = h_ref.shape[0]
    slot = lax.rem(t, 2)

    @pl.when(t == 0)
    def _():
        zbuf[...] = jnp.zeros(zbuf.shape, zbuf.dtype)

        def fill(start):
            return pltpu.make_async_copy(zbuf, xs_ref.at[pl.ds(pl.multiple_of(start, MOE_ROWS), MOE_ROWS), :], zsem)

        def nonempty(e):
            return ends_ref[e] > (ends_ref[e - 1] if e else 0)

        used = lax.shift_right_logical(ends_ref[N_EXPERTS - 1], MOE_ROWS.bit_length() - 1)
        n_blocks = xs_ref.shape[0] // MOE_ROWS
        for e in range(N_EXPERTS):
            pl.when(nonempty(e))(lambda e=e: fill(ends_ref[e] - MOE_ROWS).start())
        lax.fori_loop(used, n_blocks, lambda b, c: (fill(b * MOE_ROWS).start(), c)[1], 0)
        for e in range(N_EXPERTS):
            pl.when(nonempty(e))(lambda e=e: fill(ends_ref[e] - MOE_ROWS).wait())
        lax.fori_loop(used, n_blocks, lambda b, c: (fill(b * MOE_ROWS).wait(), c)[1], 0)

    hb = h_ref[...]
    low = [rt_ref[k:k + 1, :] for k in range(TOP_K)]
    sub = [rt_ref[TOP_K + k:TOP_K + k + 1, :] for k in range(TOP_K)]
    srow = lax.broadcasted_iota(jnp.int32, (SLOT_SUB, tm), 0).astype(F32).astype(BF16)
    one = jnp.ones((SLOT_SUB, tm), BF16)

    for ci in range(SLOTS_PER_TILE // SLOT_CHUNK):
        parts = []
        for b in range(SLOT_CHUNK // SLOT_SUB):
            blk = float(ci * (SLOT_CHUNK // SLOT_SUB) + b)
            own = jnp.zeros((SLOT_SUB, tm), BF16)
            for k in range(TOP_K):
                here = jnp.where(sub[k] == blk, low[k], -1.0).astype(BF16)
                own = jnp.where(srow == here, one, own)
            parts.append(own)
        own = jnp.concatenate(parts, axis=0)
        buf[slot, ci * SLOT_CHUNK:(ci + 1) * SLOT_CHUNK, :] = jnp.dot(own, hb, preferred_element_type=F32)

    def copy(s, lo, go, rows):
        return pltpu.make_async_copy(buf.at[s, pl.ds(lo, rows), :], xs_ref.at[pl.ds(go, rows), :], sem.at[s])

    _start_runs(t, g8_ref, loff_ref, goff_ref, functools.partial(copy, slot))

    def drain(tile, s):
        rows = pl.multiple_of(tot_ref[tile], GROUP_ALIGN)
        copy(s, 0, 0, rows).wait()

    pl.when(t > 0)(lambda: drain(t - 1, 1 - slot))
    pl.when(t == nt - 1)(lambda: drain(t, slot))


def _dispatch(g8f, lofff, gofff, totf, ends, h2, routet, p_rows):
    n = h2.shape[0]
    tm = ROUTE_TILE
    return pl.pallas_call(
        _dispatch_kernel,
        grid_spec=pltpu.PrefetchScalarGridSpec(
            num_scalar_prefetch=5,
            grid=(n // tm,),
            in_specs=[pl.BlockSpec((tm, XS_COLS), lambda i, *_: (i, 0)),
                      pl.BlockSpec((8, tm), lambda i, *_: (0, i))],
            out_specs=pl.BlockSpec(memory_space=pl.ANY),
            scratch_shapes=[pltpu.VMEM((2, SLOTS_PER_TILE, XS_COLS), F32),
                            pltpu.VMEM((MOE_ROWS, XS_COLS), F32),
                            pltpu.SemaphoreType.DMA((2,)), pltpu.SemaphoreType.DMA(())]),
        out_shape=jax.ShapeDtypeStruct((p_rows, XS_COLS), F32),
        compiler_params=_params(("arbitrary",)),
        name="dispatch",
    )(g8f, lofff, gofff, totf, ends, h2, routet)


def _experts_kernel(be_ref, nu_ref, xs_ref, w1_ref, b1_ref, w2_ref, b2_ref, ys_ref):
    @pl.when(pl.program_id(0) < nu_ref[0])
    def _():
        xb = xs_ref[:, 0:D_MODEL].astype(BF16)
        e = be_ref[pl.program_id(0)]
        wext = xs_ref[:, D_MODEL:XS_COLS]
        lane = lax.broadcasted_iota(jnp.int32, wext.shape, 1)
        wcol = jnp.sum(jnp.where((lane == e) | (lane == e + N_EXPERTS), wext, 0.0), axis=1, keepdims=True)
        hcat = jnp.dot(xb, w1_ref[0].astype(BF16), preferred_element_type=F32) + b1_ref[0]
        x_glu = jnp.minimum(hcat[:, :D_EXPERT], SWIGLU_LIMIT)
        x_lin = jnp.clip(hcat[:, D_EXPERT:], -SWIGLU_LIMIT, SWIGLU_LIMIT)
        act = (x_glu * jax.nn.sigmoid(SWIGLU_ALPHA * x_glu) * (x_lin + 1.0)).astype(BF16)
        ys_ref[...] = (jnp.dot(act, w2_ref[0].astype(BF16), preferred_element_type=F32) + b2_ref[0]) * wcol

    @pl.when(pl.program_id(0) >= nu_ref[0])
    def _():
        ys_ref[...] = jnp.zeros(ys_ref.shape, F32)


def _experts(block_e, n_used, xs, w1, b1, w2, b2):
    p = xs.shape[0]
    mb = MOE_ROWS
    last = lambda i, nu: jnp.minimum(i, nu[0] - 1)
    rowmap = lambda i, be, nu: (last(i, nu), 0)
    wmap = lambda i, be, nu: (be[last(i, nu)], 0, 0)
    return pl.pallas_call(
        _experts_kernel,
        grid_spec=pltpu.PrefetchScalarGridSpec(
            num_scalar_prefetch=2,
            grid=(p // mb,),
            in_specs=[pl.BlockSpec((mb, XS_COLS), rowmap),
                      pl.BlockSpec((1, D_MODEL, 2 * D_EXPERT), wmap),
                      pl.BlockSpec((1, 1, 2 * D_EXPERT), wmap),
                      pl.BlockSpec((1, D_EXPERT, D_MODEL), wmap),
                      pl.BlockSpec((1, 1, D_MODEL), wmap)],
            out_specs=pl.BlockSpec((mb, D_MODEL), lambda i, be, nu: (i, 0))),
        out_shape=jax.ShapeDtypeStruct((p, D_MODEL), F32),
        compiler_params=_params(("arbitrary",)),
        name="experts",
    )(block_e, n_used, xs, w1, b1, w2, b2)


def _combine_kernel(g8_ref, loff_ref, goff_ref, tot_ref, ys_ref, route_ref, x1_ref, g2_ref, fn_ref, o_ref, stg, sem):
    t = pl.program_id(0)
    nt = pl.num_programs(0)
    tm = x1_ref.shape[0]
    slot = lax.rem(t, 2)

    def copy(s, lo, go, rows):
        return pltpu.make_async_copy(ys_ref.at[pl.ds(go, rows), :], stg.at[s, pl.ds(lo, rows), :], sem.at[s])

    @pl.when(t == 0)
    def _():
        stg[...] = jnp.zeros(stg.shape, F32)
        _start_runs(t, g8_ref, loff_ref, goff_ref, functools.partial(copy, 0))

    pl.when(t + 1 < nt)(lambda: _start_runs(t + 1, g8_ref, loff_ref, goff_ref, functools.partial(copy, 1 - slot)))
    copy(slot, 0, 0, pl.multiple_of(tot_ref[t], GROUP_ALIGN)).wait()

    route = route_ref[...]
    low = [jnp.broadcast_to(route[:, k:k + 1], (tm, SLOT_SUB)).astype(BF16) for k in range(TOP_K)]
    sub = [jnp.broadcast_to(route[:, TOP_K + k:TOP_K + k + 1], (tm, SLOT_SUB)).astype(BF16) for k in range(TOP_K)]
    scol = lax.broadcasted_iota(jnp.int32, (tm, SLOT_SUB), 1).astype(F32).astype(BF16)
    one = jnp.ones((tm, SLOT_SUB), BF16)
    never = jnp.full((tm, SLOT_SUB), -1.0, BF16)

    y = None
    for ci in range(SLOTS_PER_TILE // SLOT_CHUNK):
        parts = []
        for b in range(SLOT_CHUNK // SLOT_SUB):
            blk = float(ci * (SLOT_CHUNK // SLOT_SUB) + b)
            own = jnp.zeros((tm, SLOT_SUB), BF16)
            for k in range(TOP_K):
                own = jnp.where(scol == jnp.where(sub[k] == blk, low[k], never), one, own)
            parts.append(own)
        own = jnp.concatenate(parts, axis=1)
        rows = stg[slot, ci * SLOT_CHUNK:(ci + 1) * SLOT_CHUNK, :].astype(BF16)
        part = jnp.dot(own, rows, preferred_element_type=F32)
        y = part if y is None else y + part
    x = x1_ref[...] + g2_ref[0] * y
    o_ref[...] = _rms(x) * fn_ref[...]


def _combine(g8f, lofff, gofff, totf, ys, route, x1, g2, fn, seq):
    n = x1.shape[0]
    tm = ROUTE_TILE
    per_b = seq // tm
    return pl.pallas_call(
        _combine_kernel,
        grid_spec=pltpu.PrefetchScalarGridSpec(
            num_scalar_prefetch=4,
            grid=(n // tm,),
            in_specs=[pl.BlockSpec(memory_space=pl.ANY),
                      pl.BlockSpec((tm, LANE), lambda i, *_: (i, 0)),
                      pl.BlockSpec((tm, D_MODEL), lambda i, *_: (i, 0)),
                      pl.BlockSpec((1, 1, D_MODEL), lambda i, *_: (i // per_b, 0, 0)),
                      pl.BlockSpec((1, D_MODEL), lambda i, *_: (0, 0))],
            out_specs=pl.BlockSpec((tm, D_MODEL), lambda i, *_: (i, 0)),
            scratch_shapes=[pltpu.VMEM((2, SLOTS_PER_TILE, D_MODEL), F32), pltpu.SemaphoreType.DMA((2,))]),
        out_shape=jax.ShapeDtypeStruct((n, D_MODEL), F32),
        compiler_params=_params(("arbitrary",)),
        name="combine",
    )(g8f, lofff, gofff, totf, ys, route, x1, g2, fn)


def _split_w_in(w_in):
    o_kr = _W_MIX_COLS
    w_kr = jnp.pad(w_in[:, o_kr:o_kr + MLA_ROPE_DIM],
                   ((0, 0), (MLA_NOPE_DIM, LANE - MLA_NOPE_DIM - MLA_ROPE_DIM)))
    return w_in[:, :o_kr].astype(BF16), w_kr.astype(BF16), w_in[:, o_kr + MLA_ROPE_DIM:].astype(BF16)


def _pack_heads(w, width, pieces):
    rows = w.shape[0]
    w3 = w.reshape(rows, MLA_HEADS, width)
    cols = [w3[:, :, a:b] for a, b in pieces]
    used = sum(b - a for a, b in pieces)
    cols.append(jnp.zeros((rows, MLA_HEADS, LANE - used), w.dtype))
    return jnp.concatenate(cols, axis=2).reshape(rows, MLA_HEADS * LANE)


def kernel(x, c, positions, w_ada, b_ada, norm_mix, norm_ffn, w_in, sinks, q_norm, kv_norm, w_uq, w_uk, w_uv,
           w_branch_a, w_branch_b, w_out, w_router, b_router, w_moe1, b_moe1, w_moe2, b_moe2, final_norm):
    batch, seq, d = x.shape
    n = batch * seq
    assert d == D_MODEL and w_ada.shape[0] == 1
    assert seq % TQ_MLA == 0 and seq % (SWA_QB * ATTN_BLOCK) == 0 and seq % TM_PROJ == 0 and seq % ROUTE_TILE == 0
    hw = MLA_HEADS * LANE
    qk = MLA_NOPE_DIM + MLA_ROPE_DIM

    w_mix, w_kr, w_gates = _split_w_in(w_in[0])
    wuq_p = _pack_heads(w_uq[0], qk, [(0, qk)]).astype(BF16)
    wuk_p = _pack_heads(w_uk[0], MLA_NOPE_DIM, [(0, MLA_NOPE_DIM)]).astype(BF16)
    wuv_p = _pack_heads(w_uv[0], MLA_V_DIM, [(0, MLA_V_DIM)]).astype(BF16)
    one_row = jnp.tile((jnp.arange(LANE) == MLA_V_DIM).astype(F32), MLA_HEADS)[None, :]
    freqs = ROPE_THETA ** (-jnp.arange(0, MLA_ROPE_DIM, 2, dtype=F32) / MLA_ROPE_DIM)
    frq = jnp.concatenate([jnp.zeros((MLA_NOPE_DIM,), F32), freqs, freqs,
                           jnp.zeros((LANE - qk,), F32)])[None, :]
    wb3 = w_branch_b[0].reshape(MLA_HEADS, MLA_V_DIM, D_MODEL)
    wb_p = jnp.concatenate([wb3, jnp.zeros((MLA_HEADS, LANE - MLA_V_DIM, D_MODEL), F32)],
                           axis=1).reshape(hw, D_MODEL).astype(BF16)
    wa = w_branch_a[0].astype(BF16)
    wo = w_out[0].astype(BF16)
    wr = jnp.pad(w_router[0], ((0, 0), (0, LANE - N_EXPERTS)))
    wr_hi = wr.astype(BF16)
    wr_lo = (wr - wr_hi.astype(F32)).astype(BF16)
    br_p = jnp.pad(b_router[0], (0, LANE - N_EXPERTS))[None, :]
    b1 = b_moe1[0][:, None, :]
    b2 = b_moe2[0][:, None, :]
    slopes = jnp.asarray(np.exp2(-8.0 * np.arange(1, SWA_HEADS + 1) / SWA_HEADS), dtype=F32)

    c8 = jnp.pad(c, ((0, 8 - batch), (0, 0)))
    mod = _ada(c8, w_ada[0], b_ada[0][None, :])[:batch]
    sh1, sc1, g1, sh2, sc2, g2 = [m[:, None, :] for m in jnp.split(mod, 6, axis=-1)]

    x2 = x.reshape(n, D_MODEL)
    pos2 = positions.reshape(n, 1).astype(jnp.int32)
    qa, ka2, va2, qm, km, vm, ga, gb, stats = _inproj(
        x2, pos2, sh1, sc1, norm_mix, w_mix, w_kr, w_gates, q_norm, kv_norm, wuq_p, wuk_p, wuv_p, frq, one_row, seq)
    st = stats.reshape(batch, seq // TM_PROJ, 8, LANE)
    bound_sq = jnp.max(st[:, :, :, 0], axis=1) * jnp.max(st[:, :, :, 1], axis=1)
    plain = (bound_sq <= MLA_PLAIN_BOUND ** 2).astype(jnp.int32).reshape(-1)
    yb = _mla(plain, qm, km, vm, batch, seq)
    ya = _swa(sinks[0], slopes, qa, ka2, va2, batch, seq)

    x1, h2, route, routet, g8 = _outproj(
        ya, yb, ga, gb, x2, g1, sh2, sc2, norm_ffn, wa, wb_p, wo, wr_hi, wr_lo, br_p, seq)

    n_tiles = n // ROUTE_TILE
    g8t = g8.reshape(n_tiles, 8, LANE)[:, 0, :N_EXPERTS].astype(jnp.int32)
    loff = jnp.cumsum(g8t, axis=1) - g8t
    padded = ((jnp.sum(g8t, axis=0) + MOE_ROWS - 1) // MOE_ROWS) * MOE_ROWS
    pends = jnp.cumsum(padded)
    goff = (pends - padded)[None, :] + jnp.cumsum(g8t, axis=0) - g8t
    tot = jnp.sum(g8t, axis=1).astype(jnp.int32)
    p_rows = n * TOP_K + n_tiles * N_EXPERTS * (GROUP_ALIGN - 1) + N_EXPERTS * (MOE_ROWS - 1)
    p_rows = -(-p_rows // MOE_ROWS) * MOE_ROWS
    n_blocks = p_rows // MOE_ROWS
    block_start = jnp.arange(n_blocks, dtype=jnp.int32) * MOE_ROWS
    block_e = jnp.minimum(jnp.sum((pends[None, :] <= block_start[:, None]).astype(jnp.int32), axis=1),
                          N_EXPERTS - 1).astype(jnp.int32)
    n_used = (pends[-1:] // MOE_ROWS).astype(jnp.int32)
    tabs = (g8t.reshape(-1), loff.reshape(-1).astype(jnp.int32), goff.reshape(-1).astype(jnp.int32), tot)

    xs = _dispatch(*tabs, pends.astype(jnp.int32), h2, routet, p_rows)
    ys = _experts(block_e, n_used, xs, w_moe1[0], b1, w_moe2[0], b2)
    out = _combine(*tabs, ys, route, x1, g2, final_norm[None, :], seq)
    return out.reshape(batch, seq, D_MODEL)
```

```python
import functools

import numpy as np
import jax
import jax.numpy as jnp
from jax import lax
from jax.experimental import pallas as pl
from jax.experimental.pallas import tpu as pltpu

D_MODEL = 1024
SWA_HEADS = 8
SWA_KV_HEADS = 2
SWA_HEAD_DIM = 64
ATTN_BLOCK = 128
MLA_HEADS = 8
MLA_Q_RANK = 384
MLA_KV_RANK = 256
MLA_NOPE_DIM = 64
MLA_ROPE_DIM = 32
MLA_V_DIM = 64
ROPE_THETA = 10000.0
N_EXPERTS = 32
TOP_K = 4
D_EXPERT = 1024
SWIGLU_LIMIT = 7.0
SWIGLU_ALPHA = 1.702
NORM_EPS = 1e-6

LANE = 128
LOG2E = 1.4426950408889634
NEG = -1e30
SWA_QSCALE = SWA_HEAD_DIM ** -0.5 * LOG2E
MLA_QSCALE = (MLA_NOPE_DIM + MLA_ROPE_DIM) ** -0.5 * LOG2E
HALF_ROPE = MLA_ROPE_DIM // 2
MLA_PLAIN_BOUND = 60.0
MLA_WIDE = 8

BF16 = jnp.bfloat16
F32 = jnp.float32

TM_PROJ = 512
TQ_MLA = 512
SWA_QB = 4
MOE_ROWS = 512
ROUTE_TILE = 512
GROUP_ALIGN = 8
SLOTS_PER_TILE = ROUTE_TILE * TOP_K + N_EXPERTS * GROUP_ALIGN
SLOT_SUB = 256
SLOT_CHUNK = 3 * SLOT_SUB
XS_COLS = D_MODEL + LANE
VMEM_LIMIT = 56 * 1024 * 1024


def _params(sem, vmem=VMEM_LIMIT):
    return pltpu.CompilerParams(dimension_semantics=sem, vmem_limit_bytes=vmem)


def _rms(x):
    return x * lax.rsqrt(jnp.mean(x * x, axis=-1, keepdims=True) + NORM_EPS)


def _ada_kernel(c_ref, w_ref, b_ref, o_ref):
    c = c_ref[...]
    a = (c * jax.nn.sigmoid(c)).astype(BF16)
    o_ref[...] = jnp.dot(a, w_ref[...].astype(BF16), preferred_element_type=F32) + b_ref[...]


def _ada(c8, w_ada, b_ada):
    n_out = w_ada.shape[1]
    return pl.pallas_call(
        _ada_kernel,
        grid=(n_out // D_MODEL,),
        in_specs=[pl.BlockSpec((8, D_MODEL), lambda j: (0, 0)),
                  pl.BlockSpec((D_MODEL, D_MODEL), lambda j: (0, j)),
                  pl.BlockSpec((1, D_MODEL), lambda j: (0, j))],
        out_specs=pl.BlockSpec((8, D_MODEL), lambda j: (0, j)),
        out_shape=jax.ShapeDtypeStruct((8, n_out), F32),
        compiler_params=_params(("arbitrary",)),
        name="ada",
    )(c8, w_ada, b_ada)


_C_QA = (0, 512)
_C_KV = (512, 768)
_C_LAT = (768, 1408)
_W_MIX_COLS = 1408
_ROPE_LO = (MLA_NOPE_DIM, MLA_NOPE_DIM + HALF_ROPE)
_ROPE_HI = (MLA_NOPE_DIM + HALF_ROPE, MLA_NOPE_DIM + MLA_ROPE_DIM)


def _inproj_kernel(x_ref, pos_ref, sh_ref, sc_ref, nm_ref, win_ref, wkr_ref, wg_ref, qn_ref, kvn_ref, wuq_ref, wuk_ref,
                   wuv_ref, frq_ref, one_ref,
                   qa_ref, ka_ref, va_ref, qm_ref, km_ref, vm_ref, ga_ref, gb_ref, st_ref):
    tm = x_ref.shape[0]
    x = x_ref[...]
    h = (_rms(x) * nm_ref[...] * (1.0 + sc_ref[0]) + sh_ref[0]).astype(BF16)

    def proj(c):
        return jnp.dot(h, win_ref[:, c[0]:c[1]], preferred_element_type=F32)

    lane = lax.broadcasted_iota(jnp.int32, (tm, LANE), 1)
    first = lane < SWA_HEAD_DIM

    def twice(t):
        r = pltpu.roll(t, SWA_HEAD_DIM, axis=1)
        return jnp.concatenate([jnp.where(first, t, r), jnp.where(first, r, t)], axis=1)

    kv = proj(_C_KV)
    ka_ref[...] = twice(kv[:, :LANE]).astype(BF16)
    va_ref[...] = twice(kv[:, LANE:]).astype(BF16)

    ang = pos_ref[...].astype(F32) * frq_ref[...]
    cs = jnp.cos(ang)
    sn = jnp.sin(ang)
    sn_lo = jnp.where((lane >= _ROPE_LO[0]) & (lane < _ROPE_LO[1]), -sn, 0.0)
    sn_hi = jnp.where((lane >= _ROPE_HI[0]) & (lane < _ROPE_HI[1]), sn, 0.0)

    def rotary(t):
        return (t * cs + pltpu.roll(t, LANE - HALF_ROPE, axis=1) * sn_lo
                + pltpu.roll(t, HALF_ROPE, axis=1) * sn_hi)

    lat = proj(_C_LAT)
    cqn = (_rms(lat[:, 0:MLA_Q_RANK]) * qn_ref[...]).astype(BF16)
    ckvn = (_rms(lat[:, MLA_Q_RANK:MLA_Q_RANK + MLA_KV_RANK]) * kvn_ref[...]).astype(BF16)
    krr = rotary(jnp.dot(h, wkr_ref[...], preferred_element_type=F32))
    q = jnp.dot(cqn, wuq_ref[...], preferred_element_type=F32)
    kn = jnp.dot(ckvn, wuk_ref[...], preferred_element_type=F32)

    def max_sq_norm(t):
        tf = t.astype(F32)
        return jnp.max(jnp.sum(tf * tf, axis=1, keepdims=True), axis=0, keepdims=True)

    rid = lax.broadcasted_iota(jnp.int32, (8, LANE), 0)
    lid = lax.broadcasted_iota(jnp.int32, (8, LANE), 1)
    stats = jnp.zeros((8, LANE), F32)
    for hh in range(MLA_HEADS):
        sl = slice(hh * LANE, (hh + 1) * LANE)
        qb = (rotary(q[:, sl]) * MLA_QSCALE).astype(BF16)
        kb = (kn[:, sl] + krr).astype(BF16)
        qm_ref[:, sl] = qb
        km_ref[:, sl] = kb
        stats = jnp.where((rid == hh) & (lid == 0), max_sq_norm(qb), stats)
        stats = jnp.where((rid == hh) & (lid == 1), max_sq_norm(kb), stats)
    st_ref[...] = stats
    vm_ref[...] = (jnp.dot(ckvn, wuv_ref[...], preferred_element_type=F32) + one_ref[...]).astype(BF16)
    qa_ref[...] = (proj(_C_QA) * SWA_QSCALE).astype(BF16)
    ga_ref[...] = jax.nn.sigmoid(jnp.dot(h, wg_ref[:, 0:D_MODEL], preferred_element_type=F32)).astype(BF16)
    gb_ref[...] = jax.nn.sigmoid(jnp.dot(h, wg_ref[:, D_MODEL:], preferred_element_type=F32)).astype(BF16)


def _inproj(x2, pos2, sh1, sc1, nm, w_mix, w_kr, w_gates, qn, kvn, wuq_p, wuk_p, wuv_p, frq, one_row, seq):
    n = x2.shape[0]
    tm = TM_PROJ
    per_b = seq // tm
    hw = MLA_HEADS * LANE
    row = lambda i: (i, 0)
    fix = lambda i: (0, 0)
    bsel = lambda i: (i // per_b, 0, 0)
    widths = (512, 256, 256, hw, hw, hw, D_MODEL, D_MODEL)
    return pl.pallas_call(
        _inproj_kernel,
        grid=(n // tm,),
        in_specs=[pl.BlockSpec((tm, D_MODEL), row),
                  pl.BlockSpec((tm, 1), row),
                  pl.BlockSpec((1, 1, D_MODEL), bsel),
                  pl.BlockSpec((1, 1, D_MODEL), bsel),
                  pl.BlockSpec((1, D_MODEL), fix),
                  pl.BlockSpec((D_MODEL, _W_MIX_COLS), fix),
                  pl.BlockSpec((D_MODEL, LANE), fix),
                  pl.BlockSpec((D_MODEL, 2 * D_MODEL), fix),
                  pl.BlockSpec((1, MLA_Q_RANK), fix),
                  pl.BlockSpec((1, MLA_KV_RANK), fix),
                  pl.BlockSpec((MLA_Q_RANK, hw), fix),
                  pl.BlockSpec((MLA_KV_RANK, hw), fix),
                  pl.BlockSpec((MLA_KV_RANK, hw), fix),
                  pl.BlockSpec((1, LANE), fix),
                  pl.BlockSpec((1, hw), fix)],
        out_specs=[pl.BlockSpec((tm, w), row) for w in widths] + [pl.BlockSpec((8, LANE), row)],
        out_shape=[jax.ShapeDtypeStruct((n, w), BF16) for w in widths]
        + [jax.ShapeDtypeStruct((n // tm * 8, LANE), F32)],
        compiler_params=_params(("arbitrary",)),
        name="inproj",
    )(x2, pos2, sh1, sc1, nm, w_mix, w_kr, w_gates, qn, kvn, wuq_p, wuk_p, wuv_p, frq, one_row)


def _mla_kernel(plain_ref, q_ref, k_ref, v_ref, o_ref, m_scr, acc_scr):
    t = TQ_MLA
    nq = q_ref.shape[0] // t

    def scores(i, j, nblk, diag_at):
        q = q_ref[pl.ds(pl.multiple_of(i * t, t), t), :]
        start = pl.multiple_of(j * t, t)
        k = k_ref[pl.ds(start, nblk * t), :]
        v = v_ref[pl.ds(start, nblk * t), :]
        s = lax.dot_general(q, k, (((1,), (1,)), ((), ())), preferred_element_type=F32)
        if diag_at is not None:
            qi = lax.broadcasted_iota(jnp.int32, s.shape, 0)
            kj = lax.broadcasted_iota(jnp.int32, s.shape, 1)
            s = jnp.where(kj <= qi + diag_at * t, s, NEG)
        return s, v

    def plain_step(i, j, nblk, diag_at):
        s, v = scores(i, j, nblk, diag_at)
        acc_scr[...] += jnp.dot(jnp.exp2(s).astype(BF16), v, preferred_element_type=F32)

    def online_step(i, j, nblk, diag_at):
        s, v = scores(i, j, nblk, diag_at)
        m_old = m_scr[...]
        m_new = jnp.maximum(m_old, jnp.max(s, axis=1, keepdims=True))
        p = jnp.exp2(s - m_new)
        alpha = jnp.exp2(m_old - m_new)
        acc_scr[...] = acc_scr[...] * alpha + jnp.dot(p.astype(BF16), v, preferred_element_type=F32)
        m_scr[...] = m_new

    def run(step, wide, online):
        def q_tile(i, carry):
            acc_scr[...] = jnp.zeros(acc_scr.shape, F32)
            if online:
                m_scr[...] = jnp.full(m_scr.shape, NEG, F32)
            nw = lax.shift_right_logical(i, wide.bit_length() - 1)
            left = i - nw * wide

            def wide_body(jj, c):
                step(i, jj * wide, wide, None)
                return c

            lax.fori_loop(0, nw, wide_body, 0)
            for r in range(wide):
                pl.when(left == r)(lambda r=r: step(i, nw * wide, r + 1, r))
            acc = acc_scr[...]
            o_ref[pl.ds(pl.multiple_of(i * t, t), t), :] = (acc / acc[:, MLA_V_DIM:MLA_V_DIM + 1]).astype(BF16)
            return carry

        lax.fori_loop(0, nq, q_tile, 0)

    small = plain_ref[pl.program_id(0) * MLA_HEADS + pl.program_id(1)] != 0
    pl.when(small)(lambda: run(plain_step, MLA_WIDE, False))
    pl.when(jnp.logical_not(small))(lambda: run(online_step, 1, True))


def _mla(plain, qm, km, vm, batch, seq):
    n = qm.shape[0]
    t = TQ_MLA
    kmap = lambda b, h, *_: (b, h)
    return pl.pallas_call(
        _mla_kernel,
        grid_spec=pltpu.PrefetchScalarGridSpec(
            num_scalar_prefetch=1,
            grid=(batch, MLA_HEADS),
            in_specs=[pl.BlockSpec((seq, LANE), kmap),
                      pl.BlockSpec((seq, LANE), kmap),
                      pl.BlockSpec((seq, LANE), kmap)],
            out_specs=pl.BlockSpec((seq, LANE), kmap),
            scratch_shapes=[pltpu.VMEM((t, 1), F32), pltpu.VMEM((t, LANE), F32)]),
        out_shape=jax.ShapeDtypeStruct((n, MLA_HEADS * LANE), BF16),
        compiler_params=_params(("arbitrary", "arbitrary")),
        name="mla",
    )(plain, qm, km, vm)


def _swa_kernel(sink_ref, slope_ref, q_ref, kc_ref, kp_ref, vc_ref, vp_ref, o_ref, bias_scr):
    b = pl.program_id(0)
    i = pl.program_id(1)
    blk = ATTN_BLOCK

    @pl.when((b == 0) & (i == 0))
    def _():
        qi = lax.broadcasted_iota(jnp.int32, (blk, 2 * blk), 0)
        kj = lax.broadcasted_iota(jnp.int32, (blk, 2 * blk), 1)
        dist = qi - kj + blk
        valid = (dist >= 0) & (dist < blk)
        distf = dist.astype(F32)
        for hd in range(SWA_HEADS):
            bias_scr[hd] = jnp.where(valid, -(slope_ref[hd] * LOG2E) * distf, NEG)

    lane = lax.broadcasted_iota(jnp.int32, (2 * blk, LANE), 1)
    lo = lane < SWA_HEAD_DIM
    kcol = lax.broadcasted_iota(jnp.int32, (blk, 2 * blk), 1)
    first_pen = jnp.where(kcol < blk, jnp.where(i == 0, NEG, 0.0), 0.0)
    zero = jnp.zeros((2 * blk, LANE), BF16)

    for qb in range(SWA_QB):
        if qb == 0:
            kprev, vprev = kp_ref[...], vp_ref[...]
        else:
            kprev = kc_ref[(qb - 1) * blk:qb * blk, :]
            vprev = vc_ref[(qb - 1) * blk:qb * blk, :]
        kcat = jnp.concatenate([kprev, kc_ref[qb * blk:(qb + 1) * blk, :]], axis=0)
        vcat = jnp.concatenate([vprev, vc_ref[qb * blk:(qb + 1) * blk, :]], axis=0)
        for g in range(SWA_KV_HEADS):
            kg = kcat[:, g * LANE:(g + 1) * LANE]
            vg = vcat[:, g * LANE:(g + 1) * LANE]
            halves = ((jnp.where(lo, kg, zero), jnp.where(lo, vg, zero)),
                      (jnp.where(lo, zero, kg), jnp.where(lo, zero, vg)))
            for pp in range(2):
                pr = g * 2 + pp
                qpair = q_ref[qb * blk:(qb + 1) * blk, pr * LANE:(pr + 1) * LANE]
                out = None
                for half in range(2):
                    hd = pr * 2 + half
                    kx, vx = halves[half]
                    s = lax.dot_general(qpair, kx, (((1,), (1,)), ((), ())), preferred_element_type=F32)
                    s = s + bias_scr[hd]
                    if qb == 0:
                        s = s + first_pen
                    sink = sink_ref[hd] * LOG2E
                    m = jnp.maximum(jnp.max(s, axis=1, keepdims=True), sink)
                    p = jnp.exp2(s - m)
                    denom = jnp.sum(p, axis=1, keepdims=True) + jnp.exp2(sink - m)
                    o = jnp.dot(p.astype(BF16), vx, preferred_element_type=F32) * (1.0 / denom)
                    out = o if out is None else out + o
                o_ref[qb * blk:(qb + 1) * blk, pr * LANE:(pr + 1) * LANE] = out.astype(BF16)


def _swa(sinks, slopes, qa, ka2, va2, batch, seq):
    n = qa.shape[0]
    blk = ATTN_BLOCK
    rows = SWA_QB * blk
    steps = seq // rows
    cur = lambda b, i: (b * steps + i, 0)
    prev = lambda b, i: (b * (seq // blk) + jnp.maximum(i * SWA_QB - 1, 0), 0)
    smem = pl.BlockSpec(memory_space=pltpu.SMEM)
    return pl.pallas_call(
        _swa_kernel,
        grid=(batch, steps),
        in_specs=[smem, smem,
                  pl.BlockSpec((rows, 512), cur),
                  pl.BlockSpec((rows, 256), cur),
                  pl.BlockSpec((blk, 256), prev),
                  pl.BlockSpec((rows, 256), cur),
                  pl.BlockSpec((blk, 256), prev)],
        out_specs=pl.BlockSpec((rows, 512), cur),
        out_shape=jax.ShapeDtypeStruct((n, 512), BF16),
        scratch_shapes=[pltpu.VMEM((SWA_HEADS, blk, 2 * blk), F32)],
        compiler_params=_params(("arbitrary", "arbitrary")),
        name="swa",
    )(sinks, slopes, qa, ka2, ka2, va2, va2)


def _outproj_kernel(ya_ref, yb_ref, ga_ref, gb_ref, x_ref, g1_ref, sh_ref, sc_ref, nm_ref, wa_ref, wb_ref,
                    wo_ref, wrh_ref, wrl_ref, br_ref,
                    x1_ref, h2_ref, route_ref, routet_ref, g8_ref):
    tm = x_ref.shape[0]
    a = jnp.dot(ya_ref[...], wa_ref[...], preferred_element_type=F32)
    bm = jnp.dot(yb_ref[...], wb_ref[...], preferred_element_type=F32)
    mixed = (ga_ref[...].astype(F32) * a + gb_ref[...].astype(F32) * bm).astype(BF16)
    x1 = x_ref[...] + g1_ref[0] * jnp.dot(mixed, wo_ref[...], preferred_element_type=F32)
    x1_ref[...] = x1
    h2 = _rms(x1) * nm_ref[...] * (1.0 + sc_ref[0]) + sh_ref[0]
    hi = h2.astype(BF16)
    h2_ref[:, 0:D_MODEL] = hi

    lo = (h2 - hi.astype(F32)).astype(BF16)
    logits = (jnp.dot(hi, wrh_ref[...], preferred_element_type=F32)
              + jnp.dot(lo, wrh_ref[...], preferred_element_type=F32)
              + jnp.dot(hi, wrl_ref[...], preferred_element_type=F32)) + br_ref[...]

    lane = lax.broadcasted_iota(jnp.int32, (tm, LANE), 1).astype(F32)
    work = jnp.where(lane < N_EXPERTS, logits, -jnp.inf)
    vals, idxs = [], []
    for _ in range(TOP_K):
        m = jnp.max(work, axis=1, keepdims=True)
        idx = jnp.min(jnp.where(work == m, lane, float(LANE)), axis=1, keepdims=True)
        vals.append(m)
        idxs.append(idx)
        work = jnp.where(lane == idx, -jnp.inf, work)
    exps = [jnp.exp(v - vals[0]) for v in vals]
    tot = exps[0] + exps[1] + exps[2] + exps[3]

    onehot = jnp.zeros((tm, LANE), F32)
    for idx in idxs:
        onehot = onehot + (lane == idx).astype(F32)
    r = lax.broadcasted_iota(jnp.int32, (tm, tm), 0)
    c = lax.broadcasted_iota(jnp.int32, (tm, tm), 1)
    prefix = jnp.dot((c < r).astype(BF16), onehot.astype(BF16), preferred_element_type=F32)
    groups = jnp.ceil(jnp.sum(onehot, axis=0, keepdims=True) * (1.0 / GROUP_ALIGN))
    er = lax.broadcasted_iota(jnp.int32, (LANE, LANE), 0)
    ec = lax.broadcasted_iota(jnp.int32, (LANE, LANE), 1)
    before = jnp.dot(jnp.broadcast_to(groups, (8, LANE)).astype(BF16), (er < ec).astype(BF16),
                     preferred_element_type=F32)[0:1, :]
    slot_e = before * GROUP_ALIGN + prefix

    route = jnp.zeros((tm, LANE), F32)
    wext = jnp.zeros((tm, LANE), F32)
    for k in range(TOP_K):
        sk = jnp.sum(jnp.where(lane == idxs[k], slot_e, 0.0), axis=1, keepdims=True)
        sub = jnp.floor(sk * (1.0 / SLOT_SUB))
        route = jnp.where(lane == float(k), sk - sub * SLOT_SUB, route)
        route = jnp.where(lane == float(TOP_K + k), sub, route)
        wk = exps[k] / tot
        wk_hi = wk.astype(BF16).astype(F32)
        wext = jnp.where(lane == idxs[k], wk_hi, wext)
        wext = jnp.where(lane == idxs[k] + float(N_EXPERTS), wk - wk_hi, wext)
    h2_ref[:, D_MODEL:XS_COLS] = wext.astype(BF16)
    route_ref[...] = route
    routet_ref[...] = route.T[0:8, :]
    g8_ref[...] = jnp.broadcast_to(groups * GROUP_ALIGN, (8, LANE))


def _outproj(ya, yb, ga, gb, x2, g1, sh2, sc2, nm, wa, wb_p, wo, wr_hi, wr_lo, br_p, seq):
    n = x2.shape[0]
    tm = ROUTE_TILE
    per_b = seq // tm
    hw = MLA_HEADS * LANE
    row = lambda i: (i, 0)
    fix = lambda i: (0, 0)
    bsel = lambda i: (i // per_b, 0, 0)
    return pl.pallas_call(
        _outproj_kernel,
        grid=(n // tm,),
        in_specs=[pl.BlockSpec((tm, 512), row),
                  pl.BlockSpec((tm, hw), row),
                  pl.BlockSpec((tm, D_MODEL), row),
                  pl.BlockSpec((tm, D_MODEL), row),
                  pl.BlockSpec((tm, D_MODEL), row),
                  pl.BlockSpec((1, 1, D_MODEL), bsel),
                  pl.BlockSpec((1, 1, D_MODEL), bsel),
                  pl.BlockSpec((1, 1, D_MODEL), bsel),
                  pl.BlockSpec((1, D_MODEL), fix),
                  pl.BlockSpec((512, D_MODEL), fix),
                  pl.BlockSpec((hw, D_MODEL), fix),
                  pl.BlockSpec((D_MODEL, D_MODEL), fix),
                  pl.BlockSpec((D_MODEL, LANE), fix),
                  pl.BlockSpec((D_MODEL, LANE), fix),
                  pl.BlockSpec((1, LANE), fix)],
        out_specs=[pl.BlockSpec((tm, D_MODEL), row),
                   pl.BlockSpec((tm, XS_COLS), row),
                   pl.BlockSpec((tm, LANE), row),
                   pl.BlockSpec((8, tm), lambda i: (0, i)),
                   pl.BlockSpec((8, LANE), row)],
        out_shape=[jax.ShapeDtypeStruct((n, D_MODEL), F32),
                   jax.ShapeDtypeStruct((n, XS_COLS), BF16),
                   jax.ShapeDtypeStruct((n, LANE), F32),
                   jax.ShapeDtypeStruct((8, n), F32),
                   jax.ShapeDtypeStruct((n // tm * 8, LANE), F32)],
        compiler_params=_params(("arbitrary",)),
        name="outproj",
    )(ya, yb, ga, gb, x2, g1, sh2, sc2, nm, wa, wb_p, wo, wr_hi, wr_lo, br_p)


def _start_runs(tile, g8_ref, loff_ref, goff_ref, make_copy):
    for e in range(N_EXPERTS):
        g = g8_ref[tile * N_EXPERTS + e]
        lo = loff_ref[tile * N_EXPERTS + e]
        go = goff_ref[tile * N_EXPERTS + e]
        def go_copy(lo=lo, go=go, g=g):
            make_copy(pl.multiple_of(lo, GROUP_ALIGN), pl.multiple_of(go, GROUP_ALIGN),
                      pl.multiple_of(g, GROUP_ALIGN)).start()

        pl.when(g > 0)(go_copy)


def _dispatch_kernel(g8_ref, loff_ref, goff_ref, tot_ref, ends_ref, h_ref, rt_ref, xs_ref, buf, zbuf, sem, zsem):
    t = pl.program_id(0)
    nt = pl.num_programs(0)
    tm = h_ref.shape[0]
    slot = lax.rem(t, 2)

    @pl.when(t == 0)
    def _():
        zbuf[...] = jnp.zeros(zbuf.shape, zbuf.dtype)

        def fill(start):
            return pltpu.make_async_copy(zbuf, xs_ref.at[pl.ds(pl.multiple_of(start, MOE_ROWS), MOE_ROWS), :], zsem)

        def nonempty(e):
            return ends_ref[e] > (ends_ref[e - 1] if e else 0)

        used = lax.shift_right_logical(ends_ref[N_EXPERTS - 1], MOE_ROWS.bit_length() - 1)
        n_blocks = xs_ref.shape[0] // MOE_ROWS
        for e in range(N_EXPERTS):
            pl.when(nonempty(e))(lambda e=e: fill(ends_ref[e] - MOE_ROWS).start())
        lax.fori_loop(used, n_blocks, lambda b, c: (fill(b * MOE_ROWS).start(), c)[1], 0)
        for e in range(N_EXPERTS):
            pl.when(nonempty(e))(lambda e=e: fill(ends_ref[e] - MOE_ROWS).wait())
        lax.fori_loop(used, n_blocks, lambda b, c: (fill(b * MOE_ROWS).wait(), c)[1], 0)

    hb = h_ref[...]
    low = [rt_ref[k:k + 1, :] for k in range(TOP_K)]
    sub = [rt_ref[TOP_K + k:TOP_K + k + 1, :] for k in range(TOP_K)]
    srow = lax.broadcasted_iota(jnp.int32, (SLOT_SUB, tm), 0).astype(F32).astype(BF16)
    one = jnp.ones((SLOT_SUB, tm), BF16)

    for ci in range(SLOTS_PER_TILE // SLOT_CHUNK):
        parts = []
        for b in range(SLOT_CHUNK // SLOT_SUB):
            blk = float(ci * (SLOT_CHUNK // SLOT_SUB) + b)
            own = jnp.zeros((SLOT_SUB, tm), BF16)
            for k in range(TOP_K):
                here = jnp.where(sub[k] == blk, low[k], -1.0).astype(BF16)
                own = jnp.where(srow == here, one, own)
            parts.append(own)
        own = jnp.concatenate(parts, axis=0)
        buf[slot, ci * SLOT_CHUNK:(ci + 1) * SLOT_CHUNK, :] = jnp.dot(own, hb, preferred_element_type=F32)

    def copy(s, lo, go, rows):
        return pltpu.make_async_copy(buf.at[s, pl.ds(lo, rows), :], xs_ref.at[pl.ds(go, rows), :], sem.at[s])

    _start_runs(t, g8_ref, loff_ref, goff_ref, functools.partial(copy, slot))

    def drain(tile, s):
        rows = pl.multiple_of(tot_ref[tile], GROUP_ALIGN)
        copy(s, 0, 0, rows).wait()

    pl.when(t > 0)(lambda: drain(t - 1, 1 - slot))
    pl.when(t == nt - 1)(lambda: drain(t, slot))


def _dispatch(g8f, lofff, gofff, totf, ends, h2, routet, p_rows):
    n = h2.shape[0]
    tm = ROUTE_TILE
    return pl.pallas_call(
        _dispatch_kernel,
        grid_spec=pltpu.PrefetchScalarGridSpec(
            num_scalar_prefetch=5,
            grid=(n // tm,),
            in_specs=[pl.BlockSpec((tm, XS_COLS), lambda i, *_: (i, 0)),
                      pl.BlockSpec((8, tm), lambda i, *_: (0, i))],
            out_specs=pl.BlockSpec(memory_space=pl.ANY),
            scratch_shapes=[pltpu.VMEM((2, SLOTS_PER_TILE, XS_COLS), F32),
                            pltpu.VMEM((MOE_ROWS, XS_COLS), F32),
                            pltpu.SemaphoreType.DMA((2,)), pltpu.SemaphoreType.DMA(())]),
        out_shape=jax.ShapeDtypeStruct((p_rows, XS_COLS), F32),
        compiler_params=_params(("arbitrary",)),
        name="dispatch",
    )(g8f, lofff, gofff, totf, ends, h2, routet)


def _experts_kernel(be_ref, nu_ref, xs_ref, w1_ref, b1_ref, w2_ref, b2_ref, ys_ref):
    @pl.when(pl.program_id(0) < nu_ref[0])
    def _():
        xb = xs_ref[:, 0:D_MODEL].astype(BF16)
        e = be_ref[pl.program_id(0)]
        wext = xs_ref[:, D_MODEL:XS_COLS]
        lane = lax.broadcasted_iota(jnp.int32, wext.shape, 1)
        wcol = jnp.sum(jnp.where((lane == e) | (lane == e + N_EXPERTS), wext, 0.0), axis=1, keepdims=True)
        hcat = jnp.dot(xb, w1_ref[0].astype(BF16), preferred_element_type=F32) + b1_ref[0]
        x_glu = jnp.minimum(hcat[:, :D_EXPERT], SWIGLU_LIMIT)
        x_lin = jnp.clip(hcat[:, D_EXPERT:], -SWIGLU_LIMIT, SWIGLU_LIMIT)
        act = (x_glu * jax.nn.sigmoid(SWIGLU_ALPHA * x_glu) * (x_lin + 1.0)).astype(BF16)
        ys_ref[...] = (jnp.dot(act, w2_ref[0].astype(BF16), preferred_element_type=F32) + b2_ref[0]) * wcol

    @pl.when(pl.program_id(0) >= nu_ref[0])
    def _():
        ys_ref[...] = jnp.zeros(ys_ref.shape, F32)


def _experts(block_e, n_used, xs, w1, b1, w2, b2):
    p = xs.shape[0]
    mb = MOE_ROWS
    last = lambda i, nu: jnp.minimum(i, nu[0] - 1)
    rowmap = lambda i, be, nu: (last(i, nu), 0)
    wmap = lambda i, be, nu: (be[last(i, nu)], 0, 0)
    return pl.pallas_call(
        _experts_kernel,
        grid_spec=pltpu.PrefetchScalarGridSpec(
            num_scalar_prefetch=2,
            grid=(p // mb,),
            in_specs=[pl.BlockSpec((mb, XS_COLS), rowmap),
                      pl.BlockSpec((1, D_MODEL, 2 * D_EXPERT), wmap),
                      pl.BlockSpec((1, 1, 2 * D_EXPERT), wmap),
                      pl.BlockSpec((1, D_EXPERT, D_MODEL), wmap),
                      pl.BlockSpec((1, 1, D_MODEL), wmap)],
            out_specs=pl.BlockSpec((mb, D_MODEL), lambda i, be, nu: (i, 0))),
        out_shape=jax.ShapeDtypeStruct((p, D_MODEL), F32),
        compiler_params=_params(("arbitrary",)),
        name="experts",
    )(block_e, n_used, xs, w1, b1, w2, b2)


def _combine_kernel(g8_ref, loff_ref, goff_ref, tot_ref, ys_ref, route_ref, x1_ref, g2_ref, fn_ref, o_ref, stg, sem):
    t = pl.program_id(0)
    nt = pl.num_programs(0)
    tm = x1_ref.shape[0]
    slot = lax.rem(t, 2)

    def copy(s, lo, go, rows):
        return pltpu.make_async_copy(ys_ref.at[pl.ds(go, rows), :], stg.at[s, pl.ds(lo, rows), :], sem.at[s])

    @pl.when(t == 0)
    def _():
        stg[...] = jnp.zeros(stg.shape, F32)
        _start_runs(t, g8_ref, loff_ref, goff_ref, functools.partial(copy, 0))

    pl.when(t + 1 < nt)(lambda: _start_runs(t + 1, g8_ref, loff_ref, goff_ref, functools.partial(copy, 1 - slot)))
    copy(slot, 0, 0, pl.multiple_of(tot_ref[t], GROUP_ALIGN)).wait()

    route = route_ref[...]
    low = [jnp.broadcast_to(route[:, k:k + 1], (tm, SLOT_SUB)).astype(BF16) for k in range(TOP_K)]
    sub = [jnp.broadcast_to(route[:, TOP_K + k:TOP_K + k + 1], (tm, SLOT_SUB)).astype(BF16) for k in range(TOP_K)]
    scol = lax.broadcasted_iota(jnp.int32, (tm, SLOT_SUB), 1).astype(F32).astype(BF16)
    one = jnp.ones((tm, SLOT_SUB), BF16)
    never = jnp.full((tm, SLOT_SUB), -1.0, BF16)

    y = None
    for ci in range(SLOTS_PER_TILE // SLOT_CHUNK):
        parts = []
        for b in range(SLOT_CHUNK // SLOT_SUB):
            blk = float(ci * (SLOT_CHUNK // SLOT_SUB) + b)
            own = jnp.zeros((tm, SLOT_SUB), BF16)
            for k in range(TOP_K):
                own = jnp.where(scol == jnp.where(sub[k] == blk, low[k], never), one, own)
            parts.append(own)
        own = jnp.concatenate(parts, axis=1)
        rows = stg[slot, ci * SLOT_CHUNK:(ci + 1) * SLOT_CHUNK, :].astype(BF16)
        part = jnp.dot(own, rows, preferred_element_type=F32)
        y = part if y is None else y + part
    x = x1_ref[...] + g2_ref[0] * y
    o_ref[...] = _rms(x) * fn_ref[...]


def _combine(g8f, lofff, gofff, totf, ys, route, x1, g2, fn, seq):
    n = x1.shape[0]
    tm = ROUTE_TILE
    per_b = seq // tm
    return pl.pallas_call(
        _combine_kernel,
        grid_spec=pltpu.PrefetchScalarGridSpec(
            num_scalar_prefetch=4,
            grid=(n // tm,),
            in_specs=[pl.BlockSpec(memory_space=pl.ANY),
                      pl.BlockSpec((tm, LANE), lambda i, *_: (i, 0)),
                      pl.BlockSpec((tm, D_MODEL), lambda i, *_: (i, 0)),
                      pl.BlockSpec((1, 1, D_MODEL), lambda i, *_: (i // per_b, 0, 0)),
                      pl.BlockSpec((1, D_MODEL), lambda i, *_: (0, 0))],
            out_specs=pl.BlockSpec((tm, D_MODEL), lambda i, *_: (i, 0)),
            scratch_shapes=[pltpu.VMEM((2, SLOTS_PER_TILE, D_MODEL), F32), pltpu.SemaphoreType.DMA((2,))]),
        out_shape=jax.ShapeDtypeStruct((n, D_MODEL), F32),
        compiler_params=_params(("arbitrary",)),
        name="combine",
    )(g8f, lofff, gofff, totf, ys, route, x1, g2, fn)


def _split_w_in(w_in):
    o_kr = _W_MIX_COLS
    w_kr = jnp.pad(w_in[:, o_kr:o_kr + MLA_ROPE_DIM],
                   ((0, 0), (MLA_NOPE_DIM, LANE - MLA_NOPE_DIM - MLA_ROPE_DIM)))
    return w_in[:, :o_kr].astype(BF16), w_kr.astype(BF16), w_in[:, o_kr + MLA_ROPE_DIM:].astype(BF16)


def _pack_heads(w, width, pieces):
    rows = w.shape[0]
    w3 = w.reshape(rows, MLA_HEADS, width)
    cols = [w3[:, :, a:b] for a, b in pieces]
    used = sum(b - a for a, b in pieces)
    cols.append(jnp.zeros((rows, MLA_HEADS, LANE - used), w.dtype))
    return jnp.concatenate(cols, axis=2).reshape(rows, MLA_HEADS * LANE)


def kernel(x, c, positions, w_ada, b_ada, norm_mix, norm_ffn, w_in, sinks, q_norm, kv_norm, w_uq, w_uk, w_uv,
           w_branch_a, w_branch_b, w_out, w_router, b_router, w_moe1, b_moe1, w_moe2, b_moe2, final_norm):
    batch, seq, d = x.shape
    n = batch * seq
    assert d == D_MODEL and w_ada.shape[0] == 1
    assert seq % TQ_MLA == 0 and seq % (SWA_QB * ATTN_BLOCK) == 0 and seq % TM_PROJ == 0 and seq % ROUTE_TILE == 0
    hw = MLA_HEADS * LANE
    qk = MLA_NOPE_DIM + MLA_ROPE_DIM

    w_mix, w_kr, w_gates = _split_w_in(w_in[0])
    wuq_p = _pack_heads(w_uq[0], qk, [(0, qk)]).astype(BF16)
    wuk_p = _pack_heads(w_uk[0], MLA_NOPE_DIM, [(0, MLA_NOPE_DIM)]).astype(BF16)
    wuv_p = _pack_heads(w_uv[0], MLA_V_DIM, [(0, MLA_V_DIM)]).astype(BF16)
    one_row = jnp.tile((jnp.arange(LANE) == MLA_V_DIM).astype(F32), MLA_HEADS)[None, :]
    freqs = ROPE_THETA ** (-jnp.arange(0, MLA_ROPE_DIM, 2, dtype=F32) / MLA_ROPE_DIM)
    frq = jnp.concatenate([jnp.zeros((MLA_NOPE_DIM,), F32), freqs, freqs,
                           jnp.zeros((LANE - qk,), F32)])[None, :]
    wb3 = w_branch_b[0].reshape(MLA_HEADS, MLA_V_DIM, D_MODEL)
    wb_p = jnp.concatenate([wb3, jnp.zeros((MLA_HEADS, LANE - MLA_V_DIM, D_MODEL), F32)],
                           axis=1).reshape(hw, D_MODEL).astype(BF16)
    wa = w_branch_a[0].astype(BF16)
    wo = w_out[0].astype(BF16)
    wr = jnp.pad(w_router[0], ((0, 0), (0, LANE - N_EXPERTS)))
    wr_hi = wr.astype(BF16)
    wr_lo = (wr - wr_hi.astype(F32)).astype(BF16)
    br_p = jnp.pad(b_router[0], (0, LANE - N_EXPERTS))[None, :]
    b1 = b_moe1[0][:, None, :]
    b2 = b_moe2[0][:, None, :]
    slopes = jnp.asarray(np.exp2(-8.0 * np.arange(1, SWA_HEADS + 1) / SWA_HEADS), dtype=F32)

    c8 = jnp.pad(c, ((0, 8 - batch), (0, 0)))
    mod = _ada(c8, w_ada[0], b_ada[0][None, :])[:batch]
    sh1, sc1, g1, sh2, sc2, g2 = [m[:, None, :] for m in jnp.split(mod, 6, axis=-1)]

    x2 = x.reshape(n, D_MODEL)
    pos2 = positions.reshape(n, 1).astype(jnp.int32)
    qa, ka2, va2, qm, km, vm, ga, gb, stats = _inproj(
        x2, pos2, sh1, sc1, norm_mix, w_mix, w_kr, w_gates, q_norm, kv_norm, wuq_p, wuk_p, wuv_p, frq, one_row, seq)
    st = stats.reshape(batch, seq // TM_PROJ, 8, LANE)
    bound_sq = jnp.max(st[:, :, :, 0], axis=1) * jnp.max(st[:, :, :, 1], axis=1)
    plain = (bound_sq <= MLA_PLAIN_BOUND ** 2).astype(jnp.int32).reshape(-1)
    yb = _mla(plain, qm, km, vm, batch, seq)
    ya = _swa(sinks[0], slopes, qa, ka2, va2, batch, seq)

    x1, h2, route, routet, g8 = _outproj(
        ya, yb, ga, gb, x2, g1, sh2, sc2, norm_ffn, wa, wb_p, wo, wr_hi, wr_lo, br_p, seq)

    n_tiles = n // ROUTE_TILE
    g8t = g8.reshape(n_tiles, 8, LANE)[:, 0, :N_EXPERTS].astype(jnp.int32)
    loff = jnp.cumsum(g8t, axis=1) - g8t
    padded = ((jnp.sum(g8t, axis=0) + MOE_ROWS - 1) // MOE_ROWS) * MOE_ROWS
    pends = jnp.cumsum(padded)
    goff = (pends - padded)[None, :] + jnp.cumsum(g8t, axis=0) - g8t
    tot = jnp.sum(g8t, axis=1).astype(jnp.int32)
    p_rows = n * TOP_K + n_tiles * N_EXPERTS * (GROUP_ALIGN - 1) + N_EXPERTS * (MOE_ROWS - 1)
    p_rows = -(-p_rows // MOE_ROWS) * MOE_ROWS
    n_blocks = p_rows // MOE_ROWS
    block_start = jnp.arange(n_blocks, dtype=jnp.int32) * MOE_ROWS
    block_e = jnp.minimum(jnp.sum((pends[None, :] <= block_start[:, None]).astype(jnp.int32), axis=1),
                          N_EXPERTS - 1).astype(jnp.int32)
    n_used = (pends[-1:] // MOE_ROWS).astype(jnp.int32)
    tabs = (g8t.reshape(-1), loff.reshape(-1).astype(jnp.int32), goff.reshape(-1).astype(jnp.int32), tot)

    xs = _dispatch(*tabs, pends.astype(jnp.int32), h2, routet, p_rows)
    ys = _experts(block_e, n_used, xs, w_moe1[0], b1, w_moe2[0], b2)
    out = _combine(*tabs, ys, route, x1, g2, final_norm[None, :], seq)
    return out.reshape(batch, seq, D_MODEL)
```

```python
import functools

import numpy as np
import jax
import jax.numpy as jnp
from jax import lax
from jax.experimental import pallas as pl
from jax.experimental.pallas import tpu as pltpu

D_MODEL = 1024
SWA_HEADS = 8
SWA_KV_HEADS = 2
SWA_HEAD_DIM = 64
ATTN_BLOCK = 128
MLA_HEADS = 8
MLA_Q_RANK = 384
MLA_KV_RANK = 256
MLA_NOPE_DIM = 64
MLA_ROPE_DIM = 32
MLA_V_DIM = 64
ROPE_THETA = 10000.0
N_EXPERTS = 32
TOP_K = 4
D_EXPERT = 1024
SWIGLU_LIMIT = 7.0
SWIGLU_ALPHA = 1.702
NORM_EPS = 1e-6

LANE = 128
LOG2E = 1.4426950408889634
NEG = -1e30
SWA_QSCALE = SWA_HEAD_DIM ** -0.5 * LOG2E
MLA_QSCALE = (MLA_NOPE_DIM + MLA_ROPE_DIM) ** -0.5 * LOG2E
HALF_ROPE = MLA_ROPE_DIM // 2
MLA_PLAIN_BOUND = 60.0
MLA_WIDE = 8

BF16 = jnp.bfloat16
F32 = jnp.float32

TM_PROJ = 512
TQ_MLA = 512
SWA_QB = 4
MOE_ROWS = 512
ROUTE_TILE = 512
GROUP_ALIGN = 8
SLOTS_PER_TILE = ROUTE_TILE * TOP_K + N_EXPERTS * GROUP_ALIGN
SLOT_SUB = 256
SLOT_CHUNK = 3 * SLOT_SUB
XS_COLS = D_MODEL + LANE
VMEM_LIMIT = 56 * 1024 * 1024


def _params(sem, vmem=VMEM_LIMIT):
    return pltpu.CompilerParams(dimension_semantics=sem, vmem_limit_bytes=vmem)


def _rms(x):
    return x * lax.rsqrt(jnp.mean(x * x, axis=-1, keepdims=True) + NORM_EPS)


def _ada_kernel(c_ref, w_ref, b_ref, o_ref):
    c = c_ref[...]
    a = (c * jax.nn.sigmoid(c)).astype(BF16)
    o_ref[...] = jnp.dot(a, w_ref[...].astype(BF16), preferred_element_type=F32) + b_ref[...]


def _ada(c8, w_ada, b_ada):
    n_out = w_ada.shape[1]
    return pl.pallas_call(
        _ada_kernel,
        grid=(n_out // D_MODEL,),
        in_specs=[pl.BlockSpec((8, D_MODEL), lambda j: (0, 0)),
                  pl.BlockSpec((D_MODEL, D_MODEL), lambda j: (0, j)),
                  pl.BlockSpec((1, D_MODEL), lambda j: (0, j))],
        out_specs=pl.BlockSpec((8, D_MODEL), lambda j: (0, j)),
        out_shape=jax.ShapeDtypeStruct((8, n_out), F32),
        compiler_params=_params(("arbitrary",)),
        name="ada",
    )(c8, w_ada, b_ada)


_C_QA = (0, 512)
_C_KV = (512, 768)
_C_LAT = (768, 1408)
_W_MIX_COLS = 1408
_ROPE_LO = (MLA_NOPE_DIM, MLA_NOPE_DIM + HALF_ROPE)
_ROPE_HI = (MLA_NOPE_DIM + HALF_ROPE, MLA_NOPE_DIM + MLA_ROPE_DIM)


def _inproj_kernel(x_ref, pos_ref, sh_ref, sc_ref, nm_ref, win_ref, wkr_ref, wg_ref, qn_ref, kvn_ref, wuq_ref, wuk_ref,
                   wuv_ref, frq_ref, one_ref,
                   qa_ref, ka_ref, va_ref, qm_ref, km_ref, vm_ref, ga_ref, gb_ref, st_ref):
    tm = x_ref.shape[0]
    x = x_ref[...]
    h = (_rms(x) * nm_ref[...] * (1.0 + sc_ref[0]) + sh_ref[0]).astype(BF16)

    def proj(c):
        return jnp.dot(h, win_ref[:, c[0]:c[1]], preferred_element_type=F32)

    lane = lax.broadcasted_iota(jnp.int32, (tm, LANE), 1)
    first = lane < SWA_HEAD_DIM

    def twice(t):
        r = pltpu.roll(t, SWA_HEAD_DIM, axis=1)
        return jnp.concatenate([jnp.where(first, t, r), jnp.where(first, r, t)], axis=1)

    kv = proj(_C_KV)
    ka_ref[...] = twice(kv[:, :LANE]).astype(BF16)
    va_ref[...] = twice(kv[:, LANE:]).astype(BF16)

    ang = pos_ref[...].astype(F32) * frq_ref[...]
    cs = jnp.cos(ang)
    sn = jnp.sin(ang)
    sn_lo = jnp.where((lane >= _ROPE_LO[0]) & (lane < _ROPE_LO[1]), -sn, 0.0)
    sn_hi = jnp.where((lane >= _ROPE_HI[0]) & (lane < _ROPE_HI[1]), sn, 0.0)

    def rotary(t):
        return (t * cs + pltpu.roll(t, LANE - HALF_ROPE, axis=1) * sn_lo
                + pltpu.roll(t, HALF_ROPE, axis=1) * sn_hi)

    lat = proj(_C_LAT)
    cqn = (_rms(lat[:, 0:MLA_Q_RANK]) * qn_ref[...]).astype(BF16)
    ckvn = (_rms(lat[:, MLA_Q_RANK:MLA_Q_RANK + MLA_KV_RANK]) * kvn_ref[...]).astype(BF16)
    krr = rotary(jnp.dot(h, wkr_ref[...], preferred_element_type=F32))
    q = jnp.dot(cqn, wuq_ref[...], preferred_element_type=F32)
    kn = jnp.dot(ckvn, wuk_ref[...], preferred_element_type=F32)

    def max_sq_norm(t):
        tf = t.astype(F32)
        return jnp.max(jnp.sum(tf * tf, axis=1, keepdims=True), axis=0, keepdims=True)

    rid = lax.broadcasted_iota(jnp.int32, (8, LANE), 0)
    lid = lax.broadcasted_iota(jnp.int32, (8, LANE), 1)
    stats = jnp.zeros((8, LANE), F32)
    for hh in range(MLA_HEADS):
        sl = slice(hh * LANE, (hh + 1) * LANE)
        qb = (rotary(q[:, sl]) * MLA_QSCALE).astype(BF16)
        kb = (kn[:, sl] + krr).astype(BF16)
        qm_ref[:, sl] = qb
        km_ref[:, sl] = kb
        stats = jnp.where((rid == hh) & (lid == 0), max_sq_norm(qb), stats)
        stats = jnp.where((rid == hh) & (lid == 1), max_sq_norm(kb), stats)
    st_ref[...] = stats
    vm_ref[...] = (jnp.dot(ckvn, wuv_ref[...], preferred_element_type=F32) + one_ref[...]).astype(BF16)
    qa_ref[...] = (proj(_C_QA) * SWA_QSCALE).astype(BF16)
    ga_ref[...] = jax.nn.sigmoid(jnp.dot(h, wg_ref[:, 0:D_MODEL], preferred_element_type=F32)).astype(BF16)
    gb_ref[...] = jax.nn.sigmoid(jnp.dot(h, wg_ref[:, D_MODEL:], preferred_element_type=F32)).astype(BF16)


def _inproj(x2, pos2, sh1, sc1, nm, w_mix, w_kr, w_gates, qn, kvn, wuq_p, wuk_p, wuv_p, frq, one_row, seq):
    n = x2.shape[0]
    tm = TM_PROJ
    per_b = seq // tm
    hw = MLA_HEADS * LANE
    row = lambda i: (i, 0)
    fix = lambda i: (0, 0)
    bsel = lambda i: (i // per_b, 0, 0)
    widths = (512, 256, 256, hw, hw, hw, D_MODEL, D_MODEL)
    return pl.pallas_call(
        _inproj_kernel,
        grid=(n // tm,),
        in_specs=[pl.BlockSpec((tm, D_MODEL), row),
                  pl.BlockSpec((tm, 1), row),
                  pl.BlockSpec((1, 1, D_MODEL), bsel),
                  pl.BlockSpec((1, 1, D_MODEL), bsel),
                  pl.BlockSpec((1, D_MODEL), fix),
                  pl.BlockSpec((D_MODEL, _W_MIX_COLS), fix),
                  pl.BlockSpec((D_MODEL, LANE), fix),
                  pl.BlockSpec((D_MODEL, 2 * D_MODEL), fix),
                  pl.BlockSpec((1, MLA_Q_RANK), fix),
                  pl.BlockSpec((1, MLA_KV_RANK), fix),
                  pl.BlockSpec((MLA_Q_RANK, hw), fix),
                  pl.BlockSpec((MLA_KV_RANK, hw), fix),
                  pl.BlockSpec((MLA_KV_RANK, hw), fix),
                  pl.BlockSpec((1, LANE), fix),
                  pl.BlockSpec((1, hw), fix)],
        out_specs=[pl.BlockSpec((tm, w), row) for w in widths] + [pl.BlockSpec((8, LANE), row)],
        out_shape=[jax.ShapeDtypeStruct((n, w), BF16) for w in widths]
        + [jax.ShapeDtypeStruct((n // tm * 8, LANE), F32)],
        compiler_params=_params(("arbitrary",)),
        name="inproj",
    )(x2, pos2, sh1, sc1, nm, w_mix, w_kr, w_gates, qn, kvn, wuq_p, wuk_p, wuv_p, frq, one_row)


def _mla_kernel(plain_ref, q_ref, k_ref, v_ref, o_ref, m_scr, acc_scr):
    t = TQ_MLA
    nq = q_ref.shape[0] // t

    def scores(i, j, nblk, diag_at):
        q = q_ref[pl.ds(pl.multiple_of(i * t, t), t), :]
        start = pl.multiple_of(j * t, t)
        k = k_ref[pl.ds(start, nblk * t), :]
        v = v_ref[pl.ds(start, nblk * t), :]
        s = lax.dot_general(q, k, (((1,), (1,)), ((), ())), preferred_element_type=F32)
        if diag_at is not None:
            qi = lax.broadcasted_iota(jnp.int32, s.shape, 0)
            kj = lax.broadcasted_iota(jnp.int32, s.shape, 1)
            s = jnp.where(kj <= qi + diag_at * t, s, NEG)
        return s, v

    def plain_step(i, j, nblk, diag_at):
        s, v = scores(i, j, nblk, diag_at)
        acc_scr[...] += jnp.dot(jnp.exp2(s).astype(BF16), v, preferred_element_type=F32)

    def online_step(i, j, nblk, diag_at):
        s, v = scores(i, j, nblk, diag_at)
        m_old = m_scr[...]
        m_new = jnp.maximum(m_old, jnp.max(s, axis=1, keepdims=True))
        p = jnp.exp2(s - m_new)
        alpha = jnp.exp2(m_old - m_new)
        acc_scr[...] = acc_scr[...] * alpha + jnp.dot(p.astype(BF16), v, preferred_element_type=F32)
        m_scr[...] = m_new

    def run(step, wide, online):
        def q_tile(i, carry):
            acc_scr[...] = jnp.zeros(acc_scr.shape, F32)
            if online:
                m_scr[...] = jnp.full(m_scr.shape, NEG, F32)
            nw = lax.shift_right_logical(i, wide.bit_length() - 1)
            left = i - nw * wide

            def wide_body(jj, c):
                step(i, jj * wide, wide, None)
                return c

            lax.fori_loop(0, nw, wide_body, 0)
            for r in range(wide):
                pl.when(left == r)(lambda r=r: step(i, nw * wide, r + 1, r))
            acc = acc_scr[...]
            o_ref[pl.ds(pl.multiple_of(i * t, t), t), :] = (acc / acc[:, MLA_V_DIM:MLA_V_DIM + 1]).astype(BF16)
            return carry

        lax.fori_loop(0, nq, q_tile, 0)

    small = plain_ref[pl.program_id(0) * MLA_HEADS + pl.program_id(1)] != 0
    pl.when(small)(lambda: run(plain_step, MLA_WIDE, False))
    pl.when(jnp.logical_not(small))(lambda: run(online_step, 1, True))


def _mla(plain, qm, km, vm, batch, seq):
    n = qm.shape[0]
    t = TQ_MLA
    kmap = lambda b, h, *_: (b, h)
    return pl.pallas_call(
        _mla_kernel,
        grid_spec=pltpu.PrefetchScalarGridSpec(
            num_scalar_prefetch=1,
            grid=(batch, MLA_HEADS),
            in_specs=[pl.BlockSpec((seq, LANE), kmap),
                      pl.BlockSpec((seq, LANE), kmap),
                      pl.BlockSpec((seq, LANE), kmap)],
            out_specs=pl.BlockSpec((seq, LANE), kmap),
            scratch_shapes=[pltpu.VMEM((t, 1), F32), pltpu.VMEM((t, LANE), F32)]),
        out_shape=jax.ShapeDtypeStruct((n, MLA_HEADS * LANE), BF16),
        compiler_params=_params(("arbitrary", "arbitrary")),
        name="mla",
    )(plain, qm, km, vm)


def _swa_kernel(sink_ref, slope_ref, q_ref, kc_ref, kp_ref, vc_ref, vp_ref, o_ref, bias_scr):
    b = pl.program_id(0)
    i = pl.program_id(1)
    blk = ATTN_BLOCK

    @pl.when((b == 0) & (i == 0))
    def _():
        qi = lax.broadcasted_iota(jnp.int32, (blk, 2 * blk), 0)
        kj = lax.broadcasted_iota(jnp.int32, (blk, 2 * blk), 1)
        dist = qi - kj + blk
        valid = (dist >= 0) & (dist < blk)
        distf = dist.astype(F32)
        for hd in range(SWA_HEADS):
            bias_scr[hd] = jnp.where(valid, -(slope_ref[hd] * LOG2E) * distf, NEG)

    lane = lax.broadcasted_iota(jnp.int32, (2 * blk, LANE), 1)
    lo = lane < SWA_HEAD_DIM
    kcol = lax.broadcasted_iota(jnp.int32, (blk, 2 * blk), 1)
    first_pen = jnp.where(kcol < blk, jnp.where(i == 0, NEG, 0.0), 0.0)
    zero = jnp.zeros((2 * blk, LANE), BF16)

    for qb in range(SWA_QB):
        if qb == 0:
            kprev, vprev = kp_ref[...], vp_ref[...]
        else:
            kprev = kc_ref[(qb - 1) * blk:qb * blk, :]
            vprev = vc_ref[(qb - 1) * blk:qb * blk, :]
        kcat = jnp.concatenate([kprev, kc_ref[qb * blk:(qb + 1) * blk, :]], axis=0)
        vcat = jnp.concatenate([vprev, vc_ref[qb * blk:(qb + 1) * blk, :]], axis=0)
        for g in range(SWA_KV_HEADS):
            kg = kcat[:, g * LANE:(g + 1) * LANE]
            vg = vcat[:, g * LANE:(g + 1) * LANE]
            halves = ((jnp.where(lo, kg, zero), jnp.where(lo, vg, zero)),
                      (jnp.where(lo, zero, kg), jnp.where(lo, zero, vg)))
            for pp in range(2):
                pr = g * 2 + pp
                qpair = q_ref[qb * blk:(qb + 1) * blk, pr * LANE:(pr + 1) * LANE]
                out = None
                for half in range(2):
                    hd = pr * 2 + half
                    kx, vx = halves[half]
                    s = lax.dot_general(qpair, kx, (((1,), (1,)), ((), ())), preferred_element_type=F32)
                    s = s + bias_scr[hd]
                    if qb == 0:
                        s = s + first_pen
                    sink = sink_ref[hd] * LOG2E
                    m = jnp.maximum(jnp.max(s, axis=1, keepdims=True), sink)
                    p = jnp.exp2(s - m)
                    denom = jnp.sum(p, axis=1, keepdims=True) + jnp.exp2(sink - m)
                    o = jnp.dot(p.astype(BF16), vx, preferred_element_type=F32) * (1.0 / denom)
                    out = o if out is None else out + o
                o_ref[qb * blk:(qb + 1) * blk, pr * LANE:(pr + 1) * LANE] = out.astype(BF16)


def _swa(sinks, slopes, qa, ka2, va2, batch, seq):
    n = qa.shape[0]
    blk = ATTN_BLOCK
    rows = SWA_QB * blk
    steps = seq // rows
    cur = lambda b, i: (b * steps + i, 0)
    prev = lambda b, i: (b * (seq // blk) + jnp.maximum(i * SWA_QB - 1, 0), 0)
    smem = pl.BlockSpec(memory_space=pltpu.SMEM)
    return pl.pallas_call(
        _swa_kernel,
        grid=(batch, steps),
        in_specs=[smem, smem,
                  pl.BlockSpec((rows, 512), cur),
                  pl.BlockSpec((rows, 256), cur),
                  pl.BlockSpec((blk, 256), prev),
                  pl.BlockSpec((rows, 256), cur),
                  pl.BlockSpec((blk, 256), prev)],
        out_specs=pl.BlockSpec((rows, 512), cur),
        out_shape=jax.ShapeDtypeStruct((n, 512), BF16),
        scratch_shapes=[pltpu.VMEM((SWA_HEADS, blk, 2 * blk), F32)],
        compiler_params=_params(("arbitrary", "arbitrary")),
        name="swa",
    )(sinks, slopes, qa, ka2, ka2, va2, va2)


def _outproj_kernel(ya_ref, yb_ref, ga_ref, gb_ref, x_ref, g1_ref, sh_ref, sc_ref, nm_ref, wa_ref, wb_ref,
                    wo_ref, wrh_ref, wrl_ref, br_ref,
                    x1_ref, h2_ref, route_ref, routet_ref, g8_ref):
    tm = x_ref.shape[0]
    a = jnp.dot(ya_ref[...], wa_ref[...], preferred_element_type=F32)
    bm = jnp.dot(yb_ref[...], wb_ref[...], preferred_element_type=F32)
    mixed = (ga_ref[...].astype(F32) * a + gb_ref[...].astype(F32) * bm).astype(BF16)
    x1 = x_ref[...] + g1_ref[0] * jnp.dot(mixed, wo_ref[...], preferred_element_type=F32)
    x1_ref[...] = x1
    h2 = _rms(x1) * nm_ref[...] * (1.0 + sc_ref[0]) + sh_ref[0]
    hi = h2.astype(BF16)
    h2_ref[:, 0:D_MODEL] = hi

    lo = (h2 - hi.astype(F32)).astype(BF16)
    logits = (jnp.dot(hi, wrh_ref[...], preferred_element_type=F32)
              + jnp.dot(lo, wrh_ref[...], preferred_element_type=F32)
              + jnp.dot(hi, wrl_ref[...], preferred_element_type=F32)) + br_ref[...]

    lane = lax.broadcasted_iota(jnp.int32, (tm, LANE), 1).astype(F32)
    work = jnp.where(lane < N_EXPERTS, logits, -jnp.inf)
    vals, idxs = [], []
    for _ in range(TOP_K):
        m = jnp.max(work, axis=1, keepdims=True)
        idx = jnp.min(jnp.where(work == m, lane, float(LANE)), axis=1, keepdims=True)
        vals.append(m)
        idxs.append(idx)
        work = jnp.where(lane == idx, -jnp.inf, work)
    exps = [jnp.exp(v - vals[0]) for v in vals]
    tot = exps[0] + exps[1] + exps[2] + exps[3]

    onehot = jnp.zeros((tm, LANE), F32)
    for idx in idxs:
        onehot = onehot + (lane == idx).astype(F32)
    r = lax.broadcasted_iota(jnp.int32, (tm, tm), 0)
    c = lax.broadcasted_iota(jnp.int32, (tm, tm), 1)
    prefix = jnp.dot((c < r).astype(BF16), onehot.astype(BF16), preferred_element_type=F32)
    groups = jnp.ceil(jnp.sum(onehot, axis=0, keepdims=True) * (1.0 / GROUP_ALIGN))
    er = lax.broadcasted_iota(jnp.int32, (LANE, LANE), 0)
    ec = lax.broadcasted_iota(jnp.int32, (LANE, LANE), 1)
    before = jnp.dot(jnp.broadcast_to(groups, (8, LANE)).astype(BF16), (er < ec).astype(BF16),
                     preferred_element_type=F32)[0:1, :]
    slot_e = before * GROUP_ALIGN + prefix

    route = jnp.zeros((tm, LANE), F32)
    wext = jnp.zeros((tm, LANE), F32)
    for k in range(TOP_K):
        sk = jnp.sum(jnp.where(lane == idxs[k], slot_e, 0.0), axis=1, keepdims=True)
        sub = jnp.floor(sk * (1.0 / SLOT_SUB))
        route = jnp.where(lane == float(k), sk - sub * SLOT_SUB, route)
        route = jnp.where(lane == float(TOP_K + k), sub, route)
        wk = exps[k] / tot
        wk_hi = wk.astype(BF16).astype(F32)
        wext = jnp.where(lane == idxs[k], wk_hi, wext)
        wext = jnp.where(lane == idxs[k] + float(N_EXPERTS), wk - wk_hi, wext)
    h2_ref[:, D_MODEL:XS_COLS] = wext.astype(BF16)
    route_ref[...] = route
    routet_ref[...] = route.T[0:8, :]
    g8_ref[...] = jnp.broadcast_to(groups * GROUP_ALIGN, (8, LANE))


def _outproj(ya, yb, ga, gb, x2, g1, sh2, sc2, nm, wa, wb_p, wo, wr_hi, wr_lo, br_p, seq):
    n = x2.shape[0]
    tm = ROUTE_TILE
    per_b = seq // tm
    hw = MLA_HEADS * LANE
    row = lambda i: (i, 0)
    fix = lambda i: (0, 0)
    bsel = lambda i: (i // per_b, 0, 0)
    return pl.pallas_call(
        _outproj_kernel,
        grid=(n // tm,),
        in_specs=[pl.BlockSpec((tm, 512), row),
                  pl.BlockSpec((tm, hw), row),
                  pl.BlockSpec((tm, D_MODEL), row),
                  pl.BlockSpec((tm, D_MODEL), row),
                  pl.BlockSpec((tm, D_MODEL), row),
                  pl.BlockSpec((1, 1, D_MODEL), bsel),
                  pl.BlockSpec((1, 1, D_MODEL), bsel),
                  pl.BlockSpec((1, 1, D_MODEL), bsel),
                  pl.BlockSpec((1, D_MODEL), fix),
                  pl.BlockSpec((512, D_MODEL), fix),
                  pl.BlockSpec((hw, D_MODEL), fix),
                  pl.BlockSpec((D_MODEL, D_MODEL), fix),
                  pl.BlockSpec((D_MODEL, LANE), fix),
                  pl.BlockSpec((D_MODEL, LANE), fix),
                  pl.BlockSpec((1, LANE), fix)],
        out_specs=[pl.BlockSpec((tm, D_MODEL), row),
                   pl.BlockSpec((tm, XS_COLS), row),
                   pl.BlockSpec((tm, LANE), row),
                   pl.BlockSpec((8, tm), lambda i: (0, i)),
                   pl.BlockSpec((8, LANE), row)],
        out_shape=[jax.ShapeDtypeStruct((n, D_MODEL), F32),
                   jax.ShapeDtypeStruct((n, XS_COLS), BF16),
                   jax.ShapeDtypeStruct((n, LANE), F32),
                   jax.ShapeDtypeStruct((8, n), F32),
                   jax.ShapeDtypeStruct((n // tm * 8, LANE), F32)],
        compiler_params=_params(("arbitrary",)),
        name="outproj",
    )(ya, yb, ga, gb, x2, g1, sh2, sc2, nm, wa, wb_p, wo, wr_hi, wr_lo, br_p)


def _start_runs(tile, g8_ref, loff_ref, goff_ref, make_copy):
    for e in range(N_EXPERTS):
        g = g8_ref[tile * N_EXPERTS + e]
        lo = loff_ref[tile * N_EXPERTS + e]
        go = goff_ref[tile * N_EXPERTS + e]
        def go_copy(lo=lo, go=go, g=g):
            make_copy(pl.multiple_of(lo, GROUP_ALIGN), pl.multiple_of(go, GROUP_ALIGN),
                      pl.multiple_of(g, GROUP_ALIGN)).start()

        pl.when(g > 0)(go_copy)


def _dispatch_kernel(g8_ref, loff_ref, goff_ref, tot_ref, ends_ref, h_ref, rt_ref, xs_ref, buf, zbuf, sem, zsem):
    t = pl.program_id(0)
    nt = pl.num_programs(0)
    tm = h_ref.shape[0]
    slot = lax.rem(t, 2)

    @pl.when(t == 0)
    def _():
        zbuf[...] = jnp.zeros(zbuf.shape, zbuf.dtype)

        def fill(start):
            return pltpu.make_async_copy(zbuf, xs_ref.at[pl.ds(pl.multiple_of(start, MOE_ROWS), MOE_ROWS), :], zsem)

        def nonempty(e):
            return ends_ref[e] > (ends_ref[e - 1] if e else 0)

        used = lax.shift_right_logical(ends_ref[N_EXPERTS - 1], MOE_ROWS.bit_length() - 1)
        n_blocks = xs_ref.shape[0] // MOE_ROWS
        for e in range(N_EXPERTS):
            pl.when(nonempty(e))(lambda e=e: fill(ends_ref[e] - MOE_ROWS).start())
        lax.fori_loop(used, n_blocks, lambda b, c: (fill(b * MOE_ROWS).start(), c)[1], 0)
        for e in range(N_EXPERTS):
            pl.when(nonempty(e))(lambda e=e: fill(ends_ref[e] - MOE_ROWS).wait())
        lax.fori_loop(used, n_blocks, lambda b, c: (fill(b * MOE_ROWS).wait(), c)[1], 0)

    hb = h_ref[...]
    low = [rt_ref[k:k + 1, :] for k in range(TOP_K)]
    sub = [rt_ref[TOP_K + k:TOP_K + k + 1, :] for k in range(TOP_K)]
    srow = lax.broadcasted_iota(jnp.int32, (SLOT_SUB, tm), 0).astype(F32).astype(BF16)
    one = jnp.ones((SLOT_SUB, tm), BF16)

    for ci in range(SLOTS_PER_TILE // SLOT_CHUNK):
        parts = []
        for b in range(SLOT_CHUNK // SLOT_SUB):
            blk = float(ci * (SLOT_CHUNK // SLOT_SUB) + b)
            own = jnp.zeros((SLOT_SUB, tm), BF16)
            for k in range(TOP_K):
                here = jnp.where(sub[k] == blk, low[k], -1.0).astype(BF16)
                own = jnp.where(srow == here, one, own)
            parts.append(own)
        own = jnp.concatenate(parts, axis=0)
        buf[slot, ci * SLOT_CHUNK:(ci + 1) * SLOT_CHUNK, :] = jnp.dot(own, hb, preferred_element_type=F32)

    def copy(s, lo, go, rows):
        return pltpu.make_async_copy(buf.at[s, pl.ds(lo, rows), :], xs_ref.at[pl.ds(go, rows), :], sem.at[s])

    _start_runs(t, g8_ref, loff_ref, goff_ref, functools.partial(copy, slot))

    def drain(tile, s):
        rows = pl.multiple_of(tot_ref[tile], GROUP_ALIGN)
        copy(s, 0, 0, rows).wait()

    pl.when(t > 0)(lambda: drain(t - 1, 1 - slot))
    pl.when(t == nt - 1)(lambda: drain(t, slot))


def _dispatch(g8f, lofff, gofff, totf, ends, h2, routet, p_rows):
    n = h2.shape[0]
    tm = ROUTE_TILE
    return pl.pallas_call(
        _dispatch_kernel,
        grid_spec=pltpu.PrefetchScalarGridSpec(
            num_scalar_prefetch=5,
            grid=(n // tm,),
            in_specs=[pl.BlockSpec((tm, XS_COLS), lambda i, *_: (i, 0)),
                      pl.BlockSpec((8, tm), lambda i, *_: (0, i))],
            out_specs=pl.BlockSpec(memory_space=pl.ANY),
            scratch_shapes=[pltpu.VMEM((2, SLOTS_PER_TILE, XS_COLS), F32),
                            pltpu.VMEM((MOE_ROWS, XS_COLS), F32),
                            pltpu.SemaphoreType.DMA((2,)), pltpu.SemaphoreType.DMA(())]),
        out_shape=jax.ShapeDtypeStruct((p_rows, XS_COLS), F32),
        compiler_params=_params(("arbitrary",)),
        name="dispatch",
    )(g8f, lofff, gofff, totf, ends, h2, routet)


def _experts_kernel(be_ref, nu_ref, bv_ref, xs_ref, w1_ref, b1_ref, w2_ref, b2_ref, ys_ref):
    del nu_ref
    i = pl.program_id(0)
    half = MOE_ROWS // 2
    used = bv_ref[i]

    def mlp(n_rows):
        xb = xs_ref[0:n_rows, 0:D_MODEL].astype(BF16)
        e = be_ref[i]
        wext = xs_ref[0:n_rows, D_MODEL:XS_COLS]
        lane = lax.broadcasted_iota(jnp.int32, wext.shape, 1)
        wcol = jnp.sum(jnp.where((lane == e) | (lane == e + N_EXPERTS), wext, 0.0), axis=1, keepdims=True)
        hcat = jnp.dot(xb, w1_ref[0].astype(BF16), preferred_element_type=F32) + b1_ref[0]
        x_glu = jnp.minimum(hcat[:, :D_EXPERT], SWIGLU_LIMIT)
        x_lin = jnp.clip(hcat[:, D_EXPERT:], -SWIGLU_LIMIT, SWIGLU_LIMIT)
        act = (x_glu * jax.nn.sigmoid(SWIGLU_ALPHA * x_glu) * (x_lin + 1.0)).astype(BF16)
        ys_ref[0:n_rows, :] = (jnp.dot(act, w2_ref[0].astype(BF16), preferred_element_type=F32) + b2_ref[0]) * wcol

    pl.when(used > half)(lambda: mlp(MOE_ROWS))

    @pl.when((used > 0) & (used <= half))
    def _():
        mlp(half)
        ys_ref[half:, :] = jnp.zeros((MOE_ROWS - half, D_MODEL), F32)

    @pl.when(used == 0)
    def _():
        ys_ref[...] = jnp.zeros(ys_ref.shape, F32)


def _experts(block_e, n_used, block_used, xs, w1, b1, w2, b2):
    p = xs.shape[0]
    mb = MOE_ROWS
    last = lambda i, nu: jnp.minimum(i, nu[0] - 1)
    rowmap = lambda i, be, nu, bv: (last(i, nu), 0)
    wmap = lambda i, be, nu, bv: (be[last(i, nu)], 0, 0)
    return pl.pallas_call(
        _experts_kernel,
        grid_spec=pltpu.PrefetchScalarGridSpec(
            num_scalar_prefetch=3,
            grid=(p // mb,),
            in_specs=[pl.BlockSpec((mb, XS_COLS), rowmap),
                      pl.BlockSpec((1, D_MODEL, 2 * D_EXPERT), wmap),
                      pl.BlockSpec((1, 1, 2 * D_EXPERT), wmap),
                      pl.BlockSpec((1, D_EXPERT, D_MODEL), wmap),
                      pl.BlockSpec((1, 1, D_MODEL), wmap)],
            out_specs=pl.BlockSpec((mb, D_MODEL), lambda i, be, nu, bv: (i, 0))),
        out_shape=jax.ShapeDtypeStruct((p, D_MODEL), F32),
        compiler_params=_params(("arbitrary",)),
        name="experts",
    )(block_e, n_used, block_used, xs, w1, b1, w2, b2)


def _combine_kernel(g8_ref, loff_ref, goff_ref, tot_ref, ys_ref, route_ref, x1_ref, g2_ref, fn_ref, o_ref, stg, sem):
    t = pl.program_id(0)
    nt = pl.num_programs(0)
    tm = x1_ref.shape[0]
    slot = lax.rem(t, 2)

    def copy(s, lo, go, rows):
        return pltpu.make_async_copy(ys_ref.at[pl.ds(go, rows), :], stg.at[s, pl.ds(lo, rows), :], sem.at[s])

    @pl.when(t == 0)
    def _():
        stg[...] = jnp.zeros(stg.shape, F32)
        _start_runs(t, g8_ref, loff_ref, goff_ref, functools.partial(copy, 0))

    pl.when(t + 1 < nt)(lambda: _start_runs(t + 1, g8_ref, loff_ref, goff_ref, functools.partial(copy, 1 - slot)))
    copy(slot, 0, 0, pl.multiple_of(tot_ref[t], GROUP_ALIGN)).wait()

    route = route_ref[...]
    low = [jnp.broadcast_to(route[:, k:k + 1], (tm, SLOT_SUB)).astype(BF16) for k in range(TOP_K)]
    sub = [jnp.broadcast_to(route[:, TOP_K + k:TOP_K + k + 1], (tm, SLOT_SUB)).astype(BF16) for k in range(TOP_K)]
    scol = lax.broadcasted_iota(jnp.int32, (tm, SLOT_SUB), 1).astype(F32).astype(BF16)
    one = jnp.ones((tm, SLOT_SUB), BF16)
    never = jnp.full((tm, SLOT_SUB), -1.0, BF16)

    y = None
    for ci in range(SLOTS_PER_TILE // SLOT_CHUNK):
        parts = []
        for b in range(SLOT_CHUNK // SLOT_SUB):
            blk = float(ci * (SLOT_CHUNK // SLOT_SUB) + b)
            own = jnp.zeros((tm, SLOT_SUB), BF16)
            for k in range(TOP_K):
                own = jnp.where(scol == jnp.where(sub[k] == blk, low[k], never), one, own)
            parts.append(own)
        own = jnp.concatenate(parts, axis=1)
        rows = stg[slot, ci * SLOT_CHUNK:(ci + 1) * SLOT_CHUNK, :].astype(BF16)
        part = jnp.dot(own, rows, preferred_element_type=F32)
        y = part if y is None else y + part
    x = x1_ref[...] + g2_ref[0] * y
    o_ref[...] = _rms(x) * fn_ref[...]


def _combine(g8f, lofff, gofff, totf, ys, route, x1, g2, fn, seq):
    n = x1.shape[0]
    tm = ROUTE_TILE
    per_b = seq // tm
    return pl.pallas_call(
        _combine_kernel,
        grid_spec=pltpu.PrefetchScalarGridSpec(
            num_scalar_prefetch=4,
            grid=(n // tm,),
            in_specs=[pl.BlockSpec(memory_space=pl.ANY),
                      pl.BlockSpec((tm, LANE), lambda i, *_: (i, 0)),
                      pl.BlockSpec((tm, D_MODEL), lambda i, *_: (i, 0)),
                      pl.BlockSpec((1, 1, D_MODEL), lambda i, *_: (i // per_b, 0, 0)),
                      pl.BlockSpec((1, D_MODEL), lambda i, *_: (0, 0))],
            out_specs=pl.BlockSpec((tm, D_MODEL), lambda i, *_: (i, 0)),
            scratch_shapes=[pltpu.VMEM((2, SLOTS_PER_TILE, D_MODEL), F32), pltpu.SemaphoreType.DMA((2,))]),
        out_shape=jax.ShapeDtypeStruct((n, D_MODEL), F32),
        compiler_params=_params(("arbitrary",)),
        name="combine",
    )(g8f, lofff, gofff, totf, ys, route, x1, g2, fn)


def _split_w_in(w_in):
    o_kr = _W_MIX_COLS
    w_kr = jnp.pad(w_in[:, o_kr:o_kr + MLA_ROPE_DIM],
                   ((0, 0), (MLA_NOPE_DIM, LANE - MLA_NOPE_DIM - MLA_ROPE_DIM)))
    return w_in[:, :o_kr].astype(BF16), w_kr.astype(BF16), w_in[:, o_kr + MLA_ROPE_DIM:].astype(BF16)


def _pack_heads(w, width, pieces):
    rows = w.shape[0]
    w3 = w.reshape(rows, MLA_HEADS, width)
    cols = [w3[:, :, a:b] for a, b in pieces]
    used = sum(b - a for a, b in pieces)
    cols.append(jnp.zeros((rows, MLA_HEADS, LANE - used), w.dtype))
    return jnp.concatenate(cols, axis=2).reshape(rows, MLA_HEADS * LANE)


def kernel(x, c, positions, w_ada, b_ada, norm_mix, norm_ffn, w_in, sinks, q_norm, kv_norm, w_uq, w_uk, w_uv,
           w_branch_a, w_branch_b, w_out, w_router, b_router, w_moe1, b_moe1, w_moe2, b_moe2, final_norm):
    batch, seq, d = x.shape
    n = batch * seq
    assert d == D_MODEL and w_ada.shape[0] == 1
    assert seq % TQ_MLA == 0 and seq % (SWA_QB * ATTN_BLOCK) == 0 and seq % TM_PROJ == 0 and seq % ROUTE_TILE == 0
    hw = MLA_HEADS * LANE
    qk = MLA_NOPE_DIM + MLA_ROPE_DIM

    w_mix, w_kr, w_gates = _split_w_in(w_in[0])
    wuq_p = _pack_heads(w_uq[0], qk, [(0, qk)]).astype(BF16)
    wuk_p = _pack_heads(w_uk[0], MLA_NOPE_DIM, [(0, MLA_NOPE_DIM)]).astype(BF16)
    wuv_p = _pack_heads(w_uv[0], MLA_V_DIM, [(0, MLA_V_DIM)]).astype(BF16)
    one_row = jnp.tile((jnp.arange(LANE) == MLA_V_DIM).astype(F32), MLA_HEADS)[None, :]
    freqs = ROPE_THETA ** (-jnp.arange(0, MLA_ROPE_DIM, 2, dtype=F32) / MLA_ROPE_DIM)
    frq = jnp.concatenate([jnp.zeros((MLA_NOPE_DIM,), F32), freqs, freqs,
                           jnp.zeros((LANE - qk,), F32)])[None, :]
    wb3 = w_branch_b[0].reshape(MLA_HEADS, MLA_V_DIM, D_MODEL)
    wb_p = jnp.concatenate([wb3, jnp.zeros((MLA_HEADS, LANE - MLA_V_DIM, D_MODEL), F32)],
                           axis=1).reshape(hw, D_MODEL).astype(BF16)
    wa = w_branch_a[0].astype(BF16)
    wo = w_out[0].astype(BF16)
    wr = jnp.pad(w_router[0], ((0, 0), (0, LANE - N_EXPERTS)))
    wr_hi = wr.astype(BF16)
    wr_lo = (wr - wr_hi.astype(F32)).astype(BF16)
    br_p = jnp.pad(b_router[0], (0, LANE - N_EXPERTS))[None, :]
    b1 = b_moe1[0][:, None, :]
    b2 = b_moe2[0][:, None, :]
    slopes = jnp.asarray(np.exp2(-8.0 * np.arange(1, SWA_HEADS + 1) / SWA_HEADS), dtype=F32)

    c8 = jnp.pad(c, ((0, 8 - batch), (0, 0)))
    mod = _ada(c8, w_ada[0], b_ada[0][None, :])[:batch]
    sh1, sc1, g1, sh2, sc2, g2 = [m[:, None, :] for m in jnp.split(mod, 6, axis=-1)]

    x2 = x.reshape(n, D_MODEL)
    pos2 = positions.reshape(n, 1).astype(jnp.int32)
    qa, ka2, va2, qm, km, vm, ga, gb, stats = _inproj(
        x2, pos2, sh1, sc1, norm_mix, w_mix, w_kr, w_gates, q_norm, kv_norm, wuq_p, wuk_p, wuv_p, frq, one_row, seq)
    st = stats.reshape(batch, seq // TM_PROJ, 8, LANE)
    bound_sq = jnp.max(st[:, :, :, 0], axis=1) * jnp.max(st[:, :, :, 1], axis=1)
    plain = (bound_sq <= MLA_PLAIN_BOUND ** 2).astype(jnp.int32).reshape(-1)
    yb = _mla(plain, qm, km, vm, batch, seq)
    ya = _swa(sinks[0], slopes, qa, ka2, va2, batch, seq)

    x1, h2, route, routet, g8 = _outproj(
        ya, yb, ga, gb, x2, g1, sh2, sc2, norm_ffn, wa, wb_p, wo, wr_hi, wr_lo, br_p, seq)

    n_tiles = n // ROUTE_TILE
    g8t = g8.reshape(n_tiles, 8, LANE)[:, 0, :N_EXPERTS].astype(jnp.int32)
    loff = jnp.cumsum(g8t, axis=1) - g8t
    padded = ((jnp.sum(g8t, axis=0) + MOE_ROWS - 1) // MOE_ROWS) * MOE_ROWS
    pends = jnp.cumsum(padded)
    goff = (pends - padded)[None, :] + jnp.cumsum(g8t, axis=0) - g8t
    tot = jnp.sum(g8t, axis=1).astype(jnp.int32)
    p_rows = n * TOP_K + n_tiles * N_EXPERTS * (GROUP_ALIGN - 1) + N_EXPERTS * (MOE_ROWS - 1)
    p_rows = -(-p_rows // MOE_ROWS) * MOE_ROWS
    n_blocks = p_rows // MOE_ROWS
    block_start = jnp.arange(n_blocks, dtype=jnp.int32) * MOE_ROWS
    block_e = jnp.minimum(jnp.sum((pends[None, :] <= block_start[:, None]).astype(jnp.int32), axis=1),
                          N_EXPERTS - 1).astype(jnp.int32)
    n_used = (pends[-1:] // MOE_ROWS).astype(jnp.int32)
    real_end = pends - padded + jnp.sum(g8t, axis=0)
    block_used = jnp.clip(real_end[block_e] - block_start, 0, MOE_ROWS).astype(jnp.int32)
    tabs = (g8t.reshape(-1), loff.reshape(-1).astype(jnp.int32), goff.reshape(-1).astype(jnp.int32), tot)

    xs = _dispatch(*tabs, pends.astype(jnp.int32), h2, routet, p_rows)
    ys = _experts(block_e, n_used, block_used, xs, w_moe1[0], b1, w_moe2[0], b2)
    out = _combine(*tabs, ys, route, x1, g2, final_norm[None, :], seq)
    return out.reshape(batch, seq, D_MODEL)
```

```python
import functools

import numpy as np
import jax
import jax.numpy as jnp
from jax import lax
from jax.experimental import pallas as pl
from jax.experimental.pallas import tpu as pltpu

D_MODEL = 1024
SWA_HEADS = 8
SWA_KV_HEADS = 2
SWA_HEAD_DIM = 64
ATTN_BLOCK = 128
MLA_HEADS = 8
MLA_Q_RANK = 384
MLA_KV_RANK = 256
MLA_NOPE_DIM = 64
MLA_ROPE_DIM = 32
MLA_V_DIM = 64
ROPE_THETA = 10000.0
N_EXPERTS = 32
TOP_K = 4
D_EXPERT = 1024
SWIGLU_LIMIT = 7.0
SWIGLU_ALPHA = 1.702
NORM_EPS = 1e-6

LANE = 128
LOG2E = 1.4426950408889634
NEG = -1e30
SWA_QSCALE = SWA_HEAD_DIM ** -0.5 * LOG2E
MLA_QSCALE = (MLA_NOPE_DIM + MLA_ROPE_DIM) ** -0.5 * LOG2E
HALF_ROPE = MLA_ROPE_DIM // 2
MLA_PLAIN_BOUND = 60.0
MLA_WIDE = 8

BF16 = jnp.bfloat16
F32 = jnp.float32

TM_PROJ = 512
TQ_MLA = 512
SWA_QB = 4
MOE_ROWS = 512
ROUTE_TILE = 512
GROUP_ALIGN = 8
SLOTS_PER_TILE = ROUTE_TILE * TOP_K + N_EXPERTS * GROUP_ALIGN
SLOT_SUB = 256
SLOT_CHUNK = 3 * SLOT_SUB
XS_COLS = D_MODEL + LANE
VMEM_LIMIT = 56 * 1024 * 1024


def _params(sem, vmem=VMEM_LIMIT):
    return pltpu.CompilerParams(dimension_semantics=sem, vmem_limit_bytes=vmem)


def _rms(x):
    return x * lax.rsqrt(jnp.mean(x * x, axis=-1, keepdims=True) + NORM_EPS)


def _ada_kernel(c_ref, w_ref, b_ref, o_ref):
    c = c_ref[...]
    a = (c * jax.nn.sigmoid(c)).astype(BF16)
    o_ref[...] = jnp.dot(a, w_ref[...].astype(BF16), preferred_element_type=F32) + b_ref[...]


def _ada(c8, w_ada, b_ada):
    n_out = w_ada.shape[1]
    return pl.pallas_call(
        _ada_kernel,
        grid=(n_out // D_MODEL,),
        in_specs=[pl.BlockSpec((8, D_MODEL), lambda j: (0, 0)),
                  pl.BlockSpec((D_MODEL, D_MODEL), lambda j: (0, j)),
                  pl.BlockSpec((1, D_MODEL), lambda j: (0, j))],
        out_specs=pl.BlockSpec((8, D_MODEL), lambda j: (0, j)),
        out_shape=jax.ShapeDtypeStruct((8, n_out), F32),
        compiler_params=_params(("arbitrary",)),
        name="ada",
    )(c8, w_ada, b_ada)


_C_QA = (0, 512)
_C_KV = (512, 768)
_C_LAT = (768, 1408)
_W_MIX_COLS = 1408
_ROPE_LO = (MLA_NOPE_DIM, MLA_NOPE_DIM + HALF_ROPE)
_ROPE_HI = (MLA_NOPE_DIM + HALF_ROPE, MLA_NOPE_DIM + MLA_ROPE_DIM)


def _inproj_kernel(x_ref, pos_ref, sh_ref, sc_ref, nm_ref, win_ref, wkr_ref, wg_ref, qn_ref, kvn_ref, wuq_ref, wuk_ref,
                   wuv_ref, frq_ref, one_ref,
                   qa_ref, ka_ref, va_ref, qm_ref, km_ref, vm_ref, ga_ref, gb_ref, st_ref):
    tm = x_ref.shape[0]
    x = x_ref[...]
    h = (_rms(x) * nm_ref[...] * (1.0 + sc_ref[0]) + sh_ref[0]).astype(BF16)

    def proj(c):
        return jnp.dot(h, win_ref[:, c[0]:c[1]], preferred_element_type=F32)

    lane = lax.broadcasted_iota(jnp.int32, (tm, LANE), 1)
    first = lane < SWA_HEAD_DIM

    def twice(t):
        r = pltpu.roll(t, SWA_HEAD_DIM, axis=1)
        return jnp.concatenate([jnp.where(first, t, r), jnp.where(first, r, t)], axis=1)

    kv = proj(_C_KV)
    ka_ref[...] = twice(kv[:, :LANE]).astype(BF16)
    va_ref[...] = twice(kv[:, LANE:]).astype(BF16)

    ang = pos_ref[...].astype(F32) * frq_ref[...]
    cs = jnp.cos(ang)
    sn = jnp.sin(ang)
    sn_lo = jnp.where((lane >= _ROPE_LO[0]) & (lane < _ROPE_LO[1]), -sn, 0.0)
    sn_hi = jnp.where((lane >= _ROPE_HI[0]) & (lane < _ROPE_HI[1]), sn, 0.0)

    def rotary(t):
        return (t * cs + pltpu.roll(t, LANE - HALF_ROPE, axis=1) * sn_lo
                + pltpu.roll(t, HALF_ROPE, axis=1) * sn_hi)

    lat = proj(_C_LAT)
    cqn = (_rms(lat[:, 0:MLA_Q_RANK]) * qn_ref[...]).astype(BF16)
    ckvn = (_rms(lat[:, MLA_Q_RANK:MLA_Q_RANK + MLA_KV_RANK]) * kvn_ref[...]).astype(BF16)
    krr = rotary(jnp.dot(h, wkr_ref[...], preferred_element_type=F32))
    q = jnp.dot(cqn, wuq_ref[...], preferred_element_type=F32)
    kn = jnp.dot(ckvn, wuk_ref[...], preferred_element_type=F32)

    def max_sq_norm(t):
        tf = t.astype(F32)
        return jnp.max(jnp.sum(tf * tf, axis=1, keepdims=True), axis=0, keepdims=True)

    rid = lax.broadcasted_iota(jnp.int32, (8, LANE), 0)
    lid = lax.broadcasted_iota(jnp.int32, (8, LANE), 1)
    stats = jnp.zeros((8, LANE), F32)
    for hh in range(MLA_HEADS):
        sl = slice(hh * LANE, (hh + 1) * LANE)
        qb = (rotary(q[:, sl]) * MLA_QSCALE).astype(BF16)
        kb = (kn[:, sl] + krr).astype(BF16)
        qm_ref[:, sl] = qb
        km_ref[:, sl] = kb
        stats = jnp.where((rid == hh) & (lid == 0), max_sq_norm(qb), stats)
        stats = jnp.where((rid == hh) & (lid == 1), max_sq_norm(kb), stats)
    st_ref[...] = stats
    vm_ref[...] = (jnp.dot(ckvn, wuv_ref[...], preferred_element_type=F32) + one_ref[...]).astype(BF16)
    qa_ref[...] = (proj(_C_QA) * SWA_QSCALE).astype(BF16)
    ga_ref[...] = jax.nn.sigmoid(jnp.dot(h, wg_ref[:, 0:D_MODEL], preferred_element_type=F32)).astype(BF16)
    gb_ref[...] = jax.nn.sigmoid(jnp.dot(h, wg_ref[:, D_MODEL:], preferred_element_type=F32)).astype(BF16)


def _inproj(x2, pos2, sh1, sc1, nm, w_mix, w_kr, w_gates, qn, kvn, wuq_p, wuk_p, wuv_p, frq, one_row, seq):
    n = x2.shape[0]
    tm = TM_PROJ
    per_b = seq // tm
    hw = MLA_HEADS * LANE
    row = lambda i: (i, 0)
    fix = lambda i: (0, 0)
    bsel = lambda i: (i // per_b, 0, 0)
    widths = (512, 256, 256, hw, hw, hw, D_MODEL, D_MODEL)
    return pl.pallas_call(
        _inproj_kernel,
        grid=(n // tm,),
        in_specs=[pl.BlockSpec((tm, D_MODEL), row),
                  pl.BlockSpec((tm, 1), row),
                  pl.BlockSpec((1, 1, D_MODEL), bsel),
                  pl.BlockSpec((1, 1, D_MODEL), bsel),
                  pl.BlockSpec((1, D_MODEL), fix),
                  pl.BlockSpec((D_MODEL, _W_MIX_COLS), fix),
                  pl.BlockSpec((D_MODEL, LANE), fix),
                  pl.BlockSpec((D_MODEL, 2 * D_MODEL), fix),
                  pl.BlockSpec((1, MLA_Q_RANK), fix),
                  pl.BlockSpec((1, MLA_KV_RANK), fix),
                  pl.BlockSpec((MLA_Q_RANK, hw), fix),
                  pl.BlockSpec((MLA_KV_RANK, hw), fix),
                  pl.BlockSpec((MLA_KV_RANK, hw), fix),
                  pl.BlockSpec((1, LANE), fix),
                  pl.BlockSpec((1, hw), fix)],
        out_specs=[pl.BlockSpec((tm, w), row) for w in widths] + [pl.BlockSpec((8, LANE), row)],
        out_shape=[jax.ShapeDtypeStruct((n, w), BF16) for w in widths]
        + [jax.ShapeDtypeStruct((n // tm * 8, LANE), F32)],
        compiler_params=_params(("arbitrary",)),
        name="inproj",
    )(x2, pos2, sh1, sc1, nm, w_mix, w_kr, w_gates, qn, kvn, wuq_p, wuk_p, wuv_p, frq, one_row)


def _mla_kernel(plain_ref, q_ref, k_ref, v_ref, o_ref, m_scr, acc_scr):
    t = TQ_MLA
    nq = q_ref.shape[0] // t

    def scores(i, j, nblk, diag_at):
        q = q_ref[pl.ds(pl.multiple_of(i * t, t), t), :]
        start = pl.multiple_of(j * t, t)
        k = k_ref[pl.ds(start, nblk * t), :]
        v = v_ref[pl.ds(start, nblk * t), :]
        s = lax.dot_general(q, k, (((1,), (1,)), ((), ())), preferred_element_type=F32)
        if diag_at is not None:
            qi = lax.broadcasted_iota(jnp.int32, s.shape, 0)
            kj = lax.broadcasted_iota(jnp.int32, s.shape, 1)
            s = jnp.where(kj <= qi + diag_at * t, s, NEG)
        return s, v

    def plain_step(i, j, nblk, diag_at):
        s, v = scores(i, j, nblk, diag_at)
        acc_scr[...] += jnp.dot(jnp.exp2(s).astype(BF16), v, preferred_element_type=F32)

    def online_step(i, j, nblk, diag_at):
        s, v = scores(i, j, nblk, diag_at)
        m_old = m_scr[...]
        m_new = jnp.maximum(m_old, jnp.max(s, axis=1, keepdims=True))
        p = jnp.exp2(s - m_new)
        alpha = jnp.exp2(m_old - m_new)
        acc_scr[...] = acc_scr[...] * alpha + jnp.dot(p.astype(BF16), v, preferred_element_type=F32)
        m_scr[...] = m_new

    def run(step, wide, online):
        def q_tile(i, carry):
            acc_scr[...] = jnp.zeros(acc_scr.shape, F32)
            if online:
                m_scr[...] = jnp.full(m_scr.shape, NEG, F32)
            nw = lax.shift_right_logical(i, wide.bit_length() - 1)
            left = i - nw * wide

            def wide_body(jj, c):
                step(i, jj * wide, wide, None)
                return c

            lax.fori_loop(0, nw, wide_body, 0)
            for r in range(wide):
                pl.when(left == r)(lambda r=r: step(i, nw * wide, r + 1, r))
            acc = acc_scr[...]
            o_ref[pl.ds(pl.multiple_of(i * t, t), t), :] = (acc / acc[:, MLA_V_DIM:MLA_V_DIM + 1]).astype(BF16)
            return carry

        lax.fori_loop(0, nq, q_tile, 0)

    small = plain_ref[pl.program_id(0) * MLA_HEADS + pl.program_id(1)] != 0
    pl.when(small)(lambda: run(plain_step, MLA_WIDE, False))
    pl.when(jnp.logical_not(small))(lambda: run(online_step, 1, True))


def _mla(plain, qm, km, vm, batch, seq):
    n = qm.shape[0]
    t = TQ_MLA
    kmap = lambda b, h, *_: (b, h)
    return pl.pallas_call(
        _mla_kernel,
        grid_spec=pltpu.PrefetchScalarGridSpec(
            num_scalar_prefetch=1,
            grid=(batch, MLA_HEADS),
            in_specs=[pl.BlockSpec((seq, LANE), kmap),
                      pl.BlockSpec((seq, LANE), kmap),
                      pl.BlockSpec((seq, LANE), kmap)],
            out_specs=pl.BlockSpec((seq, LANE), kmap),
            scratch_shapes=[pltpu.VMEM((t, 1), F32), pltpu.VMEM((t, LANE), F32)]),
        out_shape=jax.ShapeDtypeStruct((n, MLA_HEADS * LANE), BF16),
        compiler_params=_params(("arbitrary", "arbitrary")),
        name="mla",
    )(plain, qm, km, vm)


def _swa_kernel(sink_ref, slope_ref, q_ref, kc_ref, kp_ref, vc_ref, vp_ref, o_ref, bias_scr):
    b = pl.program_id(0)
    i = pl.program_id(1)
    blk = ATTN_BLOCK

    @pl.when((b == 0) & (i == 0))
    def _():
        qi = lax.broadcasted_iota(jnp.int32, (blk, 2 * blk), 0)
        kj = lax.broadcasted_iota(jnp.int32, (blk, 2 * blk), 1)
        dist = qi - kj + blk
        valid = (dist >= 0) & (dist < blk)
        distf = dist.astype(F32)
        for hd in range(SWA_HEADS):
            bias_scr[hd] = jnp.where(valid, -(slope_ref[hd] * LOG2E) * distf, NEG)

    lane = lax.broadcasted_iota(jnp.int32, (2 * blk, LANE), 1)
    lo = lane < SWA_HEAD_DIM
    kcol = lax.broadcasted_iota(jnp.int32, (blk, 2 * blk), 1)
    first_pen = jnp.where(kcol < blk, jnp.where(i == 0, NEG, 0.0), 0.0)
    zero = jnp.zeros((2 * blk, LANE), BF16)

    for qb in range(SWA_QB):
        if qb == 0:
            kprev, vprev = kp_ref[...], vp_ref[...]
        else:
            kprev = kc_ref[(qb - 1) * blk:qb * blk, :]
            vprev = vc_ref[(qb - 1) * blk:qb * blk, :]
        kcat = jnp.concatenate([kprev, kc_ref[qb * blk:(qb + 1) * blk, :]], axis=0)
        vcat = jnp.concatenate([vprev, vc_ref[qb * blk:(qb + 1) * blk, :]], axis=0)
        for g in range(SWA_KV_HEADS):
            kg = kcat[:, g * LANE:(g + 1) * LANE]
            vg = vcat[:, g * LANE:(g + 1) * LANE]
            halves = ((jnp.where(lo, kg, zero), jnp.where(lo, vg, zero)),
                      (jnp.where(lo, zero, kg), jnp.where(lo, zero, vg)))
            for pp in range(2):
                pr = g * 2 + pp
                qpair = q_ref[qb * blk:(qb + 1) * blk, pr * LANE:(pr + 1) * LANE]
                out = None
                for half in range(2):
                    hd = pr * 2 + half
                    kx, vx = halves[half]
                    s = lax.dot_general(qpair, kx, (((1,), (1,)), ((), ())), preferred_element_type=F32)
                    s = s + bias_scr[hd]
                    if qb == 0:
                        s = s + first_pen
                    sink = sink_ref[hd] * LOG2E
                    m = jnp.maximum(jnp.max(s, axis=1, keepdims=True), sink)
                    p = jnp.exp2(s - m)
                    denom = jnp.sum(p, axis=1, keepdims=True) + jnp.exp2(sink - m)
                    o = jnp.dot(p.astype(BF16), vx, preferred_element_type=F32) * (1.0 / denom)
                    out = o if out is None else out + o
                o_ref[qb * blk:(qb + 1) * blk, pr * LANE:(pr + 1) * LANE] = out.astype(BF16)


def _swa(sinks, slopes, qa, ka2, va2, batch, seq):
    n = qa.shape[0]
    blk = ATTN_BLOCK
    rows = SWA_QB * blk
    steps = seq // rows
    cur = lambda b, i: (b * steps + i, 0)
    prev = lambda b, i: (b * (seq // blk) + jnp.maximum(i * SWA_QB - 1, 0), 0)
    smem = pl.BlockSpec(memory_space=pltpu.SMEM)
    return pl.pallas_call(
        _swa_kernel,
        grid=(batch, steps),
        in_specs=[smem, smem,
                  pl.BlockSpec((rows, 512), cur),
                  pl.BlockSpec((rows, 256), cur),
                  pl.BlockSpec((blk, 256), prev),
                  pl.BlockSpec((rows, 256), cur),
                  pl.BlockSpec((blk, 256), prev)],
        out_specs=pl.BlockSpec((rows, 512), cur),
        out_shape=jax.ShapeDtypeStruct((n, 512), BF16),
        scratch_shapes=[pltpu.VMEM((SWA_HEADS, blk, 2 * blk), F32)],
        compiler_params=_params(("arbitrary", "arbitrary")),
        name="swa",
    )(sinks, slopes, qa, ka2, ka2, va2, va2)


def _outproj_kernel(ya_ref, yb_ref, ga_ref, gb_ref, x_ref, g1_ref, sh_ref, sc_ref, nm_ref, wa_ref, wb_ref,
                    wo_ref, wrh_ref, wrl_ref, br_ref,
                    x1_ref, h2_ref, route_ref, routet_ref, g8_ref):
    tm = x_ref.shape[0]
    a = jnp.dot(ya_ref[...], wa_ref[...], preferred_element_type=F32)
    bm = jnp.dot(yb_ref[...], wb_ref[...], preferred_element_type=F32)
    mixed = (ga_ref[...].astype(F32) * a + gb_ref[...].astype(F32) * bm).astype(BF16)
    x1 = x_ref[...] + g1_ref[0] * jnp.dot(mixed, wo_ref[...], preferred_element_type=F32)
    x1_ref[...] = x1
    h2 = _rms(x1) * nm_ref[...] * (1.0 + sc_ref[0]) + sh_ref[0]
    hi = h2.astype(BF16)
    h2_ref[:, 0:D_MODEL] = hi

    lo = (h2 - hi.astype(F32)).astype(BF16)
    logits = (jnp.dot(hi, wrh_ref[...], preferred_element_type=F32)
              + jnp.dot(lo, wrh_ref[...], preferred_element_type=F32)
              + jnp.dot(hi, wrl_ref[...], preferred_element_type=F32)) + br_ref[...]

    lane = lax.broadcasted_iota(jnp.int32, (tm, LANE), 1).astype(F32)
    work = jnp.where(lane < N_EXPERTS, logits, -jnp.inf)
    vals, idxs = [], []
    for _ in range(TOP_K):
        m = jnp.max(work, axis=1, keepdims=True)
        idx = jnp.min(jnp.where(work == m, lane, float(LANE)), axis=1, keepdims=True)
        vals.append(m)
        idxs.append(idx)
        work = jnp.where(lane == idx, -jnp.inf, work)
    exps = [jnp.exp(v - vals[0]) for v in vals]
    tot = exps[0] + exps[1] + exps[2] + exps[3]

    onehot = jnp.zeros((tm, LANE), F32)
    for idx in idxs:
        onehot = onehot + (lane == idx).astype(F32)
    r = lax.broadcasted_iota(jnp.int32, (tm, tm), 0)
    c = lax.broadcasted_iota(jnp.int32, (tm, tm), 1)
    prefix = jnp.dot((c < r).astype(BF16), onehot.astype(BF16), preferred_element_type=F32)
    groups = jnp.ceil(jnp.sum(onehot, axis=0, keepdims=True) * (1.0 / GROUP_ALIGN))
    er = lax.broadcasted_iota(jnp.int32, (LANE, LANE), 0)
    ec = lax.broadcasted_iota(jnp.int32, (LANE, LANE), 1)
    before = jnp.dot(jnp.broadcast_to(groups, (8, LANE)).astype(BF16), (er < ec).astype(BF16),
                     preferred_element_type=F32)[0:1, :]
    slot_e = before * GROUP_ALIGN + prefix

    route = jnp.zeros((tm, LANE), F32)
    wext = jnp.zeros((tm, LANE), F32)
    for k in range(TOP_K):
        sk = jnp.sum(jnp.where(lane == idxs[k], slot_e, 0.0), axis=1, keepdims=True)
        sub = jnp.floor(sk * (1.0 / SLOT_SUB))
        route = jnp.where(lane == float(k), sk - sub * SLOT_SUB, route)
        route = jnp.where(lane == float(TOP_K + k), sub, route)
        wk = exps[k] / tot
        wk_hi = wk.astype(BF16).astype(F32)
        wext = jnp.where(lane == idxs[k], wk_hi, wext)
        wext = jnp.where(lane == idxs[k] + float(N_EXPERTS), wk - wk_hi, wext)
    h2_ref[:, D_MODEL:XS_COLS] = wext.astype(BF16)
    route_ref[...] = route
    routet_ref[...] = route.T[0:8, :]
    g8_ref[...] = jnp.broadcast_to(groups * GROUP_ALIGN, (8, LANE))


def _outproj(ya, yb, ga, gb, x2, g1, sh2, sc2, nm, wa, wb_p, wo, wr_hi, wr_lo, br_p, seq):
    n = x2.shape[0]
    tm = ROUTE_TILE
    per_b = seq // tm
    hw = MLA_HEADS * LANE
    row = lambda i: (i, 0)
    fix = lambda i: (0, 0)
    bsel = lambda i: (i // per_b, 0, 0)
    return pl.pallas_call(
        _outproj_kernel,
        grid=(n // tm,),
        in_specs=[pl.BlockSpec((tm, 512), row),
                  pl.BlockSpec((tm, hw), row),
                  pl.BlockSpec((tm, D_MODEL), row),
                  pl.BlockSpec((tm, D_MODEL), row),
                  pl.BlockSpec((tm, D_MODEL), row),
                  pl.BlockSpec((1, 1, D_MODEL), bsel),
                  pl.BlockSpec((1, 1, D_MODEL), bsel),
                  pl.BlockSpec((1, 1, D_MODEL), bsel),
                  pl.BlockSpec((1, D_MODEL), fix),
                  pl.BlockSpec((512, D_MODEL), fix),
                  pl.BlockSpec((hw, D_MODEL), fix),
                  pl.BlockSpec((D_MODEL, D_MODEL), fix),
                  pl.BlockSpec((D_MODEL, LANE), fix),
                  pl.BlockSpec((D_MODEL, LANE), fix),
                  pl.BlockSpec((1, LANE), fix)],
        out_specs=[pl.BlockSpec((tm, D_MODEL), row),
                   pl.BlockSpec((tm, XS_COLS), row),
                   pl.BlockSpec((tm, LANE), row),
                   pl.BlockSpec((8, tm), lambda i: (0, i)),
                   pl.BlockSpec((8, LANE), row)],
        out_shape=[jax.ShapeDtypeStruct((n, D_MODEL), F32),
                   jax.ShapeDtypeStruct((n, XS_COLS), BF16),
                   jax.ShapeDtypeStruct((n, LANE), F32),
                   jax.ShapeDtypeStruct((8, n), F32),
                   jax.ShapeDtypeStruct((n // tm * 8, LANE), F32)],
        compiler_params=_params(("arbitrary",)),
        name="outproj",
    )(ya, yb, ga, gb, x2, g1, sh2, sc2, nm, wa, wb_p, wo, wr_hi, wr_lo, br_p)


def _start_runs(tile, g8_ref, loff_ref, goff_ref, make_copy):
    for e in range(N_EXPERTS):
        g = g8_ref[tile * N_EXPERTS + e]
        lo = loff_ref[tile * N_EXPERTS + e]
        go = goff_ref[tile * N_EXPERTS + e]
        def go_copy(lo=lo, go=go, g=g):
            make_copy(pl.multiple_of(lo, GROUP_ALIGN), pl.multiple_of(go, GROUP_ALIGN),
                      pl.multiple_of(g, GROUP_ALIGN)).start()

        pl.when(g > 0)(go_copy)


def _dispatch_kernel(g8_ref, loff_ref, goff_ref, tot_ref, ends_ref, h_ref, rt_ref, xs_ref, buf, zbuf, sem, zsem):
    t = pl.program_id(0)
    nt = pl.num_programs(0)
    tm = h_ref.shape[0]
    slot = lax.rem(t, 2)

    @pl.when(t == 0)
    def _():
        zbuf[...] = jnp.zeros(zbuf.shape, zbuf.dtype)

        def fill(start):
            return pltpu.make_async_copy(zbuf, xs_ref.at[pl.ds(pl.multiple_of(start, MOE_ROWS), MOE_ROWS), :], zsem)

        def nonempty(e):
            return ends_ref[e] > (ends_ref[e - 1] if e else 0)

        used = lax.shift_right_logical(ends_ref[N_EXPERTS - 1], MOE_ROWS.bit_length() - 1)
        n_blocks = xs_ref.shape[0] // MOE_ROWS
        for e in range(N_EXPERTS):
            pl.when(nonempty(e))(lambda e=e: fill(ends_ref[e] - MOE_ROWS).start())
        lax.fori_loop(used, n_blocks, lambda b, c: (fill(b * MOE_ROWS).start(), c)[1], 0)
        for e in range(N_EXPERTS):
            pl.when(nonempty(e))(lambda e=e: fill(ends_ref[e] - MOE_ROWS).wait())
        lax.fori_loop(used, n_blocks, lambda b, c: (fill(b * MOE_ROWS).wait(), c)[1], 0)

    hb = h_ref[...]
    low = [rt_ref[k:k + 1, :] for k in range(TOP_K)]
    sub = [rt_ref[TOP_K + k:TOP_K + k + 1, :] for k in range(TOP_K)]
    srow = lax.broadcasted_iota(jnp.int32, (SLOT_SUB, tm), 0).astype(F32).astype(BF16)
    one = jnp.ones((SLOT_SUB, tm), BF16)

    for ci in range(SLOTS_PER_TILE // SLOT_CHUNK):
        parts = []
        for b in range(SLOT_CHUNK // SLOT_SUB):
            blk = float(ci * (SLOT_CHUNK // SLOT_SUB) + b)
            own = jnp.zeros((SLOT_SUB, tm), BF16)
            for k in range(TOP_K):
                here = jnp.where(sub[k] == blk, low[k], -1.0).astype(BF16)
                own = jnp.where(srow == here, one, own)
            parts.append(own)
        own = jnp.concatenate(parts, axis=0)
        buf[slot, ci * SLOT_CHUNK:(ci + 1) * SLOT_CHUNK, :] = jnp.dot(own, hb, preferred_element_type=F32)

    def copy(s, lo, go, rows):
        return pltpu.make_async_copy(buf.at[s, pl.ds(lo, rows), :], xs_ref.at[pl.ds(go, rows), :], sem.at[s])

    _start_runs(t, g8_ref, loff_ref, goff_ref, functools.partial(copy, slot))

    def drain(tile, s):
        rows = pl.multiple_of(tot_ref[tile], GROUP_ALIGN)
        copy(s, 0, 0, rows).wait()

    pl.when(t > 0)(lambda: drain(t - 1, 1 - slot))
    pl.when(t == nt - 1)(lambda: drain(t, slot))


def _dispatch(g8f, lofff, gofff, totf, ends, h2, routet, p_rows):
    n = h2.shape[0]
    tm = ROUTE_TILE
    return pl.pallas_call(
        _dispatch_kernel,
        grid_spec=pltpu.PrefetchScalarGridSpec(
            num_scalar_prefetch=5,
            grid=(n // tm,),
            in_specs=[pl.BlockSpec((tm, XS_COLS), lambda i, *_: (i, 0)),
                      pl.BlockSpec((8, tm), lambda i, *_: (0, i))],
            out_specs=pl.BlockSpec(memory_space=pl.ANY),
            scratch_shapes=[pltpu.VMEM((2, SLOTS_PER_TILE, XS_COLS), F32),
                            pltpu.VMEM((MOE_ROWS, XS_COLS), F32),
                            pltpu.SemaphoreType.DMA((2,)), pltpu.SemaphoreType.DMA(())]),
        out_shape=jax.ShapeDtypeStruct((p_rows, XS_COLS), F32),
        compiler_params=_params(("arbitrary",)),
        name="dispatch",
    )(g8f, lofff, gofff, totf, ends, h2, routet)


def _experts_kernel(be_ref, nu_ref, bv_ref, nx_ref, xs_ref, w1_hbm, b1_ref, w2_hbm, b2_ref, ys_ref,
                    w1buf, w2buf, wsem, ord_ref):
    del nu_ref
    i = pl.program_id(0)
    half = MOE_ROWS // 2
    used = bv_ref[i]
    e = be_ref[i]

    def fetch(ex, s):
        return (pltpu.make_async_copy(w1_hbm.at[ex], w1buf.at[s], wsem.at[0, s]),
                pltpu.make_async_copy(w2_hbm.at[ex], w2buf.at[s], wsem.at[1, s]))

    @pl.when(i == 0)
    def _():
        ord_ref[0] = 0
        for c in fetch(e, 0):
            c.start()

    @pl.when((used > 0) & ((i == 0) | (e != be_ref[jnp.maximum(i - 1, 0)])))
    def _():
        @pl.when(i > 0)
        def _():
            ord_ref[0] = ord_ref[0] + 1

        s = lax.rem(ord_ref[0], 2)
        for c in fetch(e, s):
            c.wait()
        nxt = nx_ref[i]

        @pl.when(nxt >= 0)
        def _():
            for c in fetch(nxt, 1 - s):
                c.start()

    slot = lax.rem(ord_ref[0], 2)

    def mlp(n_rows):
        xb = xs_ref[0:n_rows, 0:D_MODEL].astype(BF16)
        wext = xs_ref[0:n_rows, D_MODEL:XS_COLS]
        lane = lax.broadcasted_iota(jnp.int32, wext.shape, 1)
        wcol = jnp.sum(jnp.where((lane == e) | (lane == e + N_EXPERTS), wext, 0.0), axis=1, keepdims=True)
        hcat = jnp.dot(xb, w1buf[slot].astype(BF16), preferred_element_type=F32) + b1_ref[0]
        x_glu = jnp.minimum(hcat[:, :D_EXPERT], SWIGLU_LIMIT)
        x_lin = jnp.clip(hcat[:, D_EXPERT:], -SWIGLU_LIMIT, SWIGLU_LIMIT)
        act = (x_glu * jax.nn.sigmoid(SWIGLU_ALPHA * x_glu) * (x_lin + 1.0)).astype(BF16)
        ys_ref[0:n_rows, :] = (jnp.dot(act, w2buf[slot].astype(BF16), preferred_element_type=F32) + b2_ref[0]) * wcol

    pl.when(used > half)(lambda: mlp(MOE_ROWS))

    @pl.when((used > 0) & (used <= half))
    def _():
        mlp(half)
        ys_ref[half:, :] = jnp.zeros((MOE_ROWS - half, D_MODEL), F32)

    @pl.when(used == 0)
    def _():
        ys_ref[...] = jnp.zeros(ys_ref.shape, F32)


def _experts(block_e, n_used, block_used, next_e, xs, w1, b1, w2, b2):
    p = xs.shape[0]
    mb = MOE_ROWS
    last = lambda i, nu: jnp.minimum(i, nu[0] - 1)
    rowmap = lambda i, be, nu, bv, nx: (last(i, nu), 0)
    wmap = lambda i, be, nu, bv, nx: (be[last(i, nu)], 0, 0)
    hbm = pl.BlockSpec(memory_space=pl.ANY)
    return pl.pallas_call(
        _experts_kernel,
        grid_spec=pltpu.PrefetchScalarGridSpec(
            num_scalar_prefetch=4,
            grid=(p // mb,),
            in_specs=[pl.BlockSpec((mb, XS_COLS), rowmap),
                      hbm,
                      pl.BlockSpec((1, 1, 2 * D_EXPERT), wmap),
                      hbm,
                      pl.BlockSpec((1, 1, D_MODEL), wmap)],
            out_specs=pl.BlockSpec((mb, D_MODEL), lambda i, be, nu, bv, nx: (i, 0)),
            scratch_shapes=[pltpu.VMEM((2, D_MODEL, 2 * D_EXPERT), F32),
                            pltpu.VMEM((2, D_EXPERT, D_MODEL), F32),
                            pltpu.SemaphoreType.DMA((2, 2)),
                            pltpu.SMEM((1,), jnp.int32)]),
        out_shape=jax.ShapeDtypeStruct((p, D_MODEL), F32),
        compiler_params=_params(("arbitrary",)),
        name="experts",
    )(block_e, n_used, block_used, next_e, xs, w1, b1, w2, b2)


def _combine_kernel(g8_ref, loff_ref, goff_ref, tot_ref, ys_ref, route_ref, x1_ref, g2_ref, fn_ref, o_ref, stg, sem):
    t = pl.program_id(0)
    nt = pl.num_programs(0)
    tm = x1_ref.shape[0]
    slot = lax.rem(t, 2)

    def copy(s, lo, go, rows):
        return pltpu.make_async_copy(ys_ref.at[pl.ds(go, rows), :], stg.at[s, pl.ds(lo, rows), :], sem.at[s])

    @pl.when(t == 0)
    def _():
        stg[...] = jnp.zeros(stg.shape, F32)
        _start_runs(t, g8_ref, loff_ref, goff_ref, functools.partial(copy, 0))

    pl.when(t + 1 < nt)(lambda: _start_runs(t + 1, g8_ref, loff_ref, goff_ref, functools.partial(copy, 1 - slot)))
    copy(slot, 0, 0, pl.multiple_of(tot_ref[t], GROUP_ALIGN)).wait()

    route = route_ref[...]
    low = [jnp.broadcast_to(route[:, k:k + 1], (tm, SLOT_SUB)).astype(BF16) for k in range(TOP_K)]
    sub = [jnp.broadcast_to(route[:, TOP_K + k:TOP_K + k + 1], (tm, SLOT_SUB)).astype(BF16) for k in range(TOP_K)]
    scol = lax.broadcasted_iota(jnp.int32, (tm, SLOT_SUB), 1).astype(F32).astype(BF16)
    one = jnp.ones((tm, SLOT_SUB), BF16)
    never = jnp.full((tm, SLOT_SUB), -1.0, BF16)

    y = None
    for ci in range(SLOTS_PER_TILE // SLOT_CHUNK):
        parts = []
        for b in range(SLOT_CHUNK // SLOT_SUB):
            blk = float(ci * (SLOT_CHUNK // SLOT_SUB) + b)
            own = jnp.zeros((tm, SLOT_SUB), BF16)
            for k in range(TOP_K):
                own = jnp.where(scol == jnp.where(sub[k] == blk, low[k], never), one, own)
            parts.append(own)
        own = jnp.concatenate(parts, axis=1)
        rows = stg[slot, ci * SLOT_CHUNK:(ci + 1) * SLOT_CHUNK, :].astype(BF16)
        part = jnp.dot(own, rows, preferred_element_type=F32)
        y = part if y is None else y + part
    x = x1_ref[...] + g2_ref[0] * y
    o_ref[...] = _rms(x) * fn_ref[...]


def _combine(g8f, lofff, gofff, totf, ys, route, x1, g2, fn, seq):
    n = x1.shape[0]
    tm = ROUTE_TILE
    per_b = seq // tm
    return pl.pallas_call(
        _combine_kernel,
        grid_spec=pltpu.PrefetchScalarGridSpec(
            num_scalar_prefetch=4,
            grid=(n // tm,),
            in_specs=[pl.BlockSpec(memory_space=pl.ANY),
                      pl.BlockSpec((tm, LANE), lambda i, *_: (i, 0)),
                      pl.BlockSpec((tm, D_MODEL), lambda i, *_: (i, 0)),
                      pl.BlockSpec((1, 1, D_MODEL), lambda i, *_: (i // per_b, 0, 0)),
                      pl.BlockSpec((1, D_MODEL), lambda i, *_: (0, 0))],
            out_specs=pl.BlockSpec((tm, D_MODEL), lambda i, *_: (i, 0)),
            scratch_shapes=[pltpu.VMEM((2, SLOTS_PER_TILE, D_MODEL), F32), pltpu.SemaphoreType.DMA((2,))]),
        out_shape=jax.ShapeDtypeStruct((n, D_MODEL), F32),
        compiler_params=_params(("arbitrary",)),
        name="combine",
    )(g8f, lofff, gofff, totf, ys, route, x1, g2, fn)


def _split_w_in(w_in):
    o_kr = _W_MIX_COLS
    w_kr = jnp.pad(w_in[:, o_kr:o_kr + MLA_ROPE_DIM],
                   ((0, 0), (MLA_NOPE_DIM, LANE - MLA_NOPE_DIM - MLA_ROPE_DIM)))
    return w_in[:, :o_kr].astype(BF16), w_kr.astype(BF16), w_in[:, o_kr + MLA_ROPE_DIM:].astype(BF16)


def _pack_heads(w, width, pieces):
    rows = w.shape[0]
    w3 = w.reshape(rows, MLA_HEADS, width)
    cols = [w3[:, :, a:b] for a, b in pieces]
    used = sum(b - a for a, b in pieces)
    cols.append(jnp.zeros((rows, MLA_HEADS, LANE - used), w.dtype))
    return jnp.concatenate(cols, axis=2).reshape(rows, MLA_HEADS * LANE)


def kernel(x, c, positions, w_ada, b_ada, norm_mix, norm_ffn, w_in, sinks, q_norm, kv_norm, w_uq, w_uk, w_uv,
           w_branch_a, w_branch_b, w_out, w_router, b_router, w_moe1, b_moe1, w_moe2, b_moe2, final_norm):
    batch, seq, d = x.shape
    n = batch * seq
    assert d == D_MODEL and w_ada.shape[0] == 1
    assert seq % TQ_MLA == 0 and seq % (SWA_QB * ATTN_BLOCK) == 0 and seq % TM_PROJ == 0 and seq % ROUTE_TILE == 0
    hw = MLA_HEADS * LANE
    qk = MLA_NOPE_DIM + MLA_ROPE_DIM

    w_mix, w_kr, w_gates = _split_w_in(w_in[0])
    wuq_p = _pack_heads(w_uq[0], qk, [(0, qk)]).astype(BF16)
    wuk_p = _pack_heads(w_uk[0], MLA_NOPE_DIM, [(0, MLA_NOPE_DIM)]).astype(BF16)
    wuv_p = _pack_heads(w_uv[0], MLA_V_DIM, [(0, MLA_V_DIM)]).astype(BF16)
    one_row = jnp.tile((jnp.arange(LANE) == MLA_V_DIM).astype(F32), MLA_HEADS)[None, :]
    freqs = ROPE_THETA ** (-jnp.arange(0, MLA_ROPE_DIM, 2, dtype=F32) / MLA_ROPE_DIM)
    frq = jnp.concatenate([jnp.zeros((MLA_NOPE_DIM,), F32), freqs, freqs,
                           jnp.zeros((LANE - qk,), F32)])[None, :]
    wb3 = w_branch_b[0].reshape(MLA_HEADS, MLA_V_DIM, D_MODEL)
    wb_p = jnp.concatenate([wb3, jnp.zeros((MLA_HEADS, LANE - MLA_V_DIM, D_MODEL), F32)],
                           axis=1).reshape(hw, D_MODEL).astype(BF16)
    wa = w_branch_a[0].astype(BF16)
    wo = w_out[0].astype(BF16)
    wr = jnp.pad(w_router[0], ((0, 0), (0, LANE - N_EXPERTS)))
    wr_hi = wr.astype(BF16)
    wr_lo = (wr - wr_hi.astype(F32)).astype(BF16)
    br_p = jnp.pad(b_router[0], (0, LANE - N_EXPERTS))[None, :]
    b1 = b_moe1[0][:, None, :]
    b2 = b_moe2[0][:, None, :]
    slopes = jnp.asarray(np.exp2(-8.0 * np.arange(1, SWA_HEADS + 1) / SWA_HEADS), dtype=F32)

    c8 = jnp.pad(c, ((0, 8 - batch), (0, 0)))
    mod = _ada(c8, w_ada[0], b_ada[0][None, :])[:batch]
    sh1, sc1, g1, sh2, sc2, g2 = [m[:, None, :] for m in jnp.split(mod, 6, axis=-1)]

    x2 = x.reshape(n, D_MODEL)
    pos2 = positions.reshape(n, 1).astype(jnp.int32)
    qa, ka2, va2, qm, km, vm, ga, gb, stats = _inproj(
        x2, pos2, sh1, sc1, norm_mix, w_mix, w_kr, w_gates, q_norm, kv_norm, wuq_p, wuk_p, wuv_p, frq, one_row, seq)
    st = stats.reshape(batch, seq // TM_PROJ, 8, LANE)
    bound_sq = jnp.max(st[:, :, :, 0], axis=1) * jnp.max(st[:, :, :, 1], axis=1)
    plain = (bound_sq <= MLA_PLAIN_BOUND ** 2).astype(jnp.int32).reshape(-1)
    yb = _mla(plain, qm, km, vm, batch, seq)
    ya = _swa(sinks[0], slopes, qa, ka2, va2, batch, seq)

    x1, h2, route, routet, g8 = _outproj(
        ya, yb, ga, gb, x2, g1, sh2, sc2, norm_ffn, wa, wb_p, wo, wr_hi, wr_lo, br_p, seq)

    n_tiles = n // ROUTE_TILE
    g8t = g8.reshape(n_tiles, 8, LANE)[:, 0, :N_EXPERTS].astype(jnp.int32)
    loff = jnp.cumsum(g8t, axis=1) - g8t
    padded = ((jnp.sum(g8t, axis=0) + MOE_ROWS - 1) // MOE_ROWS) * MOE_ROWS
    pends = jnp.cumsum(padded)
    goff = (pends - padded)[None, :] + jnp.cumsum(g8t, axis=0) - g8t
    tot = jnp.sum(g8t, axis=1).astype(jnp.int32)
    p_rows = n * TOP_K + n_tiles * N_EXPERTS * (GROUP_ALIGN - 1) + N_EXPERTS * (MOE_ROWS - 1)
    p_rows = -(-p_rows // MOE_ROWS) * MOE_ROWS
    n_blocks = p_rows // MOE_ROWS
    block_start = jnp.arange(n_blocks, dtype=jnp.int32) * MOE_ROWS
    block_e = jnp.minimum(jnp.sum((pends[None, :] <= block_start[:, None]).astype(jnp.int32), axis=1),
                          N_EXPERTS - 1).astype(jnp.int32)
    n_used = (pends[-1:] // MOE_ROWS).astype(jnp.int32)
    eids = jnp.arange(N_EXPERTS, dtype=jnp.int32)
    of_block = block_e[:, None] == eids[None, :]
    pick = lambda table: jnp.sum(jnp.where(of_block, table[None, :], 0), axis=1).astype(jnp.int32)
    real_end = pends - padded + jnp.sum(g8t, axis=0)
    block_used = jnp.clip(pick(real_end) - block_start, 0, MOE_ROWS).astype(jnp.int32)
    later = (eids[None, :] > eids[:, None]) & (padded[None, :] > 0)
    after = jnp.min(jnp.where(later, eids[None, :], N_EXPERTS), axis=1)
    next_e = pick(jnp.where(after == N_EXPERTS, -1, after))
    tabs = (g8t.reshape(-1), loff.reshape(-1).astype(jnp.int32), goff.reshape(-1).astype(jnp.int32), tot)

    xs = _dispatch(*tabs, pends.astype(jnp.int32), h2, routet, p_rows)
    ys = _experts(block_e, n_used, block_used, next_e, xs, w_moe1[0], b1, w_moe2[0], b2)
    out = _combine(*tabs, ys, route, x1, g2, final_norm[None, :], seq)
    return out.reshape(batch, seq, D_MODEL)
```

```python
import functools

import numpy as np
import jax
import jax.numpy as jnp
from jax import lax
from jax.experimental import pallas as pl
from jax.experimental.pallas import tpu as pltpu

D_MODEL = 1024
SWA_HEADS = 8
SWA_KV_HEADS = 2
SWA_HEAD_DIM = 64
ATTN_BLOCK = 128
MLA_HEADS = 8
MLA_Q_RANK = 384
MLA_KV_RANK = 256
MLA_NOPE_DIM = 64
MLA_ROPE_DIM = 32
MLA_V_DIM = 64
ROPE_THETA = 10000.0
N_EXPERTS = 32
TOP_K = 4
D_EXPERT = 1024
SWIGLU_LIMIT = 7.0
SWIGLU_ALPHA = 1.702
NORM_EPS = 1e-6

LANE = 128
LOG2E = 1.4426950408889634
NEG = -1e30
SWA_QSCALE = SWA_HEAD_DIM ** -0.5 * LOG2E
MLA_QSCALE = (MLA_NOPE_DIM + MLA_ROPE_DIM) ** -0.5 * LOG2E
HALF_ROPE = MLA_ROPE_DIM // 2
MLA_PLAIN_BOUND = 60.0
MLA_WIDE = 8

BF16 = jnp.bfloat16
F32 = jnp.float32

TM_PROJ = 512
TQ_MLA = 512
SWA_QB = 4
MOE_ROWS = 512
ROUTE_TILE = 512
GROUP_ALIGN = 8
SLOTS_PER_TILE = ROUTE_TILE * TOP_K + N_EXPERTS * GROUP_ALIGN
SLOT_SUB = 256
SLOT_CHUNK = 3 * SLOT_SUB
XS_COLS = D_MODEL + LANE
VMEM_LIMIT = 56 * 1024 * 1024


def _params(sem, vmem=VMEM_LIMIT):
    return pltpu.CompilerParams(dimension_semantics=sem, vmem_limit_bytes=vmem)


def _rms(x):
    return x * lax.rsqrt(jnp.mean(x * x, axis=-1, keepdims=True) + NORM_EPS)


def _ada_kernel(c_ref, w_ref, b_ref, o_ref):
    c = c_ref[...]
    a = (c * jax.nn.sigmoid(c)).astype(BF16)
    o_ref[...] = jnp.dot(a, w_ref[...].astype(BF16), preferred_element_type=F32) + b_ref[...]


def _ada(c8, w_ada, b_ada):
    n_out = w_ada.shape[1]
    return pl.pallas_call(
        _ada_kernel,
        grid=(n_out // D_MODEL,),
        in_specs=[pl.BlockSpec((8, D_MODEL), lambda j: (0, 0)),
                  pl.BlockSpec((D_MODEL, D_MODEL), lambda j: (0, j)),
                  pl.BlockSpec((1, D_MODEL), lambda j: (0, j))],
        out_specs=pl.BlockSpec((8, D_MODEL), lambda j: (0, j)),
        out_shape=jax.ShapeDtypeStruct((8, n_out), F32),
        compiler_params=_params(("arbitrary",)),
        name="ada",
    )(c8, w_ada, b_ada)


_C_QA = (0, 512)
_C_KV = (512, 768)
_C_LAT = (768, 1408)
_W_MIX_COLS = 1408
_ROPE_LO = (MLA_NOPE_DIM, MLA_NOPE_DIM + HALF_ROPE)
_ROPE_HI = (MLA_NOPE_DIM + HALF_ROPE, MLA_NOPE_DIM + MLA_ROPE_DIM)


def _inproj_kernel(x_ref, pos_ref, sh_ref, sc_ref, nm_ref, win_ref, wkr_ref, wg_ref, qn_ref, kvn_ref, wuq_ref, wuk_ref,
                   wuv_ref, frq_ref, one_ref,
                   qa_ref, ka_ref, va_ref, qm_ref, km_ref, vm_ref, ga_ref, gb_ref, st_ref):
    tm = x_ref.shape[0]
    x = x_ref[...]
    h = (_rms(x) * nm_ref[...] * (1.0 + sc_ref[0]) + sh_ref[0]).astype(BF16)

    def proj(c):
        return jnp.dot(h, win_ref[:, c[0]:c[1]], preferred_element_type=F32)

    lane = lax.broadcasted_iota(jnp.int32, (tm, LANE), 1)
    first = lane < SWA_HEAD_DIM

    def twice(t):
        r = pltpu.roll(t, SWA_HEAD_DIM, axis=1)
        return jnp.concatenate([jnp.where(first, t, r), jnp.where(first, r, t)], axis=1)

    kv = proj(_C_KV)
    ka_ref[...] = twice(kv[:, :LANE]).astype(BF16)
    va_ref[...] = twice(kv[:, LANE:]).astype(BF16)

    ang = pos_ref[...].astype(F32) * frq_ref[...]
    cs = jnp.cos(ang)
    sn = jnp.sin(ang)
    sn_lo = jnp.where((lane >= _ROPE_LO[0]) & (lane < _ROPE_LO[1]), -sn, 0.0)
    sn_hi = jnp.where((lane >= _ROPE_HI[0]) & (lane < _ROPE_HI[1]), sn, 0.0)

    def rotary(t):
        return (t * cs + pltpu.roll(t, LANE - HALF_ROPE, axis=1) * sn_lo
                + pltpu.roll(t, HALF_ROPE, axis=1) * sn_hi)

    lat = proj(_C_LAT)
    cqn = (_rms(lat[:, 0:MLA_Q_RANK]) * qn_ref[...]).astype(BF16)
    ckvn = (_rms(lat[:, MLA_Q_RANK:MLA_Q_RANK + MLA_KV_RANK]) * kvn_ref[...]).astype(BF16)
    krr = rotary(jnp.dot(h, wkr_ref[...], preferred_element_type=F32))
    q = jnp.dot(cqn, wuq_ref[...], preferred_element_type=F32)
    kn = jnp.dot(ckvn, wuk_ref[...], preferred_element_type=F32)

    def max_sq_norm(t):
        tf = t.astype(F32)
        return jnp.max(jnp.sum(tf * tf, axis=1, keepdims=True), axis=0, keepdims=True)

    rid = lax.broadcasted_iota(jnp.int32, (8, LANE), 0)
    lid = lax.broadcasted_iota(jnp.int32, (8, LANE), 1)
    stats = jnp.zeros((8, LANE), F32)
    for hh in range(MLA_HEADS):
        sl = slice(hh * LANE, (hh + 1) * LANE)
        qb = (rotary(q[:, sl]) * MLA_QSCALE).astype(BF16)
        kb = (kn[:, sl] + krr).astype(BF16)
        qm_ref[:, sl] = qb
        km_ref[:, sl] = kb
        stats = jnp.where((rid == hh) & (lid == 0), max_sq_norm(qb), stats)
        stats = jnp.where((rid == hh) & (lid == 1), max_sq_norm(kb), stats)
    st_ref[...] = stats
    vm_ref[...] = (jnp.dot(ckvn, wuv_ref[...], preferred_element_type=F32) + one_ref[...]).astype(BF16)
    qa_ref[...] = (proj(_C_QA) * SWA_QSCALE).astype(BF16)
    ga_ref[...] = jax.nn.sigmoid(jnp.dot(h, wg_ref[:, 0:D_MODEL], preferred_element_type=F32)).astype(BF16)
    gb_ref[...] = jax.nn.sigmoid(jnp.dot(h, wg_ref[:, D_MODEL:], preferred_element_type=F32)).astype(BF16)


def _inproj(x2, pos2, sh1, sc1, nm, w_mix, w_kr, w_gates, qn, kvn, wuq_p, wuk_p, wuv_p, frq, one_row, seq):
    n = x2.shape[0]
    tm = TM_PROJ
    per_b = seq // tm
    hw = MLA_HEADS * LANE
    row = lambda i: (i, 0)
    fix = lambda i: (0, 0)
    bsel = lambda i: (i // per_b, 0, 0)
    widths = (512, 256, 256, hw, hw, hw, D_MODEL, D_MODEL)
    return pl.pallas_call(
        _inproj_kernel,
        grid=(n // tm,),
        in_specs=[pl.BlockSpec((tm, D_MODEL), row),
                  pl.BlockSpec((tm, 1), row),
                  pl.BlockSpec((1, 1, D_MODEL), bsel),
                  pl.BlockSpec((1, 1, D_MODEL), bsel),
                  pl.BlockSpec((1, D_MODEL), fix),
                  pl.BlockSpec((D_MODEL, _W_MIX_COLS), fix),
                  pl.BlockSpec((D_MODEL, LANE), fix),
                  pl.BlockSpec((D_MODEL, 2 * D_MODEL), fix),
                  pl.BlockSpec((1, MLA_Q_RANK), fix),
                  pl.BlockSpec((1, MLA_KV_RANK), fix),
                  pl.BlockSpec((MLA_Q_RANK, hw), fix),
                  pl.BlockSpec((MLA_KV_RANK, hw), fix),
                  pl.BlockSpec((MLA_KV_RANK, hw), fix),
                  pl.BlockSpec((1, LANE), fix),
                  pl.BlockSpec((1, hw), fix)],
        out_specs=[pl.BlockSpec((tm, w), row) for w in widths] + [pl.BlockSpec((8, LANE), row)],
        out_shape=[jax.ShapeDtypeStruct((n, w), BF16) for w in widths]
        + [jax.ShapeDtypeStruct((n // tm * 8, LANE), F32)],
        compiler_params=_params(("arbitrary",)),
        name="inproj",
    )(x2, pos2, sh1, sc1, nm, w_mix, w_kr, w_gates, qn, kvn, wuq_p, wuk_p, wuv_p, frq, one_row)


def _mla_kernel(plain_ref, q_ref, k_ref, v_ref, o_ref, m_scr, acc_scr):
    t = TQ_MLA
    nq = q_ref.shape[0] // t

    def scores(i, j, nblk, diag_at):
        q = q_ref[pl.ds(pl.multiple_of(i * t, t), t), :]
        start = pl.multiple_of(j * t, t)
        k = k_ref[pl.ds(start, nblk * t), :]
        v = v_ref[pl.ds(start, nblk * t), :]
        s = lax.dot_general(q, k, (((1,), (1,)), ((), ())), preferred_element_type=F32)
        if diag_at is not None:
            qi = lax.broadcasted_iota(jnp.int32, s.shape, 0)
            kj = lax.broadcasted_iota(jnp.int32, s.shape, 1)
            s = jnp.where(kj <= qi + diag_at * t, s, NEG)
        return s, v

    def plain_step(i, j, nblk, diag_at):
        s, v = scores(i, j, nblk, diag_at)
        acc_scr[...] += jnp.dot(jnp.exp2(s).astype(BF16), v, preferred_element_type=F32)

    def online_step(i, j, nblk, diag_at):
        s, v = scores(i, j, nblk, diag_at)
        m_old = m_scr[...]
        m_new = jnp.maximum(m_old, jnp.max(s, axis=1, keepdims=True))
        p = jnp.exp2(s - m_new)
        alpha = jnp.exp2(m_old - m_new)
        acc_scr[...] = acc_scr[...] * alpha + jnp.dot(p.astype(BF16), v, preferred_element_type=F32)
        m_scr[...] = m_new

    def run(step, wide, online):
        def q_tile(i, carry):
            acc_scr[...] = jnp.zeros(acc_scr.shape, F32)
            if online:
                m_scr[...] = jnp.full(m_scr.shape, NEG, F32)
            nw = lax.shift_right_logical(i, wide.bit_length() - 1)
            left = i - nw * wide

            def wide_body(jj, c):
                step(i, jj * wide, wide, None)
                return c

            lax.fori_loop(0, nw, wide_body, 0)
            for r in range(wide):
                pl.when(left == r)(lambda r=r: step(i, nw * wide, r + 1, r))
            acc = acc_scr[...]
            o_ref[pl.ds(pl.multiple_of(i * t, t), t), :] = (acc / acc[:, MLA_V_DIM:MLA_V_DIM + 1]).astype(BF16)
            return carry

        lax.fori_loop(0, nq, q_tile, 0)

    small = plain_ref[pl.program_id(0) * MLA_HEADS + pl.program_id(1)] != 0
    pl.when(small)(lambda: run(plain_step, MLA_WIDE, False))
    pl.when(jnp.logical_not(small))(lambda: run(online_step, 1, True))


def _mla(plain, qm, km, vm, batch, seq):
    n = qm.shape[0]
    t = TQ_MLA
    kmap = lambda b, h, *_: (b, h)
    return pl.pallas_call(
        _mla_kernel,
        grid_spec=pltpu.PrefetchScalarGridSpec(
            num_scalar_prefetch=1,
            grid=(batch, MLA_HEADS),
            in_specs=[pl.BlockSpec((seq, LANE), kmap),
                      pl.BlockSpec((seq, LANE), kmap),
                      pl.BlockSpec((seq, LANE), kmap)],
            out_specs=pl.BlockSpec((seq, LANE), kmap),
            scratch_shapes=[pltpu.VMEM((t, 1), F32), pltpu.VMEM((t, LANE), F32)]),
        out_shape=jax.ShapeDtypeStruct((n, MLA_HEADS * LANE), BF16),
        compiler_params=_params(("arbitrary", "arbitrary")),
        name="mla",
    )(plain, qm, km, vm)


def _swa_kernel(sink_ref, slope_ref, q_ref, kc_ref, kp_ref, vc_ref, vp_ref, o_ref, bias_scr):
    b = pl.program_id(0)
    i = pl.program_id(1)
    blk = ATTN_BLOCK

    @pl.when((b == 0) & (i == 0))
    def _():
        qi = lax.broadcasted_iota(jnp.int32, (blk, 2 * blk), 0)
        kj = lax.broadcasted_iota(jnp.int32, (blk, 2 * blk), 1)
        dist = qi - kj + blk
        valid = (dist >= 0) & (dist < blk)
        distf = dist.astype(F32)
        for hd in range(SWA_HEADS):
            bias_scr[hd] = jnp.where(valid, -(slope_ref[hd] * LOG2E) * distf, NEG)

    lane = lax.broadcasted_iota(jnp.int32, (2 * blk, LANE), 1)
    lo = lane < SWA_HEAD_DIM
    kcol = lax.broadcasted_iota(jnp.int32, (blk, 2 * blk), 1)
    first_pen = jnp.where(kcol < blk, jnp.where(i == 0, NEG, 0.0), 0.0)
    zero = jnp.zeros((2 * blk, LANE), BF16)

    for qb in range(SWA_QB):
        if qb == 0:
            kprev, vprev = kp_ref[...], vp_ref[...]
        else:
            kprev = kc_ref[(qb - 1) * blk:qb * blk, :]
            vprev = vc_ref[(qb - 1) * blk:qb * blk, :]
        kcat = jnp.concatenate([kprev, kc_ref[qb * blk:(qb + 1) * blk, :]], axis=0)
        vcat = jnp.concatenate([vprev, vc_ref[qb * blk:(qb + 1) * blk, :]], axis=0)
        for g in range(SWA_KV_HEADS):
            kg = kcat[:, g * LANE:(g + 1) * LANE]
            vg = vcat[:, g * LANE:(g + 1) * LANE]
            halves = ((jnp.where(lo, kg, zero), jnp.where(lo, vg, zero)),
                      (jnp.where(lo, zero, kg), jnp.where(lo, zero, vg)))
            for pp in range(2):
                pr = g * 2 + pp
                qpair = q_ref[qb * blk:(qb + 1) * blk, pr * LANE:(pr + 1) * LANE]
                out = None
                for half in range(2):
                    hd = pr * 2 + half
                    kx, vx = halves[half]
                    s = lax.dot_general(qpair, kx, (((1,), (1,)), ((), ())), preferred_element_type=F32)
                    s = s + bias_scr[hd]
                    if qb == 0:
                        s = s + first_pen
                    sink = sink_ref[hd] * LOG2E
                    m = jnp.maximum(jnp.max(s, axis=1, keepdims=True), sink)
                    p = jnp.exp2(s - m)
                    denom = jnp.sum(p, axis=1, keepdims=True) + jnp.exp2(sink - m)
                    o = jnp.dot(p.astype(BF16), vx, preferred_element_type=F32) * (1.0 / denom)
                    out = o if out is None else out + o
                o_ref[qb * blk:(qb + 1) * blk, pr * LANE:(pr + 1) * LANE] = out.astype(BF16)


def _swa(sinks, slopes, qa, ka2, va2, batch, seq):
    n = qa.shape[0]
    blk = ATTN_BLOCK
    rows = SWA_QB * blk
    steps = seq // rows
    cur = lambda b, i: (b * steps + i, 0)
    prev = lambda b, i: (b * (seq // blk) + jnp.maximum(i * SWA_QB - 1, 0), 0)
    smem = pl.BlockSpec(memory_space=pltpu.SMEM)
    return pl.pallas_call(
        _swa_kernel,
        grid=(batch, steps),
        in_specs=[smem, smem,
                  pl.BlockSpec((rows, 512), cur),
                  pl.BlockSpec((rows, 256), cur),
                  pl.BlockSpec((blk, 256), prev),
                  pl.BlockSpec((rows, 256), cur),
                  pl.BlockSpec((blk, 256), prev)],
        out_specs=pl.BlockSpec((rows, 512), cur),
        out_shape=jax.ShapeDtypeStruct((n, 512), BF16),
        scratch_shapes=[pltpu.VMEM((SWA_HEADS, blk, 2 * blk), F32)],
        compiler_params=_params(("arbitrary", "arbitrary")),
        name="swa",
    )(sinks, slopes, qa, ka2, ka2, va2, va2)


def _outproj_kernel(ya_ref, yb_ref, ga_ref, gb_ref, x_ref, g1_ref, sh_ref, sc_ref, nm_ref, wa_ref, wb_ref,
                    wo_ref, wrh_ref, wrl_ref, br_ref,
                    x1_ref, h2_ref, route_ref, routet_ref, g8_ref):
    tm = x_ref.shape[0]
    a = jnp.dot(ya_ref[...], wa_ref[...], preferred_element_type=F32)
    bm = jnp.dot(yb_ref[...], wb_ref[...], preferred_element_type=F32)
    mixed = (ga_ref[...].astype(F32) * a + gb_ref[...].astype(F32) * bm).astype(BF16)
    x1 = x_ref[...] + g1_ref[0] * jnp.dot(mixed, wo_ref[...], preferred_element_type=F32)
    x1_ref[...] = x1
    h2 = _rms(x1) * nm_ref[...] * (1.0 + sc_ref[0]) + sh_ref[0]
    hi = h2.astype(BF16)
    h2_ref[:, 0:D_MODEL] = hi

    lo = (h2 - hi.astype(F32)).astype(BF16)
    logits = (jnp.dot(hi, wrh_ref[...], preferred_element_type=F32)
              + jnp.dot(lo, wrh_ref[...], preferred_element_type=F32)
              + jnp.dot(hi, wrl_ref[...], preferred_element_type=F32)) + br_ref[...]

    lane = lax.broadcasted_iota(jnp.int32, (tm, LANE), 1).astype(F32)
    work = jnp.where(lane < N_EXPERTS, logits, -jnp.inf)
    vals, idxs = [], []
    for _ in range(TOP_K):
        m = jnp.max(work, axis=1, keepdims=True)
        idx = jnp.min(jnp.where(work == m, lane, float(LANE)), axis=1, keepdims=True)
        vals.append(m)
        idxs.append(idx)
        work = jnp.where(lane == idx, -jnp.inf, work)
    exps = [jnp.exp(v - vals[0]) for v in vals]
    tot = exps[0] + exps[1] + exps[2] + exps[3]

    onehot = jnp.zeros((tm, LANE), F32)
    for idx in idxs:
        onehot = onehot + (lane == idx).astype(F32)
    r = lax.broadcasted_iota(jnp.int32, (tm, tm), 0)
    c = lax.broadcasted_iota(jnp.int32, (tm, tm), 1)
    prefix = jnp.dot((c < r).astype(BF16), onehot.astype(BF16), preferred_element_type=F32)
    groups = jnp.ceil(jnp.sum(onehot, axis=0, keepdims=True) * (1.0 / GROUP_ALIGN))
    er = lax.broadcasted_iota(jnp.int32, (LANE, LANE), 0)
    ec = lax.broadcasted_iota(jnp.int32, (LANE, LANE), 1)
    before = jnp.dot(jnp.broadcast_to(groups, (8, LANE)).astype(BF16), (er < ec).astype(BF16),
                     preferred_element_type=F32)[0:1, :]
    slot_e = before * GROUP_ALIGN + prefix

    route = jnp.zeros((tm, LANE), F32)
    wext = jnp.zeros((tm, LANE), F32)
    for k in range(TOP_K):
        sk = jnp.sum(jnp.where(lane == idxs[k], slot_e, 0.0), axis=1, keepdims=True)
        sub = jnp.floor(sk * (1.0 / SLOT_SUB))
        route = jnp.where(lane == float(k), sk - sub * SLOT_SUB, route)
        route = jnp.where(lane == float(TOP_K + k), sub, route)
        wk = exps[k] / tot
        wk_hi = wk.astype(BF16).astype(F32)
        wext = jnp.where(lane == idxs[k], wk_hi, wext)
        wext = jnp.where(lane == idxs[k] + float(N_EXPERTS), wk - wk_hi, wext)
    h2_ref[:, D_MODEL:XS_COLS] = wext.astype(BF16)
    route_ref[...] = route
    routet_ref[...] = route.T[0:8, :]
    g8_ref[...] = jnp.broadcast_to(groups * GROUP_ALIGN, (8, LANE))


def _outproj(ya, yb, ga, gb, x2, g1, sh2, sc2, nm, wa, wb_p, wo, wr_hi, wr_lo, br_p, seq):
    n = x2.shape[0]
    tm = ROUTE_TILE
    per_b = seq // tm
    hw = MLA_HEADS * LANE
    row = lambda i: (i, 0)
    fix = lambda i: (0, 0)
    bsel = lambda i: (i // per_b, 0, 0)
    return pl.pallas_call(
        _outproj_kernel,
        grid=(n // tm,),
        in_specs=[pl.BlockSpec((tm, 512), row),
                  pl.BlockSpec((tm, hw), row),
                  pl.BlockSpec((tm, D_MODEL), row),
                  pl.BlockSpec((tm, D_MODEL), row),
                  pl.BlockSpec((tm, D_MODEL), row),
                  pl.BlockSpec((1, 1, D_MODEL), bsel),
                  pl.BlockSpec((1, 1, D_MODEL), bsel),
                  pl.BlockSpec((1, 1, D_MODEL), bsel),
                  pl.BlockSpec((1, D_MODEL), fix),
                  pl.BlockSpec((512, D_MODEL), fix),
                  pl.BlockSpec((hw, D_MODEL), fix),
                  pl.BlockSpec((D_MODEL, D_MODEL), fix),
                  pl.BlockSpec((D_MODEL, LANE), fix),
                  pl.BlockSpec((D_MODEL, LANE), fix),
                  pl.BlockSpec((1, LANE), fix)],
        out_specs=[pl.BlockSpec((tm, D_MODEL), row),
                   pl.BlockSpec((tm, XS_COLS), row),
                   pl.BlockSpec((tm, LANE), row),
                   pl.BlockSpec((8, tm), lambda i: (0, i)),
                   pl.BlockSpec((8, LANE), row)],
        out_shape=[jax.ShapeDtypeStruct((n, D_MODEL), F32),
                   jax.ShapeDtypeStruct((n, XS_COLS), BF16),
                   jax.ShapeDtypeStruct((n, LANE), F32),
                   jax.ShapeDtypeStruct((8, n), F32),
                   jax.ShapeDtypeStruct((n // tm * 8, LANE), F32)],
        compiler_params=_params(("arbitrary",)),
        name="outproj",
    )(ya, yb, ga, gb, x2, g1, sh2, sc2, nm, wa, wb_p, wo, wr_hi, wr_lo, br_p)


def _start_runs(tile, g8_ref, loff_ref, goff_ref, make_copy):
    for e in range(N_EXPERTS):
        g = g8_ref[tile * N_EXPERTS + e]
        lo = loff_ref[tile * N_EXPERTS + e]
        go = goff_ref[tile * N_EXPERTS + e]
        def go_copy(lo=lo, go=go, g=g, e=e):
            make_copy(pl.multiple_of(lo, GROUP_ALIGN), pl.multiple_of(go, GROUP_ALIGN),
                      pl.multiple_of(g, GROUP_ALIGN)).start(priority=e % 2)

        pl.when(g > 0)(go_copy)


def _dispatch_kernel(g8_ref, loff_ref, goff_ref, tot_ref, ends_ref, h_ref, rt_ref, xs_ref, buf, zbuf, sem, zsem):
    t = pl.program_id(0)
    nt = pl.num_programs(0)
    tm = h_ref.shape[0]
    slot = lax.rem(t, 2)

    @pl.when(t == 0)
    def _():
        zbuf[...] = jnp.zeros(zbuf.shape, zbuf.dtype)

        def fill(start):
            return pltpu.make_async_copy(zbuf, xs_ref.at[pl.ds(pl.multiple_of(start, MOE_ROWS), MOE_ROWS), :], zsem)

        def nonempty(e):
            return ends_ref[e] > (ends_ref[e - 1] if e else 0)

        used = lax.shift_right_logical(ends_ref[N_EXPERTS - 1], MOE_ROWS.bit_length() - 1)
        n_blocks = xs_ref.shape[0] // MOE_ROWS
        for e in range(N_EXPERTS):
            pl.when(nonempty(e))(lambda e=e: fill(ends_ref[e] - MOE_ROWS).start())
        lax.fori_loop(used, n_blocks, lambda b, c: (fill(b * MOE_ROWS).start(), c)[1], 0)
        for e in range(N_EXPERTS):
            pl.when(nonempty(e))(lambda e=e: fill(ends_ref[e] - MOE_ROWS).wait())
        lax.fori_loop(used, n_blocks, lambda b, c: (fill(b * MOE_ROWS).wait(), c)[1], 0)

    hb = h_ref[...]
    low = [rt_ref[k:k + 1, :] for k in range(TOP_K)]
    sub = [rt_ref[TOP_K + k:TOP_K + k + 1, :] for k in range(TOP_K)]
    srow = lax.broadcasted_iota(jnp.int32, (SLOT_SUB, tm), 0).astype(F32).astype(BF16)
    one = jnp.ones((SLOT_SUB, tm), BF16)

    for ci in range(SLOTS_PER_TILE // SLOT_CHUNK):
        parts = []
        for b in range(SLOT_CHUNK // SLOT_SUB):
            blk = float(ci * (SLOT_CHUNK // SLOT_SUB) + b)
            own = jnp.zeros((SLOT_SUB, tm), BF16)
            for k in range(TOP_K):
                here = jnp.where(sub[k] == blk, low[k], -1.0).astype(BF16)
                own = jnp.where(srow == here, one, own)
            parts.append(own)
        own = jnp.concatenate(parts, axis=0)
        buf[slot, ci * SLOT_CHUNK:(ci + 1) * SLOT_CHUNK, :] = jnp.dot(own, hb, preferred_element_type=F32)

    def copy(s, lo, go, rows):
        return pltpu.make_async_copy(buf.at[s, pl.ds(lo, rows), :], xs_ref.at[pl.ds(go, rows), :], sem.at[s])

    _start_runs(t, g8_ref, loff_ref, goff_ref, functools.partial(copy, slot))

    def drain(tile, s):
        rows = pl.multiple_of(tot_ref[tile], GROUP_ALIGN)
        copy(s, 0, 0, rows).wait()

    pl.when(t > 0)(lambda: drain(t - 1, 1 - slot))
    pl.when(t == nt - 1)(lambda: drain(t, slot))


def _dispatch(g8f, lofff, gofff, totf, ends, h2, routet, p_rows):
    n = h2.shape[0]
    tm = ROUTE_TILE
    return pl.pallas_call(
        _dispatch_kernel,
        grid_spec=pltpu.PrefetchScalarGridSpec(
            num_scalar_prefetch=5,
            grid=(n // tm,),
            in_specs=[pl.BlockSpec((tm, XS_COLS), lambda i, *_: (i, 0)),
                      pl.BlockSpec((8, tm), lambda i, *_: (0, i))],
            out_specs=pl.BlockSpec(memory_space=pl.ANY),
            scratch_shapes=[pltpu.VMEM((2, SLOTS_PER_TILE, XS_COLS), F32),
                            pltpu.VMEM((MOE_ROWS, XS_COLS), F32),
                            pltpu.SemaphoreType.DMA((2,)), pltpu.SemaphoreType.DMA(())]),
        out_shape=jax.ShapeDtypeStruct((p_rows, XS_COLS), F32),
        compiler_params=_params(("arbitrary",)),
        name="dispatch",
    )(g8f, lofff, gofff, totf, ends, h2, routet)


def _experts_kernel(be_ref, nu_ref, bv_ref, nx_ref, xs_ref, w1_hbm, b1_ref, w2_hbm, b2_ref, ys_ref,
                    w1buf, w2buf, wsem, ord_ref):
    del nu_ref
    i = pl.program_id(0)
    half = MOE_ROWS // 2
    used = bv_ref[i]
    e = be_ref[i]

    def fetch(ex, s):
        return (pltpu.make_async_copy(w1_hbm.at[ex], w1buf.at[s], wsem.at[0, s]),
                pltpu.make_async_copy(w2_hbm.at[ex], w2buf.at[s], wsem.at[1, s]))

    @pl.when(i == 0)
    def _():
        ord_ref[0] = 0
        for c in fetch(e, 0):
            c.start()

    @pl.when((used > 0) & ((i == 0) | (e != be_ref[jnp.maximum(i - 1, 0)])))
    def _():
        @pl.when(i > 0)
        def _():
            ord_ref[0] = ord_ref[0] + 1

        s = lax.rem(ord_ref[0], 2)
        for c in fetch(e, s):
            c.wait()
        nxt = nx_ref[i]

        @pl.when(nxt >= 0)
        def _():
            for c in fetch(nxt, 1 - s):
                c.start()

    slot = lax.rem(ord_ref[0], 2)

    def mlp(n_rows):
        xb = xs_ref[0:n_rows, 0:D_MODEL].astype(BF16)
        wext = xs_ref[0:n_rows, D_MODEL:XS_COLS]
        lane = lax.broadcasted_iota(jnp.int32, wext.shape, 1)
        wcol = jnp.sum(jnp.where((lane == e) | (lane == e + N_EXPERTS), wext, 0.0), axis=1, keepdims=True)
        hcat = jnp.dot(xb, w1buf[slot].astype(BF16), preferred_element_type=F32) + b1_ref[0]
        x_glu = jnp.minimum(hcat[:, :D_EXPERT], SWIGLU_LIMIT)
        x_lin = jnp.clip(hcat[:, D_EXPERT:], -SWIGLU_LIMIT, SWIGLU_LIMIT)
        act = (x_glu * jax.nn.sigmoid(SWIGLU_ALPHA * x_glu) * (x_lin + 1.0)).astype(BF16)
        ys_ref[0:n_rows, :] = (jnp.dot(act, w2buf[slot].astype(BF16), preferred_element_type=F32) + b2_ref[0]) * wcol

    pl.when(used > half)(lambda: mlp(MOE_ROWS))

    @pl.when((used > 0) & (used <= half))
    def _():
        mlp(half)
        ys_ref[half:, :] = jnp.zeros((MOE_ROWS - half, D_MODEL), F32)

    @pl.when(used == 0)
    def _():
        ys_ref[...] = jnp.zeros(ys_ref.shape, F32)


def _experts(block_e, n_used, block_used, next_e, xs, w1, b1, w2, b2):
    p = xs.shape[0]
    mb = MOE_ROWS
    last = lambda i, nu: jnp.minimum(i, nu[0] - 1)
    rowmap = lambda i, be, nu, bv, nx: (last(i, nu), 0)
    wmap = lambda i, be, nu, bv, nx: (be[last(i, nu)], 0, 0)
    hbm = pl.BlockSpec(memory_space=pl.ANY)
    return pl.pallas_call(
        _experts_kernel,
        grid_spec=pltpu.PrefetchScalarGridSpec(
            num_scalar_prefetch=4,
            grid=(p // mb,),
            in_specs=[pl.BlockSpec((mb, XS_COLS), rowmap),
                      hbm,
                      pl.BlockSpec((1, 1, 2 * D_EXPERT), wmap),
                      hbm,
                      pl.BlockSpec((1, 1, D_MODEL), wmap)],
            out_specs=pl.BlockSpec((mb, D_MODEL), lambda i, be, nu, bv, nx: (i, 0)),
            scratch_shapes=[pltpu.VMEM((2, D_MODEL, 2 * D_EXPERT), F32),
                            pltpu.VMEM((2, D_EXPERT, D_MODEL), F32),
                            pltpu.SemaphoreType.DMA((2, 2)),
                            pltpu.SMEM((1,), jnp.int32)]),
        out_shape=jax.ShapeDtypeStruct((p, D_MODEL), F32),
        compiler_params=_params(("arbitrary",)),
        name="experts",
    )(block_e, n_used, block_used, next_e, xs, w1, b1, w2, b2)


def _combine_kernel(g8_ref, loff_ref, goff_ref, tot_ref, ys_ref, route_ref, x1_ref, g2_ref, fn_ref, o_ref, stg, sem):
    t = pl.program_id(0)
    nt = pl.num_programs(0)
    tm = x1_ref.shape[0]
    slot = lax.rem(t, 2)

    def copy(s, lo, go, rows):
        return pltpu.make_async_copy(ys_ref.at[pl.ds(go, rows), :], stg.at[s, pl.ds(lo, rows), :], sem.at[s])

    @pl.when(t == 0)
    def _():
        stg[...] = jnp.zeros(stg.shape, F32)
        _start_runs(t, g8_ref, loff_ref, goff_ref, functools.partial(copy, 0))

    pl.when(t + 1 < nt)(lambda: _start_runs(t + 1, g8_ref, loff_ref, goff_ref, functools.partial(copy, 1 - slot)))
    copy(slot, 0, 0, pl.multiple_of(tot_ref[t], GROUP_ALIGN)).wait()

    route = route_ref[...]
    low = [jnp.broadcast_to(route[:, k:k + 1], (tm, SLOT_SUB)).astype(BF16) for k in range(TOP_K)]
    sub = [jnp.broadcast_to(route[:, TOP_K + k:TOP_K + k + 1], (tm, SLOT_SUB)).astype(BF16) for k in range(TOP_K)]
    scol = lax.broadcasted_iota(jnp.int32, (tm, SLOT_SUB), 1).astype(F32).astype(BF16)
    one = jnp.ones((tm, SLOT_SUB), BF16)
    never = jnp.full((tm, SLOT_SUB), -1.0, BF16)

    y = None
    for ci in range(SLOTS_PER_TILE // SLOT_CHUNK):
        parts = []
        for b in range(SLOT_CHUNK // SLOT_SUB):
            blk = float(ci * (SLOT_CHUNK // SLOT_SUB) + b)
            own = jnp.zeros((tm, SLOT_SUB), BF16)
            for k in range(TOP_K):
                own = jnp.where(scol == jnp.where(sub[k] == blk, low[k], never), one, own)
            parts.append(own)
        own = jnp.concatenate(parts, axis=1)
        rows = stg[slot, ci * SLOT_CHUNK:(ci + 1) * SLOT_CHUNK, :].astype(BF16)
        part = jnp.dot(own, rows, preferred_element_type=F32)
        y = part if y is None else y + part
    x = x1_ref[...] + g2_ref[0] * y
    o_ref[...] = _rms(x) * fn_ref[...]


def _combine(g8f, lofff, gofff, totf, ys, route, x1, g2, fn, seq):
    n = x1.shape[0]
    tm = ROUTE_TILE
    per_b = seq // tm
    return pl.pallas_call(
        _combine_kernel,
        grid_spec=pltpu.PrefetchScalarGridSpec(
            num_scalar_prefetch=4,
            grid=(n // tm,),
            in_specs=[pl.BlockSpec(memory_space=pl.ANY),
                      pl.BlockSpec((tm, LANE), lambda i, *_: (i, 0)),
                      pl.BlockSpec((tm, D_MODEL), lambda i, *_: (i, 0)),
                      pl.BlockSpec((1, 1, D_MODEL), lambda i, *_: (i // per_b, 0, 0)),
                      pl.BlockSpec((1, D_MODEL), lambda i, *_: (0, 0))],
            out_specs=pl.BlockSpec((tm, D_MODEL), lambda i, *_: (i, 0)),
            scratch_shapes=[pltpu.VMEM((2, SLOTS_PER_TILE, D_MODEL), F32), pltpu.SemaphoreType.DMA((2,))]),
        out_shape=jax.ShapeDtypeStruct((n, D_MODEL), F32),
        compiler_params=_params(("arbitrary",)),
        name="combine",
    )(g8f, lofff, gofff, totf, ys, route, x1, g2, fn)


def _split_w_in(w_in):
    o_kr = _W_MIX_COLS
    w_kr = jnp.pad(w_in[:, o_kr:o_kr + MLA_ROPE_DIM],
                   ((0, 0), (MLA_NOPE_DIM, LANE - MLA_NOPE_DIM - MLA_ROPE_DIM)))
    return w_in[:, :o_kr].astype(BF16), w_kr.astype(BF16), w_in[:, o_kr + MLA_ROPE_DIM:].astype(BF16)


def _pack_heads(w, width, pieces):
    rows = w.shape[0]
    w3 = w.reshape(rows, MLA_HEADS, width)
    cols = [w3[:, :, a:b] for a, b in pieces]
    used = sum(b - a for a, b in pieces)
    cols.append(jnp.zeros((rows, MLA_HEADS, LANE - used), w.dtype))
    return jnp.concatenate(cols, axis=2).reshape(rows, MLA_HEADS * LANE)


def kernel(x, c, positions, w_ada, b_ada, norm_mix, norm_ffn, w_in, sinks, q_norm, kv_norm, w_uq, w_uk, w_uv,
           w_branch_a, w_branch_b, w_out, w_router, b_router, w_moe1, b_moe1, w_moe2, b_moe2, final_norm):
    batch, seq, d = x.shape
    n = batch * seq
    assert d == D_MODEL and w_ada.shape[0] == 1
    assert seq % TQ_MLA == 0 and seq % (SWA_QB * ATTN_BLOCK) == 0 and seq % TM_PROJ == 0 and seq % ROUTE_TILE == 0
    hw = MLA_HEADS * LANE
    qk = MLA_NOPE_DIM + MLA_ROPE_DIM

    w_mix, w_kr, w_gates = _split_w_in(w_in[0])
    wuq_p = _pack_heads(w_uq[0], qk, [(0, qk)]).astype(BF16)
    wuk_p = _pack_heads(w_uk[0], MLA_NOPE_DIM, [(0, MLA_NOPE_DIM)]).astype(BF16)
    wuv_p = _pack_heads(w_uv[0], MLA_V_DIM, [(0, MLA_V_DIM)]).astype(BF16)
    one_row = jnp.tile((jnp.arange(LANE) == MLA_V_DIM).astype(F32), MLA_HEADS)[None, :]
    freqs = ROPE_THETA ** (-jnp.arange(0, MLA_ROPE_DIM, 2, dtype=F32) / MLA_ROPE_DIM)
    frq = jnp.concatenate([jnp.zeros((MLA_NOPE_DIM,), F32), freqs, freqs,
                           jnp.zeros((LANE - qk,), F32)])[None, :]
    wb3 = w_branch_b[0].reshape(MLA_HEADS, MLA_V_DIM, D_MODEL)
    wb_p = jnp.concatenate([wb3, jnp.zeros((MLA_HEADS, LANE - MLA_V_DIM, D_MODEL), F32)],
                           axis=1).reshape(hw, D_MODEL).astype(BF16)
    wa = w_branch_a[0].astype(BF16)
    wo = w_out[0].astype(BF16)
    wr = jnp.pad(w_router[0], ((0, 0), (0, LANE - N_EXPERTS)))
    wr_hi = wr.astype(BF16)
    wr_lo = (wr - wr_hi.astype(F32)).astype(BF16)
    br_p = jnp.pad(b_router[0], (0, LANE - N_EXPERTS))[None, :]
    b1 = b_moe1[0][:, None, :]
    b2 = b_moe2[0][:, None, :]
    slopes = jnp.asarray(np.exp2(-8.0 * np.arange(1, SWA_HEADS + 1) / SWA_HEADS), dtype=F32)

    c8 = jnp.pad(c, ((0, 8 - batch), (0, 0)))
    mod = _ada(c8, w_ada[0], b_ada[0][None, :])[:batch]
    sh1, sc1, g1, sh2, sc2, g2 = [m[:, None, :] for m in jnp.split(mod, 6, axis=-1)]

    x2 = x.reshape(n, D_MODEL)
    pos2 = positions.reshape(n, 1).astype(jnp.int32)
    qa, ka2, va2, qm, km, vm, ga, gb, stats = _inproj(
        x2, pos2, sh1, sc1, norm_mix, w_mix, w_kr, w_gates, q_norm, kv_norm, wuq_p, wuk_p, wuv_p, frq, one_row, seq)
    st = stats.reshape(batch, seq // TM_PROJ, 8, LANE)
    bound_sq = jnp.max(st[:, :, :, 0], axis=1) * jnp.max(st[:, :, :, 1], axis=1)
    plain = (bound_sq <= MLA_PLAIN_BOUND ** 2).astype(jnp.int32).reshape(-1)
    yb = _mla(plain, qm, km, vm, batch, seq)
    ya = _swa(sinks[0], slopes, qa, ka2, va2, batch, seq)

    x1, h2, route, routet, g8 = _outproj(
        ya, yb, ga, gb, x2, g1, sh2, sc2, norm_ffn, wa, wb_p, wo, wr_hi, wr_lo, br_p, seq)

    n_tiles = n // ROUTE_TILE
    g8t = g8.reshape(n_tiles, 8, LANE)[:, 0, :N_EXPERTS].astype(jnp.int32)
    loff = jnp.cumsum(g8t, axis=1) - g8t
    padded = ((jnp.sum(g8t, axis=0) + MOE_ROWS - 1) // MOE_ROWS) * MOE_ROWS
    pends = jnp.cumsum(padded)
    goff = (pends - padded)[None, :] + jnp.cumsum(g8t, axis=0) - g8t
    tot = jnp.sum(g8t, axis=1).astype(jnp.int32)
    p_rows = n * TOP_K + n_tiles * N_EXPERTS * (GROUP_ALIGN - 1) + N_EXPERTS * (MOE_ROWS - 1)
    p_rows = -(-p_rows // MOE_ROWS) * MOE_ROWS
    n_blocks = p_rows // MOE_ROWS
    block_start = jnp.arange(n_blocks, dtype=jnp.int32) * MOE_ROWS
    block_e = jnp.minimum(jnp.sum((pends[None, :] <= block_start[:, None]).astype(jnp.int32), axis=1),
                          N_EXPERTS - 1).astype(jnp.int32)
    n_used = (pends[-1:] // MOE_ROWS).astype(jnp.int32)
    eids = jnp.arange(N_EXPERTS, dtype=jnp.int32)
    of_block = block_e[:, None] == eids[None, :]
    pick = lambda table: jnp.sum(jnp.where(of_block, table[None, :], 0), axis=1).astype(jnp.int32)
    real_end = pends - padded + jnp.sum(g8t, axis=0)
    block_used = jnp.clip(pick(real_end) - block_start, 0, MOE_ROWS).astype(jnp.int32)
    later = (eids[None, :] > eids[:, None]) & (padded[None, :] > 0)
    after = jnp.min(jnp.where(later, eids[None, :], N_EXPERTS), axis=1)
    next_e = pick(jnp.where(after == N_EXPERTS, -1, after))
    tabs = (g8t.reshape(-1), loff.reshape(-1).astype(jnp.int32), goff.reshape(-1).astype(jnp.int32), tot)

    xs = _dispatch(*tabs, pends.astype(jnp.int32), h2, routet, p_rows)
    ys = _experts(block_e, n_used, block_used, next_e, xs, w_moe1[0], b1, w_moe2[0], b2)
    out = _combine(*tabs, ys, route, x1, g2, final_norm[None, :], seq)
    return out.reshape(batch, seq, D_MODEL)
```

```python
import functools

import numpy as np
import jax
import jax.numpy as jnp
from jax import lax
from jax.experimental import pallas as pl
from jax.experimental.pallas import tpu as pltpu

D_MODEL = 1024
SWA_HEADS = 8
SWA_KV_HEADS = 2
SWA_HEAD_DIM = 64
ATTN_BLOCK = 128
MLA_HEADS = 8
MLA_Q_RANK = 384
MLA_KV_RANK = 256
MLA_NOPE_DIM = 64
MLA_ROPE_DIM = 32
MLA_V_DIM = 64
ROPE_THETA = 10000.0
N_EXPERTS = 32
TOP_K = 4
D_EXPERT = 1024
SWIGLU_LIMIT = 7.0
SWIGLU_ALPHA = 1.702
NORM_EPS = 1e-6

LANE = 128
LOG2E = 1.4426950408889634
NEG = -1e30
SWA_QSCALE = SWA_HEAD_DIM ** -0.5 * LOG2E
MLA_QSCALE = (MLA_NOPE_DIM + MLA_ROPE_DIM) ** -0.5 * LOG2E
HALF_ROPE = MLA_ROPE_DIM // 2
MLA_PLAIN_BOUND = 60.0
MLA_WIDE = 8

BF16 = jnp.bfloat16
F32 = jnp.float32

TM_PROJ = 512
TQ_MLA = 512
SWA_QB = 4
MOE_ROWS = 512
ROUTE_TILE = 512
GROUP_ALIGN = 8
SLOTS_PER_TILE = ROUTE_TILE * TOP_K + N_EXPERTS * GROUP_ALIGN
SLOT_SUB = 256
SLOT_CHUNK = 3 * SLOT_SUB
XS_COLS = D_MODEL + LANE
VMEM_LIMIT = 56 * 1024 * 1024


def _params(sem, vmem=VMEM_LIMIT):
    return pltpu.CompilerParams(dimension_semantics=sem, vmem_limit_bytes=vmem)


def _rms(x):
    return x * lax.rsqrt(jnp.mean(x * x, axis=-1, keepdims=True) + NORM_EPS)


def _ada_kernel(c_ref, w_ref, b_ref, o_ref):
    c = c_ref[...]
    a = (c * jax.nn.sigmoid(c)).astype(BF16)
    o_ref[...] = jnp.dot(a, w_ref[...].astype(BF16), preferred_element_type=F32) + b_ref[...]


def _ada(c8, w_ada, b_ada):
    n_out = w_ada.shape[1]
    return pl.pallas_call(
        _ada_kernel,
        grid=(n_out // D_MODEL,),
        in_specs=[pl.BlockSpec((8, D_MODEL), lambda j: (0, 0)),
                  pl.BlockSpec((D_MODEL, D_MODEL), lambda j: (0, j)),
                  pl.BlockSpec((1, D_MODEL), lambda j: (0, j))],
        out_specs=pl.BlockSpec((8, D_MODEL), lambda j: (0, j)),
        out_shape=jax.ShapeDtypeStruct((8, n_out), F32),
        compiler_params=_params(("arbitrary",)),
        name="ada",
    )(c8, w_ada, b_ada)


_C_QA = (0, 512)
_C_KV = (512, 768)
_C_LAT = (768, 1408)
_W_MIX_COLS = 1408
_ROPE_LO = (MLA_NOPE_DIM, MLA_NOPE_DIM + HALF_ROPE)
_ROPE_HI = (MLA_NOPE_DIM + HALF_ROPE, MLA_NOPE_DIM + MLA_ROPE_DIM)


def _inproj_kernel(x_ref, pos_ref, sh_ref, sc_ref, nm_ref, win_ref, wkr_ref, wg_ref, qn_ref, kvn_ref, wuq_ref, wuk_ref,
                   wuv_ref, frq_ref, one_ref,
                   qa_ref, ka_ref, va_ref, qm_ref, km_ref, vm_ref, ga_ref, gb_ref, st_ref):
    tm = x_ref.shape[0]
    x = x_ref[...]
    h = (_rms(x) * nm_ref[...] * (1.0 + sc_ref[0]) + sh_ref[0]).astype(BF16)

    def proj(c):
        return jnp.dot(h, win_ref[:, c[0]:c[1]], preferred_element_type=F32)

    lane = lax.broadcasted_iota(jnp.int32, (tm, LANE), 1)
    first = lane < SWA_HEAD_DIM

    def twice(t):
        r = pltpu.roll(t, SWA_HEAD_DIM, axis=1)
        return jnp.concatenate([jnp.where(first, t, r), jnp.where(first, r, t)], axis=1)

    kv = proj(_C_KV)
    ka_ref[...] = twice(kv[:, :LANE]).astype(BF16)
    va_ref[...] = twice(kv[:, LANE:]).astype(BF16)

    ang = pos_ref[...].astype(F32) * frq_ref[...]
    cs = jnp.cos(ang)
    sn = jnp.sin(ang)
    sn_lo = jnp.where((lane >= _ROPE_LO[0]) & (lane < _ROPE_LO[1]), -sn, 0.0)
    sn_hi = jnp.where((lane >= _ROPE_HI[0]) & (lane < _ROPE_HI[1]), sn, 0.0)

    def rotary(t):
        return (t * cs + pltpu.roll(t, LANE - HALF_ROPE, axis=1) * sn_lo
                + pltpu.roll(t, HALF_ROPE, axis=1) * sn_hi)

    lat = proj(_C_LAT)
    cqn = (_rms(lat[:, 0:MLA_Q_RANK]) * qn_ref[...]).astype(BF16)
    ckvn = (_rms(lat[:, MLA_Q_RANK:MLA_Q_RANK + MLA_KV_RANK]) * kvn_ref[...]).astype(BF16)
    krr = rotary(jnp.dot(h, wkr_ref[...], preferred_element_type=F32))
    q = jnp.dot(cqn, wuq_ref[...], preferred_element_type=F32)
    kn = jnp.dot(ckvn, wuk_ref[...], preferred_element_type=F32)

    def max_sq_norm(t):
        tf = t.astype(F32)
        return jnp.max(jnp.sum(tf * tf, axis=1, keepdims=True), axis=0, keepdims=True)

    rid = lax.broadcasted_iota(jnp.int32, (8, LANE), 0)
    lid = lax.broadcasted_iota(jnp.int32, (8, LANE), 1)
    stats = jnp.zeros((8, LANE), F32)
    for hh in range(MLA_HEADS):
        sl = slice(hh * LANE, (hh + 1) * LANE)
        qb = (rotary(q[:, sl]) * MLA_QSCALE).astype(BF16)
        kb = (kn[:, sl] + krr).astype(BF16)
        qm_ref[:, sl] = qb
        km_ref[:, sl] = kb
        stats = jnp.where((rid == hh) & (lid == 0), max_sq_norm(qb), stats)
        stats = jnp.where((rid == hh) & (lid == 1), max_sq_norm(kb), stats)
    st_ref[...] = stats
    vm_ref[...] = (jnp.dot(ckvn, wuv_ref[...], preferred_element_type=F32) + one_ref[...]).astype(BF16)
    qa_ref[...] = (proj(_C_QA) * SWA_QSCALE).astype(BF16)
    ga_ref[...] = jax.nn.sigmoid(jnp.dot(h, wg_ref[:, 0:D_MODEL], preferred_element_type=F32)).astype(BF16)
    gb_ref[...] = jax.nn.sigmoid(jnp.dot(h, wg_ref[:, D_MODEL:], preferred_element_type=F32)).astype(BF16)


def _inproj(x2, pos2, sh1, sc1, nm, w_mix, w_kr, w_gates, qn, kvn, wuq_p, wuk_p, wuv_p, frq, one_row, seq):
    n = x2.shape[0]
    tm = TM_PROJ
    per_b = seq // tm
    hw = MLA_HEADS * LANE
    row = lambda i: (i, 0)
    fix = lambda i: (0, 0)
    bsel = lambda i: (i // per_b, 0, 0)
    widths = (512, 256, 256, hw, hw, hw, D_MODEL, D_MODEL)
    return pl.pallas_call(
        _inproj_kernel,
        grid=(n // tm,),
        in_specs=[pl.BlockSpec((tm, D_MODEL), row),
                  pl.BlockSpec((tm, 1), row),
                  pl.BlockSpec((1, 1, D_MODEL), bsel),
                  pl.BlockSpec((1, 1, D_MODEL), bsel),
                  pl.BlockSpec((1, D_MODEL), fix),
                  pl.BlockSpec((D_MODEL, _W_MIX_COLS), fix),
                  pl.BlockSpec((D_MODEL, LANE), fix),
                  pl.BlockSpec((D_MODEL, 2 * D_MODEL), fix),
                  pl.BlockSpec((1, MLA_Q_RANK), fix),
                  pl.BlockSpec((1, MLA_KV_RANK), fix),
                  pl.BlockSpec((MLA_Q_RANK, hw), fix),
                  pl.BlockSpec((MLA_KV_RANK, hw), fix),
                  pl.BlockSpec((MLA_KV_RANK, hw), fix),
                  pl.BlockSpec((1, LANE), fix),
                  pl.BlockSpec((1, hw), fix)],
        out_specs=[pl.BlockSpec((tm, w), row) for w in widths] + [pl.BlockSpec((8, LANE), row)],
        out_shape=[jax.ShapeDtypeStruct((n, w), BF16) for w in widths]
        + [jax.ShapeDtypeStruct((n // tm * 8, LANE), F32)],
        compiler_params=_params(("arbitrary",)),
        name="inproj",
    )(x2, pos2, sh1, sc1, nm, w_mix, w_kr, w_gates, qn, kvn, wuq_p, wuk_p, wuv_p, frq, one_row)


def _mla_kernel(plain_ref, q_ref, k_ref, v_ref, o_ref, m_scr, acc_scr):
    t = TQ_MLA
    nq = q_ref.shape[0] // t

    def scores(i, j, nblk, diag_at):
        q = q_ref[pl.ds(pl.multiple_of(i * t, t), t), :]
        start = pl.multiple_of(j * t, t)
        k = k_ref[pl.ds(start, nblk * t), :]
        v = v_ref[pl.ds(start, nblk * t), :]
        s = lax.dot_general(q, k, (((1,), (1,)), ((), ())), preferred_element_type=F32)
        if diag_at is not None:
            qi = lax.broadcasted_iota(jnp.int32, s.shape, 0)
            kj = lax.broadcasted_iota(jnp.int32, s.shape, 1)
            s = jnp.where(kj <= qi + diag_at * t, s, NEG)
        return s, v

    def plain_step(i, j, nblk, diag_at):
        s, v = scores(i, j, nblk, diag_at)
        acc_scr[...] += jnp.dot(jnp.exp2(s).astype(BF16), v, preferred_element_type=F32)

    def online_step(i, j, nblk, diag_at):
        s, v = scores(i, j, nblk, diag_at)
        m_old = m_scr[...]
        m_new = jnp.maximum(m_old, jnp.max(s, axis=1, keepdims=True))
        p = jnp.exp2(s - m_new)
        alpha = jnp.exp2(m_old - m_new)
        acc_scr[...] = acc_scr[...] * alpha + jnp.dot(p.astype(BF16), v, preferred_element_type=F32)
        m_scr[...] = m_new

    def run(step, wide, online):
        def q_tile(i, carry):
            acc_scr[...] = jnp.zeros(acc_scr.shape, F32)
            if online:
                m_scr[...] = jnp.full(m_scr.shape, NEG, F32)
            nw = lax.shift_right_logical(i, wide.bit_length() - 1)
            left = i - nw * wide

            def wide_body(jj, c):
                step(i, jj * wide, wide, None)
                return c

            lax.fori_loop(0, nw, wide_body, 0)
            for r in range(wide):
                pl.when(left == r)(lambda r=r: step(i, nw * wide, r + 1, r))
            acc = acc_scr[...]
            o_ref[pl.ds(pl.multiple_of(i * t, t), t), :] = (acc / acc[:, MLA_V_DIM:MLA_V_DIM + 1]).astype(BF16)
            return carry

        lax.fori_loop(0, nq, q_tile, 0)

    small = plain_ref[pl.program_id(0) * MLA_HEADS + pl.program_id(1)] != 0
    pl.when(small)(lambda: run(plain_step, MLA_WIDE, False))
    pl.when(jnp.logical_not(small))(lambda: run(online_step, 1, True))


def _mla(plain, qm, km, vm, batch, seq):
    n = qm.shape[0]
    t = TQ_MLA
    kmap = lambda b, h, *_: (b, h)
    return pl.pallas_call(
        _mla_kernel,
        grid_spec=pltpu.PrefetchScalarGridSpec(
            num_scalar_prefetch=1,
            grid=(batch, MLA_HEADS),
            in_specs=[pl.BlockSpec((seq, LANE), kmap),
                      pl.BlockSpec((seq, LANE), kmap),
                      pl.BlockSpec((seq, LANE), kmap)],
            out_specs=pl.BlockSpec((seq, LANE), kmap),
            scratch_shapes=[pltpu.VMEM((t, 1), F32), pltpu.VMEM((t, LANE), F32)]),
        out_shape=jax.ShapeDtypeStruct((n, MLA_HEADS * LANE), BF16),
        compiler_params=_params(("arbitrary", "arbitrary")),
        name="mla",
    )(plain, qm, km, vm)


def _swa_kernel(sink_ref, slope_ref, q_ref, kc_ref, kp_ref, vc_ref, vp_ref, o_ref, bias_scr):
    b = pl.program_id(0)
    i = pl.program_id(1)
    blk = ATTN_BLOCK

    @pl.when((b == 0) & (i == 0))
    def _():
        qi = lax.broadcasted_iota(jnp.int32, (blk, 2 * blk), 0)
        kj = lax.broadcasted_iota(jnp.int32, (blk, 2 * blk), 1)
        dist = qi - kj + blk
        valid = (dist >= 0) & (dist < blk)
        distf = dist.astype(F32)
        for hd in range(SWA_HEADS):
            bias_scr[hd] = jnp.where(valid, -(slope_ref[hd] * LOG2E) * distf, NEG)

    lane = lax.broadcasted_iota(jnp.int32, (2 * blk, LANE), 1)
    lo = lane < SWA_HEAD_DIM
    kcol = lax.broadcasted_iota(jnp.int32, (blk, 2 * blk), 1)
    first_pen = jnp.where(kcol < blk, jnp.where(i == 0, NEG, 0.0), 0.0)
    zero = jnp.zeros((2 * blk, LANE), BF16)

    for qb in range(SWA_QB):
        if qb == 0:
            kprev, vprev = kp_ref[...], vp_ref[...]
        else:
            kprev = kc_ref[(qb - 1) * blk:qb * blk, :]
            vprev = vc_ref[(qb - 1) * blk:qb * blk, :]
        kcat = jnp.concatenate([kprev, kc_ref[qb * blk:(qb + 1) * blk, :]], axis=0)
        vcat = jnp.concatenate([vprev, vc_ref[qb * blk:(qb + 1) * blk, :]], axis=0)
        for g in range(SWA_KV_HEADS):
            kg = kcat[:, g * LANE:(g + 1) * LANE]
            vg = vcat[:, g * LANE:(g + 1) * LANE]
            halves = ((jnp.where(lo, kg, zero), jnp.where(lo, vg, zero)),
                      (jnp.where(lo, zero, kg), jnp.where(lo, zero, vg)))
            for pp in range(2):
                pr = g * 2 + pp
                qpair = q_ref[qb * blk:(qb + 1) * blk, pr * LANE:(pr + 1) * LANE]
                out = None
                for half in range(2):
                    hd = pr * 2 + half
                    kx, vx = halves[half]
                    s = lax.dot_general(qpair, kx, (((1,), (1,)), ((), ())), preferred_element_type=F32)
                    s = s + bias_scr[hd]
                    if qb == 0:
                        s = s + first_pen
                    sink = sink_ref[hd] * LOG2E
                    m = jnp.maximum(jnp.max(s, axis=1, keepdims=True), sink)
                    p = jnp.exp2(s - m)
                    denom = jnp.sum(p, axis=1, keepdims=True) + jnp.exp2(sink - m)
                    o = jnp.dot(p.astype(BF16), vx, preferred_element_type=F32) * (1.0 / denom)
                    out = o if out is None else out + o
                o_ref[qb * blk:(qb + 1) * blk, pr * LANE:(pr + 1) * LANE] = out.astype(BF16)


def _swa(sinks, slopes, qa, ka2, va2, batch, seq):
    n = qa.shape[0]
    blk = ATTN_BLOCK
    rows = SWA_QB * blk
    steps = seq // rows
    cur = lambda b, i: (b * steps + i, 0)
    prev = lambda b, i: (b * (seq // blk) + jnp.maximum(i * SWA_QB - 1, 0), 0)
    smem = pl.BlockSpec(memory_space=pltpu.SMEM)
    return pl.pallas_call(
        _swa_kernel,
        grid=(batch, steps),
        in_specs=[smem, smem,
                  pl.BlockSpec((rows, 512), cur),
                  pl.BlockSpec((rows, 256), cur),
                  pl.BlockSpec((blk, 256), prev),
                  pl.BlockSpec((rows, 256), cur),
                  pl.BlockSpec((blk, 256), prev)],
        out_specs=pl.BlockSpec((rows, 512), cur),
        out_shape=jax.ShapeDtypeStruct((n, 512), BF16),
        scratch_shapes=[pltpu.VMEM((SWA_HEADS, blk, 2 * blk), F32)],
        compiler_params=_params(("arbitrary", "arbitrary")),
        name="swa",
    )(sinks, slopes, qa, ka2, ka2, va2, va2)


def _outproj_kernel(ya_ref, yb_ref, ga_ref, gb_ref, x_ref, g1_ref, sh_ref, sc_ref, nm_ref, wa_ref, wb_ref,
                    wo_ref, wrh_ref, wrl_ref, br_ref,
                    x1_ref, h2_ref, route_ref, routet_ref, g8_ref):
    tm = x_ref.shape[0]
    a = jnp.dot(ya_ref[...], wa_ref[...], preferred_element_type=F32)
    bm = jnp.dot(yb_ref[...], wb_ref[...], preferred_element_type=F32)
    mixed = (ga_ref[...].astype(F32) * a + gb_ref[...].astype(F32) * bm).astype(BF16)
    x1 = x_ref[...] + g1_ref[0] * jnp.dot(mixed, wo_ref[...], preferred_element_type=F32)
    x1_ref[...] = x1
    h2 = _rms(x1) * nm_ref[...] * (1.0 + sc_ref[0]) + sh_ref[0]
    hi = h2.astype(BF16)
    h2_ref[:, 0:D_MODEL] = hi

    lo = (h2 - hi.astype(F32)).astype(BF16)
    logits = (jnp.dot(hi, wrh_ref[...], preferred_element_type=F32)
              + jnp.dot(lo, wrh_ref[...], preferred_element_type=F32)
              + jnp.dot(hi, wrl_ref[...], preferred_element_type=F32)) + br_ref[...]

    lane = lax.broadcasted_iota(jnp.int32, (tm, LANE), 1).astype(F32)
    work = jnp.where(lane < N_EXPERTS, logits, -jnp.inf)
    vals, idxs = [], []
    for _ in range(TOP_K):
        m = jnp.max(work, axis=1, keepdims=True)
        idx = jnp.min(jnp.where(work == m, lane, float(LANE)), axis=1, keepdims=True)
        vals.append(m)
        idxs.append(idx)
        work = jnp.where(lane == idx, -jnp.inf, work)
    exps = [jnp.exp(v - vals[0]) for v in vals]
    tot = exps[0] + exps[1] + exps[2] + exps[3]

    onehot = jnp.zeros((tm, LANE), F32)
    for idx in idxs:
        onehot = onehot + (lane == idx).astype(F32)
    r = lax.broadcasted_iota(jnp.int32, (tm, tm), 0)
    c = lax.broadcasted_iota(jnp.int32, (tm, tm), 1)
    prefix = jnp.dot((c < r).astype(BF16), onehot.astype(BF16), preferred_element_type=F32)
    groups = jnp.ceil(jnp.sum(onehot, axis=0, keepdims=True) * (1.0 / GROUP_ALIGN))
    er = lax.broadcasted_iota(jnp.int32, (LANE, LANE), 0)
    ec = lax.broadcasted_iota(jnp.int32, (LANE, LANE), 1)
    before = jnp.dot(jnp.broadcast_to(groups, (8, LANE)).astype(BF16), (er < ec).astype(BF16),
                     preferred_element_type=F32)[0:1, :]
    slot_e = before * GROUP_ALIGN + prefix

    route = jnp.zeros((tm, LANE), F32)
    wext = jnp.zeros((tm, LANE), F32)
    for k in range(TOP_K):
        sk = jnp.sum(jnp.where(lane == idxs[k], slot_e, 0.0), axis=1, keepdims=True)
        sub = jnp.floor(sk * (1.0 / SLOT_SUB))
        route = jnp.where(lane == float(k), sk - sub * SLOT_SUB, route)
        route = jnp.where(lane == float(TOP_K + k), sub, route)
        wk = exps[k] / tot
        wk_hi = wk.astype(BF16).astype(F32)
        wext = jnp.where(lane == idxs[k], wk_hi, wext)
        wext = jnp.where(lane == idxs[k] + float(N_EXPERTS), wk - wk_hi, wext)
    h2_ref[:, D_MODEL:XS_COLS] = wext.astype(BF16)
    route_ref[...] = route
    routet_ref[...] = route.T[0:8, :]
    g8_ref[...] = jnp.broadcast_to(groups * GROUP_ALIGN, (8, LANE))


def _outproj(ya, yb, ga, gb, x2, g1, sh2, sc2, nm, wa, wb_p, wo, wr_hi, wr_lo, br_p, seq):
    n = x2.shape[0]
    tm = ROUTE_TILE
    per_b = seq // tm
    hw = MLA_HEADS * LANE
    row = lambda i: (i, 0)
    fix = lambda i: (0, 0)
    bsel = lambda i: (i // per_b, 0, 0)
    return pl.pallas_call(
        _outproj_kernel,
        grid=(n // tm,),
        in_specs=[pl.BlockSpec((tm, 512), row),
                  pl.BlockSpec((tm, hw), row),
                  pl.BlockSpec((tm, D_MODEL), row),
                  pl.BlockSpec((tm, D_MODEL), row),
                  pl.BlockSpec((tm, D_MODEL), row),
                  pl.BlockSpec((1, 1, D_MODEL), bsel),
                  pl.BlockSpec((1, 1, D_MODEL), bsel),
                  pl.BlockSpec((1, 1, D_MODEL), bsel),
                  pl.BlockSpec((1, D_MODEL), fix),
                  pl.BlockSpec((512, D_MODEL), fix),
                  pl.BlockSpec((hw, D_MODEL), fix),
                  pl.BlockSpec((D_MODEL, D_MODEL), fix),
                  pl.BlockSpec((D_MODEL, LANE), fix),
                  pl.BlockSpec((D_MODEL, LANE), fix),
                  pl.BlockSpec((1, LANE), fix)],
        out_specs=[pl.BlockSpec((tm, D_MODEL), row),
                   pl.BlockSpec((tm, XS_COLS), row),
                   pl.BlockSpec((tm, LANE), row),
                   pl.BlockSpec((8, tm), lambda i: (0, i)),
                   pl.BlockSpec((8, LANE), row)],
        out_shape=[jax.ShapeDtypeStruct((n, D_MODEL), F32),
                   jax.ShapeDtypeStruct((n, XS_COLS), BF16),
                   jax.ShapeDtypeStruct((n, LANE), F32),
                   jax.ShapeDtypeStruct((8, n), F32),
                   jax.ShapeDtypeStruct((n // tm * 8, LANE), F32)],
        compiler_params=_params(("arbitrary",)),
        name="outproj",
    )(ya, yb, ga, gb, x2, g1, sh2, sc2, nm, wa, wb_p, wo, wr_hi, wr_lo, br_p)


def _start_runs(tile, g8_ref, loff_ref, goff_ref, make_copy):
    for e in range(N_EXPERTS):
        g = g8_ref[tile * N_EXPERTS + e]
        lo = loff_ref[tile * N_EXPERTS + e]
        go = goff_ref[tile * N_EXPERTS + e]
        def go_copy(lo=lo, go=go, g=g):
            make_copy(pl.multiple_of(lo, GROUP_ALIGN), pl.multiple_of(go, GROUP_ALIGN),
                      pl.multiple_of(g, GROUP_ALIGN)).start()

        pl.when(g > 0)(go_copy)


def _dispatch_kernel(g8_ref, loff_ref, goff_ref, tot_ref, ends_ref, h_ref, rt_ref, xs_ref, buf, zbuf, sem, zsem):
    t = pl.program_id(0)
    nt = pl.num_programs(0)
    tm = h_ref.shape[0]
    slot = lax.rem(t, 2)

    @pl.when(t == 0)
    def _():
        zbuf[...] = jnp.zeros(zbuf.shape, zbuf.dtype)

        def fill(start, rows):
            return pltpu.make_async_copy(zbuf.at[pl.ds(0, rows), :], xs_ref.at[pl.ds(start, rows), :], zsem)

        def pads(e):
            real_end = pl.multiple_of(ends_ref[N_EXPERTS + e], GROUP_ALIGN)
            return real_end, pl.multiple_of(ends_ref[e] - real_end, GROUP_ALIGN)

        used = lax.shift_right_logical(ends_ref[N_EXPERTS - 1], MOE_ROWS.bit_length() - 1)
        n_blocks = xs_ref.shape[0] // MOE_ROWS
        tail = lambda b: fill(pl.multiple_of(b * MOE_ROWS, MOE_ROWS), MOE_ROWS)
        for e in range(N_EXPERTS):
            pl.when(pads(e)[1] > 0)(lambda e=e: fill(*pads(e)).start())
        lax.fori_loop(used, n_blocks, lambda b, c: (tail(b).start(), c)[1], 0)
        for e in range(N_EXPERTS):
            pl.when(pads(e)[1] > 0)(lambda e=e: fill(*pads(e)).wait())
        lax.fori_loop(used, n_blocks, lambda b, c: (tail(b).wait(), c)[1], 0)

    hb = h_ref[...]
    low = [rt_ref[k:k + 1, :] for k in range(TOP_K)]
    sub = [rt_ref[TOP_K + k:TOP_K + k + 1, :] for k in range(TOP_K)]
    srow = lax.broadcasted_iota(jnp.int32, (SLOT_SUB, tm), 0).astype(F32).astype(BF16)
    one = jnp.ones((SLOT_SUB, tm), BF16)

    for ci in range(SLOTS_PER_TILE // SLOT_CHUNK):
        parts = []
        for b in range(SLOT_CHUNK // SLOT_SUB):
            blk = float(ci * (SLOT_CHUNK // SLOT_SUB) + b)
            own = jnp.zeros((SLOT_SUB, tm), BF16)
            for k in range(TOP_K):
                here = jnp.where(sub[k] == blk, low[k], -1.0).astype(BF16)
                own = jnp.where(srow == here, one, own)
            parts.append(own)
        own = jnp.concatenate(parts, axis=0)
        buf[slot, ci * SLOT_CHUNK:(ci + 1) * SLOT_CHUNK, :] = jnp.dot(own, hb, preferred_element_type=F32)

    def copy(s, lo, go, rows):
        return pltpu.make_async_copy(buf.at[s, pl.ds(lo, rows), :], xs_ref.at[pl.ds(go, rows), :], sem.at[s])

    _start_runs(t, g8_ref, loff_ref, goff_ref, functools.partial(copy, slot))

    def drain(tile, s):
        rows = pl.multiple_of(tot_ref[tile], GROUP_ALIGN)
        copy(s, 0, 0, rows).wait()

    pl.when(t > 0)(lambda: drain(t - 1, 1 - slot))
    pl.when(t == nt - 1)(lambda: drain(t, slot))


def _dispatch(g8f, lofff, gofff, totf, ends, h2, routet, p_rows):
    n = h2.shape[0]
    tm = ROUTE_TILE
    return pl.pallas_call(
        _dispatch_kernel,
        grid_spec=pltpu.PrefetchScalarGridSpec(
            num_scalar_prefetch=5,
            grid=(n // tm,),
            in_specs=[pl.BlockSpec((tm, XS_COLS), lambda i, *_: (i, 0)),
                      pl.BlockSpec((8, tm), lambda i, *_: (0, i))],
            out_specs=pl.BlockSpec(memory_space=pl.ANY),
            scratch_shapes=[pltpu.VMEM((2, SLOTS_PER_TILE, XS_COLS), F32),
                            pltpu.VMEM((MOE_ROWS, XS_COLS), F32),
                            pltpu.SemaphoreType.DMA((2,)), pltpu.SemaphoreType.DMA(())]),
        out_shape=jax.ShapeDtypeStruct((p_rows, XS_COLS), F32),
        compiler_params=_params(("arbitrary",)),
        name="dispatch",
    )(g8f, lofff, gofff, totf, ends, h2, routet)


def _experts_kernel(be_ref, nu_ref, bv_ref, nx_ref, xs_ref, w1_hbm, b1_ref, w2_hbm, b2_ref, ys_ref,
                    w1buf, w2buf, wsem, ord_ref):
    del nu_ref
    i = pl.program_id(0)
    half = MOE_ROWS // 2
    used = bv_ref[i]
    e = be_ref[i]

    def fetch(ex, s):
        return (pltpu.make_async_copy(w1_hbm.at[ex], w1buf.at[s], wsem.at[0, s]),
                pltpu.make_async_copy(w2_hbm.at[ex], w2buf.at[s], wsem.at[1, s]))

    @pl.when(i == 0)
    def _():
        ord_ref[0] = 0
        for c in fetch(e, 0):
            c.start()

    @pl.when((used > 0) & ((i == 0) | (e != be_ref[jnp.maximum(i - 1, 0)])))
    def _():
        @pl.when(i > 0)
        def _():
            ord_ref[0] = ord_ref[0] + 1

        s = lax.rem(ord_ref[0], 2)
        for c in fetch(e, s):
            c.wait()
        nxt = nx_ref[i]

        @pl.when(nxt >= 0)
        def _():
            for c in fetch(nxt, 1 - s):
                c.start()

    slot = lax.rem(ord_ref[0], 2)

    def mlp(n_rows):
        xb = xs_ref[0:n_rows, 0:D_MODEL].astype(BF16)
        wext = xs_ref[0:n_rows, D_MODEL:XS_COLS]
        lane = lax.broadcasted_iota(jnp.int32, wext.shape, 1)
        wcol = jnp.sum(jnp.where((lane == e) | (lane == e + N_EXPERTS), wext, 0.0), axis=1, keepdims=True)
        hcat = jnp.dot(xb, w1buf[slot].astype(BF16), preferred_element_type=F32) + b1_ref[0]
        x_glu = jnp.minimum(hcat[:, :D_EXPERT], SWIGLU_LIMIT)
        x_lin = jnp.clip(hcat[:, D_EXPERT:], -SWIGLU_LIMIT, SWIGLU_LIMIT)
        act = (x_glu * jax.nn.sigmoid(SWIGLU_ALPHA * x_glu) * (x_lin + 1.0)).astype(BF16)
        ys_ref[0:n_rows, :] = (jnp.dot(act, w2buf[slot].astype(BF16), preferred_element_type=F32) + b2_ref[0]) * wcol

    pl.when(used > half)(lambda: mlp(MOE_ROWS))

    @pl.when((used > 0) & (used <= half))
    def _():
        mlp(half)
        ys_ref[half:, :] = jnp.zeros((MOE_ROWS - half, D_MODEL), F32)

    @pl.when(used == 0)
    def _():
        ys_ref[...] = jnp.zeros(ys_ref.shape, F32)


def _experts(block_e, n_used, block_used, next_e, xs, w1, b1, w2, b2):
    p = xs.shape[0]
    mb = MOE_ROWS
    last = lambda i, nu: jnp.minimum(i, nu[0] - 1)
    rowmap = lambda i, be, nu, bv, nx: (last(i, nu), 0)
    wmap = lambda i, be, nu, bv, nx: (be[last(i, nu)], 0, 0)
    hbm = pl.BlockSpec(memory_space=pl.ANY)
    return pl.pallas_call(
        _experts_kernel,
        grid_spec=pltpu.PrefetchScalarGridSpec(
            num_scalar_prefetch=4,
            grid=(p // mb,),
            in_specs=[pl.BlockSpec((mb, XS_COLS), rowmap),
                      hbm,
                      pl.BlockSpec((1, 1, 2 * D_EXPERT), wmap),
                      hbm,
                      pl.BlockSpec((1, 1, D_MODEL), wmap)],
            out_specs=pl.BlockSpec((mb, D_MODEL), lambda i, be, nu, bv, nx: (i, 0)),
            scratch_shapes=[pltpu.VMEM((2, D_MODEL, 2 * D_EXPERT), F32),
                            pltpu.VMEM((2, D_EXPERT, D_MODEL), F32),
                            pltpu.SemaphoreType.DMA((2, 2)),
                            pltpu.SMEM((1,), jnp.int32)]),
        out_shape=jax.ShapeDtypeStruct((p, D_MODEL), F32),
        compiler_params=_params(("arbitrary",)),
        name="experts",
    )(block_e, n_used, block_used, next_e, xs, w1, b1, w2, b2)


def _combine_kernel(g8_ref, loff_ref, goff_ref, tot_ref, ys_ref, route_ref, x1_ref, g2_ref, fn_ref, o_ref, stg, sem):
    t = pl.program_id(0)
    nt = pl.num_programs(0)
    tm = x1_ref.shape[0]
    slot = lax.rem(t, 2)

    def copy(s, lo, go, rows):
        return pltpu.make_async_copy(ys_ref.at[pl.ds(go, rows), :], stg.at[s, pl.ds(lo, rows), :], sem.at[s])

    @pl.when(t == 0)
    def _():
        stg[...] = jnp.zeros(stg.shape, F32)
        _start_runs(t, g8_ref, loff_ref, goff_ref, functools.partial(copy, 0))

    pl.when(t + 1 < nt)(lambda: _start_runs(t + 1, g8_ref, loff_ref, goff_ref, functools.partial(copy, 1 - slot)))
    copy(slot, 0, 0, pl.multiple_of(tot_ref[t], GROUP_ALIGN)).wait()

    route = route_ref[...]
    low = [jnp.broadcast_to(route[:, k:k + 1], (tm, SLOT_SUB)).astype(BF16) for k in range(TOP_K)]
    sub = [jnp.broadcast_to(route[:, TOP_K + k:TOP_K + k + 1], (tm, SLOT_SUB)).astype(BF16) for k in range(TOP_K)]
    scol = lax.broadcasted_iota(jnp.int32, (tm, SLOT_SUB), 1).astype(F32).astype(BF16)
    one = jnp.ones((tm, SLOT_SUB), BF16)
    never = jnp.full((tm, SLOT_SUB), -1.0, BF16)

    y = None
    for ci in range(SLOTS_PER_TILE // SLOT_CHUNK):
        parts = []
        for b in range(SLOT_CHUNK // SLOT_SUB):
            blk = float(ci * (SLOT_CHUNK // SLOT_SUB) + b)
            own = jnp.zeros((tm, SLOT_SUB), BF16)
            for k in range(TOP_K):
                own = jnp.where(scol == jnp.where(sub[k] == blk, low[k], never), one, own)
            parts.append(own)
        own = jnp.concatenate(parts, axis=1)
        rows = stg[slot, ci * SLOT_CHUNK:(ci + 1) * SLOT_CHUNK, :].astype(BF16)
        part = jnp.dot(own, rows, preferred_element_type=F32)
        y = part if y is None else y + part
    x = x1_ref[...] + g2_ref[0] * y
    o_ref[...] = _rms(x) * fn_ref[...]


def _combine(g8f, lofff, gofff, totf, ys, route, x1, g2, fn, seq):
    n = x1.shape[0]
    tm = ROUTE_TILE
    per_b = seq // tm
    return pl.pallas_call(
        _combine_kernel,
        grid_spec=pltpu.PrefetchScalarGridSpec(
            num_scalar_prefetch=4,
            grid=(n // tm,),
            in_specs=[pl.BlockSpec(memory_space=pl.ANY),
                      pl.BlockSpec((tm, LANE), lambda i, *_: (i, 0)),
                      pl.BlockSpec((tm, D_MODEL), lambda i, *_: (i, 0)),
                      pl.BlockSpec((1, 1, D_MODEL), lambda i, *_: (i // per_b, 0, 0)),
                      pl.BlockSpec((1, D_MODEL), lambda i, *_: (0, 0))],
            out_specs=pl.BlockSpec((tm, D_MODEL), lambda i, *_: (i, 0)),
            scratch_shapes=[pltpu.VMEM((2, SLOTS_PER_TILE, D_MODEL), F32), pltpu.SemaphoreType.DMA((2,))]),
        out_shape=jax.ShapeDtypeStruct((n, D_MODEL), F32),
        compiler_params=_params(("arbitrary",)),
        name="combine",
    )(g8f, lofff, gofff, totf, ys, route, x1, g2, fn)


def _split_w_in(w_in):
    o_kr = _W_MIX_COLS
    w_kr = jnp.pad(w_in[:, o_kr:o_kr + MLA_ROPE_DIM],
                   ((0, 0), (MLA_NOPE_DIM, LANE - MLA_NOPE_DIM - MLA_ROPE_DIM)))
    return w_in[:, :o_kr].astype(BF16), w_kr.astype(BF16), w_in[:, o_kr + MLA_ROPE_DIM:].astype(BF16)


def _pack_heads(w, width, pieces):
    rows = w.shape[0]
    w3 = w.reshape(rows, MLA_HEADS, width)
    cols = [w3[:, :, a:b] for a, b in pieces]
    used = sum(b - a for a, b in pieces)
    cols.append(jnp.zeros((rows, MLA_HEADS, LANE - used), w.dtype))
    return jnp.concatenate(cols, axis=2).reshape(rows, MLA_HEADS * LANE)


def kernel(x, c, positions, w_ada, b_ada, norm_mix, norm_ffn, w_in, sinks, q_norm, kv_norm, w_uq, w_uk, w_uv,
           w_branch_a, w_branch_b, w_out, w_router, b_router, w_moe1, b_moe1, w_moe2, b_moe2, final_norm):
    batch, seq, d = x.shape
    n = batch * seq
    assert d == D_MODEL and w_ada.shape[0] == 1
    assert seq % TQ_MLA == 0 and seq % (SWA_QB * ATTN_BLOCK) == 0 and seq % TM_PROJ == 0 and seq % ROUTE_TILE == 0
    hw = MLA_HEADS * LANE
    qk = MLA_NOPE_DIM + MLA_ROPE_DIM

    w_mix, w_kr, w_gates = _split_w_in(w_in[0])
    wuq_p = _pack_heads(w_uq[0], qk, [(0, qk)]).astype(BF16)
    wuk_p = _pack_heads(w_uk[0], MLA_NOPE_DIM, [(0, MLA_NOPE_DIM)]).astype(BF16)
    wuv_p = _pack_heads(w_uv[0], MLA_V_DIM, [(0, MLA_V_DIM)]).astype(BF16)
    one_row = jnp.tile((jnp.arange(LANE) == MLA_V_DIM).astype(F32), MLA_HEADS)[None, :]
    freqs = ROPE_THETA ** (-jnp.arange(0, MLA_ROPE_DIM, 2, dtype=F32) / MLA_ROPE_DIM)
    frq = jnp.concatenate([jnp.zeros((MLA_NOPE_DIM,), F32), freqs, freqs,
                           jnp.zeros((LANE - qk,), F32)])[None, :]
    wb3 = w_branch_b[0].reshape(MLA_HEADS, MLA_V_DIM, D_MODEL)
    wb_p = jnp.concatenate([wb3, jnp.zeros((MLA_HEADS, LANE - MLA_V_DIM, D_MODEL), F32)],
                           axis=1).reshape(hw, D_MODEL).astype(BF16)
    wa = w_branch_a[0].astype(BF16)
    wo = w_out[0].astype(BF16)
    wr = jnp.pad(w_router[0], ((0, 0), (0, LANE - N_EXPERTS)))
    wr_hi = wr.astype(BF16)
    wr_lo = (wr - wr_hi.astype(F32)).astype(BF16)
    br_p = jnp.pad(b_router[0], (0, LANE - N_EXPERTS))[None, :]
    b1 = b_moe1[0][:, None, :]
    b2 = b_moe2[0][:, None, :]
    slopes = jnp.asarray(np.exp2(-8.0 * np.arange(1, SWA_HEADS + 1) / SWA_HEADS), dtype=F32)

    c8 = jnp.pad(c, ((0, 8 - batch), (0, 0)))
    mod = _ada(c8, w_ada[0], b_ada[0][None, :])[:batch]
    sh1, sc1, g1, sh2, sc2, g2 = [m[:, None, :] for m in jnp.split(mod, 6, axis=-1)]

    x2 = x.reshape(n, D_MODEL)
    pos2 = positions.reshape(n, 1).astype(jnp.int32)
    qa, ka2, va2, qm, km, vm, ga, gb, stats = _inproj(
        x2, pos2, sh1, sc1, norm_mix, w_mix, w_kr, w_gates, q_norm, kv_norm, wuq_p, wuk_p, wuv_p, frq, one_row, seq)
    st = stats.reshape(batch, seq // TM_PROJ, 8, LANE)
    bound_sq = jnp.max(st[:, :, :, 0], axis=1) * jnp.max(st[:, :, :, 1], axis=1)
    plain = (bound_sq <= MLA_PLAIN_BOUND ** 2).astype(jnp.int32).reshape(-1)
    yb = _mla(plain, qm, km, vm, batch, seq)
    ya = _swa(sinks[0], slopes, qa, ka2, va2, batch, seq)

    x1, h2, route, routet, g8 = _outproj(
        ya, yb, ga, gb, x2, g1, sh2, sc2, norm_ffn, wa, wb_p, wo, wr_hi, wr_lo, br_p, seq)

    n_tiles = n // ROUTE_TILE
    g8t = g8.reshape(n_tiles, 8, LANE)[:, 0, :N_EXPERTS].astype(jnp.int32)
    loff = jnp.cumsum(g8t, axis=1) - g8t
    padded = ((jnp.sum(g8t, axis=0) + MOE_ROWS - 1) // MOE_ROWS) * MOE_ROWS
    pends = jnp.cumsum(padded)
    goff = (pends - padded)[None, :] + jnp.cumsum(g8t, axis=0) - g8t
    tot = jnp.sum(g8t, axis=1).astype(jnp.int32)
    p_rows = n * TOP_K + n_tiles * N_EXPERTS * (GROUP_ALIGN - 1) + N_EXPERTS * (MOE_ROWS - 1)
    p_rows = -(-p_rows // MOE_ROWS) * MOE_ROWS
    n_blocks = p_rows // MOE_ROWS
    block_start = jnp.arange(n_blocks, dtype=jnp.int32) * MOE_ROWS
    block_e = jnp.minimum(jnp.sum((pends[None, :] <= block_start[:, None]).astype(jnp.int32), axis=1),
                          N_EXPERTS - 1).astype(jnp.int32)
    n_used = (pends[-1:] // MOE_ROWS).astype(jnp.int32)
    eids = jnp.arange(N_EXPERTS, dtype=jnp.int32)
    of_block = block_e[:, None] == eids[None, :]
    pick = lambda table: jnp.sum(jnp.where(of_block, table[None, :], 0), axis=1).astype(jnp.int32)
    real_end = pends - padded + jnp.sum(g8t, axis=0)
    block_used = jnp.clip(pick(real_end) - block_start, 0, MOE_ROWS).astype(jnp.int32)
    later = (eids[None, :] > eids[:, None]) & (padded[None, :] > 0)
    after = jnp.min(jnp.where(later, eids[None, :], N_EXPERTS), axis=1)
    next_e = pick(jnp.where(after == N_EXPERTS, -1, after))
    tabs = (g8t.reshape(-1), loff.reshape(-1).astype(jnp.int32), goff.reshape(-1).astype(jnp.int32), tot)

    xs = _dispatch(*tabs, jnp.concatenate([pends, real_end]).astype(jnp.int32), h2, routet, p_rows)
    ys = _experts(block_e, n_used, block_used, next_e, xs, w_moe1[0], b1, w_moe2[0], b2)
    out = _combine(*tabs, ys, route, x1, g2, final_norm[None, :], seq)
    return out.reshape(batch, seq, D_MODEL)
```

```python
import functools

import numpy as np
import jax
import jax.numpy as jnp
from jax import lax
from jax.experimental import pallas as pl
from jax.experimental.pallas import tpu as pltpu

D_MODEL = 1024
SWA_HEADS = 8
SWA_KV_HEADS = 2
SWA_HEAD_DIM = 64
ATTN_BLOCK = 128
MLA_HEADS = 8
MLA_Q_RANK = 384
MLA_KV_RANK = 256
MLA_NOPE_DIM = 64
MLA_ROPE_DIM = 32
MLA_V_DIM = 64
ROPE_THETA = 10000.0
N_EXPERTS = 32
TOP_K = 4
D_EXPERT = 1024
SWIGLU_LIMIT = 7.0
SWIGLU_ALPHA = 1.702
NORM_EPS = 1e-6

LANE = 128
LOG2E = 1.4426950408889634
NEG = -1e30
SWA_QSCALE = SWA_HEAD_DIM ** -0.5 * LOG2E
MLA_QSCALE = (MLA_NOPE_DIM + MLA_ROPE_DIM) ** -0.5 * LOG2E
HALF_ROPE = MLA_ROPE_DIM // 2
MLA_PLAIN_BOUND = 60.0
MLA_WIDE = 8

BF16 = jnp.bfloat16
F32 = jnp.float32

TM_PROJ = 512
TQ_MLA = 512
SWA_QB = 4
MOE_ROWS = 512
ROUTE_TILE = 512
GROUP_ALIGN = 8
SLOTS_PER_TILE = ROUTE_TILE * TOP_K + N_EXPERTS * GROUP_ALIGN
SLOT_SUB = 256
SLOT_CHUNK = 3 * SLOT_SUB
XS_COLS = D_MODEL + LANE
VMEM_LIMIT = 56 * 1024 * 1024


def _params(sem, vmem=VMEM_LIMIT):
    return pltpu.CompilerParams(dimension_semantics=sem, vmem_limit_bytes=vmem)


def _rms(x):
    return x * lax.rsqrt(jnp.mean(x * x, axis=-1, keepdims=True) + NORM_EPS)


def _ada_kernel(c_ref, w_ref, b_ref, o_ref):
    c = c_ref[...]
    a = (c * jax.nn.sigmoid(c)).astype(BF16)
    o_ref[...] = jnp.dot(a, w_ref[...].astype(BF16), preferred_element_type=F32) + b_ref[...]


def _ada(c8, w_ada, b_ada):
    n_out = w_ada.shape[1]
    return pl.pallas_call(
        _ada_kernel,
        grid=(n_out // D_MODEL,),
        in_specs=[pl.BlockSpec((8, D_MODEL), lambda j: (0, 0)),
                  pl.BlockSpec((D_MODEL, D_MODEL), lambda j: (0, j)),
                  pl.BlockSpec((1, D_MODEL), lambda j: (0, j))],
        out_specs=pl.BlockSpec((8, D_MODEL), lambda j: (0, j)),
        out_shape=jax.ShapeDtypeStruct((8, n_out), F32),
        compiler_params=_params(("arbitrary",)),
        name="ada",
    )(c8, w_ada, b_ada)


_C_QA = (0, 512)
_C_KV = (512, 768)
_C_LAT = (768, 1408)
_W_MIX_COLS = 1408
_ROPE_LO = (MLA_NOPE_DIM, MLA_NOPE_DIM + HALF_ROPE)
_ROPE_HI = (MLA_NOPE_DIM + HALF_ROPE, MLA_NOPE_DIM + MLA_ROPE_DIM)


def _inproj_kernel(x_ref, pos_ref, sh_ref, sc_ref, nm_ref, win_ref, wkr_ref, wg_ref, qn_ref, kvn_ref, wuq_ref, wuk_ref,
                   wuv_ref, frq_ref, one_ref,
                   qa_ref, ka_ref, va_ref, qm_ref, km_ref, vm_ref, ga_ref, gb_ref, st_ref):
    tm = x_ref.shape[0]
    x = x_ref[...]
    h = (_rms(x) * nm_ref[...] * (1.0 + sc_ref[0]) + sh_ref[0]).astype(BF16)

    def proj(c):
        return jnp.dot(h, win_ref[:, c[0]:c[1]], preferred_element_type=F32)

    lane = lax.broadcasted_iota(jnp.int32, (tm, LANE), 1)
    first = lane < SWA_HEAD_DIM

    def twice(t):
        r = pltpu.roll(t, SWA_HEAD_DIM, axis=1)
        return jnp.concatenate([jnp.where(first, t, r), jnp.where(first, r, t)], axis=1)

    kv = proj(_C_KV)
    ka_ref[...] = twice(kv[:, :LANE]).astype(BF16)
    va_ref[...] = twice(kv[:, LANE:]).astype(BF16)

    ang = pos_ref[...].astype(F32) * frq_ref[...]
    cs = jnp.cos(ang)
    sn = jnp.sin(ang)
    sn_lo = jnp.where((lane >= _ROPE_LO[0]) & (lane < _ROPE_LO[1]), -sn, 0.0)
    sn_hi = jnp.where((lane >= _ROPE_HI[0]) & (lane < _ROPE_HI[1]), sn, 0.0)

    def rotary(t):
        return (t * cs + pltpu.roll(t, LANE - HALF_ROPE, axis=1) * sn_lo
                + pltpu.roll(t, HALF_ROPE, axis=1) * sn_hi)

    lat = proj(_C_LAT)
    cqn = (_rms(lat[:, 0:MLA_Q_RANK]) * qn_ref[...]).astype(BF16)
    ckvn = (_rms(lat[:, MLA_Q_RANK:MLA_Q_RANK + MLA_KV_RANK]) * kvn_ref[...]).astype(BF16)
    krr = rotary(jnp.dot(h, wkr_ref[...], preferred_element_type=F32))
    q = jnp.dot(cqn, wuq_ref[...], preferred_element_type=F32)
    kn = jnp.dot(ckvn, wuk_ref[...], preferred_element_type=F32)

    def max_sq_norm(t):
        tf = t.astype(F32)
        return jnp.max(jnp.sum(tf * tf, axis=1, keepdims=True), axis=0, keepdims=True)

    rid = lax.broadcasted_iota(jnp.int32, (8, LANE), 0)
    lid = lax.broadcasted_iota(jnp.int32, (8, LANE), 1)
    stats = jnp.zeros((8, LANE), F32)
    for hh in range(MLA_HEADS):
        sl = slice(hh * LANE, (hh + 1) * LANE)
        qb = (rotary(q[:, sl]) * MLA_QSCALE).astype(BF16)
        kb = (kn[:, sl] + krr).astype(BF16)
        qm_ref[:, sl] = qb
        km_ref[:, sl] = kb
        stats = jnp.where((rid == hh) & (lid == 0), max_sq_norm(qb), stats)
        stats = jnp.where((rid == hh) & (lid == 1), max_sq_norm(kb), stats)
    st_ref[...] = stats
    vm_ref[...] = (jnp.dot(ckvn, wuv_ref[...], preferred_element_type=F32) + one_ref[...]).astype(BF16)
    qa_ref[...] = (proj(_C_QA) * SWA_QSCALE).astype(BF16)
    ga_ref[...] = jax.nn.sigmoid(jnp.dot(h, wg_ref[:, 0:D_MODEL], preferred_element_type=F32)).astype(BF16)
    gb_ref[...] = jax.nn.sigmoid(jnp.dot(h, wg_ref[:, D_MODEL:], preferred_element_type=F32)).astype(BF16)


def _inproj(x2, pos2, sh1, sc1, nm, w_mix, w_kr, w_gates, qn, kvn, wuq_p, wuk_p, wuv_p, frq, one_row, seq):
    n = x2.shape[0]
    tm = TM_PROJ
    per_b = seq // tm
    hw = MLA_HEADS * LANE
    row = lambda i: (i, 0)
    fix = lambda i: (0, 0)
    bsel = lambda i: (i // per_b, 0, 0)
    widths = (512, 256, 256, hw, hw, hw, D_MODEL, D_MODEL)
    return pl.pallas_call(
        _inproj_kernel,
        grid=(n // tm,),
        in_specs=[pl.BlockSpec((tm, D_MODEL), row),
                  pl.BlockSpec((tm, 1), row),
                  pl.BlockSpec((1, 1, D_MODEL), bsel),
                  pl.BlockSpec((1, 1, D_MODEL), bsel),
                  pl.BlockSpec((1, D_MODEL), fix),
                  pl.BlockSpec((D_MODEL, _W_MIX_COLS), fix),
                  pl.BlockSpec((D_MODEL, LANE), fix),
                  pl.BlockSpec((D_MODEL, 2 * D_MODEL), fix),
                  pl.BlockSpec((1, MLA_Q_RANK), fix),
                  pl.BlockSpec((1, MLA_KV_RANK), fix),
                  pl.BlockSpec((MLA_Q_RANK, hw), fix),
                  pl.BlockSpec((MLA_KV_RANK, hw), fix),
                  pl.BlockSpec((MLA_KV_RANK, hw), fix),
                  pl.BlockSpec((1, LANE), fix),
                  pl.BlockSpec((1, hw), fix)],
        out_specs=[pl.BlockSpec((tm, w), row) for w in widths] + [pl.BlockSpec((8, LANE), row)],
        out_shape=[jax.ShapeDtypeStruct((n, w), BF16) for w in widths]
        + [jax.ShapeDtypeStruct((n // tm * 8, LANE), F32)],
        compiler_params=_params(("arbitrary",)),
        name="inproj",
    )(x2, pos2, sh1, sc1, nm, w_mix, w_kr, w_gates, qn, kvn, wuq_p, wuk_p, wuv_p, frq, one_row)


def _mla_kernel(plain_ref, q_ref, k_ref, v_ref, o_ref, m_scr, acc_scr):
    t = TQ_MLA
    nq = q_ref.shape[0] // t

    def scores(i, j, nblk, diag_at):
        q = q_ref[pl.ds(pl.multiple_of(i * t, t), t), :]
        start = pl.multiple_of(j * t, t)
        k = k_ref[pl.ds(start, nblk * t), :]
        v = v_ref[pl.ds(start, nblk * t), :]
        s = lax.dot_general(q, k, (((1,), (1,)), ((), ())), preferred_element_type=F32)
        if diag_at is not None:
            qi = lax.broadcasted_iota(jnp.int32, s.shape, 0)
            kj = lax.broadcasted_iota(jnp.int32, s.shape, 1)
            s = jnp.where(kj <= qi + diag_at * t, s, NEG)
        return s, v

    def plain_step(i, j, nblk, diag_at):
        s, v = scores(i, j, nblk, diag_at)
        acc_scr[...] += jnp.dot(jnp.exp2(s).astype(BF16), v, preferred_element_type=F32)

    def online_step(i, j, nblk, diag_at):
        s, v = scores(i, j, nblk, diag_at)
        m_old = m_scr[...]
        m_new = jnp.maximum(m_old, jnp.max(s, axis=1, keepdims=True))
        p = jnp.exp2(s - m_new)
        alpha = jnp.exp2(m_old - m_new)
        acc_scr[...] = acc_scr[...] * alpha + jnp.dot(p.astype(BF16), v, preferred_element_type=F32)
        m_scr[...] = m_new

    def run(step, wide, online):
        def q_tile(i, carry):
            acc_scr[...] = jnp.zeros(acc_scr.shape, F32)
            if online:
                m_scr[...] = jnp.full(m_scr.shape, NEG, F32)
            nw = lax.shift_right_logical(i, wide.bit_length() - 1)
            left = i - nw * wide

            def wide_body(jj, c):
                step(i, jj * wide, wide, None)
                return c

            lax.fori_loop(0, nw, wide_body, 0)
            for r in range(wide):
                pl.when(left == r)(lambda r=r: step(i, nw * wide, r + 1, r))
            acc = acc_scr[...]
            o_ref[pl.ds(pl.multiple_of(i * t, t), t), :] = (acc / acc[:, MLA_V_DIM:MLA_V_DIM + 1]).astype(BF16)
            return carry

        lax.fori_loop(0, nq, q_tile, 0)

    small = plain_ref[pl.program_id(0) * MLA_HEADS + pl.program_id(1)] != 0
    pl.when(small)(lambda: run(plain_step, MLA_WIDE, False))
    pl.when(jnp.logical_not(small))(lambda: run(online_step, 1, True))


def _mla(plain, qm, km, vm, batch, seq):
    n = qm.shape[0]
    t = TQ_MLA
    kmap = lambda b, h, *_: (b, h)
    return pl.pallas_call(
        _mla_kernel,
        grid_spec=pltpu.PrefetchScalarGridSpec(
            num_scalar_prefetch=1,
            grid=(batch, MLA_HEADS),
            in_specs=[pl.BlockSpec((seq, LANE), kmap),
                      pl.BlockSpec((seq, LANE), kmap),
                      pl.BlockSpec((seq, LANE), kmap)],
            out_specs=pl.BlockSpec((seq, LANE), kmap),
            scratch_shapes=[pltpu.VMEM((t, 1), F32), pltpu.VMEM((t, LANE), F32)]),
        out_shape=jax.ShapeDtypeStruct((n, MLA_HEADS * LANE), BF16),
        compiler_params=_params(("arbitrary", "arbitrary")),
        name="mla",
    )(plain, qm, km, vm)


def _swa_kernel(sink_ref, slope_ref, q_ref, kc_ref, kp_ref, vc_ref, vp_ref, o_ref, bias_scr):
    b = pl.program_id(0)
    i = pl.program_id(1)
    blk = ATTN_BLOCK

    @pl.when((b == 0) & (i == 0))
    def _():
        qi = lax.broadcasted_iota(jnp.int32, (blk, 2 * blk), 0)
        kj = lax.broadcasted_iota(jnp.int32, (blk, 2 * blk), 1)
        dist = qi - kj + blk
        valid = (dist >= 0) & (dist < blk)
        distf = dist.astype(F32)
        for hd in range(SWA_HEADS):
            bias_scr[hd] = jnp.where(valid, -(slope_ref[hd] * LOG2E) * distf, NEG)

    lane = lax.broadcasted_iota(jnp.int32, (2 * blk, LANE), 1)
    lo = lane < SWA_HEAD_DIM
    kcol = lax.broadcasted_iota(jnp.int32, (blk, 2 * blk), 1)
    first_pen = jnp.where(kcol < blk, jnp.where(i == 0, NEG, 0.0), 0.0)
    zero = jnp.zeros((2 * blk, LANE), BF16)

    for qb in range(SWA_QB):
        if qb == 0:
            kprev, vprev = kp_ref[...], vp_ref[...]
        else:
            kprev = kc_ref[(qb - 1) * blk:qb * blk, :]
            vprev = vc_ref[(qb - 1) * blk:qb * blk, :]
        kcat = jnp.concatenate([kprev, kc_ref[qb * blk:(qb + 1) * blk, :]], axis=0)
        vcat = jnp.concatenate([vprev, vc_ref[qb * blk:(qb + 1) * blk, :]], axis=0)
        for g in range(SWA_KV_HEADS):
            kg = kcat[:, g * LANE:(g + 1) * LANE]
            vg = vcat[:, g * LANE:(g + 1) * LANE]
            halves = ((jnp.where(lo, kg, zero), jnp.where(lo, vg, zero)),
                      (jnp.where(lo, zero, kg), jnp.where(lo, zero, vg)))
            for pp in range(2):
                pr = g * 2 + pp
                qpair = q_ref[qb * blk:(qb + 1) * blk, pr * LANE:(pr + 1) * LANE]
                out = None
                for half in range(2):
                    hd = pr * 2 + half
                    kx, vx = halves[half]
                    s = lax.dot_general(qpair, kx, (((1,), (1,)), ((), ())), preferred_element_type=F32)
                    s = s + bias_scr[hd]
                    if qb == 0:
                        s = s + first_pen
                    sink = sink_ref[hd] * LOG2E
                    m = jnp.maximum(jnp.max(s, axis=1, keepdims=True), sink)
                    p = jnp.exp2(s - m)
                    denom = jnp.sum(p, axis=1, keepdims=True) + jnp.exp2(sink - m)
                    o = jnp.dot(p.astype(BF16), vx, preferred_element_type=F32) * (1.0 / denom)
                    out = o if out is None else out + o
                o_ref[qb * blk:(qb + 1) * blk, pr * LANE:(pr + 1) * LANE] = out.astype(BF16)


def _swa(sinks, slopes, qa, ka2, va2, batch, seq):
    n = qa.shape[0]
    blk = ATTN_BLOCK
    rows = SWA_QB * blk
    steps = seq // rows
    cur = lambda b, i: (b * steps + i, 0)
    prev = lambda b, i: (b * (seq // blk) + jnp.maximum(i * SWA_QB - 1, 0), 0)
    smem = pl.BlockSpec(memory_space=pltpu.SMEM)
    return pl.pallas_call(
        _swa_kernel,
        grid=(batch, steps),
        in_specs=[smem, smem,
                  pl.BlockSpec((rows, 512), cur),
                  pl.BlockSpec((rows, 256), cur),
                  pl.BlockSpec((blk, 256), prev),
                  pl.BlockSpec((rows, 256), cur),
                  pl.BlockSpec((blk, 256), prev)],
        out_specs=pl.BlockSpec((rows, 512), cur),
        out_shape=jax.ShapeDtypeStruct((n, 512), BF16),
        scratch_shapes=[pltpu.VMEM((SWA_HEADS, blk, 2 * blk), F32)],
        compiler_params=_params(("arbitrary", "arbitrary")),
        name="swa",
    )(sinks, slopes, qa, ka2, ka2, va2, va2)


def _outproj_kernel(ya_ref, yb_ref, ga_ref, gb_ref, x_ref, g1_ref, sh_ref, sc_ref, nm_ref, wa_ref, wb_ref,
                    wo_ref, wrh_ref, wrl_ref, br_ref,
                    x1_ref, h2_ref, route_ref, routet_ref, g8_ref):
    tm = x_ref.shape[0]
    a = jnp.dot(ya_ref[...], wa_ref[...], preferred_element_type=F32)
    bm = jnp.dot(yb_ref[...], wb_ref[...], preferred_element_type=F32)
    mixed = (ga_ref[...].astype(F32) * a + gb_ref[...].astype(F32) * bm).astype(BF16)
    x1 = x_ref[...] + g1_ref[0] * jnp.dot(mixed, wo_ref[...], preferred_element_type=F32)
    x1_ref[...] = x1
    h2 = _rms(x1) * nm_ref[...] * (1.0 + sc_ref[0]) + sh_ref[0]
    hi = h2.astype(BF16)
    h2_ref[:, 0:D_MODEL] = hi

    lo = (h2 - hi.astype(F32)).astype(BF16)
    logits = (jnp.dot(hi, wrh_ref[...], preferred_element_type=F32)
              + jnp.dot(lo, wrh_ref[...], preferred_element_type=F32)
              + jnp.dot(hi, wrl_ref[...], preferred_element_type=F32)) + br_ref[...]

    lane = lax.broadcasted_iota(jnp.int32, (tm, LANE), 1).astype(F32)
    work = jnp.where(lane < N_EXPERTS, logits, -jnp.inf)
    vals, idxs = [], []
    for _ in range(TOP_K):
        m = jnp.max(work, axis=1, keepdims=True)
        idx = jnp.min(jnp.where(work == m, lane, float(LANE)), axis=1, keepdims=True)
        vals.append(m)
        idxs.append(idx)
        work = jnp.where(lane == idx, -jnp.inf, work)
    exps = [jnp.exp(v - vals[0]) for v in vals]
    tot = exps[0] + exps[1] + exps[2] + exps[3]

    onehot = jnp.zeros((tm, LANE), F32)
    for idx in idxs:
        onehot = onehot + (lane == idx).astype(F32)
    r = lax.broadcasted_iota(jnp.int32, (tm, tm), 0)
    c = lax.broadcasted_iota(jnp.int32, (tm, tm), 1)
    prefix = jnp.dot((c < r).astype(BF16), onehot.astype(BF16), preferred_element_type=F32)
    groups = jnp.ceil(jnp.sum(onehot, axis=0, keepdims=True) * (1.0 / GROUP_ALIGN))
    er = lax.broadcasted_iota(jnp.int32, (LANE, LANE), 0)
    ec = lax.broadcasted_iota(jnp.int32, (LANE, LANE), 1)
    before = jnp.dot(jnp.broadcast_to(groups, (8, LANE)).astype(BF16), (er < ec).astype(BF16),
                     preferred_element_type=F32)[0:1, :]
    slot_e = before * GROUP_ALIGN + prefix

    route = jnp.zeros((tm, LANE), F32)
    wext = jnp.zeros((tm, LANE), F32)
    for k in range(TOP_K):
        sk = jnp.sum(jnp.where(lane == idxs[k], slot_e, 0.0), axis=1, keepdims=True)
        sub = jnp.floor(sk * (1.0 / SLOT_SUB))
        route = jnp.where(lane == float(k), sk - sub * SLOT_SUB, route)
        route = jnp.where(lane == float(TOP_K + k), sub, route)
        wk = exps[k] / tot
        wk_hi = wk.astype(BF16).astype(F32)
        wext = jnp.where(lane == idxs[k], wk_hi, wext)
        wext = jnp.where(lane == idxs[k] + float(N_EXPERTS), wk - wk_hi, wext)
    h2_ref[:, D_MODEL:XS_COLS] = wext.astype(BF16)
    route_ref[...] = route
    routet_ref[...] = route.T[0:8, :]
    g8_ref[...] = jnp.broadcast_to(groups * GROUP_ALIGN, (8, LANE))


def _outproj(ya, yb, ga, gb, x2, g1, sh2, sc2, nm, wa, wb_p, wo, wr_hi, wr_lo, br_p, seq):
    n = x2.shape[0]
    tm = ROUTE_TILE
    per_b = seq // tm
    hw = MLA_HEADS * LANE
    row = lambda i: (i, 0)
    fix = lambda i: (0, 0)
    bsel = lambda i: (i // per_b, 0, 0)
    return pl.pallas_call(
        _outproj_kernel,
        grid=(n // tm,),
        in_specs=[pl.BlockSpec((tm, 512), row),
                  pl.BlockSpec((tm, hw), row),
                  pl.BlockSpec((tm, D_MODEL), row),
                  pl.BlockSpec((tm, D_MODEL), row),
                  pl.BlockSpec((tm, D_MODEL), row),
                  pl.BlockSpec((1, 1, D_MODEL), bsel),
                  pl.BlockSpec((1, 1, D_MODEL), bsel),
                  pl.BlockSpec((1, 1, D_MODEL), bsel),
                  pl.BlockSpec((1, D_MODEL), fix),
                  pl.BlockSpec((512, D_MODEL), fix),
                  pl.BlockSpec((hw, D_MODEL), fix),
                  pl.BlockSpec((D_MODEL, D_MODEL), fix),
                  pl.BlockSpec((D_MODEL, LANE), fix),
                  pl.BlockSpec((D_MODEL, LANE), fix),
                  pl.BlockSpec((1, LANE), fix)],
        out_specs=[pl.BlockSpec((tm, D_MODEL), row),
                   pl.BlockSpec((tm, XS_COLS), row),
                   pl.BlockSpec((tm, LANE), row),
                   pl.BlockSpec((8, tm), lambda i: (0, i)),
                   pl.BlockSpec((8, LANE), row)],
        out_shape=[jax.ShapeDtypeStruct((n, D_MODEL), F32),
                   jax.ShapeDtypeStruct((n, XS_COLS), BF16),
                   jax.ShapeDtypeStruct((n, LANE), F32),
                   jax.ShapeDtypeStruct((8, n), F32),
                   jax.ShapeDtypeStruct((n // tm * 8, LANE), F32)],
        compiler_params=_params(("arbitrary",)),
        name="outproj",
    )(ya, yb, ga, gb, x2, g1, sh2, sc2, nm, wa, wb_p, wo, wr_hi, wr_lo, br_p)


def _start_runs(tile, g8_ref, loff_ref, goff_ref, make_copy):
    for e in range(N_EXPERTS):
        g = g8_ref[tile * N_EXPERTS + e]
        lo = loff_ref[tile * N_EXPERTS + e]
        go = goff_ref[tile * N_EXPERTS + e]
        def go_copy(lo=lo, go=go, g=g):
            make_copy(pl.multiple_of(lo, GROUP_ALIGN), pl.multiple_of(go, GROUP_ALIGN),
                      pl.multiple_of(g, GROUP_ALIGN)).start()

        pl.when(g > 0)(go_copy)


def _dispatch_kernel(g8_ref, loff_ref, goff_ref, tot_ref, ends_ref, h_ref, rt_ref, xs_ref, buf, zbuf, sem, zsem):
    t = pl.program_id(0)
    nt = pl.num_programs(0)
    tm = h_ref.shape[0]
    slot = lax.rem(t, 2)

    def fill(start, rows):
        return pltpu.make_async_copy(zbuf.at[pl.ds(0, rows), :], xs_ref.at[pl.ds(start, rows), :], zsem)

    def pads(e):
        real_end = pl.multiple_of(ends_ref[N_EXPERTS + e], GROUP_ALIGN)
        return real_end, pl.multiple_of(ends_ref[e] - real_end, GROUP_ALIGN)

    def zero_fills(act):
        used = lax.shift_right_logical(ends_ref[N_EXPERTS - 1], MOE_ROWS.bit_length() - 1)
        n_blocks = xs_ref.shape[0] // MOE_ROWS
        for e in range(N_EXPERTS):
            pl.when(pads(e)[1] > 0)(lambda e=e: act(fill(*pads(e))))
        lax.fori_loop(used, n_blocks,
                      lambda b, c: (act(fill(pl.multiple_of(b * MOE_ROWS, MOE_ROWS), MOE_ROWS)), c)[1], 0)

    @pl.when(t == 0)
    def _():
        zbuf[...] = jnp.zeros(zbuf.shape, zbuf.dtype)
        zero_fills(lambda c: c.start())

    hb = h_ref[...]
    low = [rt_ref[k:k + 1, :] for k in range(TOP_K)]
    sub = [rt_ref[TOP_K + k:TOP_K + k + 1, :] for k in range(TOP_K)]
    srow = lax.broadcasted_iota(jnp.int32, (SLOT_SUB, tm), 0).astype(F32).astype(BF16)
    one = jnp.ones((SLOT_SUB, tm), BF16)

    for ci in range(SLOTS_PER_TILE // SLOT_CHUNK):
        parts = []
        for b in range(SLOT_CHUNK // SLOT_SUB):
            blk = float(ci * (SLOT_CHUNK // SLOT_SUB) + b)
            own = jnp.zeros((SLOT_SUB, tm), BF16)
            for k in range(TOP_K):
                here = jnp.where(sub[k] == blk, low[k], -1.0).astype(BF16)
                own = jnp.where(srow == here, one, own)
            parts.append(own)
        own = jnp.concatenate(parts, axis=0)
        buf[slot, ci * SLOT_CHUNK:(ci + 1) * SLOT_CHUNK, :] = jnp.dot(own, hb, preferred_element_type=F32)

    def copy(s, lo, go, rows):
        return pltpu.make_async_copy(buf.at[s, pl.ds(lo, rows), :], xs_ref.at[pl.ds(go, rows), :], sem.at[s])

    _start_runs(t, g8_ref, loff_ref, goff_ref, functools.partial(copy, slot))

    def drain(tile, s):
        rows = pl.multiple_of(tot_ref[tile], GROUP_ALIGN)
        copy(s, 0, 0, rows).wait()

    pl.when(t > 0)(lambda: drain(t - 1, 1 - slot))

    @pl.when(t == nt - 1)
    def _():
        drain(t, slot)
        zero_fills(lambda c: c.wait())


def _dispatch(g8f, lofff, gofff, totf, ends, h2, routet, p_rows):
    n = h2.shape[0]
    tm = ROUTE_TILE
    return pl.pallas_call(
        _dispatch_kernel,
        grid_spec=pltpu.PrefetchScalarGridSpec(
            num_scalar_prefetch=5,
            grid=(n // tm,),
            in_specs=[pl.BlockSpec((tm, XS_COLS), lambda i, *_: (i, 0)),
                      pl.BlockSpec((8, tm), lambda i, *_: (0, i))],
            out_specs=pl.BlockSpec(memory_space=pl.ANY),
            scratch_shapes=[pltpu.VMEM((2, SLOTS_PER_TILE, XS_COLS), F32),
                            pltpu.VMEM((MOE_ROWS, XS_COLS), F32),
                            pltpu.SemaphoreType.DMA((2,)), pltpu.SemaphoreType.DMA(())]),
        out_shape=jax.ShapeDtypeStruct((p_rows, XS_COLS), F32),
        compiler_params=_params(("arbitrary",)),
        name="dispatch",
    )(g8f, lofff, gofff, totf, ends, h2, routet)


def _experts_kernel(be_ref, nu_ref, bv_ref, nx_ref, xs_ref, w1_hbm, b1_ref, w2_hbm, b2_ref, ys_ref,
                    w1buf, w2buf, wsem, ord_ref):
    del nu_ref
    i = pl.program_id(0)
    half = MOE_ROWS // 2
    used = bv_ref[i]
    e = be_ref[i]

    def fetch(ex, s):
        return (pltpu.make_async_copy(w1_hbm.at[ex], w1buf.at[s], wsem.at[0, s]),
                pltpu.make_async_copy(w2_hbm.at[ex], w2buf.at[s], wsem.at[1, s]))

    @pl.when(i == 0)
    def _():
        ord_ref[0] = 0
        for c in fetch(e, 0):
            c.start()

    @pl.when((used > 0) & ((i == 0) | (e != be_ref[jnp.maximum(i - 1, 0)])))
    def _():
        @pl.when(i > 0)
        def _():
            ord_ref[0] = ord_ref[0] + 1

        s = lax.rem(ord_ref[0], 2)
        for c in fetch(e, s):
            c.wait()
        nxt = nx_ref[i]

        @pl.when(nxt >= 0)
        def _():
            for c in fetch(nxt, 1 - s):
                c.start()

    slot = lax.rem(ord_ref[0], 2)

    def mlp(n_rows):
        xb = xs_ref[0:n_rows, 0:D_MODEL].astype(BF16)
        wext = xs_ref[0:n_rows, D_MODEL:XS_COLS]
        lane = lax.broadcasted_iota(jnp.int32, wext.shape, 1)
        wcol = jnp.sum(jnp.where((lane == e) | (lane == e + N_EXPERTS), wext, 0.0), axis=1, keepdims=True)
        hcat = jnp.dot(xb, w1buf[slot].astype(BF16), preferred_element_type=F32) + b1_ref[0]
        x_glu = jnp.minimum(hcat[:, :D_EXPERT], SWIGLU_LIMIT)
        x_lin = jnp.clip(hcat[:, D_EXPERT:], -SWIGLU_LIMIT, SWIGLU_LIMIT)
        act = (x_glu * jax.nn.sigmoid(SWIGLU_ALPHA * x_glu) * (x_lin + 1.0)).astype(BF16)
        ys_ref[0:n_rows, :] = (jnp.dot(act, w2buf[slot].astype(BF16), preferred_element_type=F32) + b2_ref[0]) * wcol

    pl.when(used > half)(lambda: mlp(MOE_ROWS))

    @pl.when((used > 0) & (used <= half))
    def _():
        mlp(half)
        ys_ref[half:, :] = jnp.zeros((MOE_ROWS - half, D_MODEL), F32)

    @pl.when(used == 0)
    def _():
        ys_ref[...] = jnp.zeros(ys_ref.shape, F32)


def _experts(block_e, n_used, block_used, next_e, xs, w1, b1, w2, b2):
    p = xs.shape[0]
    mb = MOE_ROWS
    last = lambda i, nu: jnp.minimum(i, nu[0] - 1)
    rowmap = lambda i, be, nu, bv, nx: (last(i, nu), 0)
    wmap = lambda i, be, nu, bv, nx: (be[last(i, nu)], 0, 0)
    hbm = pl.BlockSpec(memory_space=pl.ANY)
    return pl.pallas_call(
        _experts_kernel,
        grid_spec=pltpu.PrefetchScalarGridSpec(
            num_scalar_prefetch=4,
            grid=(p // mb,),
            in_specs=[pl.BlockSpec((mb, XS_COLS), rowmap),
                      hbm,
                      pl.BlockSpec((1, 1, 2 * D_EXPERT), wmap),
                      hbm,
                      pl.BlockSpec((1, 1, D_MODEL), wmap)],
            out_specs=pl.BlockSpec((mb, D_MODEL), lambda i, be, nu, bv, nx: (i, 0)),
            scratch_shapes=[pltpu.VMEM((2, D_MODEL, 2 * D_EXPERT), F32),
                            pltpu.VMEM((2, D_EXPERT, D_MODEL), F32),
                            pltpu.SemaphoreType.DMA((2, 2)),
                            pltpu.SMEM((1,), jnp.int32)]),
        out_shape=jax.ShapeDtypeStruct((p, D_MODEL), F32),
        compiler_params=_params(("arbitrary",)),
        name="experts",
    )(block_e, n_used, block_used, next_e, xs, w1, b1, w2, b2)


def _combine_kernel(g8_ref, loff_ref, goff_ref, tot_ref, ys_ref, route_ref, x1_ref, g2_ref, fn_ref, o_ref, stg, sem):
    t = pl.program_id(0)
    nt = pl.num_programs(0)
    tm = x1_ref.shape[0]
    slot = lax.rem(t, 2)

    def copy(s, lo, go, rows):
        return pltpu.make_async_copy(ys_ref.at[pl.ds(go, rows), :], stg.at[s, pl.ds(lo, rows), :], sem.at[s])

    @pl.when(t == 0)
    def _():
        stg[...] = jnp.zeros(stg.shape, F32)
        _start_runs(t, g8_ref, loff_ref, goff_ref, functools.partial(copy, 0))

    pl.when(t + 1 < nt)(lambda: _start_runs(t + 1, g8_ref, loff_ref, goff_ref, functools.partial(copy, 1 - slot)))
    copy(slot, 0, 0, pl.multiple_of(tot_ref[t], GROUP_ALIGN)).wait()

    route = route_ref[...]
    low = [jnp.broadcast_to(route[:, k:k + 1], (tm, SLOT_SUB)).astype(BF16) for k in range(TOP_K)]
    sub = [jnp.broadcast_to(route[:, TOP_K + k:TOP_K + k + 1], (tm, SLOT_SUB)).astype(BF16) for k in range(TOP_K)]
    scol = lax.broadcasted_iota(jnp.int32, (tm, SLOT_SUB), 1).astype(F32).astype(BF16)
    one = jnp.ones((tm, SLOT_SUB), BF16)
    never = jnp.full((tm, SLOT_SUB), -1.0, BF16)

    y = None
    for ci in range(SLOTS_PER_TILE // SLOT_CHUNK):
        parts = []
        for b in range(SLOT_CHUNK // SLOT_SUB):
            blk = float(ci * (SLOT_CHUNK // SLOT_SUB) + b)
            own = jnp.zeros((tm, SLOT_SUB), BF16)
            for k in range(TOP_K):
                own = jnp.where(scol == jnp.where(sub[k] == blk, low[k], never), one, own)
            parts.append(own)
        own = jnp.concatenate(parts, axis=1)
        rows = stg[slot, ci * SLOT_CHUNK:(ci + 1) * SLOT_CHUNK, :].astype(BF16)
        part = jnp.dot(own, rows, preferred_element_type=F32)
        y = part if y is None else y + part
    x = x1_ref[...] + g2_ref[0] * y
    o_ref[...] = _rms(x) * fn_ref[...]


def _combine(g8f, lofff, gofff, totf, ys, route, x1, g2, fn, seq):
    n = x1.shape[0]
    tm = ROUTE_TILE
    per_b = seq // tm
    return pl.pallas_call(
        _combine_kernel,
        grid_spec=pltpu.PrefetchScalarGridSpec(
            num_scalar_prefetch=4,
            grid=(n // tm,),
            in_specs=[pl.BlockSpec(memory_space=pl.ANY),
                      pl.BlockSpec((tm, LANE), lambda i, *_: (i, 0)),
                      pl.BlockSpec((tm, D_MODEL), lambda i, *_: (i, 0)),
                      pl.BlockSpec((1, 1, D_MODEL), lambda i, *_: (i // per_b, 0, 0)),
                      pl.BlockSpec((1, D_MODEL), lambda i, *_: (0, 0))],
            out_specs=pl.BlockSpec((tm, D_MODEL), lambda i, *_: (i, 0)),
            scratch_shapes=[pltpu.VMEM((2, SLOTS_PER_TILE, D_MODEL), F32), pltpu.SemaphoreType.DMA((2,))]),
        out_shape=jax.ShapeDtypeStruct((n, D_MODEL), F32),
        compiler_params=_params(("arbitrary",)),
        name="combine",
    )(g8f, lofff, gofff, totf, ys, route, x1, g2, fn)


def _split_w_in(w_in):
    o_kr = _W_MIX_COLS
    w_kr = jnp.pad(w_in[:, o_kr:o_kr + MLA_ROPE_DIM],
                   ((0, 0), (MLA_NOPE_DIM, LANE - MLA_NOPE_DIM - MLA_ROPE_DIM)))
    return w_in[:, :o_kr].astype(BF16), w_kr.astype(BF16), w_in[:, o_kr + MLA_ROPE_DIM:].astype(BF16)


def _pack_heads(w, width, pieces):
    rows = w.shape[0]
    w3 = w.reshape(rows, MLA_HEADS, width)
    cols = [w3[:, :, a:b] for a, b in pieces]
    used = sum(b - a for a, b in pieces)
    cols.append(jnp.zeros((rows, MLA_HEADS, LANE - used), w.dtype))
    return jnp.concatenate(cols, axis=2).reshape(rows, MLA_HEADS * LANE)


def kernel(x, c, positions, w_ada, b_ada, norm_mix, norm_ffn, w_in, sinks, q_norm, kv_norm, w_uq, w_uk, w_uv,
           w_branch_a, w_branch_b, w_out, w_router, b_router, w_moe1, b_moe1, w_moe2, b_moe2, final_norm):
    batch, seq, d = x.shape
    n = batch * seq
    assert d == D_MODEL and w_ada.shape[0] == 1
    assert seq % TQ_MLA == 0 and seq % (SWA_QB * ATTN_BLOCK) == 0 and seq % TM_PROJ == 0 and seq % ROUTE_TILE == 0
    hw = MLA_HEADS * LANE
    qk = MLA_NOPE_DIM + MLA_ROPE_DIM

    w_mix, w_kr, w_gates = _split_w_in(w_in[0])
    wuq_p = _pack_heads(w_uq[0], qk, [(0, qk)]).astype(BF16)
    wuk_p = _pack_heads(w_uk[0], MLA_NOPE_DIM, [(0, MLA_NOPE_DIM)]).astype(BF16)
    wuv_p = _pack_heads(w_uv[0], MLA_V_DIM, [(0, MLA_V_DIM)]).astype(BF16)
    one_row = jnp.tile((jnp.arange(LANE) == MLA_V_DIM).astype(F32), MLA_HEADS)[None, :]
    freqs = ROPE_THETA ** (-jnp.arange(0, MLA_ROPE_DIM, 2, dtype=F32) / MLA_ROPE_DIM)
    frq = jnp.concatenate([jnp.zeros((MLA_NOPE_DIM,), F32), freqs, freqs,
                           jnp.zeros((LANE - qk,), F32)])[None, :]
    wb3 = w_branch_b[0].reshape(MLA_HEADS, MLA_V_DIM, D_MODEL)
    wb_p = jnp.concatenate([wb3, jnp.zeros((MLA_HEADS, LANE - MLA_V_DIM, D_MODEL), F32)],
                           axis=1).reshape(hw, D_MODEL).astype(BF16)
    wa = w_branch_a[0].astype(BF16)
    wo = w_out[0].astype(BF16)
    wr = jnp.pad(w_router[0], ((0, 0), (0, LANE - N_EXPERTS)))
    wr_hi = wr.astype(BF16)
    wr_lo = (wr - wr_hi.astype(F32)).astype(BF16)
    br_p = jnp.pad(b_router[0], (0, LANE - N_EXPERTS))[None, :]
    b1 = b_moe1[0][:, None, :]
    b2 = b_moe2[0][:, None, :]
    slopes = jnp.asarray(np.exp2(-8.0 * np.arange(1, SWA_HEADS + 1) / SWA_HEADS), dtype=F32)

    c8 = jnp.pad(c, ((0, 8 - batch), (0, 0)))
    mod = _ada(c8, w_ada[0], b_ada[0][None, :])[:batch]
    sh1, sc1, g1, sh2, sc2, g2 = [m[:, None, :] for m in jnp.split(mod, 6, axis=-1)]

    x2 = x.reshape(n, D_MODEL)
    pos2 = positions.reshape(n, 1).astype(jnp.int32)
    qa, ka2, va2, qm, km, vm, ga, gb, stats = _inproj(
        x2, pos2, sh1, sc1, norm_mix, w_mix, w_kr, w_gates, q_norm, kv_norm, wuq_p, wuk_p, wuv_p, frq, one_row, seq)
    st = stats.reshape(batch, seq // TM_PROJ, 8, LANE)
    bound_sq = jnp.max(st[:, :, :, 0], axis=1) * jnp.max(st[:, :, :, 1], axis=1)
    plain = (bound_sq <= MLA_PLAIN_BOUND ** 2).astype(jnp.int32).reshape(-1)
    yb = _mla(plain, qm, km, vm, batch, seq)
    ya = _swa(sinks[0], slopes, qa, ka2, va2, batch, seq)

    x1, h2, route, routet, g8 = _outproj(
        ya, yb, ga, gb, x2, g1, sh2, sc2, norm_ffn, wa, wb_p, wo, wr_hi, wr_lo, br_p, seq)

    n_tiles = n // ROUTE_TILE
    g8t = g8.reshape(n_tiles, 8, LANE)[:, 0, :N_EXPERTS].astype(jnp.int32)
    loff = jnp.cumsum(g8t, axis=1) - g8t
    padded = ((jnp.sum(g8t, axis=0) + MOE_ROWS - 1) // MOE_ROWS) * MOE_ROWS
    pends = jnp.cumsum(padded)
    goff = (pends - padded)[None, :] + jnp.cumsum(g8t, axis=0) - g8t
    tot = jnp.sum(g8t, axis=1).astype(jnp.int32)
    p_rows = n * TOP_K + n_tiles * N_EXPERTS * (GROUP_ALIGN - 1) + N_EXPERTS * (MOE_ROWS - 1)
    p_rows = -(-p_rows // MOE_ROWS) * MOE_ROWS
    n_blocks = p_rows // MOE_ROWS
    block_start = jnp.arange(n_blocks, dtype=jnp.int32) * MOE_ROWS
    block_e = jnp.minimum(jnp.sum((pends[None, :] <= block_start[:, None]).astype(jnp.int32), axis=1),
                          N_EXPERTS - 1).astype(jnp.int32)
    n_used = (pends[-1:] // MOE_ROWS).astype(jnp.int32)
    eids = jnp.arange(N_EXPERTS, dtype=jnp.int32)
    of_block = block_e[:, None] == eids[None, :]
    pick = lambda table: jnp.sum(jnp.where(of_block, table[None, :], 0), axis=1).astype(jnp.int32)
    real_end = pends - padded + jnp.sum(g8t, axis=0)
    block_used = jnp.clip(pick(real_end) - block_start, 0, MOE_ROWS).astype(jnp.int32)
    later = (eids[None, :] > eids[:, None]) & (padded[None, :] > 0)
    after = jnp.min(jnp.where(later, eids[None, :], N_EXPERTS), axis=1)
    next_e = pick(jnp.where(after == N_EXPERTS, -1, after))
    tabs = (g8t.reshape(-1), loff.reshape(-1).astype(jnp.int32), goff.reshape(-1).astype(jnp.int32), tot)

    xs = _dispatch(*tabs, jnp.concatenate([pends, real_end]).astype(jnp.int32), h2, routet, p_rows)
    ys = _experts(block_e, n_used, block_used, next_e, xs, w_moe1[0], b1, w_moe2[0], b2)
    out = _combine(*tabs, ys, route, x1, g2, final_norm[None, :], seq)
    return out.reshape(batch, seq, D_MODEL)
```

```python
import functools

import numpy as np
import jax
import jax.numpy as jnp
from jax import lax
from jax.experimental import pallas as pl
from jax.experimental.pallas import tpu as pltpu

D_MODEL = 1024
SWA_HEADS = 8
SWA_KV_HEADS = 2
SWA_HEAD_DIM = 64
ATTN_BLOCK = 128
MLA_HEADS = 8
MLA_Q_RANK = 384
MLA_KV_RANK = 256
MLA_NOPE_DIM = 64
MLA_ROPE_DIM = 32
MLA_V_DIM = 64
ROPE_THETA = 10000.0
N_EXPERTS = 32
TOP_K = 4
D_EXPERT = 1024
SWIGLU_LIMIT = 7.0
SWIGLU_ALPHA = 1.702
NORM_EPS = 1e-6

LANE = 128
LOG2E = 1.4426950408889634
NEG = -1e30
SWA_QSCALE = SWA_HEAD_DIM ** -0.5 * LOG2E
MLA_QSCALE = (MLA_NOPE_DIM + MLA_ROPE_DIM) ** -0.5 * LOG2E
HALF_ROPE = MLA_ROPE_DIM // 2
MLA_PLAIN_BOUND = 60.0
MLA_WIDE = 16

BF16 = jnp.bfloat16
F32 = jnp.float32

TM_PROJ = 512
TQ_MLA = 512
SWA_QB = 4
MOE_ROWS = 512
ROUTE_TILE = 512
GROUP_ALIGN = 8
SLOTS_PER_TILE = ROUTE_TILE * TOP_K + N_EXPERTS * GROUP_ALIGN
SLOT_SUB = 256
SLOT_CHUNK = 3 * SLOT_SUB
XS_COLS = D_MODEL + LANE
VMEM_LIMIT = 56 * 1024 * 1024


def _params(sem, vmem=VMEM_LIMIT):
    return pltpu.CompilerParams(dimension_semantics=sem, vmem_limit_bytes=vmem)


def _rms(x):
    return x * lax.rsqrt(jnp.mean(x * x, axis=-1, keepdims=True) + NORM_EPS)


def _ada_kernel(c_ref, w_ref, b_ref, o_ref):
    c = c_ref[...]
    a = (c * jax.nn.sigmoid(c)).astype(BF16)
    o_ref[...] = jnp.dot(a, w_ref[...].astype(BF16), preferred_element_type=F32) + b_ref[...]


def _ada(c8, w_ada, b_ada):
    n_out = w_ada.shape[1]
    return pl.pallas_call(
        _ada_kernel,
        grid=(n_out // D_MODEL,),
        in_specs=[pl.BlockSpec((8, D_MODEL), lambda j: (0, 0)),
                  pl.BlockSpec((D_MODEL, D_MODEL), lambda j: (0, j)),
                  pl.BlockSpec((1, D_MODEL), lambda j: (0, j))],
        out_specs=pl.BlockSpec((8, D_MODEL), lambda j: (0, j)),
        out_shape=jax.ShapeDtypeStruct((8, n_out), F32),
        compiler_params=_params(("arbitrary",)),
        name="ada",
    )(c8, w_ada, b_ada)


_C_QA = (0, 512)
_C_KV = (512, 768)
_C_LAT = (768, 1408)
_W_MIX_COLS = 1408
_ROPE_LO = (MLA_NOPE_DIM, MLA_NOPE_DIM + HALF_ROPE)
_ROPE_HI = (MLA_NOPE_DIM + HALF_ROPE, MLA_NOPE_DIM + MLA_ROPE_DIM)


def _inproj_kernel(x_ref, pos_ref, sh_ref, sc_ref, nm_ref, win_ref, wkr_ref, wg_ref, qn_ref, kvn_ref, wuq_ref, wuk_ref,
                   wuv_ref, frq_ref, one_ref,
                   qa_ref, ka_ref, va_ref, qm_ref, km_ref, vm_ref, ga_ref, gb_ref, st_ref):
    tm = x_ref.shape[0]
    x = x_ref[...]
    h = (_rms(x) * nm_ref[...] * (1.0 + sc_ref[0]) + sh_ref[0]).astype(BF16)

    def proj(c):
        return jnp.dot(h, win_ref[:, c[0]:c[1]], preferred_element_type=F32)

    lane = lax.broadcasted_iota(jnp.int32, (tm, LANE), 1)
    first = lane < SWA_HEAD_DIM

    def twice(t):
        r = pltpu.roll(t, SWA_HEAD_DIM, axis=1)
        return jnp.concatenate([jnp.where(first, t, r), jnp.where(first, r, t)], axis=1)

    kv = proj(_C_KV)
    ka_ref[...] = twice(kv[:, :LANE]).astype(BF16)
    va_ref[...] = twice(kv[:, LANE:]).astype(BF16)

    ang = pos_ref[...].astype(F32) * frq_ref[...]
    cs = jnp.cos(ang)
    sn = jnp.sin(ang)
    sn_lo = jnp.where((lane >= _ROPE_LO[0]) & (lane < _ROPE_LO[1]), -sn, 0.0)
    sn_hi = jnp.where((lane >= _ROPE_HI[0]) & (lane < _ROPE_HI[1]), sn, 0.0)

    def rotary(t):
        return (t * cs + pltpu.roll(t, LANE - HALF_ROPE, axis=1) * sn_lo
                + pltpu.roll(t, HALF_ROPE, axis=1) * sn_hi)

    lat = proj(_C_LAT)
    cqn = (_rms(lat[:, 0:MLA_Q_RANK]) * qn_ref[...]).astype(BF16)
    ckvn = (_rms(lat[:, MLA_Q_RANK:MLA_Q_RANK + MLA_KV_RANK]) * kvn_ref[...]).astype(BF16)
    krr = rotary(jnp.dot(h, wkr_ref[...], preferred_element_type=F32))
    q = jnp.dot(cqn, wuq_ref[...], preferred_element_type=F32)
    kn = jnp.dot(ckvn, wuk_ref[...], preferred_element_type=F32)

    def max_sq_norm(t):
        tf = t.astype(F32)
        return jnp.max(jnp.sum(tf * tf, axis=1, keepdims=True), axis=0, keepdims=True)

    rid = lax.broadcasted_iota(jnp.int32, (8, LANE), 0)
    lid = lax.broadcasted_iota(jnp.int32, (8, LANE), 1)
    stats = jnp.zeros((8, LANE), F32)
    for hh in range(MLA_HEADS):
        sl = slice(hh * LANE, (hh + 1) * LANE)
        qb = (rotary(q[:, sl]) * MLA_QSCALE).astype(BF16)
        kb = (kn[:, sl] + krr).astype(BF16)
        qm_ref[:, sl] = qb
        km_ref[:, sl] = kb
        stats = jnp.where((rid == hh) & (lid == 0), max_sq_norm(qb), stats)
        stats = jnp.where((rid == hh) & (lid == 1), max_sq_norm(kb), stats)
    st_ref[...] = stats
    vm_ref[...] = (jnp.dot(ckvn, wuv_ref[...], preferred_element_type=F32) + one_ref[...]).astype(BF16)
    qa_ref[...] = (proj(_C_QA) * SWA_QSCALE).astype(BF16)
    ga_ref[...] = jax.nn.sigmoid(jnp.dot(h, wg_ref[:, 0:D_MODEL], preferred_element_type=F32)).astype(BF16)
    gb_ref[...] = jax.nn.sigmoid(jnp.dot(h, wg_ref[:, D_MODEL:], preferred_element_type=F32)).astype(BF16)


def _inproj(x2, pos2, sh1, sc1, nm, w_mix, w_kr, w_gates, qn, kvn, wuq_p, wuk_p, wuv_p, frq, one_row, seq):
    n = x2.shape[0]
    tm = TM_PROJ
    per_b = seq // tm
    hw = MLA_HEADS * LANE
    row = lambda i: (i, 0)
    fix = lambda i: (0, 0)
    bsel = lambda i: (i // per_b, 0, 0)
    widths = (512, 256, 256, hw, hw, hw, D_MODEL, D_MODEL)
    return pl.pallas_call(
        _inproj_kernel,
        grid=(n // tm,),
        in_specs=[pl.BlockSpec((tm, D_MODEL), row),
                  pl.BlockSpec((tm, 1), row),
                  pl.BlockSpec((1, 1, D_MODEL), bsel),
                  pl.BlockSpec((1, 1, D_MODEL), bsel),
                  pl.BlockSpec((1, D_MODEL), fix),
                  pl.BlockSpec((D_MODEL, _W_MIX_COLS), fix),
                  pl.BlockSpec((D_MODEL, LANE), fix),
                  pl.BlockSpec((D_MODEL, 2 * D_MODEL), fix),
                  pl.BlockSpec((1, MLA_Q_RANK), fix),
                  pl.BlockSpec((1, MLA_KV_RANK), fix),
                  pl.BlockSpec((MLA_Q_RANK, hw), fix),
                  pl.BlockSpec((MLA_KV_RANK, hw), fix),
                  pl.BlockSpec((MLA_KV_RANK, hw), fix),
                  pl.BlockSpec((1, LANE), fix),
                  pl.BlockSpec((1, hw), fix)],
        out_specs=[pl.BlockSpec((tm, w), row) for w in widths] + [pl.BlockSpec((8, LANE), row)],
        out_shape=[jax.ShapeDtypeStruct((n, w), BF16) for w in widths]
        + [jax.ShapeDtypeStruct((n // tm * 8, LANE), F32)],
        compiler_params=_params(("arbitrary",)),
        name="inproj",
    )(x2, pos2, sh1, sc1, nm, w_mix, w_kr, w_gates, qn, kvn, wuq_p, wuk_p, wuv_p, frq, one_row)


def _mla_kernel(plain_ref, q_ref, k_ref, v_ref, o_ref, m_scr, acc_scr):
    t = TQ_MLA
    nq = q_ref.shape[0] // t

    def scores(i, j, nblk, diag_at):
        q = q_ref[pl.ds(pl.multiple_of(i * t, t), t), :]
        start = pl.multiple_of(j * t, t)
        k = k_ref[pl.ds(start, nblk * t), :]
        v = v_ref[pl.ds(start, nblk * t), :]
        s = lax.dot_general(q, k, (((1,), (1,)), ((), ())), preferred_element_type=F32)
        if diag_at is not None:
            qi = lax.broadcasted_iota(jnp.int32, s.shape, 0)
            kj = lax.broadcasted_iota(jnp.int32, s.shape, 1)
            s = jnp.where(kj <= qi + diag_at * t, s, NEG)
        return s, v

    def plain_step(i, j, nblk, diag_at):
        s, v = scores(i, j, nblk, diag_at)
        acc_scr[...] += jnp.dot(jnp.exp2(s).astype(BF16), v, preferred_element_type=F32)

    def online_step(i, j, nblk, diag_at):
        s, v = scores(i, j, nblk, diag_at)
        m_old = m_scr[...]
        m_new = jnp.maximum(m_old, jnp.max(s, axis=1, keepdims=True))
        p = jnp.exp2(s - m_new)
        alpha = jnp.exp2(m_old - m_new)
        acc_scr[...] = acc_scr[...] * alpha + jnp.dot(p.astype(BF16), v, preferred_element_type=F32)
        m_scr[...] = m_new

    def run(step, wide, online):
        def q_tile(i, carry):
            acc_scr[...] = jnp.zeros(acc_scr.shape, F32)
            if online:
                m_scr[...] = jnp.full(m_scr.shape, NEG, F32)
            nw = lax.shift_right_logical(i, wide.bit_length() - 1)
            left = i - nw * wide

            def wide_body(jj, c):
                step(i, jj * wide, wide, None)
                return c

            lax.fori_loop(0, nw, wide_body, 0)
            for r in range(wide):
                pl.when(left == r)(lambda r=r: step(i, nw * wide, r + 1, r))
            acc = acc_scr[...]
            o_ref[pl.ds(pl.multiple_of(i * t, t), t), :] = (acc / acc[:, MLA_V_DIM:MLA_V_DIM + 1]).astype(BF16)
            return carry

        lax.fori_loop(0, nq, q_tile, 0)

    small = plain_ref[pl.program_id(0) * MLA_HEADS + pl.program_id(1)] != 0
    pl.when(small)(lambda: run(plain_step, MLA_WIDE, False))
    pl.when(jnp.logical_not(small))(lambda: run(online_step, 1, True))


def _mla(plain, qm, km, vm, batch, seq):
    n = qm.shape[0]
    t = TQ_MLA
    kmap = lambda b, h, *_: (b, h)
    return pl.pallas_call(
        _mla_kernel,
        grid_spec=pltpu.PrefetchScalarGridSpec(
            num_scalar_prefetch=1,
            grid=(batch, MLA_HEADS),
            in_specs=[pl.BlockSpec((seq, LANE), kmap),
                      pl.BlockSpec((seq, LANE), kmap),
                      pl.BlockSpec((seq, LANE), kmap)],
            out_specs=pl.BlockSpec((seq, LANE), kmap),
            scratch_shapes=[pltpu.VMEM((t, 1), F32), pltpu.VMEM((t, LANE), F32)]),
        out_shape=jax.ShapeDtypeStruct((n, MLA_HEADS * LANE), BF16),
        compiler_params=_params(("arbitrary", "arbitrary")),
        name="mla",
    )(plain, qm, km, vm)


def _swa_kernel(sink_ref, slope_ref, q_ref, kc_ref, kp_ref, vc_ref, vp_ref, o_ref, bias_scr):
    b = pl.program_id(0)
    i = pl.program_id(1)
    blk = ATTN_BLOCK

    @pl.when((b == 0) & (i == 0))
    def _():
        qi = lax.broadcasted_iota(jnp.int32, (blk, 2 * blk), 0)
        kj = lax.broadcasted_iota(jnp.int32, (blk, 2 * blk), 1)
        dist = qi - kj + blk
        valid = (dist >= 0) & (dist < blk)
        distf = dist.astype(F32)
        for hd in range(SWA_HEADS):
            bias_scr[hd] = jnp.where(valid, -(slope_ref[hd] * LOG2E) * distf, NEG)

    lane = lax.broadcasted_iota(jnp.int32, (2 * blk, LANE), 1)
    lo = lane < SWA_HEAD_DIM
    kcol = lax.broadcasted_iota(jnp.int32, (blk, 2 * blk), 1)
    first_pen = jnp.where(kcol < blk, jnp.where(i == 0, NEG, 0.0), 0.0)
    zero = jnp.zeros((2 * blk, LANE), BF16)

    for qb in range(SWA_QB):
        if qb == 0:
            kprev, vprev = kp_ref[...], vp_ref[...]
        else:
            kprev = kc_ref[(qb - 1) * blk:qb * blk, :]
            vprev = vc_ref[(qb - 1) * blk:qb * blk, :]
        kcat = jnp.concatenate([kprev, kc_ref[qb * blk:(qb + 1) * blk, :]], axis=0)
        vcat = jnp.concatenate([vprev, vc_ref[qb * blk:(qb + 1) * blk, :]], axis=0)
        for g in range(SWA_KV_HEADS):
            kg = kcat[:, g * LANE:(g + 1) * LANE]
            vg = vcat[:, g * LANE:(g + 1) * LANE]
            halves = ((jnp.where(lo, kg, zero), jnp.where(lo, vg, zero)),
                      (jnp.where(lo, zero, kg), jnp.where(lo, zero, vg)))
            for pp in range(2):
                pr = g * 2 + pp
                qpair = q_ref[qb * blk:(qb + 1) * blk, pr * LANE:(pr + 1) * LANE]
                out = None
                for half in range(2):
                    hd = pr * 2 + half
                    kx, vx = halves[half]
                    s = lax.dot_general(qpair, kx, (((1,), (1,)), ((), ())), preferred_element_type=F32)
                    s = s + bias_scr[hd]
                    if qb == 0:
                        s = s + first_pen
                    sink = sink_ref[hd] * LOG2E
                    m = jnp.maximum(jnp.max(s, axis=1, keepdims=True), sink)
                    p = jnp.exp2(s - m)
                    denom = jnp.sum(p, axis=1, keepdims=True) + jnp.exp2(sink - m)
                    o = jnp.dot(p.astype(BF16), vx, preferred_element_type=F32) * (1.0 / denom)
                    out = o if out is None else out + o
                o_ref[qb * blk:(qb + 1) * blk, pr * LANE:(pr + 1) * LANE] = out.astype(BF16)


def _swa(sinks, slopes, qa, ka2, va2, batch, seq):
    n = qa.shape[0]
    blk = ATTN_BLOCK
    rows = SWA_QB * blk
    steps = seq // rows
    cur = lambda b, i: (b * steps + i, 0)
    prev = lambda b, i: (b * (seq // blk) + jnp.maximum(i * SWA_QB - 1, 0), 0)
    smem = pl.BlockSpec(memory_space=pltpu.SMEM)
    return pl.pallas_call(
        _swa_kernel,
        grid=(batch, steps),
        in_specs=[smem, smem,
                  pl.BlockSpec((rows, 512), cur),
                  pl.BlockSpec((rows, 256), cur),
                  pl.BlockSpec((blk, 256), prev),
                  pl.BlockSpec((rows, 256), cur),
                  pl.BlockSpec((blk, 256), prev)],
        out_specs=pl.BlockSpec((rows, 512), cur),
        out_shape=jax.ShapeDtypeStruct((n, 512), BF16),
        scratch_shapes=[pltpu.VMEM((SWA_HEADS, blk, 2 * blk), F32)],
        compiler_params=_params(("arbitrary", "arbitrary")),
        name="swa",
    )(sinks, slopes, qa, ka2, ka2, va2, va2)


def _outproj_kernel(ya_ref, yb_ref, ga_ref, gb_ref, x_ref, g1_ref, sh_ref, sc_ref, nm_ref, wa_ref, wb_ref,
                    wo_ref, wrh_ref, wrl_ref, br_ref,
                    x1_ref, h2_ref, route_ref, routet_ref, g8_ref):
    tm = x_ref.shape[0]
    a = jnp.dot(ya_ref[...], wa_ref[...], preferred_element_type=F32)
    bm = jnp.dot(yb_ref[...], wb_ref[...], preferred_element_type=F32)
    mixed = (ga_ref[...].astype(F32) * a + gb_ref[...].astype(F32) * bm).astype(BF16)
    x1 = x_ref[...] + g1_ref[0] * jnp.dot(mixed, wo_ref[...], preferred_element_type=F32)
    x1_ref[...] = x1
    h2 = _rms(x1) * nm_ref[...] * (1.0 + sc_ref[0]) + sh_ref[0]
    hi = h2.astype(BF16)
    h2_ref[:, 0:D_MODEL] = hi

    lo = (h2 - hi.astype(F32)).astype(BF16)
    logits = (jnp.dot(hi, wrh_ref[...], preferred_element_type=F32)
              + jnp.dot(lo, wrh_ref[...], preferred_element_type=F32)
              + jnp.dot(hi, wrl_ref[...], preferred_element_type=F32)) + br_ref[...]

    lane = lax.broadcasted_iota(jnp.int32, (tm, LANE), 1).astype(F32)
    work = jnp.where(lane < N_EXPERTS, logits, -jnp.inf)
    vals, idxs = [], []
    for _ in range(TOP_K):
        m = jnp.max(work, axis=1, keepdims=True)
        idx = jnp.min(jnp.where(work == m, lane, float(LANE)), axis=1, keepdims=True)
        vals.append(m)
        idxs.append(idx)
        work = jnp.where(lane == idx, -jnp.inf, work)
    exps = [jnp.exp(v - vals[0]) for v in vals]
    tot = exps[0] + exps[1] + exps[2] + exps[3]

    onehot = jnp.zeros((tm, LANE), F32)
    for idx in idxs:
        onehot = onehot + (lane == idx).astype(F32)
    r = lax.broadcasted_iota(jnp.int32, (tm, tm), 0)
    c = lax.broadcasted_iota(jnp.int32, (tm, tm), 1)
    prefix = jnp.dot((c < r).astype(BF16), onehot.astype(BF16), preferred_element_type=F32)
    groups = jnp.ceil(jnp.sum(onehot, axis=0, keepdims=True) * (1.0 / GROUP_ALIGN))
    er = lax.broadcasted_iota(jnp.int32, (LANE, LANE), 0)
    ec = lax.broadcasted_iota(jnp.int32, (LANE, LANE), 1)
    before = jnp.dot(jnp.broadcast_to(groups, (8, LANE)).astype(BF16), (er < ec).astype(BF16),
                     preferred_element_type=F32)[0:1, :]
    slot_e = before * GROUP_ALIGN + prefix

    route = jnp.zeros((tm, LANE), F32)
    wext = jnp.zeros((tm, LANE), F32)
    for k in range(TOP_K):
        sk = jnp.sum(jnp.where(lane == idxs[k], slot_e, 0.0), axis=1, keepdims=True)
        sub = jnp.floor(sk * (1.0 / SLOT_SUB))
        route = jnp.where(lane == float(k), sk - sub * SLOT_SUB, route)
        route = jnp.where(lane == float(TOP_K + k), sub, route)
        wk = exps[k] / tot
        wk_hi = wk.astype(BF16).astype(F32)
        wext = jnp.where(lane == idxs[k], wk_hi, wext)
        wext = jnp.where(lane == idxs[k] + float(N_EXPERTS), wk - wk_hi, wext)
    h2_ref[:, D_MODEL:XS_COLS] = wext.astype(BF16)
    route_ref[...] = route
    routet_ref[...] = route.T[0:8, :]
    g8_ref[...] = jnp.broadcast_to(groups * GROUP_ALIGN, (8, LANE))


def _outproj(ya, yb, ga, gb, x2, g1, sh2, sc2, nm, wa, wb_p, wo, wr_hi, wr_lo, br_p, seq):
    n = x2.shape[0]
    tm = ROUTE_TILE
    per_b = seq // tm
    hw = MLA_HEADS * LANE
    row = lambda i: (i, 0)
    fix = lambda i: (0, 0)
    bsel = lambda i: (i // per_b, 0, 0)
    return pl.pallas_call(
        _outproj_kernel,
        grid=(n // tm,),
        in_specs=[pl.BlockSpec((tm, 512), row),
                  pl.BlockSpec((tm, hw), row),
                  pl.BlockSpec((tm, D_MODEL), row),
                  pl.BlockSpec((tm, D_MODEL), row),
                  pl.BlockSpec((tm, D_MODEL), row),
                  pl.BlockSpec((1, 1, D_MODEL), bsel),
                  pl.BlockSpec((1, 1, D_MODEL), bsel),
                  pl.BlockSpec((1, 1, D_MODEL), bsel),
                  pl.BlockSpec((1, D_MODEL), fix),
                  pl.BlockSpec((512, D_MODEL), fix),
                  pl.BlockSpec((hw, D_MODEL), fix),
                  pl.BlockSpec((D_MODEL, D_MODEL), fix),
                  pl.BlockSpec((D_MODEL, LANE), fix),
                  pl.BlockSpec((D_MODEL, LANE), fix),
                  pl.BlockSpec((1, LANE), fix)],
        out_specs=[pl.BlockSpec((tm, D_MODEL), row),
                   pl.BlockSpec((tm, XS_COLS), row),
                   pl.BlockSpec((tm, LANE), row),
                   pl.BlockSpec((8, tm), lambda i: (0, i)),
                   pl.BlockSpec((8, LANE), row)],
        out_shape=[jax.ShapeDtypeStruct((n, D_MODEL), F32),
                   jax.ShapeDtypeStruct((n, XS_COLS), BF16),
                   jax.ShapeDtypeStruct((n, LANE), F32),
                   jax.ShapeDtypeStruct((8, n), F32),
                   jax.ShapeDtypeStruct((n // tm * 8, LANE), F32)],
        compiler_params=_params(("arbitrary",)),
        name="outproj",
    )(ya, yb, ga, gb, x2, g1, sh2, sc2, nm, wa, wb_p, wo, wr_hi, wr_lo, br_p)


def _start_runs(tile, g8_ref, loff_ref, goff_ref, make_copy):
    for e in range(N_EXPERTS):
        g = g8_ref[tile * N_EXPERTS + e]
        lo = loff_ref[tile * N_EXPERTS + e]
        go = goff_ref[tile * N_EXPERTS + e]
        def go_copy(lo=lo, go=go, g=g):
            make_copy(pl.multiple_of(lo, GROUP_ALIGN), pl.multiple_of(go, GROUP_ALIGN),
                      pl.multiple_of(g, GROUP_ALIGN)).start()

        pl.when(g > 0)(go_copy)


def _dispatch_kernel(g8_ref, loff_ref, goff_ref, tot_ref, ends_ref, h_ref, rt_ref, xs_ref, buf, zbuf, sem, zsem):
    t = pl.program_id(0)
    nt = pl.num_programs(0)
    tm = h_ref.shape[0]
    slot = lax.rem(t, 2)

    @pl.when(t == 0)
    def _():
        zbuf[...] = jnp.zeros(zbuf.shape, zbuf.dtype)

        def fill(start, rows):
            return pltpu.make_async_copy(zbuf.at[pl.ds(0, rows), :], xs_ref.at[pl.ds(start, rows), :], zsem)

        def pads(e):
            real_end = pl.multiple_of(ends_ref[N_EXPERTS + e], GROUP_ALIGN)
            return real_end, pl.multiple_of(ends_ref[e] - real_end, GROUP_ALIGN)

        used = lax.shift_right_logical(ends_ref[N_EXPERTS - 1], MOE_ROWS.bit_length() - 1)
        n_blocks = xs_ref.shape[0] // MOE_ROWS
        tail = lambda b: fill(pl.multiple_of(b * MOE_ROWS, MOE_ROWS), MOE_ROWS)
        for e in range(N_EXPERTS):
            pl.when(pads(e)[1] > 0)(lambda e=e: fill(*pads(e)).start())
        lax.fori_loop(used, n_blocks, lambda b, c: (tail(b).start(), c)[1], 0)
        for e in range(N_EXPERTS):
            pl.when(pads(e)[1] > 0)(lambda e=e: fill(*pads(e)).wait())
        lax.fori_loop(used, n_blocks, lambda b, c: (tail(b).wait(), c)[1], 0)

    hb = h_ref[...]
    low = [rt_ref[k:k + 1, :] for k in range(TOP_K)]
    sub = [rt_ref[TOP_K + k:TOP_K + k + 1, :] for k in range(TOP_K)]
    srow = lax.broadcasted_iota(jnp.int32, (SLOT_SUB, tm), 0).astype(F32).astype(BF16)
    one = jnp.ones((SLOT_SUB, tm), BF16)

    for ci in range(SLOTS_PER_TILE // SLOT_CHUNK):
        parts = []
        for b in range(SLOT_CHUNK // SLOT_SUB):
            blk = float(ci * (SLOT_CHUNK // SLOT_SUB) + b)
            own = jnp.zeros((SLOT_SUB, tm), BF16)
            for k in range(TOP_K):
                here = jnp.where(sub[k] == blk, low[k], -1.0).astype(BF16)
                own = jnp.where(srow == here, one, own)
            parts.append(own)
        own = jnp.concatenate(parts, axis=0)
        buf[slot, ci * SLOT_CHUNK:(ci + 1) * SLOT_CHUNK, :] = jnp.dot(own, hb, preferred_element_type=F32)

    def copy(s, lo, go, rows):
        return pltpu.make_async_copy(buf.at[s, pl.ds(lo, rows), :], xs_ref.at[pl.ds(go, rows), :], sem.at[s])

    _start_runs(t, g8_ref, loff_ref, goff_ref, functools.partial(copy, slot))

    def drain(tile, s):
        rows = pl.multiple_of(tot_ref[tile], GROUP_ALIGN)
        copy(s, 0, 0, rows).wait()

    pl.when(t > 0)(lambda: drain(t - 1, 1 - slot))
    pl.when(t == nt - 1)(lambda: drain(t, slot))


def _dispatch(g8f, lofff, gofff, totf, ends, h2, routet, p_rows):
    n = h2.shape[0]
    tm = ROUTE_TILE
    return pl.pallas_call(
        _dispatch_kernel,
        grid_spec=pltpu.PrefetchScalarGridSpec(
            num_scalar_prefetch=5,
            grid=(n // tm,),
            in_specs=[pl.BlockSpec((tm, XS_COLS), lambda i, *_: (i, 0)),
                      pl.BlockSpec((8, tm), lambda i, *_: (0, i))],
            out_specs=pl.BlockSpec(memory_space=pl.ANY),
            scratch_shapes=[pltpu.VMEM((2, SLOTS_PER_TILE, XS_COLS), F32),
                            pltpu.VMEM((MOE_ROWS, XS_COLS), F32),
                            pltpu.SemaphoreType.DMA((2,)), pltpu.SemaphoreType.DMA(())]),
        out_shape=jax.ShapeDtypeStruct((p_rows, XS_COLS), F32),
        compiler_params=_params(("arbitrary",)),
        name="dispatch",
    )(g8f, lofff, gofff, totf, ends, h2, routet)


def _experts_kernel(be_ref, nu_ref, bv_ref, nx_ref, xs_ref, w1_hbm, b1_ref, w2_hbm, b2_ref, ys_ref,
                    w1buf, w2buf, wsem, ord_ref):
    del nu_ref
    i = pl.program_id(0)
    half = MOE_ROWS // 2
    used = bv_ref[i]
    e = be_ref[i]

    def fetch(ex, s):
        return (pltpu.make_async_copy(w1_hbm.at[ex], w1buf.at[s], wsem.at[0, s]),
                pltpu.make_async_copy(w2_hbm.at[ex], w2buf.at[s], wsem.at[1, s]))

    @pl.when(i == 0)
    def _():
        ord_ref[0] = 0
        for c in fetch(e, 0):
            c.start()

    @pl.when((used > 0) & ((i == 0) | (e != be_ref[jnp.maximum(i - 1, 0)])))
    def _():
        @pl.when(i > 0)
        def _():
            ord_ref[0] = ord_ref[0] + 1

        s = lax.rem(ord_ref[0], 2)
        for c in fetch(e, s):
            c.wait()
        nxt = nx_ref[i]

        @pl.when(nxt >= 0)
        def _():
            for c in fetch(nxt, 1 - s):
                c.start()

    slot = lax.rem(ord_ref[0], 2)

    def mlp(n_rows):
        xb = xs_ref[0:n_rows, 0:D_MODEL].astype(BF16)
        wext = xs_ref[0:n_rows, D_MODEL:XS_COLS]
        lane = lax.broadcasted_iota(jnp.int32, wext.shape, 1)
        wcol = jnp.sum(jnp.where((lane == e) | (lane == e + N_EXPERTS), wext, 0.0), axis=1, keepdims=True)
        hcat = jnp.dot(xb, w1buf[slot].astype(BF16), preferred_element_type=F32) + b1_ref[0]
        x_glu = jnp.minimum(hcat[:, :D_EXPERT], SWIGLU_LIMIT)
        x_lin = jnp.clip(hcat[:, D_EXPERT:], -SWIGLU_LIMIT, SWIGLU_LIMIT)
        act = (x_glu * jax.nn.sigmoid(SWIGLU_ALPHA * x_glu) * (x_lin + 1.0)).astype(BF16)
        ys_ref[0:n_rows, :] = (jnp.dot(act, w2buf[slot].astype(BF16), preferred_element_type=F32) + b2_ref[0]) * wcol

    pl.when(used > half)(lambda: mlp(MOE_ROWS))

    @pl.when((used > 0) & (used <= half))
    def _():
        mlp(half)
        ys_ref[half:, :] = jnp.zeros((MOE_ROWS - half, D_MODEL), F32)

    @pl.when(used == 0)
    def _():
        ys_ref[...] = jnp.zeros(ys_ref.shape, F32)


def _experts(block_e, n_used, block_used, next_e, xs, w1, b1, w2, b2):
    p = xs.shape[0]
    mb = MOE_ROWS
    last = lambda i, nu: jnp.minimum(i, nu[0] - 1)
    rowmap = lambda i, be, nu, bv, nx: (last(i, nu), 0)
    wmap = lambda i, be, nu, bv, nx: (be[last(i, nu)], 0, 0)
    hbm = pl.BlockSpec(memory_space=pl.ANY)
    return pl.pallas_call(
        _experts_kernel,
        grid_spec=pltpu.PrefetchScalarGridSpec(
            num_scalar_prefetch=4,
            grid=(p // mb,),
            in_specs=[pl.BlockSpec((mb, XS_COLS), rowmap),
                      hbm,
                      pl.BlockSpec((1, 1, 2 * D_EXPERT), wmap),
                      hbm,
                      pl.BlockSpec((1, 1, D_MODEL), wmap)],
            out_specs=pl.BlockSpec((mb, D_MODEL), lambda i, be, nu, bv, nx: (i, 0)),
            scratch_shapes=[pltpu.VMEM((2, D_MODEL, 2 * D_EXPERT), F32),
                            pltpu.VMEM((2, D_EXPERT, D_MODEL), F32),
                            pltpu.SemaphoreType.DMA((2, 2)),
                            pltpu.SMEM((1,), jnp.int32)]),
        out_shape=jax.ShapeDtypeStruct((p, D_MODEL), F32),
        compiler_params=_params(("arbitrary",)),
        name="experts",
    )(block_e, n_used, block_used, next_e, xs, w1, b1, w2, b2)


def _combine_kernel(g8_ref, loff_ref, goff_ref, tot_ref, ys_ref, route_ref, x1_ref, g2_ref, fn_ref, o_ref, stg, sem):
    t = pl.program_id(0)
    nt = pl.num_programs(0)
    tm = x1_ref.shape[0]
    slot = lax.rem(t, 2)

    def copy(s, lo, go, rows):
        return pltpu.make_async_copy(ys_ref.at[pl.ds(go, rows), :], stg.at[s, pl.ds(lo, rows), :], sem.at[s])

    @pl.when(t == 0)
    def _():
        stg[...] = jnp.zeros(stg.shape, F32)
        _start_runs(t, g8_ref, loff_ref, goff_ref, functools.partial(copy, 0))

    pl.when(t + 1 < nt)(lambda: _start_runs(t + 1, g8_ref, loff_ref, goff_ref, functools.partial(copy, 1 - slot)))
    copy(slot, 0, 0, pl.multiple_of(tot_ref[t], GROUP_ALIGN)).wait()

    route = route_ref[...]
    low = [jnp.broadcast_to(route[:, k:k + 1], (tm, SLOT_SUB)).astype(BF16) for k in range(TOP_K)]
    sub = [jnp.broadcast_to(route[:, TOP_K + k:TOP_K + k + 1], (tm, SLOT_SUB)).astype(BF16) for k in range(TOP_K)]
    scol = lax.broadcasted_iota(jnp.int32, (tm, SLOT_SUB), 1).astype(F32).astype(BF16)
    one = jnp.ones((tm, SLOT_SUB), BF16)
    never = jnp.full((tm, SLOT_SUB), -1.0, BF16)

    y = None
    for ci in range(SLOTS_PER_TILE // SLOT_CHUNK):
        parts = []
        for b in range(SLOT_CHUNK // SLOT_SUB):
            blk = float(ci * (SLOT_CHUNK // SLOT_SUB) + b)
            own = jnp.zeros((tm, SLOT_SUB), BF16)
            for k in range(TOP_K):
                own = jnp.where(scol == jnp.where(sub[k] == blk, low[k], never), one, own)
            parts.append(own)
        own = jnp.concatenate(parts, axis=1)
        rows = stg[slot, ci * SLOT_CHUNK:(ci + 1) * SLOT_CHUNK, :].astype(BF16)
        part = jnp.dot(own, rows, preferred_element_type=F32)
        y = part if y is None else y + part
    x = x1_ref[...] + g2_ref[0] * y
    o_ref[...] = _rms(x) * fn_ref[...]


def _combine(g8f, lofff, gofff, totf, ys, route, x1, g2, fn, seq):
    n = x1.shape[0]
    tm = ROUTE_TILE
    per_b = seq // tm
    return pl.pallas_call(
        _combine_kernel,
        grid_spec=pltpu.PrefetchScalarGridSpec(
            num_scalar_prefetch=4,
            grid=(n // tm,),
            in_specs=[pl.BlockSpec(memory_space=pl.ANY),
                      pl.BlockSpec((tm, LANE), lambda i, *_: (i, 0)),
                      pl.BlockSpec((tm, D_MODEL), lambda i, *_: (i, 0)),
                      pl.BlockSpec((1, 1, D_MODEL), lambda i, *_: (i // per_b, 0, 0)),
                      pl.BlockSpec((1, D_MODEL), lambda i, *_: (0, 0))],
            out_specs=pl.BlockSpec((tm, D_MODEL), lambda i, *_: (i, 0)),
            scratch_shapes=[pltpu.VMEM((2, SLOTS_PER_TILE, D_MODEL), F32), pltpu.SemaphoreType.DMA((2,))]),
        out_shape=jax.ShapeDtypeStruct((n, D_MODEL), F32),
        compiler_params=_params(("arbitrary",)),
        name="combine",
    )(g8f, lofff, gofff, totf, ys, route, x1, g2, fn)


def _split_w_in(w_in):
    o_kr = _W_MIX_COLS
    w_kr = jnp.pad(w_in[:, o_kr:o_kr + MLA_ROPE_DIM],
                   ((0, 0), (MLA_NOPE_DIM, LANE - MLA_NOPE_DIM - MLA_ROPE_DIM)))
    return w_in[:, :o_kr].astype(BF16), w_kr.astype(BF16), w_in[:, o_kr + MLA_ROPE_DIM:].astype(BF16)


def _pack_heads(w, width, pieces):
    rows = w.shape[0]
    w3 = w.reshape(rows, MLA_HEADS, width)
    cols = [w3[:, :, a:b] for a, b in pieces]
    used = sum(b - a for a, b in pieces)
    cols.append(jnp.zeros((rows, MLA_HEADS, LANE - used), w.dtype))
    return jnp.concatenate(cols, axis=2).reshape(rows, MLA_HEADS * LANE)


def kernel(x, c, positions, w_ada, b_ada, norm_mix, norm_ffn, w_in, sinks, q_norm, kv_norm, w_uq, w_uk, w_uv,
           w_branch_a, w_branch_b, w_out, w_router, b_router, w_moe1, b_moe1, w_moe2, b_moe2, final_norm):
    batch, seq, d = x.shape
    n = batch * seq
    assert d == D_MODEL and w_ada.shape[0] == 1
    assert seq % TQ_MLA == 0 and seq % (SWA_QB * ATTN_BLOCK) == 0 and seq % TM_PROJ == 0 and seq % ROUTE_TILE == 0
    hw = MLA_HEADS * LANE
    qk = MLA_NOPE_DIM + MLA_ROPE_DIM

    w_mix, w_kr, w_gates = _split_w_in(w_in[0])
    wuq_p = _pack_heads(w_uq[0], qk, [(0, qk)]).astype(BF16)
    wuk_p = _pack_heads(w_uk[0], MLA_NOPE_DIM, [(0, MLA_NOPE_DIM)]).astype(BF16)
    wuv_p = _pack_heads(w_uv[0], MLA_V_DIM, [(0, MLA_V_DIM)]).astype(BF16)
    one_row = jnp.tile((jnp.arange(LANE) == MLA_V_DIM).astype(F32), MLA_HEADS)[None, :]
    freqs = ROPE_THETA ** (-jnp.arange(0, MLA_ROPE_DIM, 2, dtype=F32) / MLA_ROPE_DIM)
    frq = jnp.concatenate([jnp.zeros((MLA_NOPE_DIM,), F32), freqs, freqs,
                           jnp.zeros((LANE - qk,), F32)])[None, :]
    wb3 = w_branch_b[0].reshape(MLA_HEADS, MLA_V_DIM, D_MODEL)
    wb_p = jnp.concatenate([wb3, jnp.zeros((MLA_HEADS, LANE - MLA_V_DIM, D_MODEL), F32)],
                           axis=1).reshape(hw, D_MODEL).astype(BF16)
    wa = w_branch_a[0].astype(BF16)
    wo = w_out[0].astype(BF16)
    wr = jnp.pad(w_router[0], ((0, 0), (0, LANE - N_EXPERTS)))
    wr_hi = wr.astype(BF16)
    wr_lo = (wr - wr_hi.astype(F32)).astype(BF16)
    br_p = jnp.pad(b_router[0], (0, LANE - N_EXPERTS))[None, :]
    b1 = b_moe1[0][:, None, :]
    b2 = b_moe2[0][:, None, :]
    slopes = jnp.asarray(np.exp2(-8.0 * np.arange(1, SWA_HEADS + 1) / SWA_HEADS), dtype=F32)

    c8 = jnp.pad(c, ((0, 8 - batch), (0, 0)))
    mod = _ada(c8, w_ada[0], b_ada[0][None, :])[:batch]
    sh1, sc1, g1, sh2, sc2, g2 = [m[:, None, :] for m in jnp.split(mod, 6, axis=-1)]

    x2 = x.reshape(n, D_MODEL)
    pos2 = positions.reshape(n, 1).astype(jnp.int32)
    qa, ka2, va2, qm, km, vm, ga, gb, stats = _inproj(
        x2, pos2, sh1, sc1, norm_mix, w_mix, w_kr, w_gates, q_norm, kv_norm, wuq_p, wuk_p, wuv_p, frq, one_row, seq)
    st = stats.reshape(batch, seq // TM_PROJ, 8, LANE)
    bound_sq = jnp.max(st[:, :, :, 0], axis=1) * jnp.max(st[:, :, :, 1], axis=1)
    plain = (bound_sq <= MLA_PLAIN_BOUND ** 2).astype(jnp.int32).reshape(-1)
    yb = _mla(plain, qm, km, vm, batch, seq)
    ya = _swa(sinks[0], slopes, qa, ka2, va2, batch, seq)

    x1, h2, route, routet, g8 = _outproj(
        ya, yb, ga, gb, x2, g1, sh2, sc2, norm_ffn, wa, wb_p, wo, wr_hi, wr_lo, br_p, seq)

    n_tiles = n // ROUTE_TILE
    g8t = g8.reshape(n_tiles, 8, LANE)[:, 0, :N_EXPERTS].astype(jnp.int32)
    loff = jnp.cumsum(g8t, axis=1) - g8t
    padded = ((jnp.sum(g8t, axis=0) + MOE_ROWS - 1) // MOE_ROWS) * MOE_ROWS
    pends = jnp.cumsum(padded)
    goff = (pends - padded)[None, :] + jnp.cumsum(g8t, axis=0) - g8t
    tot = jnp.sum(g8t, axis=1).astype(jnp.int32)
    p_rows = n * TOP_K + n_tiles * N_EXPERTS * (GROUP_ALIGN - 1) + N_EXPERTS * (MOE_ROWS - 1)
    p_rows = -(-p_rows // MOE_ROWS) * MOE_ROWS
    n_blocks = p_rows // MOE_ROWS
    block_start = jnp.arange(n_blocks, dtype=jnp.int32) * MOE_ROWS
    block_e = jnp.minimum(jnp.sum((pends[None, :] <= block_start[:, None]).astype(jnp.int32), axis=1),
                          N_EXPERTS - 1).astype(jnp.int32)
    n_used = (pends[-1:] // MOE_ROWS).astype(jnp.int32)
    eids = jnp.arange(N_EXPERTS, dtype=jnp.int32)
    of_block = block_e[:, None] == eids[None, :]
    pick = lambda table: jnp.sum(jnp.where(of_block, table[None, :], 0), axis=1).astype(jnp.int32)
    real_end = pends - padded + jnp.sum(g8t, axis=0)
    block_used = jnp.clip(pick(real_end) - block_start, 0, MOE_ROWS).astype(jnp.int32)
    later = (eids[None, :] > eids[:, None]) & (padded[None, :] > 0)
    after = jnp.min(jnp.where(later, eids[None, :], N_EXPERTS), axis=1)
    next_e = pick(jnp.where(after == N_EXPERTS, -1, after))
    tabs = (g8t.reshape(-1), loff.reshape(-1).astype(jnp.int32), goff.reshape(-1).astype(jnp.int32), tot)

    xs = _dispatch(*tabs, jnp.concatenate([pends, real_end]).astype(jnp.int32), h2, routet, p_rows)
    ys = _experts(block_e, n_used, block_used, next_e, xs, w_moe1[0], b1, w_moe2[0], b2)
    out = _combine(*tabs, ys, route, x1, g2, final_norm[None, :], seq)
    return out.reshape(batch, seq, D_MODEL)
```

```python
import functools

import numpy as np
import jax
import jax.numpy as jnp
from jax import lax
from jax.experimental import pallas as pl
from jax.experimental.pallas import tpu as pltpu

D_MODEL = 1024
SWA_HEADS = 8
SWA_KV_HEADS = 2
SWA_HEAD_DIM = 64
ATTN_BLOCK = 128
MLA_HEADS = 8
MLA_Q_RANK = 384
MLA_KV_RANK = 256
MLA_NOPE_DIM = 64
MLA_ROPE_DIM = 32
MLA_V_DIM = 64
ROPE_THETA = 10000.0
N_EXPERTS = 32
TOP_K = 4
D_EXPERT = 1024
SWIGLU_LIMIT = 7.0
SWIGLU_ALPHA = 1.702
NORM_EPS = 1e-6

LANE = 128
LOG2E = 1.4426950408889634
NEG = -1e30
SWA_QSCALE = SWA_HEAD_DIM ** -0.5 * LOG2E
MLA_QSCALE = (MLA_NOPE_DIM + MLA_ROPE_DIM) ** -0.5 * LOG2E
HALF_ROPE = MLA_ROPE_DIM // 2
MLA_PLAIN_BOUND = 60.0
MLA_WIDE = 8

BF16 = jnp.bfloat16
F32 = jnp.float32

TM_PROJ = 512
TQ_MLA = 512
SWA_QB = 4
MOE_ROWS = 512
ROUTE_TILE = 512
GROUP_ALIGN = 8
SLOTS_PER_TILE = ROUTE_TILE * TOP_K + N_EXPERTS * GROUP_ALIGN
SLOT_SUB = 256
SLOT_CHUNK = 3 * SLOT_SUB
XS_COLS = D_MODEL + LANE
VMEM_LIMIT = 56 * 1024 * 1024


def _params(sem, vmem=VMEM_LIMIT):
    return pltpu.CompilerParams(dimension_semantics=sem, vmem_limit_bytes=vmem)


def _rms(x):
    return x * lax.rsqrt(jnp.mean(x * x, axis=-1, keepdims=True) + NORM_EPS)


def _ada_kernel(c_ref, w_ref, b_ref, o_ref):
    c = c_ref[...]
    a = (c * jax.nn.sigmoid(c)).astype(BF16)
    o_ref[...] = jnp.dot(a, w_ref[...].astype(BF16), preferred_element_type=F32) + b_ref[...]


def _ada(c8, w_ada, b_ada):
    n_out = w_ada.shape[1]
    return pl.pallas_call(
        _ada_kernel,
        grid=(n_out // D_MODEL,),
        in_specs=[pl.BlockSpec((8, D_MODEL), lambda j: (0, 0)),
                  pl.BlockSpec((D_MODEL, D_MODEL), lambda j: (0, j)),
                  pl.BlockSpec((1, D_MODEL), lambda j: (0, j))],
        out_specs=pl.BlockSpec((8, D_MODEL), lambda j: (0, j)),
        out_shape=jax.ShapeDtypeStruct((8, n_out), F32),
        compiler_params=_params(("arbitrary",)),
        name="ada",
    )(c8, w_ada, b_ada)


_C_QA = (0, 512)
_C_KV = (512, 768)
_C_LAT = (768, 1408)
_W_MIX_COLS = 1408
_ROPE_LO = (MLA_NOPE_DIM, MLA_NOPE_DIM + HALF_ROPE)
_ROPE_HI = (MLA_NOPE_DIM + HALF_ROPE, MLA_NOPE_DIM + MLA_ROPE_DIM)


def _inproj_kernel(x_ref, pos_ref, sh_ref, sc_ref, nm_ref, win_ref, wkr_ref, wg_ref, qn_ref, kvn_ref, wuq_ref, wuk_ref,
                   wuv_ref, frq_ref, one_ref,
                   qa_ref, ka_ref, va_ref, qm_ref, km_ref, vm_ref, ga_ref, gb_ref, st_ref):
    tm = x_ref.shape[0]
    x = x_ref[...]
    h = (_rms(x) * nm_ref[...] * (1.0 + sc_ref[0]) + sh_ref[0]).astype(BF16)

    def proj(c):
        return jnp.dot(h, win_ref[:, c[0]:c[1]], preferred_element_type=F32)

    lane = lax.broadcasted_iota(jnp.int32, (tm, LANE), 1)
    first = lane < SWA_HEAD_DIM

    def twice(t):
        r = pltpu.roll(t, SWA_HEAD_DIM, axis=1)
        return jnp.concatenate([jnp.where(first, t, r), jnp.where(first, r, t)], axis=1)

    kv = proj(_C_KV)
    ka_ref[...] = twice(kv[:, :LANE]).astype(BF16)
    va_ref[...] = twice(kv[:, LANE:]).astype(BF16)

    ang = pos_ref[...].astype(F32) * frq_ref[...]
    cs = jnp.cos(ang)
    sn = jnp.sin(ang)
    sn_lo = jnp.where((lane >= _ROPE_LO[0]) & (lane < _ROPE_LO[1]), -sn, 0.0)
    sn_hi = jnp.where((lane >= _ROPE_HI[0]) & (lane < _ROPE_HI[1]), sn, 0.0)

    def rotary(t):
        return (t * cs + pltpu.roll(t, LANE - HALF_ROPE, axis=1) * sn_lo
                + pltpu.roll(t, HALF_ROPE, axis=1) * sn_hi)

    lat = proj(_C_LAT)
    cqn = (_rms(lat[:, 0:MLA_Q_RANK]) * qn_ref[...]).astype(BF16)
    ckvn = (_rms(lat[:, MLA_Q_RANK:MLA_Q_RANK + MLA_KV_RANK]) * kvn_ref[...]).astype(BF16)
    krr = rotary(jnp.dot(h, wkr_ref[...], preferred_element_type=F32))
    q = jnp.dot(cqn, wuq_ref[...], preferred_element_type=F32)
    kn = jnp.dot(ckvn, wuk_ref[...], preferred_element_type=F32)

    def max_sq_norm(t):
        tf = t.astype(F32)
        return jnp.max(jnp.sum(tf * tf, axis=1, keepdims=True), axis=0, keepdims=True)

    rid = lax.broadcasted_iota(jnp.int32, (8, LANE), 0)
    lid = lax.broadcasted_iota(jnp.int32, (8, LANE), 1)
    stats = jnp.zeros((8, LANE), F32)
    for hh in range(MLA_HEADS):
        sl = slice(hh * LANE, (hh + 1) * LANE)
        qb = (rotary(q[:, sl]) * MLA_QSCALE).astype(BF16)
        kb = (kn[:, sl] + krr).astype(BF16)
        qm_ref[:, sl] = qb
        km_ref[:, sl] = kb
        stats = jnp.where((rid == hh) & (lid == 0), max_sq_norm(qb), stats)
        stats = jnp.where((rid == hh) & (lid == 1), max_sq_norm(kb), stats)
    st_ref[...] = stats
    vm_ref[...] = (jnp.dot(ckvn, wuv_ref[...], preferred_element_type=F32) + one_ref[...]).astype(BF16)
    qa_ref[...] = (proj(_C_QA) * SWA_QSCALE).astype(BF16)
    ga_ref[...] = jax.nn.sigmoid(jnp.dot(h, wg_ref[:, 0:D_MODEL], preferred_element_type=F32)).astype(BF16)
    gb_ref[...] = jax.nn.sigmoid(jnp.dot(h, wg_ref[:, D_MODEL:], preferred_element_type=F32)).astype(BF16)


def _inproj(x2, pos2, sh1, sc1, nm, w_mix, w_kr, w_gates, qn, kvn, wuq_p, wuk_p, wuv_p, frq, one_row, seq):
    n = x2.shape[0]
    tm = TM_PROJ
    per_b = seq // tm
    hw = MLA_HEADS * LANE
    row = lambda i: (i, 0)
    fix = lambda i: (0, 0)
    bsel = lambda i: (i // per_b, 0, 0)
    widths = (512, 256, 256, hw, hw, hw, D_MODEL, D_MODEL)
    return pl.pallas_call(
        _inproj_kernel,
        grid=(n // tm,),
        in_specs=[pl.BlockSpec((tm, D_MODEL), row),
                  pl.BlockSpec((tm, 1), row),
                  pl.BlockSpec((1, 1, D_MODEL), bsel),
                  pl.BlockSpec((1, 1, D_MODEL), bsel),
                  pl.BlockSpec((1, D_MODEL), fix),
                  pl.BlockSpec((D_MODEL, _W_MIX_COLS), fix),
                  pl.BlockSpec((D_MODEL, LANE), fix),
                  pl.BlockSpec((D_MODEL, 2 * D_MODEL), fix),
                  pl.BlockSpec((1, MLA_Q_RANK), fix),
                  pl.BlockSpec((1, MLA_KV_RANK), fix),
                  pl.BlockSpec((MLA_Q_RANK, hw), fix),
                  pl.BlockSpec((MLA_KV_RANK, hw), fix),
                  pl.BlockSpec((MLA_KV_RANK, hw), fix),
                  pl.BlockSpec((1, LANE), fix),
                  pl.BlockSpec((1, hw), fix)],
        out_specs=[pl.BlockSpec((tm, w), row) for w in widths] + [pl.BlockSpec((8, LANE), row)],
        out_shape=[jax.ShapeDtypeStruct((n, w), BF16) for w in widths]
        + [jax.ShapeDtypeStruct((n // tm * 8, LANE), F32)],
        compiler_params=_params(("arbitrary",)),
        name="inproj",
    )(x2, pos2, sh1, sc1, nm, w_mix, w_kr, w_gates, qn, kvn, wuq_p, wuk_p, wuv_p, frq, one_row)


def _mla_kernel(plain_ref, q_ref, k_ref, v_ref, o_ref, m_scr, acc_scr):
    t = TQ_MLA
    nq = q_ref.shape[0] // t

    def scores(i, j, nblk, diag_at):
        q = q_ref[pl.ds(pl.multiple_of(i * t, t), t), :]
        start = pl.multiple_of(j * t, t)
        k = k_ref[pl.ds(start, nblk * t), :]
        v = v_ref[pl.ds(start, nblk * t), :]
        s = lax.dot_general(q, k, (((1,), (1,)), ((), ())), preferred_element_type=F32)
        if diag_at is not None:
            qi = lax.broadcasted_iota(jnp.int32, s.shape, 0)
            kj = lax.broadcasted_iota(jnp.int32, s.shape, 1)
            s = jnp.where(kj <= qi + diag_at * t, s, NEG)
        return s, v

    def plain_step(i, j, nblk, diag_at):
        s, v = scores(i, j, nblk, diag_at)
        acc_scr[...] += jnp.dot(jnp.exp2(s).astype(BF16), v, preferred_element_type=F32)

    def online_step(i, j, nblk, diag_at):
        s, v = scores(i, j, nblk, diag_at)
        m_old = m_scr[...]
        m_new = jnp.maximum(m_old, jnp.max(s, axis=1, keepdims=True))
        p = jnp.exp2(s - m_new)
        alpha = jnp.exp2(m_old - m_new)
        acc_scr[...] = acc_scr[...] * alpha + jnp.dot(p.astype(BF16), v, preferred_element_type=F32)
        m_scr[...] = m_new

    def run(step, wide, online):
        def q_tile(i, carry):
            acc_scr[...] = jnp.zeros(acc_scr.shape, F32)
            if online:
                m_scr[...] = jnp.full(m_scr.shape, NEG, F32)
            nw = lax.shift_right_logical(i, wide.bit_length() - 1)
            left = i - nw * wide

            def wide_body(jj, c):
                step(i, jj * wide, wide, None)
                return c

            lax.fori_loop(0, nw, wide_body, 0)
            for r in range(wide):
                pl.when(left == r)(lambda r=r: step(i, nw * wide, r + 1, r))
            acc = acc_scr[...]
            o_ref[pl.ds(pl.multiple_of(i * t, t), t), :] = (acc / acc[:, MLA_V_DIM:MLA_V_DIM + 1]).astype(BF16)
            return carry

        lax.fori_loop(0, nq, q_tile, 0)

    small = plain_ref[pl.program_id(0) * MLA_HEADS + pl.program_id(1)] != 0
    pl.when(small)(lambda: run(plain_step, MLA_WIDE, False))
    pl.when(jnp.logical_not(small))(lambda: run(online_step, 1, True))


def _mla(plain, qm, km, vm, batch, seq):
    n = qm.shape[0]
    t = TQ_MLA
    kmap = lambda b, h, *_: (b, h)
    return pl.pallas_call(
        _mla_kernel,
        grid_spec=pltpu.PrefetchScalarGridSpec(
            num_scalar_prefetch=1,
            grid=(batch, MLA_HEADS),
            in_specs=[pl.BlockSpec((seq, LANE), kmap),
                      pl.BlockSpec((seq, LANE), kmap),
                      pl.BlockSpec((seq, LANE), kmap)],
            out_specs=pl.BlockSpec((seq, LANE), kmap),
            scratch_shapes=[pltpu.VMEM((t, 1), F32), pltpu.VMEM((t, LANE), F32)]),
        out_shape=jax.ShapeDtypeStruct((n, MLA_HEADS * LANE), BF16),
        compiler_params=_params(("arbitrary", "arbitrary")),
        name="mla",
    )(plain, qm, km, vm)


def _swa_kernel(sink_ref, slope_ref, q_ref, kc_ref, kp_ref, vc_ref, vp_ref, o_ref, bias_scr):
    b = pl.program_id(0)
    i = pl.program_id(1)
    blk = ATTN_BLOCK

    @pl.when((b == 0) & (i == 0))
    def _():
        qi = lax.broadcasted_iota(jnp.int32, (blk, 2 * blk), 0)
        kj = lax.broadcasted_iota(jnp.int32, (blk, 2 * blk), 1)
        dist = qi - kj + blk
        valid = (dist >= 0) & (dist < blk)
        distf = dist.astype(F32)
        for hd in range(SWA_HEADS):
            bias_scr[hd] = jnp.where(valid, -(slope_ref[hd] * LOG2E) * distf, NEG)

    lane = lax.broadcasted_iota(jnp.int32, (2 * blk, LANE), 1)
    lo = lane < SWA_HEAD_DIM
    kcol = lax.broadcasted_iota(jnp.int32, (blk, 2 * blk), 1)
    first_pen = jnp.where(kcol < blk, jnp.where(i == 0, NEG, 0.0), 0.0)
    zero = jnp.zeros((2 * blk, LANE), BF16)

    for qb in range(SWA_QB):
        if qb == 0:
            kprev, vprev = kp_ref[...], vp_ref[...]
        else:
            kprev = kc_ref[(qb - 1) * blk:qb * blk, :]
            vprev = vc_ref[(qb - 1) * blk:qb * blk, :]
        kcat = jnp.concatenate([kprev, kc_ref[qb * blk:(qb + 1) * blk, :]], axis=0)
        vcat = jnp.concatenate([vprev, vc_ref[qb * blk:(qb + 1) * blk, :]], axis=0)
        for g in range(SWA_KV_HEADS):
            kg = kcat[:, g * LANE:(g + 1) * LANE]
            vg = vcat[:, g * LANE:(g + 1) * LANE]
            halves = ((jnp.where(lo, kg, zero), jnp.where(lo, vg, zero)),
                      (jnp.where(lo, zero, kg), jnp.where(lo, zero, vg)))
            for pp in range(2):
                pr = g * 2 + pp
                qpair = q_ref[qb * blk:(qb + 1) * blk, pr * LANE:(pr + 1) * LANE]
                out = None
                for half in range(2):
                    hd = pr * 2 + half
                    kx, vx = halves[half]
                    s = lax.dot_general(qpair, kx, (((1,), (1,)), ((), ())), preferred_element_type=F32)
                    s = s + bias_scr[hd]
                    if qb == 0:
                        s = s + first_pen
                    sink = sink_ref[hd] * LOG2E
                    m = jnp.maximum(jnp.max(s, axis=1, keepdims=True), sink)
                    p = jnp.exp2(s - m)
                    denom = jnp.sum(p, axis=1, keepdims=True) + jnp.exp2(sink - m)
                    o = jnp.dot(p.astype(BF16), vx, preferred_element_type=F32) * (1.0 / denom)
                    out = o if out is None else out + o
                o_ref[qb * blk:(qb + 1) * blk, pr * LANE:(pr + 1) * LANE] = out.astype(BF16)


def _swa(sinks, slopes, qa, ka2, va2, batch, seq):
    n = qa.shape[0]
    blk = ATTN_BLOCK
    rows = SWA_QB * blk
    steps = seq // rows
    cur = lambda b, i: (b * steps + i, 0)
    prev = lambda b, i: (b * (seq // blk) + jnp.maximum(i * SWA_QB - 1, 0), 0)
    smem = pl.BlockSpec(memory_space=pltpu.SMEM)
    return pl.pallas_call(
        _swa_kernel,
        grid=(batch, steps),
        in_specs=[smem, smem,
                  pl.BlockSpec((rows, 512), cur),
                  pl.BlockSpec((rows, 256), cur),
                  pl.BlockSpec((blk, 256), prev),
                  pl.BlockSpec((rows, 256), cur),
                  pl.BlockSpec((blk, 256), prev)],
        out_specs=pl.BlockSpec((rows, 512), cur),
        out_shape=jax.ShapeDtypeStruct((n, 512), BF16),
        scratch_shapes=[pltpu.VMEM((SWA_HEADS, blk, 2 * blk), F32)],
        compiler_params=_params(("arbitrary", "arbitrary")),
        name="swa",
    )(sinks, slopes, qa, ka2, ka2, va2, va2)


def _outproj_kernel(ya_ref, yb_ref, ga_ref, gb_ref, x_ref, g1_ref, sh_ref, sc_ref, nm_ref, wa_ref, wb_ref,
                    wo_ref, wrh_ref, wrl_ref, br_ref,
                    x1_ref, h2_ref, route_ref, routet_ref, g8_ref):
    tm = x_ref.shape[0]
    a = jnp.dot(ya_ref[...], wa_ref[...], preferred_element_type=F32)
    bm = jnp.dot(yb_ref[...], wb_ref[...], preferred_element_type=F32)
    mixed = (ga_ref[...].astype(F32) * a + gb_ref[...].astype(F32) * bm).astype(BF16)
    x1 = x_ref[...] + g1_ref[0] * jnp.dot(mixed, wo_ref[...], preferred_element_type=F32)
    x1_ref[...] = x1
    h2 = _rms(x1) * nm_ref[...] * (1.0 + sc_ref[0]) + sh_ref[0]
    hi = h2.astype(BF16)
    h2_ref[:, 0:D_MODEL] = hi

    lo = (h2 - hi.astype(F32)).astype(BF16)
    logits = (jnp.dot(hi, wrh_ref[...], preferred_element_type=F32)
              + jnp.dot(lo, wrh_ref[...], preferred_element_type=F32)
              + jnp.dot(hi, wrl_ref[...], preferred_element_type=F32)) + br_ref[...]

    lane = lax.broadcasted_iota(jnp.int32, (tm, LANE), 1).astype(F32)
    work = jnp.where(lane < N_EXPERTS, logits, -jnp.inf)
    vals, idxs = [], []
    for _ in range(TOP_K):
        m = jnp.max(work, axis=1, keepdims=True)
        idx = jnp.min(jnp.where(work == m, lane, float(LANE)), axis=1, keepdims=True)
        vals.append(m)
        idxs.append(idx)
        work = jnp.where(lane == idx, -jnp.inf, work)
    exps = [jnp.exp(v - vals[0]) for v in vals]
    tot = exps[0] + exps[1] + exps[2] + exps[3]

    onehot = jnp.zeros((tm, LANE), F32)
    for idx in idxs:
        onehot = onehot + (lane == idx).astype(F32)
    r = lax.broadcasted_iota(jnp.int32, (tm, tm), 0)
    c = lax.broadcasted_iota(jnp.int32, (tm, tm), 1)
    prefix = jnp.dot((c < r).astype(BF16), onehot.astype(BF16), preferred_element_type=F32)
    groups = jnp.ceil(jnp.sum(onehot, axis=0, keepdims=True) * (1.0 / GROUP_ALIGN))
    er = lax.broadcasted_iota(jnp.int32, (LANE, LANE), 0)
    ec = lax.broadcasted_iota(jnp.int32, (LANE, LANE), 1)
    before = jnp.dot(jnp.broadcast_to(groups, (8, LANE)).astype(BF16), (er < ec).astype(BF16),
                     preferred_element_type=F32)[0:1, :]
    slot_e = before * GROUP_ALIGN + prefix

    route = jnp.zeros((tm, LANE), F32)
    wext = jnp.zeros((tm, LANE), F32)
    for k in range(TOP_K):
        sk = jnp.sum(jnp.where(lane == idxs[k], slot_e, 0.0), axis=1, keepdims=True)
        sub = jnp.floor(sk * (1.0 / SLOT_SUB))
        route = jnp.where(lane == float(k), sk - sub * SLOT_SUB, route)
        route = jnp.where(lane == float(TOP_K + k), sub, route)
        wk = exps[k] / tot
        wk_hi = wk.astype(BF16).astype(F32)
        wext = jnp.where(lane == idxs[k], wk_hi, wext)
        wext = jnp.where(lane == idxs[k] + float(N_EXPERTS), wk - wk_hi, wext)
    h2_ref[:, D_MODEL:XS_COLS] = wext.astype(BF16)
    route_ref[...] = route
    routet_ref[...] = route.T[0:8, :]
    g8_ref[...] = jnp.broadcast_to(groups * GROUP_ALIGN, (8, LANE))


def _outproj(ya, yb, ga, gb, x2, g1, sh2, sc2, nm, wa, wb_p, wo, wr_hi, wr_lo, br_p, seq):
    n = x2.shape[0]
    tm = ROUTE_TILE
    per_b = seq // tm
    hw = MLA_HEADS * LANE
    row = lambda i: (i, 0)
    fix = lambda i: (0, 0)
    bsel = lambda i: (i // per_b, 0, 0)
    return pl.pallas_call(
        _outproj_kernel,
        grid=(n // tm,),
        in_specs=[pl.BlockSpec((tm, 512), row),
                  pl.BlockSpec((tm, hw), row),
                  pl.BlockSpec((tm, D_MODEL), row),
                  pl.BlockSpec((tm, D_MODEL), row),
                  pl.BlockSpec((tm, D_MODEL), row),
                  pl.BlockSpec((1, 1, D_MODEL), bsel),
                  pl.BlockSpec((1, 1, D_MODEL), bsel),
                  pl.BlockSpec((1, 1, D_MODEL), bsel),
                  pl.BlockSpec((1, D_MODEL), fix),
                  pl.BlockSpec((512, D_MODEL), fix),
                  pl.BlockSpec((hw, D_MODEL), fix),
                  pl.BlockSpec((D_MODEL, D_MODEL), fix),
                  pl.BlockSpec((D_MODEL, LANE), fix),
                  pl.BlockSpec((D_MODEL, LANE), fix),
                  pl.BlockSpec((1, LANE), fix)],
        out_specs=[pl.BlockSpec((tm, D_MODEL), row),
                   pl.BlockSpec((tm, XS_COLS), row),
                   pl.BlockSpec((tm, LANE), row),
                   pl.BlockSpec((8, tm), lambda i: (0, i)),
                   pl.BlockSpec((8, LANE), row)],
        out_shape=[jax.ShapeDtypeStruct((n, D_MODEL), F32),
                   jax.ShapeDtypeStruct((n, XS_COLS), BF16),
                   jax.ShapeDtypeStruct((n, LANE), F32),
                   jax.ShapeDtypeStruct((8, n), F32),
                   jax.ShapeDtypeStruct((n // tm * 8, LANE), F32)],
        compiler_params=_params(("arbitrary",)),
        name="outproj",
    )(ya, yb, ga, gb, x2, g1, sh2, sc2, nm, wa, wb_p, wo, wr_hi, wr_lo, br_p)


def _start_runs(tile, g8_ref, loff_ref, goff_ref, make_copy):
    for e in range(N_EXPERTS):
        g = g8_ref[tile * N_EXPERTS + e]
        lo = loff_ref[tile * N_EXPERTS + e]
        go = goff_ref[tile * N_EXPERTS + e]
        def go_copy(lo=lo, go=go, g=g):
            make_copy(pl.multiple_of(lo, GROUP_ALIGN), pl.multiple_of(go, GROUP_ALIGN),
                      pl.multiple_of(g, GROUP_ALIGN)).start()

        pl.when(g > 0)(go_copy)


def _dispatch_kernel(g8_ref, loff_ref, goff_ref, tot_ref, ends_ref, h_ref, rt_ref, xs_ref, buf, zbuf, sem, zsem):
    t = pl.program_id(0)
    nt = pl.num_programs(0)
    tm = h_ref.shape[0]
    slot = lax.rem(t, 2)

    @pl.when(t == 0)
    def _():
        zbuf[...] = jnp.zeros(zbuf.shape, zbuf.dtype)

        def fill(start, rows):
            return pltpu.make_async_copy(zbuf.at[pl.ds(0, rows), :], xs_ref.at[pl.ds(start, rows), :], zsem)

        def pads(e):
            real_end = pl.multiple_of(ends_ref[N_EXPERTS + e], GROUP_ALIGN)
            return real_end, pl.multiple_of(ends_ref[e] - real_end, GROUP_ALIGN)

        used = lax.shift_right_logical(ends_ref[N_EXPERTS - 1], MOE_ROWS.bit_length() - 1)
        n_blocks = xs_ref.shape[0] // MOE_ROWS
        tail = lambda b: fill(pl.multiple_of(b * MOE_ROWS, MOE_ROWS), MOE_ROWS)
        for e in range(N_EXPERTS):
            pl.when(pads(e)[1] > 0)(lambda e=e: fill(*pads(e)).start())
        lax.fori_loop(used, n_blocks, lambda b, c: (tail(b).start(), c)[1], 0)
        for e in range(N_EXPERTS):
            pl.when(pads(e)[1] > 0)(lambda e=e: fill(*pads(e)).wait())
        lax.fori_loop(used, n_blocks, lambda b, c: (tail(b).wait(), c)[1], 0)

    hb = h_ref[...]
    low = [rt_ref[k:k + 1, :] for k in range(TOP_K)]
    sub = [rt_ref[TOP_K + k:TOP_K + k + 1, :] for k in range(TOP_K)]
    srow = lax.broadcasted_iota(jnp.int32, (SLOT_SUB, tm), 0).astype(F32).astype(BF16)
    one = jnp.ones((SLOT_SUB, tm), BF16)

    for ci in range(SLOTS_PER_TILE // SLOT_CHUNK):
        parts = []
        for b in range(SLOT_CHUNK // SLOT_SUB):
            blk = float(ci * (SLOT_CHUNK // SLOT_SUB) + b)
            own = jnp.zeros((SLOT_SUB, tm), BF16)
            for k in range(TOP_K):
                here = jnp.where(sub[k] == blk, low[k], -1.0).astype(BF16)
                own = jnp.where(srow == here, one, own)
            parts.append(own)
        own = jnp.concatenate(parts, axis=0)
        buf[slot, ci * SLOT_CHUNK:(ci + 1) * SLOT_CHUNK, :] = jnp.dot(own, hb, preferred_element_type=F32)

    def copy(s, lo, go, rows):
        return pltpu.make_async_copy(buf.at[s, pl.ds(lo, rows), :], xs_ref.at[pl.ds(go, rows), :], sem.at[s])

    _start_runs(t, g8_ref, loff_ref, goff_ref, functools.partial(copy, slot))

    def drain(tile, s):
        rows = pl.multiple_of(tot_ref[tile], GROUP_ALIGN)
        copy(s, 0, 0, rows).wait()

    pl.when(t > 0)(lambda: drain(t - 1, 1 - slot))
    pl.when(t == nt - 1)(lambda: drain(t, slot))


def _dispatch(g8f, lofff, gofff, totf, ends, h2, routet, p_rows):
    n = h2.shape[0]
    tm = ROUTE_TILE
    return pl.pallas_call(
        _dispatch_kernel,
        grid_spec=pltpu.PrefetchScalarGridSpec(
            num_scalar_prefetch=5,
            grid=(n // tm,),
            in_specs=[pl.BlockSpec((tm, XS_COLS), lambda i, *_: (i, 0)),
                      pl.BlockSpec((8, tm), lambda i, *_: (0, i))],
            out_specs=pl.BlockSpec(memory_space=pl.ANY),
            scratch_shapes=[pltpu.VMEM((2, SLOTS_PER_TILE, XS_COLS), F32),
                            pltpu.VMEM((MOE_ROWS, XS_COLS), F32),
                            pltpu.SemaphoreType.DMA((2,)), pltpu.SemaphoreType.DMA(())]),
        out_shape=jax.ShapeDtypeStruct((p_rows, XS_COLS), F32),
        compiler_params=_params(("arbitrary",)),
        name="dispatch",
    )(g8f, lofff, gofff, totf, ends, h2, routet)


def _experts_kernel(be_ref, nu_ref, bv_ref, nx_ref, xs_ref, w1_hbm, b1_ref, w2_hbm, b2_ref, ys_ref,
                    w1buf, w2buf, wsem, ord_ref):
    del nu_ref
    i = pl.program_id(0)
    half = MOE_ROWS // 2
    used = bv_ref[i]
    e = be_ref[i]

    def fetch(ex, s):
        return (pltpu.make_async_copy(w1_hbm.at[ex], w1buf.at[s], wsem.at[0, s]),
                pltpu.make_async_copy(w2_hbm.at[ex], w2buf.at[s], wsem.at[1, s]))

    @pl.when(i == 0)
    def _():
        ord_ref[0] = 0
        for c in fetch(e, 0):
            c.start()

    @pl.when((used > 0) & ((i == 0) | (e != be_ref[jnp.maximum(i - 1, 0)])))
    def _():
        @pl.when(i > 0)
        def _():
            ord_ref[0] = ord_ref[0] + 1

        s = lax.rem(ord_ref[0], 2)
        for c in fetch(e, s):
            c.wait()
        nxt = nx_ref[i]

        @pl.when(nxt >= 0)
        def _():
            for c in fetch(nxt, 1 - s):
                c.start()

    slot = lax.rem(ord_ref[0], 2)

    def mlp(n_rows):
        xb = xs_ref[0:n_rows, 0:D_MODEL].astype(BF16)
        wext = xs_ref[0:n_rows, D_MODEL:XS_COLS]
        lane = lax.broadcasted_iota(jnp.int32, wext.shape, 1)
        wcol = jnp.sum(jnp.where((lane == e) | (lane == e + N_EXPERTS), wext, 0.0), axis=1, keepdims=True)
        hcat = jnp.dot(xb, w1buf[slot].astype(BF16), preferred_element_type=F32) + b1_ref[0]
        x_glu = jnp.minimum(hcat[:, :D_EXPERT], SWIGLU_LIMIT)
        x_lin = jnp.clip(hcat[:, D_EXPERT:], -SWIGLU_LIMIT, SWIGLU_LIMIT)
        act = (x_glu * jax.nn.sigmoid(SWIGLU_ALPHA * x_glu) * (x_lin + 1.0)).astype(BF16)
        ys_ref[0:n_rows, :] = (jnp.dot(act, w2buf[slot].astype(BF16), preferred_element_type=F32) + b2_ref[0]) * wcol

    pl.when(used > half)(lambda: mlp(MOE_ROWS))

    @pl.when((used > 0) & (used <= half))
    def _():
        mlp(half)
        ys_ref[half:, :] = jnp.zeros((MOE_ROWS - half, D_MODEL), F32)

    @pl.when(used == 0)
    def _():
        ys_ref[...] = jnp.zeros(ys_ref.shape, F32)


def _experts(block_e, n_used, block_used, next_e, xs, w1, b1, w2, b2):
    p = xs.shape[0]
    mb = MOE_ROWS
    last = lambda i, nu: jnp.minimum(i, nu[0] - 1)
    rowmap = lambda i, be, nu, bv, nx: (last(i, nu), 0)
    wmap = lambda i, be, nu, bv, nx: (be[last(i, nu)], 0, 0)
    hbm = pl.BlockSpec(memory_space=pl.ANY)
    return pl.pallas_call(
        _experts_kernel,
        grid_spec=pltpu.PrefetchScalarGridSpec(
            num_scalar_prefetch=4,
            grid=(p // mb,),
            in_specs=[pl.BlockSpec((mb, XS_COLS), rowmap),
                      hbm,
                      pl.BlockSpec((1, 1, 2 * D_EXPERT), wmap),
                      hbm,
                      pl.BlockSpec((1, 1, D_MODEL), wmap)],
            out_specs=pl.BlockSpec((mb, D_MODEL), lambda i, be, nu, bv, nx: (i, 0)),
            scratch_shapes=[pltpu.VMEM((2, D_MODEL, 2 * D_EXPERT), F32),
                            pltpu.VMEM((2, D_EXPERT, D_MODEL), F32),
                            pltpu.SemaphoreType.DMA((2, 2)),
                            pltpu.SMEM((1,), jnp.int32)]),
        out_shape=jax.ShapeDtypeStruct((p, D_MODEL), F32),
        compiler_params=_params(("arbitrary",)),
        name="experts",
    )(block_e, n_used, block_used, next_e, xs, w1, b1, w2, b2)


def _combine_kernel(g8_ref, loff_ref, goff_ref, tot_ref, ys_ref, route_ref, x1_ref, g2_ref, fn_ref, o_ref, stg, sem):
    t = pl.program_id(0)
    nt = pl.num_programs(0)
    tm = x1_ref.shape[0]
    slot = lax.rem(t, 2)

    def copy(s, lo, go, rows):
        return pltpu.make_async_copy(ys_ref.at[pl.ds(go, rows), :], stg.at[s, pl.ds(lo, rows), :], sem.at[s])

    @pl.when(t == 0)
    def _():
        stg[...] = jnp.zeros(stg.shape, F32)
        _start_runs(t, g8_ref, loff_ref, goff_ref, functools.partial(copy, 0))

    pl.when(t + 1 < nt)(lambda: _start_runs(t + 1, g8_ref, loff_ref, goff_ref, functools.partial(copy, 1 - slot)))
    copy(slot, 0, 0, pl.multiple_of(tot_ref[t], GROUP_ALIGN)).wait()

    route = route_ref[...]
    low = [jnp.broadcast_to(route[:, k:k + 1], (tm, SLOT_SUB)).astype(BF16) for k in range(TOP_K)]
    sub = [jnp.broadcast_to(route[:, TOP_K + k:TOP_K + k + 1], (tm, SLOT_SUB)).astype(BF16) for k in range(TOP_K)]
    scol = lax.broadcasted_iota(jnp.int32, (tm, SLOT_SUB), 1).astype(F32).astype(BF16)
    one = jnp.ones((tm, SLOT_SUB), BF16)
    never = jnp.full((tm, SLOT_SUB), -1.0, BF16)

    y = None
    for ci in range(SLOTS_PER_TILE // SLOT_CHUNK):
        parts = []
        for b in range(SLOT_CHUNK // SLOT_SUB):
            blk = float(ci * (SLOT_CHUNK // SLOT_SUB) + b)
            own = jnp.zeros((tm, SLOT_SUB), BF16)
            for k in range(TOP_K):
                own = jnp.where(scol == jnp.where(sub[k] == blk, low[k], never), one, own)
            parts.append(own)
        own = jnp.concatenate(parts, axis=1)
        rows = stg[slot, ci * SLOT_CHUNK:(ci + 1) * SLOT_CHUNK, :].astype(BF16)
        part = jnp.dot(own, rows, preferred_element_type=F32)
        y = part if y is None else y + part
    x = x1_ref[...] + g2_ref[0] * y
    o_ref[...] = _rms(x) * fn_ref[...]


def _combine(g8f, lofff, gofff, totf, ys, route, x1, g2, fn, seq):
    n = x1.shape[0]
    tm = ROUTE_TILE
    per_b = seq // tm
    return pl.pallas_call(
        _combine_kernel,
        grid_spec=pltpu.PrefetchScalarGridSpec(
            num_scalar_prefetch=4,
            grid=(n // tm,),
            in_specs=[pl.BlockSpec(memory_space=pl.ANY),
                      pl.BlockSpec((tm, LANE), lambda i, *_: (i, 0)),
                      pl.BlockSpec((tm, D_MODEL), lambda i, *_: (i, 0)),
                      pl.BlockSpec((1, 1, D_MODEL), lambda i, *_: (i // per_b, 0, 0)),
                      pl.BlockSpec((1, D_MODEL), lambda i, *_: (0, 0))],
            out_specs=pl.BlockSpec((tm, D_MODEL), lambda i, *_: (i, 0)),
            scratch_shapes=[pltpu.VMEM((2, SLOTS_PER_TILE, D_MODEL), F32), pltpu.SemaphoreType.DMA((2,))]),
        out_shape=jax.ShapeDtypeStruct((n, D_MODEL), F32),
        compiler_params=_params(("arbitrary",)),
        name="combine",
    )(g8f, lofff, gofff, totf, ys, route, x1, g2, fn)


def _split_w_in(w_in):
    o_kr = _W_MIX_COLS
    w_kr = jnp.pad(w_in[:, o_kr:o_kr + MLA_ROPE_DIM],
                   ((0, 0), (MLA_NOPE_DIM, LANE - MLA_NOPE_DIM - MLA_ROPE_DIM)))
    return w_in[:, :o_kr].astype(BF16), w_kr.astype(BF16), w_in[:, o_kr + MLA_ROPE_DIM:].astype(BF16)


def _pack_heads(w, width, pieces):
    rows = w.shape[0]
    w3 = w.reshape(rows, MLA_HEADS, width)
    cols = [w3[:, :, a:b] for a, b in pieces]
    used = sum(b - a for a, b in pieces)
    cols.append(jnp.zeros((rows, MLA_HEADS, LANE - used), w.dtype))
    return jnp.concatenate(cols, axis=2).reshape(rows, MLA_HEADS * LANE)


def kernel(x, c, positions, w_ada, b_ada, norm_mix, norm_ffn, w_in, sinks, q_norm, kv_norm, w_uq, w_uk, w_uv,
           w_branch_a, w_branch_b, w_out, w_router, b_router, w_moe1, b_moe1, w_moe2, b_moe2, final_norm):
    batch, seq, d = x.shape
    n = batch * seq
    assert d == D_MODEL and w_ada.shape[0] == 1
    assert seq % TQ_MLA == 0 and seq % (SWA_QB * ATTN_BLOCK) == 0 and seq % TM_PROJ == 0 and seq % ROUTE_TILE == 0
    hw = MLA_HEADS * LANE
    qk = MLA_NOPE_DIM + MLA_ROPE_DIM

    w_mix, w_kr, w_gates = _split_w_in(w_in[0])
    wuq_p = _pack_heads(w_uq[0], qk, [(0, qk)]).astype(BF16)
    wuk_p = _pack_heads(w_uk[0], MLA_NOPE_DIM, [(0, MLA_NOPE_DIM)]).astype(BF16)
    wuv_p = _pack_heads(w_uv[0], MLA_V_DIM, [(0, MLA_V_DIM)]).astype(BF16)
    one_row = jnp.tile((jnp.arange(LANE) == MLA_V_DIM).astype(F32), MLA_HEADS)[None, :]
    freqs = ROPE_THETA ** (-jnp.arange(0, MLA_ROPE_DIM, 2, dtype=F32) / MLA_ROPE_DIM)
    frq = jnp.concatenate([jnp.zeros((MLA_NOPE_DIM,), F32), freqs, freqs,
                           jnp.zeros((LANE - qk,), F32)])[None, :]
    wb3 = w_branch_b[0].reshape(MLA_HEADS, MLA_V_DIM, D_MODEL)
    wb_p = jnp.concatenate([wb3, jnp.zeros((MLA_HEADS, LANE - MLA_V_DIM, D_MODEL), F32)],
                           axis=1).reshape(hw, D_MODEL).astype(BF16)
    wa = w_branch_a[0].astype(BF16)
    wo = w_out[0].astype(BF16)
    wr = jnp.pad(w_router[0], ((0, 0), (0, LANE - N_EXPERTS)))
    wr_hi = wr.astype(BF16)
    wr_lo = (wr - wr_hi.astype(F32)).astype(BF16)
    br_p = jnp.pad(b_router[0], (0, LANE - N_EXPERTS))[None, :]
    b1 = b_moe1[0][:, None, :]
    b2 = b_moe2[0][:, None, :]
    slopes = jnp.asarray(np.exp2(-8.0 * np.arange(1, SWA_HEADS + 1) / SWA_HEADS), dtype=F32)

    c8 = jnp.pad(c, ((0, 8 - batch), (0, 0)))
    mod = _ada(c8, w_ada[0], b_ada[0][None, :])[:batch]
    sh1, sc1, g1, sh2, sc2, g2 = [m[:, None, :] for m in jnp.split(mod, 6, axis=-1)]

    x2 = x.reshape(n, D_MODEL)
    pos2 = positions.reshape(n, 1).astype(jnp.int32)
    qa, ka2, va2, qm, km, vm, ga, gb, stats = _inproj(
        x2, pos2, sh1, sc1, norm_mix, w_mix, w_kr, w_gates, q_norm, kv_norm, wuq_p, wuk_p, wuv_p, frq, one_row, seq)
    st = stats.reshape(batch, seq // TM_PROJ, 8, LANE)
    bound_sq = jnp.max(st[:, :, :, 0], axis=1) * jnp.max(st[:, :, :, 1], axis=1)
    plain = (bound_sq <= MLA_PLAIN_BOUND ** 2).astype(jnp.int32).reshape(-1)
    yb = _mla(plain, qm, km, vm, batch, seq)
    ya = _swa(sinks[0], slopes, qa, ka2, va2, batch, seq)

    x1, h2, route, routet, g8 = _outproj(
        ya, yb, ga, gb, x2, g1, sh2, sc2, norm_ffn, wa, wb_p, wo, wr_hi, wr_lo, br_p, seq)

    n_tiles = n // ROUTE_TILE
    g8t = g8.reshape(n_tiles, 8, LANE)[:, 0, :N_EXPERTS].astype(jnp.int32)
    loff = jnp.cumsum(g8t, axis=1) - g8t
    padded = ((jnp.sum(g8t, axis=0) + MOE_ROWS - 1) // MOE_ROWS) * MOE_ROWS
    pends = jnp.cumsum(padded)
    goff = (pends - padded)[None, :] + jnp.cumsum(g8t, axis=0) - g8t
    tot = jnp.sum(g8t, axis=1).astype(jnp.int32)
    p_rows = n * TOP_K + n_tiles * N_EXPERTS * (GROUP_ALIGN - 1) + N_EXPERTS * (MOE_ROWS - 1)
    p_rows = -(-p_rows // MOE_ROWS) * MOE_ROWS
    n_blocks = p_rows // MOE_ROWS
    block_start = jnp.arange(n_blocks, dtype=jnp.int32) * MOE_ROWS
    block_e = jnp.minimum(jnp.sum((pends[None, :] <= block_start[:, None]).astype(jnp.int32), axis=1),
                          N_EXPERTS - 1).astype(jnp.int32)
    n_used = (pends[-1:] // MOE_ROWS).astype(jnp.int32)
    eids = jnp.arange(N_EXPERTS, dtype=jnp.int32)
    of_block = block_e[:, None] == eids[None, :]
    pick = lambda table: jnp.sum(jnp.where(of_block, table[None, :], 0), axis=1).astype(jnp.int32)
    real_end = pends - padded + jnp.sum(g8t, axis=0)
    block_used = jnp.clip(pick(real_end) - block_start, 0, MOE_ROWS).astype(jnp.int32)
    later = (eids[None, :] > eids[:, None]) & (padded[None, :] > 0)
    after = jnp.min(jnp.where(later, eids[None, :], N_EXPERTS), axis=1)
    next_e = pick(jnp.where(after == N_EXPERTS, -1, after))
    tabs = (g8t.reshape(-1), loff.reshape(-1).astype(jnp.int32), goff.reshape(-1).astype(jnp.int32), tot)

    xs = _dispatch(*tabs, jnp.concatenate([pends, real_end]).astype(jnp.int32), h2, routet, p_rows)
    ys = _experts(block_e, n_used, block_used, next_e, xs, w_moe1[0], b1, w_moe2[0], b2)
    out = _combine(*tabs, ys, route, x1, g2, final_norm[None, :], seq)
    return out.reshape(batch, seq, D_MODEL)
```

```python
import functools

import numpy as np
import jax
import jax.numpy as jnp
from jax import lax
from jax.experimental import pallas as pl
from jax.experimental.pallas import tpu as pltpu

D_MODEL = 1024
SWA_HEADS = 8
SWA_KV_HEADS = 2
SWA_HEAD_DIM = 64
ATTN_BLOCK = 128
MLA_HEADS = 8
MLA_Q_RANK = 384
MLA_KV_RANK = 256
MLA_NOPE_DIM = 64
MLA_ROPE_DIM = 32
MLA_V_DIM = 64
ROPE_THETA = 10000.0
N_EXPERTS = 32
TOP_K = 4
D_EXPERT = 1024
SWIGLU_LIMIT = 7.0
SWIGLU_ALPHA = 1.702
NORM_EPS = 1e-6

LANE = 128
LOG2E = 1.4426950408889634
NEG = -1e30
SWA_QSCALE = SWA_HEAD_DIM ** -0.5 * LOG2E
MLA_QSCALE = (MLA_NOPE_DIM + MLA_ROPE_DIM) ** -0.5 * LOG2E
HALF_ROPE = MLA_ROPE_DIM // 2
MLA_PLAIN_BOUND = 60.0
MLA_WIDE = 8

BF16 = jnp.bfloat16
F32 = jnp.float32

TM_PROJ = 512
TQ_MLA = 512
SWA_QB = 4
MOE_ROWS = 512
ROUTE_TILE = 512
GROUP_ALIGN = 8
SLOTS_PER_TILE = ROUTE_TILE * TOP_K + N_EXPERTS * GROUP_ALIGN
SLOT_SUB = 256
SLOT_CHUNK = 3 * SLOT_SUB
XS_COLS = D_MODEL + LANE
VMEM_LIMIT = 56 * 1024 * 1024


def _params(sem, vmem=VMEM_LIMIT):
    return pltpu.CompilerParams(dimension_semantics=sem, vmem_limit_bytes=vmem)


def _rms(x):
    return x * lax.rsqrt(jnp.mean(x * x, axis=-1, keepdims=True) + NORM_EPS)


def _ada_kernel(c_ref, w_ref, b_ref, o_ref):
    c = c_ref[...]
    a = (c * jax.nn.sigmoid(c)).astype(BF16)
    o_ref[...] = jnp.dot(a, w_ref[...].astype(BF16), preferred_element_type=F32) + b_ref[...]


def _ada(c8, w_ada, b_ada):
    n_out = w_ada.shape[1]
    return pl.pallas_call(
        _ada_kernel,
        grid=(n_out // D_MODEL,),
        in_specs=[pl.BlockSpec((8, D_MODEL), lambda j: (0, 0)),
                  pl.BlockSpec((D_MODEL, D_MODEL), lambda j: (0, j)),
                  pl.BlockSpec((1, D_MODEL), lambda j: (0, j))],
        out_specs=pl.BlockSpec((8, D_MODEL), lambda j: (0, j)),
        out_shape=jax.ShapeDtypeStruct((8, n_out), F32),
        compiler_params=_params(("arbitrary",)),
        name="ada",
    )(c8, w_ada, b_ada)


_C_QA = (0, 512)
_C_KV = (512, 768)
_C_LAT = (768, 1408)
_W_MIX_COLS = 1408
_ROPE_LO = (MLA_NOPE_DIM, MLA_NOPE_DIM + HALF_ROPE)
_ROPE_HI = (MLA_NOPE_DIM + HALF_ROPE, MLA_NOPE_DIM + MLA_ROPE_DIM)


def _inproj_kernel(x_ref, pos_ref, sh_ref, sc_ref, nm_ref, win_ref, wkr_ref, wg_ref, qn_ref, kvn_ref, wuq_ref, wuk_ref,
                   wuv_ref, frq_ref, one_ref,
                   qa_ref, ka_ref, va_ref, qm_ref, km_ref, vm_ref, ga_ref, gb_ref, st_ref):
    tm = x_ref.shape[0]
    x = x_ref[...]
    h = (_rms(x) * nm_ref[...] * (1.0 + sc_ref[0]) + sh_ref[0]).astype(BF16)

    def proj(c):
        return jnp.dot(h, win_ref[:, c[0]:c[1]], preferred_element_type=F32)

    lane = lax.broadcasted_iota(jnp.int32, (tm, LANE), 1)
    first = lane < SWA_HEAD_DIM

    def twice(t):
        r = pltpu.roll(t, SWA_HEAD_DIM, axis=1)
        return jnp.concatenate([jnp.where(first, t, r), jnp.where(first, r, t)], axis=1)

    kv = proj(_C_KV)
    ka_ref[...] = twice(kv[:, :LANE]).astype(BF16)
    va_ref[...] = twice(kv[:, LANE:]).astype(BF16)

    ang = pos_ref[...].astype(F32) * frq_ref[...]
    cs = jnp.cos(ang)
    sn = jnp.sin(ang)
    sn_lo = jnp.where((lane >= _ROPE_LO[0]) & (lane < _ROPE_LO[1]), -sn, 0.0)
    sn_hi = jnp.where((lane >= _ROPE_HI[0]) & (lane < _ROPE_HI[1]), sn, 0.0)

    def rotary(t):
        return (t * cs + pltpu.roll(t, LANE - HALF_ROPE, axis=1) * sn_lo
                + pltpu.roll(t, HALF_ROPE, axis=1) * sn_hi)

    lat = proj(_C_LAT)
    cqn = (_rms(lat[:, 0:MLA_Q_RANK]) * qn_ref[...]).astype(BF16)
    ckvn = (_rms(lat[:, MLA_Q_RANK:MLA_Q_RANK + MLA_KV_RANK]) * kvn_ref[...]).astype(BF16)
    krr = rotary(jnp.dot(h, wkr_ref[...], preferred_element_type=F32))
    q = jnp.dot(cqn, wuq_ref[...], preferred_element_type=F32)
    kn = jnp.dot(ckvn, wuk_ref[...], preferred_element_type=F32)

    def max_sq_norm(t):
        tf = t.astype(F32)
        return jnp.max(jnp.sum(tf * tf, axis=1, keepdims=True), axis=0, keepdims=True)

    rid = lax.broadcasted_iota(jnp.int32, (8, LANE), 0)
    lid = lax.broadcasted_iota(jnp.int32, (8, LANE), 1)
    stats = jnp.zeros((8, LANE), F32)
    for hh in range(MLA_HEADS):
        sl = slice(hh * LANE, (hh + 1) * LANE)
        qb = (rotary(q[:, sl]) * MLA_QSCALE).astype(BF16)
        kb = (kn[:, sl] + krr).astype(BF16)
        qm_ref[:, sl] = qb
        km_ref[:, sl] = kb
        stats = jnp.where((rid == hh) & (lid == 0), max_sq_norm(qb), stats)
        stats = jnp.where((rid == hh) & (lid == 1), max_sq_norm(kb), stats)
    st_ref[...] = stats
    vm_ref[...] = (jnp.dot(ckvn, wuv_ref[...], preferred_element_type=F32) + one_ref[...]).astype(BF16)
    qa_ref[...] = (proj(_C_QA) * SWA_QSCALE).astype(BF16)
    ga_ref[...] = jax.nn.sigmoid(jnp.dot(h, wg_ref[:, 0:D_MODEL], preferred_element_type=F32)).astype(BF16)
    gb_ref[...] = jax.nn.sigmoid(jnp.dot(h, wg_ref[:, D_MODEL:], preferred_element_type=F32)).astype(BF16)


def _inproj(x2, pos2, sh1, sc1, nm, w_mix, w_kr, w_gates, qn, kvn, wuq_p, wuk_p, wuv_p, frq, one_row, seq):
    n = x2.shape[0]
    tm = TM_PROJ
    per_b = seq // tm
    hw = MLA_HEADS * LANE
    row = lambda i: (i, 0)
    fix = lambda i: (0, 0)
    bsel = lambda i: (i // per_b, 0, 0)
    widths = (512, 256, 256, hw, hw, hw, D_MODEL, D_MODEL)
    return pl.pallas_call(
        _inproj_kernel,
        grid=(n // tm,),
        in_specs=[pl.BlockSpec((tm, D_MODEL), row),
                  pl.BlockSpec((tm, 1), row),
                  pl.BlockSpec((1, 1, D_MODEL), bsel),
                  pl.BlockSpec((1, 1, D_MODEL), bsel),
                  pl.BlockSpec((1, D_MODEL), fix),
                  pl.BlockSpec((D_MODEL, _W_MIX_COLS), fix),
                  pl.BlockSpec((D_MODEL, LANE), fix),
                  pl.BlockSpec((D_MODEL, 2 * D_MODEL), fix),
                  pl.BlockSpec((1, MLA_Q_RANK), fix),
                  pl.BlockSpec((1, MLA_KV_RANK), fix),
                  pl.BlockSpec((MLA_Q_RANK, hw), fix),
                  pl.BlockSpec((MLA_KV_RANK, hw), fix),
                  pl.BlockSpec((MLA_KV_RANK, hw), fix),
                  pl.BlockSpec((1, LANE), fix),
                  pl.BlockSpec((1, hw), fix)],
        out_specs=[pl.BlockSpec((tm, w), row) for w in widths] + [pl.BlockSpec((8, LANE), row)],
        out_shape=[jax.ShapeDtypeStruct((n, w), BF16) for w in widths]
        + [jax.ShapeDtypeStruct((n // tm * 8, LANE), F32)],
        compiler_params=_params(("arbitrary",)),
        name="inproj",
    )(x2, pos2, sh1, sc1, nm, w_mix, w_kr, w_gates, qn, kvn, wuq_p, wuk_p, wuv_p, frq, one_row)


def _mla_kernel(plain_ref, q_ref, k_ref, v_ref, o_ref, m_scr, acc_scr):
    t = TQ_MLA
    nq = q_ref.shape[0] // t

    def scores(i, j, nblk, diag_at):
        q = q_ref[pl.ds(pl.multiple_of(i * t, t), t), :]
        start = pl.multiple_of(j * t, t)
        k = k_ref[pl.ds(start, nblk * t), :]
        v = v_ref[pl.ds(start, nblk * t), :]
        s = lax.dot_general(q, k, (((1,), (1,)), ((), ())), preferred_element_type=F32)
        if diag_at is not None:
            qi = lax.broadcasted_iota(jnp.int32, s.shape, 0)
            kj = lax.broadcasted_iota(jnp.int32, s.shape, 1)
            s = jnp.where(kj <= qi + diag_at * t, s, NEG)
        return s, v

    def plain_step(i, j, nblk, diag_at):
        s, v = scores(i, j, nblk, diag_at)
        acc_scr[...] += jnp.dot(jnp.exp2(s).astype(BF16), v, preferred_element_type=F32)

    def online_step(i, j, nblk, diag_at):
        s, v = scores(i, j, nblk, diag_at)
        m_old = m_scr[...]
        m_new = jnp.maximum(m_old, jnp.max(s, axis=1, keepdims=True))
        p = jnp.exp2(s - m_new)
        alpha = jnp.exp2(m_old - m_new)
        acc_scr[...] = acc_scr[...] * alpha + jnp.dot(p.astype(BF16), v, preferred_element_type=F32)
        m_scr[...] = m_new

    def run(step, wide, online):
        def q_tile(i, carry):
            acc_scr[...] = jnp.zeros(acc_scr.shape, F32)
            if online:
                m_scr[...] = jnp.full(m_scr.shape, NEG, F32)
            nw = lax.shift_right_logical(i, wide.bit_length() - 1)
            left = i - nw * wide

            def wide_body(jj, c):
                step(i, jj * wide, wide, None)
                return c

            lax.fori_loop(0, nw, wide_body, 0)
            for r in range(wide):
                pl.when(left == r)(lambda r=r: step(i, nw * wide, r + 1, r))
            acc = acc_scr[...]
            o_ref[pl.ds(pl.multiple_of(i * t, t), t), :] = (acc / acc[:, MLA_V_DIM:MLA_V_DIM + 1]).astype(BF16)
            return carry

        lax.fori_loop(0, nq, q_tile, 0)

    small = plain_ref[pl.program_id(0) * MLA_HEADS + pl.program_id(1)] != 0
    pl.when(small)(lambda: run(plain_step, MLA_WIDE, False))
    pl.when(jnp.logical_not(small))(lambda: run(online_step, 1, True))


def _mla(plain, qm, km, vm, batch, seq):
    n = qm.shape[0]
    t = TQ_MLA
    kmap = lambda b, h, *_: (b, h)
    return pl.pallas_call(
        _mla_kernel,
        grid_spec=pltpu.PrefetchScalarGridSpec(
            num_scalar_prefetch=1,
            grid=(batch, MLA_HEADS),
            in_specs=[pl.BlockSpec((seq, LANE), kmap),
                      pl.BlockSpec((seq, LANE), kmap),
                      pl.BlockSpec((seq, LANE), kmap)],
            out_specs=pl.BlockSpec((seq, LANE), kmap),
            scratch_shapes=[pltpu.VMEM((t, 1), F32), pltpu.VMEM((t, LANE), F32)]),
        out_shape=jax.ShapeDtypeStruct((n, MLA_HEADS * LANE), BF16),
        compiler_params=_params(("arbitrary", "arbitrary")),
        name="mla",
    )(plain, qm, km, vm)


def _swa_kernel(sink_ref, slope_ref, q_ref, kc_ref, kp_ref, vc_ref, vp_ref, o_ref, bias_scr):
    b = pl.program_id(0)
    i = pl.program_id(1)
    blk = ATTN_BLOCK

    @pl.when((b == 0) & (i == 0))
    def _():
        qi = lax.broadcasted_iota(jnp.int32, (blk, 2 * blk), 0)
        kj = lax.broadcasted_iota(jnp.int32, (blk, 2 * blk), 1)
        dist = qi - kj + blk
        valid = (dist >= 0) & (dist < blk)
        distf = dist.astype(F32)
        for hd in range(SWA_HEADS):
            bias_scr[hd] = jnp.where(valid, -(slope_ref[hd] * LOG2E) * distf, NEG)

    lane = lax.broadcasted_iota(jnp.int32, (2 * blk, LANE), 1)
    lo = lane < SWA_HEAD_DIM
    kcol = lax.broadcasted_iota(jnp.int32, (blk, 2 * blk), 1)
    first_pen = jnp.where(kcol < blk, jnp.where(i == 0, NEG, 0.0), 0.0)
    zero = jnp.zeros((2 * blk, LANE), BF16)

    for qb in range(SWA_QB):
        if qb == 0:
            kprev, vprev = kp_ref[...], vp_ref[...]
        else:
            kprev = kc_ref[(qb - 1) * blk:qb * blk, :]
            vprev = vc_ref[(qb - 1) * blk:qb * blk, :]
        kcat = jnp.concatenate([kprev, kc_ref[qb * blk:(qb + 1) * blk, :]], axis=0)
        vcat = jnp.concatenate([vprev, vc_ref[qb * blk:(qb + 1) * blk, :]], axis=0)
        for g in range(SWA_KV_HEADS):
            kg = kcat[:, g * LANE:(g + 1) * LANE]
            vg = vcat[:, g * LANE:(g + 1) * LANE]
            halves = ((jnp.where(lo, kg, zero), jnp.where(lo, vg, zero)),
                      (jnp.where(lo, zero, kg), jnp.where(lo, zero, vg)))
            for pp in range(2):
                pr = g * 2 + pp
                qpair = q_ref[qb * blk:(qb + 1) * blk, pr * LANE:(pr + 1) * LANE]
                out = None
                for half in range(2):
                    hd = pr * 2 + half
                    kx, vx = halves[half]
                    s = lax.dot_general(qpair, kx, (((1,), (1,)), ((), ())), preferred_element_type=F32)
                    s = s + bias_scr[hd]
                    if qb == 0:
                        s = s + first_pen
                    sink = sink_ref[hd] * LOG2E
                    m = jnp.maximum(jnp.max(s, axis=1, keepdims=True), sink)
                    p = jnp.exp2(s - m)
                    denom = jnp.sum(p, axis=1, keepdims=True) + jnp.exp2(sink - m)
                    o = jnp.dot(p.astype(BF16), vx, preferred_element_type=F32) * (1.0 / denom)
                    out = o if out is None else out + o
                o_ref[qb * blk:(qb + 1) * blk, pr * LANE:(pr + 1) * LANE] = out.astype(BF16)


def _swa(sinks, slopes, qa, ka2, va2, batch, seq):
    n = qa.shape[0]
    blk = ATTN_BLOCK
    rows = SWA_QB * blk
    steps = seq // rows
    cur = lambda b, i: (b * steps + i, 0)
    prev = lambda b, i: (b * (seq // blk) + jnp.maximum(i * SWA_QB - 1, 0), 0)
    smem = pl.BlockSpec(memory_space=pltpu.SMEM)
    return pl.pallas_call(
        _swa_kernel,
        grid=(batch, steps),
        in_specs=[smem, smem,
                  pl.BlockSpec((rows, 512), cur),
                  pl.BlockSpec((rows, 256), cur),
                  pl.BlockSpec((blk, 256), prev),
                  pl.BlockSpec((rows, 256), cur),
                  pl.BlockSpec((blk, 256), prev)],
        out_specs=pl.BlockSpec((rows, 512), cur),
        out_shape=jax.ShapeDtypeStruct((n, 512), BF16),
        scratch_shapes=[pltpu.VMEM((SWA_HEADS, blk, 2 * blk), F32)],
        compiler_params=_params(("arbitrary", "arbitrary")),
        name="swa",
    )(sinks, slopes, qa, ka2, ka2, va2, va2)


def _outproj_kernel(ya_ref, yb_ref, ga_ref, gb_ref, x_ref, g1_ref, sh_ref, sc_ref, nm_ref, wa_ref, wb_ref,
                    wo_ref, wrh_ref, wrl_ref, br_ref,
                    x1_ref, h2_ref, route_ref, routet_ref, g8_ref):
    tm = x_ref.shape[0]
    a = jnp.dot(ya_ref[...], wa_ref[...], preferred_element_type=F32)
    bm = jnp.dot(yb_ref[...], wb_ref[...], preferred_element_type=F32)
    mixed = (ga_ref[...].astype(F32) * a + gb_ref[...].astype(F32) * bm).astype(BF16)
    x1 = x_ref[...] + g1_ref[0] * jnp.dot(mixed, wo_ref[...], preferred_element_type=F32)
    x1_ref[...] = x1
    h2 = _rms(x1) * nm_ref[...] * (1.0 + sc_ref[0]) + sh_ref[0]
    hi = h2.astype(BF16)
    h2_ref[:, 0:D_MODEL] = hi

    lo = (h2 - hi.astype(F32)).astype(BF16)
    logits = (jnp.dot(hi, wrh_ref[...], preferred_element_type=F32)
              + jnp.dot(lo, wrh_ref[...], preferred_element_type=F32)
              + jnp.dot(hi, wrl_ref[...], preferred_element_type=F32)) + br_ref[...]

    lane = lax.broadcasted_iota(jnp.int32, (tm, LANE), 1).astype(F32)
    work = jnp.where(lane < N_EXPERTS, logits, -jnp.inf)
    vals, idxs = [], []
    for _ in range(TOP_K):
        m = jnp.max(work, axis=1, keepdims=True)
        idx = jnp.min(jnp.where(work == m, lane, float(LANE)), axis=1, keepdims=True)
        vals.append(m)
        idxs.append(idx)
        work = jnp.where(lane == idx, -jnp.inf, work)
    exps = [jnp.exp(v - vals[0]) for v in vals]
    tot = exps[0] + exps[1] + exps[2] + exps[3]

    onehot = jnp.zeros((tm, LANE), F32)
    for idx in idxs:
        onehot = onehot + (lane == idx).astype(F32)
    r = lax.broadcasted_iota(jnp.int32, (tm, tm), 0)
    c = lax.broadcasted_iota(jnp.int32, (tm, tm), 1)
    prefix = jnp.dot((c < r).astype(BF16), onehot.astype(BF16), preferred_element_type=F32)
    groups = jnp.ceil(jnp.sum(onehot, axis=0, keepdims=True) * (1.0 / GROUP_ALIGN))
    er = lax.broadcasted_iota(jnp.int32, (LANE, LANE), 0)
    ec = lax.broadcasted_iota(jnp.int32, (LANE, LANE), 1)
    before = jnp.dot(jnp.broadcast_to(groups, (8, LANE)).astype(BF16), (er < ec).astype(BF16),
                     preferred_element_type=F32)[0:1, :]
    slot_e = before * GROUP_ALIGN + prefix

    route = jnp.zeros((tm, LANE), F32)
    wext = jnp.zeros((tm, LANE), F32)
    for k in range(TOP_K):
        sk = jnp.sum(jnp.where(lane == idxs[k], slot_e, 0.0), axis=1, keepdims=True)
        sub = jnp.floor(sk * (1.0 / SLOT_SUB))
        route = jnp.where(lane == float(k), sk - sub * SLOT_SUB, route)
        route = jnp.where(lane == float(TOP_K + k), sub, route)
        wk = exps[k] / tot
        wk_hi = wk.astype(BF16).astype(F32)
        wext = jnp.where(lane == idxs[k], wk_hi, wext)
        wext = jnp.where(lane == idxs[k] + float(N_EXPERTS), wk - wk_hi, wext)
    h2_ref[:, D_MODEL:XS_COLS] = wext.astype(BF16)
    route_ref[...] = route
    routet_ref[...] = route.T[0:8, :]
    g8_ref[...] = jnp.broadcast_to(groups * GROUP_ALIGN, (8, LANE))


def _outproj(ya, yb, ga, gb, x2, g1, sh2, sc2, nm, wa, wb_p, wo, wr_hi, wr_lo, br_p, seq):
    n = x2.shape[0]
    tm = ROUTE_TILE
    per_b = seq // tm
    hw = MLA_HEADS * LANE
    row = lambda i: (i, 0)
    fix = lambda i: (0, 0)
    bsel = lambda i: (i // per_b, 0, 0)
    return pl.pallas_call(
        _outproj_kernel,
        grid=(n // tm,),
        in_specs=[pl.BlockSpec((tm, 512), row),
                  pl.BlockSpec((tm, hw), row),
                  pl.BlockSpec((tm, D_MODEL), row),
                  pl.BlockSpec((tm, D_MODEL), row),
                  pl.BlockSpec((tm, D_MODEL), row),
                  pl.BlockSpec((1, 1, D_MODEL), bsel),
                  pl.BlockSpec((1, 1, D_MODEL), bsel),
                  pl.BlockSpec((1, 1, D_MODEL), bsel),
                  pl.BlockSpec((1, D_MODEL), fix),
                  pl.BlockSpec((512, D_MODEL), fix),
                  pl.BlockSpec((hw, D_MODEL), fix),
                  pl.BlockSpec((D_MODEL, D_MODEL), fix),
                  pl.BlockSpec((D_MODEL, LANE), fix),
                  pl.BlockSpec((D_MODEL, LANE), fix),
                  pl.BlockSpec((1, LANE), fix)],
        out_specs=[pl.BlockSpec((tm, D_MODEL), row),
                   pl.BlockSpec((tm, XS_COLS), row),
                   pl.BlockSpec((tm, LANE), row),
                   pl.BlockSpec((8, tm), lambda i: (0, i)),
                   pl.BlockSpec((8, LANE), row)],
        out_shape=[jax.ShapeDtypeStruct((n, D_MODEL), F32),
                   jax.ShapeDtypeStruct((n, XS_COLS), BF16),
                   jax.ShapeDtypeStruct((n, LANE), F32),
                   jax.ShapeDtypeStruct((8, n), F32),
                   jax.ShapeDtypeStruct((n // tm * 8, LANE), F32)],
        compiler_params=_params(("arbitrary",)),
        name="outproj",
    )(ya, yb, ga, gb, x2, g1, sh2, sc2, nm, wa, wb_p, wo, wr_hi, wr_lo, br_p)


def _start_runs(tile, g8_ref, loff_ref, goff_ref, make_copy):
    for e in range(N_EXPERTS):
        g = g8_ref[tile * N_EXPERTS + e]
        lo = loff_ref[tile * N_EXPERTS + e]
        go = goff_ref[tile * N_EXPERTS + e]
        def go_copy(lo=lo, go=go, g=g):
            make_copy(pl.multiple_of(lo, GROUP_ALIGN), pl.multiple_of(go, GROUP_ALIGN),
                      pl.multiple_of(g, GROUP_ALIGN)).start()

        pl.when(g > 0)(go_copy)


def _dispatch_kernel(g8_ref, loff_ref, goff_ref, tot_ref, ends_ref, h_ref, rt_ref, xs_ref, buf, zbuf, sem, zsem):
    t = pl.program_id(0)
    nt = pl.num_programs(0)
    tm = h_ref.shape[0]
    slot = lax.rem(t, 2)

    @pl.when(t == 0)
    def _():
        zbuf[...] = jnp.zeros(zbuf.shape, zbuf.dtype)

        def fill(start, rows):
            return pltpu.make_async_copy(zbuf.at[pl.ds(0, rows), :], xs_ref.at[pl.ds(start, rows), :], zsem)

        def pads(e):
            real_end = pl.multiple_of(ends_ref[N_EXPERTS + e], GROUP_ALIGN)
            return real_end, pl.multiple_of(ends_ref[e] - real_end, GROUP_ALIGN)

        used = lax.shift_right_logical(ends_ref[N_EXPERTS - 1], MOE_ROWS.bit_length() - 1)
        n_blocks = xs_ref.shape[0] // MOE_ROWS
        tail = lambda b: fill(pl.multiple_of(b * MOE_ROWS, MOE_ROWS), MOE_ROWS)
        for e in range(N_EXPERTS):
            pl.when(pads(e)[1] > 0)(lambda e=e: fill(*pads(e)).start())
        lax.fori_loop(used, n_blocks, lambda b, c: (tail(b).start(), c)[1], 0)
        for e in range(N_EXPERTS):
            pl.when(pads(e)[1] > 0)(lambda e=e: fill(*pads(e)).wait())
        lax.fori_loop(used, n_blocks, lambda b, c: (tail(b).wait(), c)[1], 0)

    hb = h_ref[...]
    low = [rt_ref[k:k + 1, :] for k in range(TOP_K)]
    sub = [rt_ref[TOP_K + k:TOP_K + k + 1, :] for k in range(TOP_K)]
    srow = lax.broadcasted_iota(jnp.int32, (SLOT_SUB, tm), 0).astype(F32).astype(BF16)
    one = jnp.ones((SLOT_SUB, tm), BF16)

    for ci in range(SLOTS_PER_TILE // SLOT_CHUNK):
        parts = []
        for b in range(SLOT_CHUNK // SLOT_SUB):
            blk = float(ci * (SLOT_CHUNK // SLOT_SUB) + b)
            own = jnp.zeros((SLOT_SUB, tm), BF16)
            for k in range(TOP_K):
                here = jnp.where(sub[k] == blk, low[k], -1.0).astype(BF16)
                own = jnp.where(srow == here, one, own)
            parts.append(own)
        own = jnp.concatenate(parts, axis=0)
        buf[slot, ci * SLOT_CHUNK:(ci + 1) * SLOT_CHUNK, :] = jnp.dot(own, hb, preferred_element_type=F32)

    def copy(s, lo, go, rows):
        return pltpu.make_async_copy(buf.at[s, pl.ds(lo, rows), :], xs_ref.at[pl.ds(go, rows), :], sem.at[s])

    _start_runs(t, g8_ref, loff_ref, goff_ref, functools.partial(copy, slot))

    def drain(tile, s):
        rows = pl.multiple_of(tot_ref[tile], GROUP_ALIGN)
        copy(s, 0, 0, rows).wait()

    pl.when(t > 0)(lambda: drain(t - 1, 1 - slot))
    pl.when(t == nt - 1)(lambda: drain(t, slot))


def _dispatch(g8f, lofff, gofff, totf, ends, h2, routet, p_rows):
    n = h2.shape[0]
    tm = ROUTE_TILE
    return pl.pallas_call(
        _dispatch_kernel,
        grid_spec=pltpu.PrefetchScalarGridSpec(
            num_scalar_prefetch=5,
            grid=(n // tm,),
            in_specs=[pl.BlockSpec((tm, XS_COLS), lambda i, *_: (i, 0)),
                      pl.BlockSpec((8, tm), lambda i, *_: (0, i))],
            out_specs=pl.BlockSpec(memory_space=pl.ANY),
            scratch_shapes=[pltpu.VMEM((2, SLOTS_PER_TILE, XS_COLS), F32),
                            pltpu.VMEM((MOE_ROWS, XS_COLS), F32),
                            pltpu.SemaphoreType.DMA((2,)), pltpu.SemaphoreType.DMA(())]),
        out_shape=jax.ShapeDtypeStruct((p_rows, XS_COLS), F32),
        compiler_params=_params(("arbitrary",)),
        name="dispatch",
    )(g8f, lofff, gofff, totf, ends, h2, routet)


def _experts_kernel(be_ref, nu_ref, bv_ref, nx_ref, xs_ref, w1_hbm, b1_ref, w2_hbm, b2_ref, ys_ref,
                    w1buf, w2buf, wsem, ord_ref):
    del nu_ref
    i = pl.program_id(0)
    used = bv_ref[i]
    e = be_ref[i]

    def fetch(ex, s):
        return (pltpu.make_async_copy(w1_hbm.at[ex], w1buf.at[s], wsem.at[0, s]),
                pltpu.make_async_copy(w2_hbm.at[ex], w2buf.at[s], wsem.at[1, s]))

    @pl.when(i == 0)
    def _():
        ord_ref[0] = 0
        for c in fetch(e, 0):
            c.start()

    @pl.when((used > 0) & ((i == 0) | (e != be_ref[jnp.maximum(i - 1, 0)])))
    def _():
        @pl.when(i > 0)
        def _():
            ord_ref[0] = ord_ref[0] + 1

        s = lax.rem(ord_ref[0], 2)
        for c in fetch(e, s):
            c.wait()
        nxt = nx_ref[i]

        @pl.when(nxt >= 0)
        def _():
            for c in fetch(nxt, 1 - s):
                c.start()

    slot = lax.rem(ord_ref[0], 2)

    def mlp(n_rows):
        xb = xs_ref[0:n_rows, 0:D_MODEL].astype(BF16)
        wext = xs_ref[0:n_rows, D_MODEL:XS_COLS]
        lane = lax.broadcasted_iota(jnp.int32, wext.shape, 1)
        wcol = jnp.sum(jnp.where((lane == e) | (lane == e + N_EXPERTS), wext, 0.0), axis=1, keepdims=True)
        hcat = jnp.dot(xb, w1buf[slot].astype(BF16), preferred_element_type=F32) + b1_ref[0]
        x_glu = jnp.minimum(hcat[:, :D_EXPERT], SWIGLU_LIMIT)
        x_lin = jnp.clip(hcat[:, D_EXPERT:], -SWIGLU_LIMIT, SWIGLU_LIMIT)
        act = (x_glu * jax.nn.sigmoid(SWIGLU_ALPHA * x_glu) * (x_lin + 1.0)).astype(BF16)
        ys_ref[0:n_rows, :] = (jnp.dot(act, w2buf[slot].astype(BF16), preferred_element_type=F32) + b2_ref[0]) * wcol

    quarter = MOE_ROWS // 4
    for q in range(5):
        n_rows = q * quarter

        def part(n_rows=n_rows):
            if n_rows:
                mlp(n_rows)
            if n_rows < MOE_ROWS:
                ys_ref[n_rows:, :] = jnp.zeros((MOE_ROWS - n_rows, D_MODEL), F32)

        pl.when((used > n_rows - quarter) & (used <= n_rows))(part)


def _experts(block_e, n_used, block_used, next_e, xs, w1, b1, w2, b2):
    p = xs.shape[0]
    mb = MOE_ROWS
    last = lambda i, nu: jnp.minimum(i, nu[0] - 1)
    rowmap = lambda i, be, nu, bv, nx: (last(i, nu), 0)
    wmap = lambda i, be, nu, bv, nx: (be[last(i, nu)], 0, 0)
    hbm = pl.BlockSpec(memory_space=pl.ANY)
    return pl.pallas_call(
        _experts_kernel,
        grid_spec=pltpu.PrefetchScalarGridSpec(
            num_scalar_prefetch=4,
            grid=(p // mb,),
            in_specs=[pl.BlockSpec((mb, XS_COLS), rowmap),
                      hbm,
                      pl.BlockSpec((1, 1, 2 * D_EXPERT), wmap),
                      hbm,
                      pl.BlockSpec((1, 1, D_MODEL), wmap)],
            out_specs=pl.BlockSpec((mb, D_MODEL), lambda i, be, nu, bv, nx: (i, 0)),
            scratch_shapes=[pltpu.VMEM((2, D_MODEL, 2 * D_EXPERT), F32),
                            pltpu.VMEM((2, D_EXPERT, D_MODEL), F32),
                            pltpu.SemaphoreType.DMA((2, 2)),
                            pltpu.SMEM((1,), jnp.int32)]),
        out_shape=jax.ShapeDtypeStruct((p, D_MODEL), F32),
        compiler_params=_params(("arbitrary",)),
        name="experts",
    )(block_e, n_used, block_used, next_e, xs, w1, b1, w2, b2)


def _combine_kernel(g8_ref, loff_ref, goff_ref, tot_ref, ys_ref, route_ref, x1_ref, g2_ref, fn_ref, o_ref, stg, sem):
    t = pl.program_id(0)
    nt = pl.num_programs(0)
    tm = x1_ref.shape[0]
    slot = lax.rem(t, 2)

    def copy(s, lo, go, rows):
        return pltpu.make_async_copy(ys_ref.at[pl.ds(go, rows), :], stg.at[s, pl.ds(lo, rows), :], sem.at[s])

    @pl.when(t == 0)
    def _():
        stg[...] = jnp.zeros(stg.shape, F32)
        _start_runs(t, g8_ref, loff_ref, goff_ref, functools.partial(copy, 0))

    pl.when(t + 1 < nt)(lambda: _start_runs(t + 1, g8_ref, loff_ref, goff_ref, functools.partial(copy, 1 - slot)))
    copy(slot, 0, 0, pl.multiple_of(tot_ref[t], GROUP_ALIGN)).wait()

    route = route_ref[...]
    low = [jnp.broadcast_to(route[:, k:k + 1], (tm, SLOT_SUB)).astype(BF16) for k in range(TOP_K)]
    sub = [jnp.broadcast_to(route[:, TOP_K + k:TOP_K + k + 1], (tm, SLOT_SUB)).astype(BF16) for k in range(TOP_K)]
    scol = lax.broadcasted_iota(jnp.int32, (tm, SLOT_SUB), 1).astype(F32).astype(BF16)
    one = jnp.ones((tm, SLOT_SUB), BF16)
    never = jnp.full((tm, SLOT_SUB), -1.0, BF16)

    y = None
    for ci in range(SLOTS_PER_TILE // SLOT_CHUNK):
        parts = []
        for b in range(SLOT_CHUNK // SLOT_SUB):
            blk = float(ci * (SLOT_CHUNK // SLOT_SUB) + b)
            own = jnp.zeros((tm, SLOT_SUB), BF16)
            for k in range(TOP_K):
                own = jnp.where(scol == jnp.where(sub[k] == blk, low[k], never), one, own)
            parts.append(own)
        own = jnp.concatenate(parts, axis=1)
        rows = stg[slot, ci * SLOT_CHUNK:(ci + 1) * SLOT_CHUNK, :].astype(BF16)
        part = jnp.dot(own, rows, preferred_element_type=F32)
        y = part if y is None else y + part
    x = x1_ref[...] + g2_ref[0] * y
    o_ref[...] = _rms(x) * fn_ref[...]


def _combine(g8f, lofff, gofff, totf, ys, route, x1, g2, fn, seq):
    n = x1.shape[0]
    tm = ROUTE_TILE
    per_b = seq // tm
    return pl.pallas_call(
        _combine_kernel,
        grid_spec=pltpu.PrefetchScalarGridSpec(
            num_scalar_prefetch=4,
            grid=(n // tm,),
            in_specs=[pl.BlockSpec(memory_space=pl.ANY),
                      pl.BlockSpec((tm, LANE), lambda i, *_: (i, 0)),
                      pl.BlockSpec((tm, D_MODEL), lambda i, *_: (i, 0)),
                      pl.BlockSpec((1, 1, D_MODEL), lambda i, *_: (i // per_b, 0, 0)),
                      pl.BlockSpec((1, D_MODEL), lambda i, *_: (0, 0))],
            out_specs=pl.BlockSpec((tm, D_MODEL), lambda i, *_: (i, 0)),
            scratch_shapes=[pltpu.VMEM((2, SLOTS_PER_TILE, D_MODEL), F32), pltpu.SemaphoreType.DMA((2,))]),
        out_shape=jax.ShapeDtypeStruct((n, D_MODEL), F32),
        compiler_params=_params(("arbitrary",)),
        name="combine",
    )(g8f, lofff, gofff, totf, ys, route, x1, g2, fn)


def _split_w_in(w_in):
    o_kr = _W_MIX_COLS
    w_kr = jnp.pad(w_in[:, o_kr:o_kr + MLA_ROPE_DIM],
                   ((0, 0), (MLA_NOPE_DIM, LANE - MLA_NOPE_DIM - MLA_ROPE_DIM)))
    return w_in[:, :o_kr].astype(BF16), w_kr.astype(BF16), w_in[:, o_kr + MLA_ROPE_DIM:].astype(BF16)


def _pack_heads(w, width, pieces):
    rows = w.shape[0]
    w3 = w.reshape(rows, MLA_HEADS, width)
    cols = [w3[:, :, a:b] for a, b in pieces]
    used = sum(b - a for a, b in pieces)
    cols.append(jnp.zeros((rows, MLA_HEADS, LANE - used), w.dtype))
    return jnp.concatenate(cols, axis=2).reshape(rows, MLA_HEADS * LANE)


def kernel(x, c, positions, w_ada, b_ada, norm_mix, norm_ffn, w_in, sinks, q_norm, kv_norm, w_uq, w_uk, w_uv,
           w_branch_a, w_branch_b, w_out, w_router, b_router, w_moe1, b_moe1, w_moe2, b_moe2, final_norm):
    batch, seq, d = x.shape
    n = batch * seq
    assert d == D_MODEL and w_ada.shape[0] == 1
    assert seq % TQ_MLA == 0 and seq % (SWA_QB * ATTN_BLOCK) == 0 and seq % TM_PROJ == 0 and seq % ROUTE_TILE == 0
    hw = MLA_HEADS * LANE
    qk = MLA_NOPE_DIM + MLA_ROPE_DIM

    w_mix, w_kr, w_gates = _split_w_in(w_in[0])
    wuq_p = _pack_heads(w_uq[0], qk, [(0, qk)]).astype(BF16)
    wuk_p = _pack_heads(w_uk[0], MLA_NOPE_DIM, [(0, MLA_NOPE_DIM)]).astype(BF16)
    wuv_p = _pack_heads(w_uv[0], MLA_V_DIM, [(0, MLA_V_DIM)]).astype(BF16)
    one_row = jnp.tile((jnp.arange(LANE) == MLA_V_DIM).astype(F32), MLA_HEADS)[None, :]
    freqs = ROPE_THETA ** (-jnp.arange(0, MLA_ROPE_DIM, 2, dtype=F32) / MLA_ROPE_DIM)
    frq = jnp.concatenate([jnp.zeros((MLA_NOPE_DIM,), F32), freqs, freqs,
                           jnp.zeros((LANE - qk,), F32)])[None, :]
    wb3 = w_branch_b[0].reshape(MLA_HEADS, MLA_V_DIM, D_MODEL)
    wb_p = jnp.concatenate([wb3, jnp.zeros((MLA_HEADS, LANE - MLA_V_DIM, D_MODEL), F32)],
                           axis=1).reshape(hw, D_MODEL).astype(BF16)
    wa = w_branch_a[0].astype(BF16)
    wo = w_out[0].astype(BF16)
    wr = jnp.pad(w_router[0], ((0, 0), (0, LANE - N_EXPERTS)))
    wr_hi = wr.astype(BF16)
    wr_lo = (wr - wr_hi.astype(F32)).astype(BF16)
    br_p = jnp.pad(b_router[0], (0, LANE - N_EXPERTS))[None, :]
    b1 = b_moe1[0][:, None, :]
    b2 = b_moe2[0][:, None, :]
    slopes = jnp.asarray(np.exp2(-8.0 * np.arange(1, SWA_HEADS + 1) / SWA_HEADS), dtype=F32)

    c8 = jnp.pad(c, ((0, 8 - batch), (0, 0)))
    mod = _ada(c8, w_ada[0], b_ada[0][None, :])[:batch]
    sh1, sc1, g1, sh2, sc2, g2 = [m[:, None, :] for m in jnp.split(mod, 6, axis=-1)]

    x2 = x.reshape(n, D_MODEL)
    pos2 = positions.reshape(n, 1).astype(jnp.int32)
    qa, ka2, va2, qm, km, vm, ga, gb, stats = _inproj(
        x2, pos2, sh1, sc1, norm_mix, w_mix, w_kr, w_gates, q_norm, kv_norm, wuq_p, wuk_p, wuv_p, frq, one_row, seq)
    st = stats.reshape(batch, seq // TM_PROJ, 8, LANE)
    bound_sq = jnp.max(st[:, :, :, 0], axis=1) * jnp.max(st[:, :, :, 1], axis=1)
    plain = (bound_sq <= MLA_PLAIN_BOUND ** 2).astype(jnp.int32).reshape(-1)
    yb = _mla(plain, qm, km, vm, batch, seq)
    ya = _swa(sinks[0], slopes, qa, ka2, va2, batch, seq)

    x1, h2, route, routet, g8 = _outproj(
        ya, yb, ga, gb, x2, g1, sh2, sc2, norm_ffn, wa, wb_p, wo, wr_hi, wr_lo, br_p, seq)

    n_tiles = n // ROUTE_TILE
    g8t = g8.reshape(n_tiles, 8, LANE)[:, 0, :N_EXPERTS].astype(jnp.int32)
    loff = jnp.cumsum(g8t, axis=1) - g8t
    padded = ((jnp.sum(g8t, axis=0) + MOE_ROWS - 1) // MOE_ROWS) * MOE_ROWS
    pends = jnp.cumsum(padded)
    goff = (pends - padded)[None, :] + jnp.cumsum(g8t, axis=0) - g8t
    tot = jnp.sum(g8t, axis=1).astype(jnp.int32)
    p_rows = n * TOP_K + n_tiles * N_EXPERTS * (GROUP_ALIGN - 1) + N_EXPERTS * (MOE_ROWS - 1)
    p_rows = -(-p_rows // MOE_ROWS) * MOE_ROWS
    n_blocks = p_rows // MOE_ROWS
    block_start = jnp.arange(n_blocks, dtype=jnp.int32) * MOE_ROWS
    block_e = jnp.minimum(jnp.sum((pends[None, :] <= block_start[:, None]).astype(jnp.int32), axis=1),
                          N_EXPERTS - 1).astype(jnp.int32)
    n_used = (pends[-1:] // MOE_ROWS).astype(jnp.int32)
    eids = jnp.arange(N_EXPERTS, dtype=jnp.int32)
    of_block = block_e[:, None] == eids[None, :]
    pick = lambda table: jnp.sum(jnp.where(of_block, table[None, :], 0), axis=1).astype(jnp.int32)
    real_end = pends - padded + jnp.sum(g8t, axis=0)
    block_used = jnp.clip(pick(real_end) - block_start, 0, MOE_ROWS).astype(jnp.int32)
    later = (eids[None, :] > eids[:, None]) & (padded[None, :] > 0)
    after = jnp.min(jnp.where(later, eids[None, :], N_EXPERTS), axis=1)
    next_e = pick(jnp.where(after == N_EXPERTS, -1, after))
    tabs = (g8t.reshape(-1), loff.reshape(-1).astype(jnp.int32), goff.reshape(-1).astype(jnp.int32), tot)

    xs = _dispatch(*tabs, jnp.concatenate([pends, real_end]).astype(jnp.int32), h2, routet, p_rows)
    ys = _experts(block_e, n_used, block_used, next_e, xs, w_moe1[0], b1, w_moe2[0], b2)
    out = _combine(*tabs, ys, route, x1, g2, final_norm[None, :], seq)
    return out.reshape(batch, seq, D_MODEL)
```

```python
import functools

import numpy as np
import jax
import jax.numpy as jnp
from jax import lax
from jax.experimental import pallas as pl
from jax.experimental.pallas import tpu as pltpu

D_MODEL = 1024
SWA_HEADS = 8
SWA_KV_HEADS = 2
SWA_HEAD_DIM = 64
ATTN_BLOCK = 128
MLA_HEADS = 8
MLA_Q_RANK = 384
MLA_KV_RANK = 256
MLA_NOPE_DIM = 64
MLA_ROPE_DIM = 32
MLA_V_DIM = 64
ROPE_THETA = 10000.0
N_EXPERTS = 32
TOP_K = 4
D_EXPERT = 1024
SWIGLU_LIMIT = 7.0
SWIGLU_ALPHA = 1.702
NORM_EPS = 1e-6

LANE = 128
LOG2E = 1.4426950408889634
NEG = -1e30
SWA_QSCALE = SWA_HEAD_DIM ** -0.5 * LOG2E
MLA_QSCALE = (MLA_NOPE_DIM + MLA_ROPE_DIM) ** -0.5 * LOG2E
HALF_ROPE = MLA_ROPE_DIM // 2
MLA_PLAIN_BOUND = 60.0
MLA_WIDE = 8

BF16 = jnp.bfloat16
F32 = jnp.float32

TM_PROJ = 512
TQ_MLA = 512
SWA_QB = 4
MOE_ROWS = 512
ROUTE_TILE = 512
GROUP_ALIGN = 8
SLOTS_PER_TILE = ROUTE_TILE * TOP_K + N_EXPERTS * GROUP_ALIGN
SLOT_SUB = 256
SLOT_CHUNK = 3 * SLOT_SUB
XS_COLS = D_MODEL + LANE
VMEM_LIMIT = 56 * 1024 * 1024


def _params(sem, vmem=VMEM_LIMIT):
    return pltpu.CompilerParams(dimension_semantics=sem, vmem_limit_bytes=vmem)


def _rms(x):
    return x * lax.rsqrt(jnp.mean(x * x, axis=-1, keepdims=True) + NORM_EPS)


def _ada_kernel(c_ref, w_ref, b_ref, o_ref):
    c = c_ref[...]
    a = (c * jax.nn.sigmoid(c)).astype(BF16)
    o_ref[...] = jnp.dot(a, w_ref[...].astype(BF16), preferred_element_type=F32) + b_ref[...]


def _ada(c8, w_ada, b_ada):
    n_out = w_ada.shape[1]
    return pl.pallas_call(
        _ada_kernel,
        grid=(n_out // D_MODEL,),
        in_specs=[pl.BlockSpec((8, D_MODEL), lambda j: (0, 0)),
                  pl.BlockSpec((D_MODEL, D_MODEL), lambda j: (0, j)),
                  pl.BlockSpec((1, D_MODEL), lambda j: (0, j))],
        out_specs=pl.BlockSpec((8, D_MODEL), lambda j: (0, j)),
        out_shape=jax.ShapeDtypeStruct((8, n_out), F32),
        compiler_params=_params(("arbitrary",)),
        name="ada",
    )(c8, w_ada, b_ada)


_C_QA = (0, 512)
_C_KV = (512, 768)
_C_LAT = (768, 1408)
_W_MIX_COLS = 1408
_ROPE_LO = (MLA_NOPE_DIM, MLA_NOPE_DIM + HALF_ROPE)
_ROPE_HI = (MLA_NOPE_DIM + HALF_ROPE, MLA_NOPE_DIM + MLA_ROPE_DIM)


def _inproj_kernel(x_ref, pos_ref, sh_ref, sc_ref, nm_ref, win_ref, wkr_ref, wg_ref, qn_ref, kvn_ref, wuq_ref, wuk_ref,
                   wuv_ref, frq_ref, one_ref,
                   qa_ref, ka_ref, va_ref, qm_ref, km_ref, vm_ref, ga_ref, gb_ref, st_ref):
    tm = x_ref.shape[0]
    x = x_ref[...]
    h = (_rms(x) * nm_ref[...] * (1.0 + sc_ref[0]) + sh_ref[0]).astype(BF16)

    def proj(c):
        return jnp.dot(h, win_ref[:, c[0]:c[1]], preferred_element_type=F32)

    lane = lax.broadcasted_iota(jnp.int32, (tm, LANE), 1)
    first = lane < SWA_HEAD_DIM

    def twice(t):
        r = pltpu.roll(t, SWA_HEAD_DIM, axis=1)
        return jnp.concatenate([jnp.where(first, t, r), jnp.where(first, r, t)], axis=1)

    kv = proj(_C_KV)
    ka_ref[...] = twice(kv[:, :LANE]).astype(BF16)
    va_ref[...] = twice(kv[:, LANE:]).astype(BF16)

    ang = pos_ref[...].astype(F32) * frq_ref[...]
    cs = jnp.cos(ang)
    sn = jnp.sin(ang)
    sn_lo = jnp.where((lane >= _ROPE_LO[0]) & (lane < _ROPE_LO[1]), -sn, 0.0)
    sn_hi = jnp.where((lane >= _ROPE_HI[0]) & (lane < _ROPE_HI[1]), sn, 0.0)

    def rotary(t):
        return (t * cs + pltpu.roll(t, LANE - HALF_ROPE, axis=1) * sn_lo
                + pltpu.roll(t, HALF_ROPE, axis=1) * sn_hi)

    lat = proj(_C_LAT)
    cqn = (_rms(lat[:, 0:MLA_Q_RANK]) * qn_ref[...]).astype(BF16)
    ckvn = (_rms(lat[:, MLA_Q_RANK:MLA_Q_RANK + MLA_KV_RANK]) * kvn_ref[...]).astype(BF16)
    krr = rotary(jnp.dot(h, wkr_ref[...], preferred_element_type=F32))
    q = jnp.dot(cqn, wuq_ref[...], preferred_element_type=F32)
    kn = jnp.dot(ckvn, wuk_ref[...], preferred_element_type=F32)

    def max_sq_norm(t):
        tf = t.astype(F32)
        return jnp.max(jnp.sum(tf * tf, axis=1, keepdims=True), axis=0, keepdims=True)

    rid = lax.broadcasted_iota(jnp.int32, (8, LANE), 0)
    lid = lax.broadcasted_iota(jnp.int32, (8, LANE), 1)
    stats = jnp.zeros((8, LANE), F32)
    for hh in range(MLA_HEADS):
        sl = slice(hh * LANE, (hh + 1) * LANE)
        qb = (rotary(q[:, sl]) * MLA_QSCALE).astype(BF16)
        kb = (kn[:, sl] + krr).astype(BF16)
        qm_ref[:, sl] = qb
        km_ref[:, sl] = kb
        stats = jnp.where((rid == hh) & (lid == 0), max_sq_norm(qb), stats)
        stats = jnp.where((rid == hh) & (lid == 1), max_sq_norm(kb), stats)
    st_ref[...] = stats
    vm_ref[...] = (jnp.dot(ckvn, wuv_ref[...], preferred_element_type=F32) + one_ref[...]).astype(BF16)
    qa_ref[...] = (proj(_C_QA) * SWA_QSCALE).astype(BF16)
    ga_ref[...] = jax.nn.sigmoid(jnp.dot(h, wg_ref[:, 0:D_MODEL], preferred_element_type=F32)).astype(BF16)
    gb_ref[...] = jax.nn.sigmoid(jnp.dot(h, wg_ref[:, D_MODEL:], preferred_element_type=F32)).astype(BF16)


def _inproj(x2, pos2, sh1, sc1, nm, w_mix, w_kr, w_gates, qn, kvn, wuq_p, wuk_p, wuv_p, frq, one_row, seq):
    n = x2.shape[0]
    tm = TM_PROJ
    per_b = seq // tm
    hw = MLA_HEADS * LANE
    row = lambda i: (i, 0)
    fix = lambda i: (0, 0)
    bsel = lambda i: (i // per_b, 0, 0)
    widths = (512, 256, 256, hw, hw, hw, D_MODEL, D_MODEL)
    return pl.pallas_call(
        _inproj_kernel,
        grid=(n // tm,),
        in_specs=[pl.BlockSpec((tm, D_MODEL), row),
                  pl.BlockSpec((tm, 1), row),
                  pl.BlockSpec((1, 1, D_MODEL), bsel),
                  pl.BlockSpec((1, 1, D_MODEL), bsel),
                  pl.BlockSpec((1, D_MODEL), fix),
                  pl.BlockSpec((D_MODEL, _W_MIX_COLS), fix),
                  pl.BlockSpec((D_MODEL, LANE), fix),
                  pl.BlockSpec((D_MODEL, 2 * D_MODEL), fix),
                  pl.BlockSpec((1, MLA_Q_RANK), fix),
                  pl.BlockSpec((1, MLA_KV_RANK), fix),
                  pl.BlockSpec((MLA_Q_RANK, hw), fix),
                  pl.BlockSpec((MLA_KV_RANK, hw), fix),
                  pl.BlockSpec((MLA_KV_RANK, hw), fix),
                  pl.BlockSpec((1, LANE), fix),
                  pl.BlockSpec((1, hw), fix)],
        out_specs=[pl.BlockSpec((tm, w), row) for w in widths] + [pl.BlockSpec((8, LANE), row)],
        out_shape=[jax.ShapeDtypeStruct((n, w), BF16) for w in widths]
        + [jax.ShapeDtypeStruct((n // tm * 8, LANE), F32)],
        compiler_params=_params(("arbitrary",)),
        name="inproj",
    )(x2, pos2, sh1, sc1, nm, w_mix, w_kr, w_gates, qn, kvn, wuq_p, wuk_p, wuv_p, frq, one_row)


def _mla_kernel(plain_ref, q_ref, k_ref, v_ref, o_ref, m_scr, acc_scr):
    t = TQ_MLA
    nq = q_ref.shape[0] // t

    def scores(i, j, nblk, diag_at):
        q = q_ref[pl.ds(pl.multiple_of(i * t, t), t), :]
        start = pl.multiple_of(j * t, t)
        k = k_ref[pl.ds(start, nblk * t), :]
        v = v_ref[pl.ds(start, nblk * t), :]
        s = lax.dot_general(q, k, (((1,), (1,)), ((), ())), preferred_element_type=F32)
        if diag_at is not None:
            qi = lax.broadcasted_iota(jnp.int32, s.shape, 0)
            kj = lax.broadcasted_iota(jnp.int32, s.shape, 1)
            s = jnp.where(kj <= qi + diag_at * t, s, NEG)
        return s, v

    def plain_step(i, j, nblk, diag_at):
        s, v = scores(i, j, nblk, diag_at)
        acc_scr[...] += jnp.dot(jnp.exp2(s).astype(BF16), v, preferred_element_type=F32)

    def online_step(i, j, nblk, diag_at):
        s, v = scores(i, j, nblk, diag_at)
        m_old = m_scr[...]
        m_new = jnp.maximum(m_old, jnp.max(s, axis=1, keepdims=True))
        p = jnp.exp2(s - m_new)
        alpha = jnp.exp2(m_old - m_new)
        acc_scr[...] = acc_scr[...] * alpha + jnp.dot(p.astype(BF16), v, preferred_element_type=F32)
        m_scr[...] = m_new

    def run(step, wide, online):
        def q_tile(i, carry):
            acc_scr[...] = jnp.zeros(acc_scr.shape, F32)
            if online:
                m_scr[...] = jnp.full(m_scr.shape, NEG, F32)
            nw = lax.shift_right_logical(i, wide.bit_length() - 1)
            left = i - nw * wide

            def wide_body(jj, c):
                step(i, jj * wide, wide, None)
                return c

            lax.fori_loop(0, nw, wide_body, 0)
            for r in range(wide):
                pl.when(left == r)(lambda r=r: step(i, nw * wide, r + 1, r))
            acc = acc_scr[...]
            o_ref[pl.ds(pl.multiple_of(i * t, t), t), :] = (acc / acc[:, MLA_V_DIM:MLA_V_DIM + 1]).astype(BF16)
            return carry

        lax.fori_loop(0, nq, q_tile, 0)

    small = plain_ref[pl.program_id(0) * MLA_HEADS + pl.program_id(1)] != 0
    pl.when(small)(lambda: run(plain_step, MLA_WIDE, False))
    pl.when(jnp.logical_not(small))(lambda: run(online_step, 1, True))


def _mla(plain, qm, km, vm, batch, seq):
    n = qm.shape[0]
    t = TQ_MLA
    kmap = lambda b, h, *_: (b, h)
    return pl.pallas_call(
        _mla_kernel,
        grid_spec=pltpu.PrefetchScalarGridSpec(
            num_scalar_prefetch=1,
            grid=(batch, MLA_HEADS),
            in_specs=[pl.BlockSpec((seq, LANE), kmap),
                      pl.BlockSpec((seq, LANE), kmap),
                      pl.BlockSpec((seq, LANE), kmap)],
            out_specs=pl.BlockSpec((seq, LANE), kmap),
            scratch_shapes=[pltpu.VMEM((t, 1), F32), pltpu.VMEM((t, LANE), F32)]),
        out_shape=jax.ShapeDtypeStruct((n, MLA_HEADS * LANE), BF16),
        compiler_params=_params(("arbitrary", "arbitrary")),
        name="mla",
    )(plain, qm, km, vm)


def _swa_kernel(sink_ref, slope_ref, q_ref, kc_ref, kp_ref, vc_ref, vp_ref, o_ref, bias_scr):
    b = pl.program_id(0)
    i = pl.program_id(1)
    blk = ATTN_BLOCK

    @pl.when((b == 0) & (i == 0))
    def _():
        qi = lax.broadcasted_iota(jnp.int32, (blk, 2 * blk), 0)
        kj = lax.broadcasted_iota(jnp.int32, (blk, 2 * blk), 1)
        dist = qi - kj + blk
        valid = (dist >= 0) & (dist < blk)
        distf = dist.astype(F32)
        for hd in range(SWA_HEADS):
            bias_scr[hd] = jnp.where(valid, -(slope_ref[hd] * LOG2E) * distf, NEG)

    lane = lax.broadcasted_iota(jnp.int32, (2 * blk, LANE), 1)
    lo = lane < SWA_HEAD_DIM
    kcol = lax.broadcasted_iota(jnp.int32, (blk, 2 * blk), 1)
    first_pen = jnp.where(kcol < blk, jnp.where(i == 0, NEG, 0.0), 0.0)
    zero = jnp.zeros((2 * blk, LANE), BF16)

    for qb in range(SWA_QB):
        if qb == 0:
            kprev, vprev = kp_ref[...], vp_ref[...]
        else:
            kprev = kc_ref[(qb - 1) * blk:qb * blk, :]
            vprev = vc_ref[(qb - 1) * blk:qb * blk, :]
        kcat = jnp.concatenate([kprev, kc_ref[qb * blk:(qb + 1) * blk, :]], axis=0)
        vcat = jnp.concatenate([vprev, vc_ref[qb * blk:(qb + 1) * blk, :]], axis=0)
        for g in range(SWA_KV_HEADS):
            kg = kcat[:, g * LANE:(g + 1) * LANE]
            vg = vcat[:, g * LANE:(g + 1) * LANE]
            halves = ((jnp.where(lo, kg, zero), jnp.where(lo, vg, zero)),
                      (jnp.where(lo, zero, kg), jnp.where(lo, zero, vg)))
            for pp in range(2):
                pr = g * 2 + pp
                qpair = q_ref[qb * blk:(qb + 1) * blk, pr * LANE:(pr + 1) * LANE]
                out = None
                for half in range(2):
                    hd = pr * 2 + half
                    kx, vx = halves[half]
                    s = lax.dot_general(qpair, kx, (((1,), (1,)), ((), ())), preferred_element_type=F32)
                    s = s + bias_scr[hd]
                    if qb == 0:
                        s = s + first_pen
                    sink = sink_ref[hd] * LOG2E
                    m = jnp.maximum(jnp.max(s, axis=1, keepdims=True), sink)
                    p = jnp.exp2(s - m)
                    denom = jnp.sum(p, axis=1, keepdims=True) + jnp.exp2(sink - m)
                    o = jnp.dot(p.astype(BF16), vx, preferred_element_type=F32) * (1.0 / denom)
                    out = o if out is None else out + o
                o_ref[qb * blk:(qb + 1) * blk, pr * LANE:(pr + 1) * LANE] = out.astype(BF16)


def _swa(sinks, slopes, qa, ka2, va2, batch, seq):
    n = qa.shape[0]
    blk = ATTN_BLOCK
    rows = SWA_QB * blk
    steps = seq // rows
    cur = lambda b, i: (b * steps + i, 0)
    prev = lambda b, i: (b * (seq // blk) + jnp.maximum(i * SWA_QB - 1, 0), 0)
    smem = pl.BlockSpec(memory_space=pltpu.SMEM)
    return pl.pallas_call(
        _swa_kernel,
        grid=(batch, steps),
        in_specs=[smem, smem,
                  pl.BlockSpec((rows, 512), cur),
                  pl.BlockSpec((rows, 256), cur),
                  pl.BlockSpec((blk, 256), prev),
                  pl.BlockSpec((rows, 256), cur),
                  pl.BlockSpec((blk, 256), prev)],
        out_specs=pl.BlockSpec((rows, 512), cur),
        out_shape=jax.ShapeDtypeStruct((n, 512), BF16),
        scratch_shapes=[pltpu.VMEM((SWA_HEADS, blk, 2 * blk), F32)],
        compiler_params=_params(("arbitrary", "arbitrary")),
        name="swa",
    )(sinks, slopes, qa, ka2, ka2, va2, va2)


def _outproj_kernel(ya_ref, yb_ref, ga_ref, gb_ref, x_ref, g1_ref, sh_ref, sc_ref, nm_ref, wa_ref, wb_ref,
                    wo_ref, wrh_ref, wrl_ref, br_ref,
                    x1_ref, h2_ref, wext_ref, route_ref, routet_ref, g8_ref, lg_scr):
    t = pl.program_id(0)
    tm = x_ref.shape[0]
    slot = lax.rem(t, 2)

    @pl.when(t == 0)
    def _():
        lg_scr[...] = jnp.zeros(lg_scr.shape, F32)

    pj = {}

    def stage_a():
        pj['a'] = jnp.dot(ya_ref[...], wa_ref[...], preferred_element_type=F32)

    def stage_b():
        pj['bm'] = jnp.dot(yb_ref[...], wb_ref[...], preferred_element_type=F32)

    def stage_out():
        mixed = (ga_ref[...].astype(F32) * pj['a'] + gb_ref[...].astype(F32) * pj['bm']).astype(BF16)
        pj['x1'] = x_ref[...] + g1_ref[0] * jnp.dot(mixed, wo_ref[...], preferred_element_type=F32)
        x1_ref[...] = pj['x1']

    def stage_h2():
        h2 = _rms(pj['x1']) * nm_ref[...] * (1.0 + sc_ref[0]) + sh_ref[0]
        hi = h2.astype(BF16)
        h2_ref[...] = hi
        lo = (h2 - hi.astype(F32)).astype(BF16)
        lg_scr[slot] = (jnp.dot(hi, wrh_ref[...], preferred_element_type=F32)
                        + jnp.dot(lo, wrh_ref[...], preferred_element_type=F32)
                        + jnp.dot(hi, wrl_ref[...], preferred_element_type=F32)) + br_ref[...]

    stages = (stage_a, stage_b, stage_out)
    project = stage_h2

    logits = lg_scr[1 - slot]
    lane = lax.broadcasted_iota(jnp.int32, (tm, LANE), 1).astype(F32)
    work = jnp.where(lane < N_EXPERTS, logits, -jnp.inf)
    vals, idxs = [], []
    for rnd in range(TOP_K):
        m = jnp.max(work, axis=1, keepdims=True)
        idx = jnp.min(jnp.where(work == m, lane, float(LANE)), axis=1, keepdims=True)
        vals.append(m)
        idxs.append(idx)
        work = jnp.where(lane == idx, -jnp.inf, work)
        if rnd < len(stages):
            stages[rnd]()
    exps = [jnp.exp(v - vals[0]) for v in vals]
    tot = exps[0] + exps[1] + exps[2] + exps[3]

    onehot = jnp.zeros((tm, LANE), F32)
    for idx in idxs:
        onehot = onehot + (lane == idx).astype(F32)
    r = lax.broadcasted_iota(jnp.int32, (tm, tm), 0)
    c = lax.broadcasted_iota(jnp.int32, (tm, tm), 1)
    prefix = jnp.dot((c < r).astype(BF16), onehot.astype(BF16), preferred_element_type=F32)
    groups = jnp.ceil(jnp.sum(onehot, axis=0, keepdims=True) * (1.0 / GROUP_ALIGN))
    er = lax.broadcasted_iota(jnp.int32, (LANE, LANE), 0)
    ec = lax.broadcasted_iota(jnp.int32, (LANE, LANE), 1)
    before = jnp.dot(jnp.broadcast_to(groups, (8, LANE)).astype(BF16), (er < ec).astype(BF16),
                     preferred_element_type=F32)[0:1, :]
    slot_e = before * GROUP_ALIGN + prefix

    route = jnp.zeros((tm, LANE), F32)
    wext = jnp.zeros((tm, LANE), F32)
    for k in range(TOP_K):
        sk = jnp.sum(jnp.where(lane == idxs[k], slot_e, 0.0), axis=1, keepdims=True)
        sub = jnp.floor(sk * (1.0 / SLOT_SUB))
        route = jnp.where(lane == float(k), sk - sub * SLOT_SUB, route)
        route = jnp.where(lane == float(TOP_K + k), sub, route)
        wk = exps[k] / tot
        wk_hi = wk.astype(BF16).astype(F32)
        wext = jnp.where(lane == idxs[k], wk_hi, wext)
        wext = jnp.where(lane == idxs[k] + float(N_EXPERTS), wk - wk_hi, wext)
    wext_ref[...] = wext.astype(BF16)
    route_ref[...] = route
    routet_ref[...] = route.T[0:8, :]
    g8_ref[...] = jnp.broadcast_to(groups * GROUP_ALIGN, (8, LANE))
    project()


def _outproj(ya, yb, ga, gb, x2, g1, sh2, sc2, nm, wa, wb_p, wo, wr_hi, wr_lo, br_p, seq):
    n = x2.shape[0]
    tm = ROUTE_TILE
    per_b = seq // tm
    hw = MLA_HEADS * LANE
    n_tiles = n // tm
    cur = lambda i: jnp.minimum(i, n_tiles - 1)
    prv = lambda i: jnp.maximum(i - 1, 0)
    row = lambda i: (cur(i), 0)
    rrow = lambda i: (prv(i), 0)
    fix = lambda i: (0, 0)
    bsel = lambda i: (cur(i) // per_b, 0, 0)
    return pl.pallas_call(
        _outproj_kernel,
        grid=(n_tiles + 1,),
        in_specs=[pl.BlockSpec((tm, 512), row),
                  pl.BlockSpec((tm, hw), row),
                  pl.BlockSpec((tm, D_MODEL), row),
                  pl.BlockSpec((tm, D_MODEL), row),
                  pl.BlockSpec((tm, D_MODEL), row),
                  pl.BlockSpec((1, 1, D_MODEL), bsel),
                  pl.BlockSpec((1, 1, D_MODEL), bsel),
                  pl.BlockSpec((1, 1, D_MODEL), bsel),
                  pl.BlockSpec((1, D_MODEL), fix),
                  pl.BlockSpec((512, D_MODEL), fix),
                  pl.BlockSpec((hw, D_MODEL), fix),
                  pl.BlockSpec((D_MODEL, D_MODEL), fix),
                  pl.BlockSpec((D_MODEL, LANE), fix),
                  pl.BlockSpec((D_MODEL, LANE), fix),
                  pl.BlockSpec((1, LANE), fix)],
        out_specs=[pl.BlockSpec((tm, D_MODEL), row),
                   pl.BlockSpec((tm, D_MODEL), row),
                   pl.BlockSpec((tm, LANE), rrow),
                   pl.BlockSpec((tm, LANE), rrow),
                   pl.BlockSpec((8, tm), lambda i: (0, prv(i))),
                   pl.BlockSpec((8, LANE), rrow)],
        out_shape=[jax.ShapeDtypeStruct((n, D_MODEL), F32),
                   jax.ShapeDtypeStruct((n, D_MODEL), BF16),
                   jax.ShapeDtypeStruct((n, LANE), BF16),
                   jax.ShapeDtypeStruct((n, LANE), F32),
                   jax.ShapeDtypeStruct((8, n), F32),
                   jax.ShapeDtypeStruct((n_tiles * 8, LANE), F32)],
        scratch_shapes=[pltpu.VMEM((2, tm, LANE), F32)],
        compiler_params=_params(("arbitrary",)),
        name="outproj",
    )(ya, yb, ga, gb, x2, g1, sh2, sc2, nm, wa, wb_p, wo, wr_hi, wr_lo, br_p)


def _start_runs(tile, g8_ref, loff_ref, goff_ref, make_copy):
    for e in range(N_EXPERTS):
        g = g8_ref[tile * N_EXPERTS + e]
        lo = loff_ref[tile * N_EXPERTS + e]
        go = goff_ref[tile * N_EXPERTS + e]
        def go_copy(lo=lo, go=go, g=g):
            make_copy(pl.multiple_of(lo, GROUP_ALIGN), pl.multiple_of(go, GROUP_ALIGN),
                      pl.multiple_of(g, GROUP_ALIGN)).start()

        pl.when(g > 0)(go_copy)


def _dispatch_kernel(g8_ref, loff_ref, goff_ref, tot_ref, ends_ref, h_ref, w_ref, rt_ref, xs_ref, buf, zbuf, sem,
                     zsem):
    t = pl.program_id(0)
    nt = pl.num_programs(0)
    tm = h_ref.shape[0]
    slot = lax.rem(t, 2)

    @pl.when(t == 0)
    def _():
        zbuf[...] = jnp.zeros(zbuf.shape, zbuf.dtype)

        def fill(start, rows):
            return pltpu.make_async_copy(zbuf.at[pl.ds(0, rows), :], xs_ref.at[pl.ds(start, rows), :], zsem)

        def pads(e):
            real_end = pl.multiple_of(ends_ref[N_EXPERTS + e], GROUP_ALIGN)
            return real_end, pl.multiple_of(ends_ref[e] - real_end, GROUP_ALIGN)

        used = lax.shift_right_logical(ends_ref[N_EXPERTS - 1], MOE_ROWS.bit_length() - 1)
        n_blocks = xs_ref.shape[0] // MOE_ROWS
        tail = lambda b: fill(pl.multiple_of(b * MOE_ROWS, MOE_ROWS), MOE_ROWS)
        for e in range(N_EXPERTS):
            pl.when(pads(e)[1] > 0)(lambda e=e: fill(*pads(e)).start())
        lax.fori_loop(used, n_blocks, lambda b, c: (tail(b).start(), c)[1], 0)
        for e in range(N_EXPERTS):
            pl.when(pads(e)[1] > 0)(lambda e=e: fill(*pads(e)).wait())
        lax.fori_loop(used, n_blocks, lambda b, c: (tail(b).wait(), c)[1], 0)

    hb = jnp.concatenate([h_ref[...], w_ref[...]], axis=1)
    low = [rt_ref[k:k + 1, :] for k in range(TOP_K)]
    sub = [rt_ref[TOP_K + k:TOP_K + k + 1, :] for k in range(TOP_K)]
    srow = lax.broadcasted_iota(jnp.int32, (SLOT_SUB, tm), 0).astype(F32).astype(BF16)
    one = jnp.ones((SLOT_SUB, tm), BF16)

    for ci in range(SLOTS_PER_TILE // SLOT_CHUNK):
        parts = []
        for b in range(SLOT_CHUNK // SLOT_SUB):
            blk = float(ci * (SLOT_CHUNK // SLOT_SUB) + b)
            own = jnp.zeros((SLOT_SUB, tm), BF16)
            for k in range(TOP_K):
                here = jnp.where(sub[k] == blk, low[k], -1.0).astype(BF16)
                own = jnp.where(srow == here, one, own)
            parts.append(own)
        own = jnp.concatenate(parts, axis=0)
        buf[slot, ci * SLOT_CHUNK:(ci + 1) * SLOT_CHUNK, :] = jnp.dot(own, hb, preferred_element_type=F32)

    def copy(s, lo, go, rows):
        return pltpu.make_async_copy(buf.at[s, pl.ds(lo, rows), :], xs_ref.at[pl.ds(go, rows), :], sem.at[s])

    _start_runs(t, g8_ref, loff_ref, goff_ref, functools.partial(copy, slot))

    def drain(tile, s):
        rows = pl.multiple_of(tot_ref[tile], GROUP_ALIGN)
        copy(s, 0, 0, rows).wait()

    pl.when(t > 0)(lambda: drain(t - 1, 1 - slot))
    pl.when(t == nt - 1)(lambda: drain(t, slot))


def _dispatch(g8f, lofff, gofff, totf, ends, h2, wext, routet, p_rows):
    n = h2.shape[0]
    tm = ROUTE_TILE
    return pl.pallas_call(
        _dispatch_kernel,
        grid_spec=pltpu.PrefetchScalarGridSpec(
            num_scalar_prefetch=5,
            grid=(n // tm,),
            in_specs=[pl.BlockSpec((tm, D_MODEL), lambda i, *_: (i, 0)),
                      pl.BlockSpec((tm, LANE), lambda i, *_: (i, 0)),
                      pl.BlockSpec((8, tm), lambda i, *_: (0, i))],
            out_specs=pl.BlockSpec(memory_space=pl.ANY),
            scratch_shapes=[pltpu.VMEM((2, SLOTS_PER_TILE, XS_COLS), F32),
                            pltpu.VMEM((MOE_ROWS, XS_COLS), F32),
                            pltpu.SemaphoreType.DMA((2,)), pltpu.SemaphoreType.DMA(())]),
        out_shape=jax.ShapeDtypeStruct((p_rows, XS_COLS), F32),
        compiler_params=_params(("arbitrary",)),
        name="dispatch",
    )(g8f, lofff, gofff, totf, ends, h2, wext, routet)


def _experts_kernel(be_ref, nu_ref, bv_ref, nx_ref, xs_ref, w1_hbm, b1_ref, w2_hbm, b2_ref, ys_ref,
                    w1buf, w2buf, wsem, ord_ref):
    del nu_ref
    i = pl.program_id(0)
    half = MOE_ROWS // 2
    used = bv_ref[i]
    e = be_ref[i]

    def fetch(ex, s):
        return (pltpu.make_async_copy(w1_hbm.at[ex], w1buf.at[s], wsem.at[0, s]),
                pltpu.make_async_copy(w2_hbm.at[ex], w2buf.at[s], wsem.at[1, s]))

    @pl.when(i == 0)
    def _():
        ord_ref[0] = 0
        for c in fetch(e, 0):
            c.start()

    @pl.when((used > 0) & ((i == 0) | (e != be_ref[jnp.maximum(i - 1, 0)])))
    def _():
        @pl.when(i > 0)
        def _():
            ord_ref[0] = ord_ref[0] + 1

        s = lax.rem(ord_ref[0], 2)
        for c in fetch(e, s):
            c.wait()
        nxt = nx_ref[i]

        @pl.when(nxt >= 0)
        def _():
            for c in fetch(nxt, 1 - s):
                c.start()

    slot = lax.rem(ord_ref[0], 2)

    def mlp(n_rows):
        xb = xs_ref[0:n_rows, 0:D_MODEL].astype(BF16)
        wext = xs_ref[0:n_rows, D_MODEL:XS_COLS]
        lane = lax.broadcasted_iota(jnp.int32, wext.shape, 1)
        wcol = jnp.sum(jnp.where((lane == e) | (lane == e + N_EXPERTS), wext, 0.0), axis=1, keepdims=True)
        hcat = jnp.dot(xb, w1buf[slot].astype(BF16), preferred_element_type=F32) + b1_ref[0]
        x_glu = jnp.minimum(hcat[:, :D_EXPERT], SWIGLU_LIMIT)
        x_lin = jnp.clip(hcat[:, D_EXPERT:], -SWIGLU_LIMIT, SWIGLU_LIMIT)
        act = (x_glu * jax.nn.sigmoid(SWIGLU_ALPHA * x_glu) * (x_lin + 1.0)).astype(BF16)
        ys_ref[0:n_rows, :] = (jnp.dot(act, w2buf[slot].astype(BF16), preferred_element_type=F32) + b2_ref[0]) * wcol

    pl.when(used > half)(lambda: mlp(MOE_ROWS))

    @pl.when((used > 0) & (used <= half))
    def _():
        mlp(half)
        ys_ref[half:, :] = jnp.zeros((MOE_ROWS - half, D_MODEL), F32)

    @pl.when(used == 0)
    def _():
        ys_ref[...] = jnp.zeros(ys_ref.shape, F32)


def _experts(block_e, n_used, block_used, next_e, xs, w1, b1, w2, b2):
    p = xs.shape[0]
    mb = MOE_ROWS
    last = lambda i, nu: jnp.minimum(i, nu[0] - 1)
    rowmap = lambda i, be, nu, bv, nx: (last(i, nu), 0)
    wmap = lambda i, be, nu, bv, nx: (be[last(i, nu)], 0, 0)
    hbm = pl.BlockSpec(memory_space=pl.ANY)
    return pl.pallas_call(
        _experts_kernel,
        grid_spec=pltpu.PrefetchScalarGridSpec(
            num_scalar_prefetch=4,
            grid=(p // mb,),
            in_specs=[pl.BlockSpec((mb, XS_COLS), rowmap),
                      hbm,
                      pl.BlockSpec((1, 1, 2 * D_EXPERT), wmap),
                      hbm,
                      pl.BlockSpec((1, 1, D_MODEL), wmap)],
            out_specs=pl.BlockSpec((mb, D_MODEL), lambda i, be, nu, bv, nx: (i, 0)),
            scratch_shapes=[pltpu.VMEM((2, D_MODEL, 2 * D_EXPERT), F32),
                            pltpu.VMEM((2, D_EXPERT, D_MODEL), F32),
                            pltpu.SemaphoreType.DMA((2, 2)),
                            pltpu.SMEM((1,), jnp.int32)]),
        out_shape=jax.ShapeDtypeStruct((p, D_MODEL), F32),
        compiler_params=_params(("arbitrary",)),
        name="experts",
    )(block_e, n_used, block_used, next_e, xs, w1, b1, w2, b2)


def _combine_kernel(g8_ref, loff_ref, goff_ref, tot_ref, ys_ref, route_ref, x1_ref, g2_ref, fn_ref, o_ref, stg, sem):
    t = pl.program_id(0)
    nt = pl.num_programs(0)
    tm = x1_ref.shape[0]
    slot = lax.rem(t, 2)

    def copy(s, lo, go, rows):
        return pltpu.make_async_copy(ys_ref.at[pl.ds(go, rows), :], stg.at[s, pl.ds(lo, rows), :], sem.at[s])

    @pl.when(t == 0)
    def _():
        stg[...] = jnp.zeros(stg.shape, F32)
        _start_runs(t, g8_ref, loff_ref, goff_ref, functools.partial(copy, 0))

    pl.when(t + 1 < nt)(lambda: _start_runs(t + 1, g8_ref, loff_ref, goff_ref, functools.partial(copy, 1 - slot)))
    copy(slot, 0, 0, pl.multiple_of(tot_ref[t], GROUP_ALIGN)).wait()

    route = route_ref[...]
    low = [jnp.broadcast_to(route[:, k:k + 1], (tm, SLOT_SUB)).astype(BF16) for k in range(TOP_K)]
    sub = [jnp.broadcast_to(route[:, TOP_K + k:TOP_K + k + 1], (tm, SLOT_SUB)).astype(BF16) for k in range(TOP_K)]
    scol = lax.broadcasted_iota(jnp.int32, (tm, SLOT_SUB), 1).astype(F32).astype(BF16)
    one = jnp.ones((tm, SLOT_SUB), BF16)
    never = jnp.full((tm, SLOT_SUB), -1.0, BF16)

    y = None
    for ci in range(SLOTS_PER_TILE // SLOT_CHUNK):
        parts = []
        for b in range(SLOT_CHUNK // SLOT_SUB):
            blk = float(ci * (SLOT_CHUNK // SLOT_SUB) + b)
            own = jnp.zeros((tm, SLOT_SUB), BF16)
            for k in range(TOP_K):
                own = jnp.where(scol == jnp.where(sub[k] == blk, low[k], never), one, own)
            parts.append(own)
        own = jnp.concatenate(parts, axis=1)
        rows = stg[slot, ci * SLOT_CHUNK:(ci + 1) * SLOT_CHUNK, :].astype(BF16)
        part = jnp.dot(own, rows, preferred_element_type=F32)
        y = part if y is None else y + part
    x = x1_ref[...] + g2_ref[0] * y
    o_ref[...] = _rms(x) * fn_ref[...]


def _combine(g8f, lofff, gofff, totf, ys, route, x1, g2, fn, seq):
    n = x1.shape[0]
    tm = ROUTE_TILE
    per_b = seq // tm
    return pl.pallas_call(
        _combine_kernel,
        grid_spec=pltpu.PrefetchScalarGridSpec(
            num_scalar_prefetch=4,
            grid=(n // tm,),
            in_specs=[pl.BlockSpec(memory_space=pl.ANY),
                      pl.BlockSpec((tm, LANE), lambda i, *_: (i, 0)),
                      pl.BlockSpec((tm, D_MODEL), lambda i, *_: (i, 0)),
                      pl.BlockSpec((1, 1, D_MODEL), lambda i, *_: (i // per_b, 0, 0)),
                      pl.BlockSpec((1, D_MODEL), lambda i, *_: (0, 0))],
            out_specs=pl.BlockSpec((tm, D_MODEL), lambda i, *_: (i, 0)),
            scratch_shapes=[pltpu.VMEM((2, SLOTS_PER_TILE, D_MODEL), F32), pltpu.SemaphoreType.DMA((2,))]),
        out_shape=jax.ShapeDtypeStruct((n, D_MODEL), F32),
        compiler_params=_params(("arbitrary",)),
        name="combine",
    )(g8f, lofff, gofff, totf, ys, route, x1, g2, fn)


def _split_w_in(w_in):
    o_kr = _W_MIX_COLS
    w_kr = jnp.pad(w_in[:, o_kr:o_kr + MLA_ROPE_DIM],
                   ((0, 0), (MLA_NOPE_DIM, LANE - MLA_NOPE_DIM - MLA_ROPE_DIM)))
    return w_in[:, :o_kr].astype(BF16), w_kr.astype(BF16), w_in[:, o_kr + MLA_ROPE_DIM:].astype(BF16)


def _pack_heads(w, width, pieces):
    rows = w.shape[0]
    w3 = w.reshape(rows, MLA_HEADS, width)
    cols = [w3[:, :, a:b] for a, b in pieces]
    used = sum(b - a for a, b in pieces)
    cols.append(jnp.zeros((rows, MLA_HEADS, LANE - used), w.dtype))
    return jnp.concatenate(cols, axis=2).reshape(rows, MLA_HEADS * LANE)


def kernel(x, c, positions, w_ada, b_ada, norm_mix, norm_ffn, w_in, sinks, q_norm, kv_norm, w_uq, w_uk, w_uv,
           w_branch_a, w_branch_b, w_out, w_router, b_router, w_moe1, b_moe1, w_moe2, b_moe2, final_norm):
    batch, seq, d = x.shape
    n = batch * seq
    assert d == D_MODEL and w_ada.shape[0] == 1
    assert seq % TQ_MLA == 0 and seq % (SWA_QB * ATTN_BLOCK) == 0 and seq % TM_PROJ == 0 and seq % ROUTE_TILE == 0
    hw = MLA_HEADS * LANE
    qk = MLA_NOPE_DIM + MLA_ROPE_DIM

    w_mix, w_kr, w_gates = _split_w_in(w_in[0])
    wuq_p = _pack_heads(w_uq[0], qk, [(0, qk)]).astype(BF16)
    wuk_p = _pack_heads(w_uk[0], MLA_NOPE_DIM, [(0, MLA_NOPE_DIM)]).astype(BF16)
    wuv_p = _pack_heads(w_uv[0], MLA_V_DIM, [(0, MLA_V_DIM)]).astype(BF16)
    one_row = jnp.tile((jnp.arange(LANE) == MLA_V_DIM).astype(F32), MLA_HEADS)[None, :]
    freqs = ROPE_THETA ** (-jnp.arange(0, MLA_ROPE_DIM, 2, dtype=F32) / MLA_ROPE_DIM)
    frq = jnp.concatenate([jnp.zeros((MLA_NOPE_DIM,), F32), freqs, freqs,
                           jnp.zeros((LANE - qk,), F32)])[None, :]
    wb3 = w_branch_b[0].reshape(MLA_HEADS, MLA_V_DIM, D_MODEL)
    wb_p = jnp.concatenate([wb3, jnp.zeros((MLA_HEADS, LANE - MLA_V_DIM, D_MODEL), F32)],
                           axis=1).reshape(hw, D_MODEL).astype(BF16)
    wa = w_branch_a[0].astype(BF16)
    wo = w_out[0].astype(BF16)
    wr = jnp.pad(w_router[0], ((0, 0), (0, LANE - N_EXPERTS)))
    wr_hi = wr.astype(BF16)
    wr_lo = (wr - wr_hi.astype(F32)).astype(BF16)
    br_p = jnp.pad(b_router[0], (0, LANE - N_EXPERTS))[None, :]
    b1 = b_moe1[0][:, None, :]
    b2 = b_moe2[0][:, None, :]
    slopes = jnp.asarray(np.exp2(-8.0 * np.arange(1, SWA_HEADS + 1) / SWA_HEADS), dtype=F32)

    c8 = jnp.pad(c, ((0, 8 - batch), (0, 0)))
    mod = _ada(c8, w_ada[0], b_ada[0][None, :])[:batch]
    sh1, sc1, g1, sh2, sc2, g2 = [m[:, None, :] for m in jnp.split(mod, 6, axis=-1)]

    x2 = x.reshape(n, D_MODEL)
    pos2 = positions.reshape(n, 1).astype(jnp.int32)
    qa, ka2, va2, qm, km, vm, ga, gb, stats = _inproj(
        x2, pos2, sh1, sc1, norm_mix, w_mix, w_kr, w_gates, q_norm, kv_norm, wuq_p, wuk_p, wuv_p, frq, one_row, seq)
    st = stats.reshape(batch, seq // TM_PROJ, 8, LANE)
    bound_sq = jnp.max(st[:, :, :, 0], axis=1) * jnp.max(st[:, :, :, 1], axis=1)
    plain = (bound_sq <= MLA_PLAIN_BOUND ** 2).astype(jnp.int32).reshape(-1)
    yb = _mla(plain, qm, km, vm, batch, seq)
    ya = _swa(sinks[0], slopes, qa, ka2, va2, batch, seq)

    x1, h2, wext, route, routet, g8 = _outproj(
        ya, yb, ga, gb, x2, g1, sh2, sc2, norm_ffn, wa, wb_p, wo, wr_hi, wr_lo, br_p, seq)

    n_tiles = n // ROUTE_TILE
    g8t = g8.reshape(n_tiles, 8, LANE)[:, 0, :N_EXPERTS].astype(jnp.int32)
    loff = jnp.cumsum(g8t, axis=1) - g8t
    padded = ((jnp.sum(g8t, axis=0) + MOE_ROWS - 1) // MOE_ROWS) * MOE_ROWS
    pends = jnp.cumsum(padded)
    goff = (pends - padded)[None, :] + jnp.cumsum(g8t, axis=0) - g8t
    tot = jnp.sum(g8t, axis=1).astype(jnp.int32)
    p_rows = n * TOP_K + n_tiles * N_EXPERTS * (GROUP_ALIGN - 1) + N_EXPERTS * (MOE_ROWS - 1)
    p_rows = -(-p_rows // MOE_ROWS) * MOE_ROWS
    n_blocks = p_rows // MOE_ROWS
    block_start = jnp.arange(n_blocks, dtype=jnp.int32) * MOE_ROWS
    block_e = jnp.minimum(jnp.sum((pends[None, :] <= block_start[:, None]).astype(jnp.int32), axis=1),
                          N_EXPERTS - 1).astype(jnp.int32)
    n_used = (pends[-1:] // MOE_ROWS).astype(jnp.int32)
    eids = jnp.arange(N_EXPERTS, dtype=jnp.int32)
    of_block = block_e[:, None] == eids[None, :]
    pick = lambda table: jnp.sum(jnp.where(of_block, table[None, :], 0), axis=1).astype(jnp.int32)
    real_end = pends - padded + jnp.sum(g8t, axis=0)
    block_used = jnp.clip(pick(real_end) - block_start, 0, MOE_ROWS).astype(jnp.int32)
    later = (eids[None, :] > eids[:, None]) & (padded[None, :] > 0)
    after = jnp.min(jnp.where(later, eids[None, :], N_EXPERTS), axis=1)
    next_e = pick(jnp.where(after == N_EXPERTS, -1, after))
    tabs = (g8t.reshape(-1), loff.reshape(-1).astype(jnp.int32), goff.reshape(-1).astype(jnp.int32), tot)

    xs = _dispatch(*tabs, jnp.concatenate([pends, real_end]).astype(jnp.int32), h2, wext, routet, p_rows)
    ys = _experts(block_e, n_used, block_used, next_e, xs, w_moe1[0], b1, w_moe2[0], b2)
    out = _combine(*tabs, ys, route, x1, g2, final_norm[None, :], seq)
    return out.reshape(batch, seq, D_MODEL)
```

```python
import functools

import numpy as np
import jax
import jax.numpy as jnp
from jax import lax
from jax.experimental import pallas as pl
from jax.experimental.pallas import tpu as pltpu

D_MODEL = 1024
SWA_HEADS = 8
SWA_KV_HEADS = 2
SWA_HEAD_DIM = 64
ATTN_BLOCK = 128
MLA_HEADS = 8
MLA_Q_RANK = 384
MLA_KV_RANK = 256
MLA_NOPE_DIM = 64
MLA_ROPE_DIM = 32
MLA_V_DIM = 64
ROPE_THETA = 10000.0
N_EXPERTS = 32
TOP_K = 4
D_EXPERT = 1024
SWIGLU_LIMIT = 7.0
SWIGLU_ALPHA = 1.702
NORM_EPS = 1e-6

LANE = 128
LOG2E = 1.4426950408889634
NEG = -1e30
SWA_QSCALE = SWA_HEAD_DIM ** -0.5 * LOG2E
MLA_QSCALE = (MLA_NOPE_DIM + MLA_ROPE_DIM) ** -0.5 * LOG2E
HALF_ROPE = MLA_ROPE_DIM // 2
MLA_PLAIN_BOUND = 60.0
MLA_WIDE = 8

BF16 = jnp.bfloat16
F32 = jnp.float32

TM_PROJ = 512
TQ_MLA = 512
SWA_QB = 16
MOE_ROWS = 512
ROUTE_TILE = 512
GROUP_ALIGN = 8
SLOTS_PER_TILE = ROUTE_TILE * TOP_K + N_EXPERTS * GROUP_ALIGN
SLOT_SUB = 256
SLOT_CHUNK = 3 * SLOT_SUB
XS_COLS = D_MODEL + LANE
VMEM_LIMIT = 56 * 1024 * 1024


def _params(sem, vmem=VMEM_LIMIT):
    return pltpu.CompilerParams(dimension_semantics=sem, vmem_limit_bytes=vmem)


def _rms(x):
    return x * lax.rsqrt(jnp.mean(x * x, axis=-1, keepdims=True) + NORM_EPS)


def _ada_kernel(c_ref, w_ref, b_ref, o_ref):
    c = c_ref[...]
    a = (c * jax.nn.sigmoid(c)).astype(BF16)
    o_ref[...] = jnp.dot(a, w_ref[...].astype(BF16), preferred_element_type=F32) + b_ref[...]


def _ada(c8, w_ada, b_ada):
    n_out = w_ada.shape[1]
    return pl.pallas_call(
        _ada_kernel,
        grid=(n_out // D_MODEL,),
        in_specs=[pl.BlockSpec((8, D_MODEL), lambda j: (0, 0)),
                  pl.BlockSpec((D_MODEL, D_MODEL), lambda j: (0, j)),
                  pl.BlockSpec((1, D_MODEL), lambda j: (0, j))],
        out_specs=pl.BlockSpec((8, D_MODEL), lambda j: (0, j)),
        out_shape=jax.ShapeDtypeStruct((8, n_out), F32),
        compiler_params=_params(("arbitrary",)),
        name="ada",
    )(c8, w_ada, b_ada)


_C_QA = (0, 512)
_C_KV = (512, 768)
_C_LAT = (768, 1408)
_W_MIX_COLS = 1408
_ROPE_LO = (MLA_NOPE_DIM, MLA_NOPE_DIM + HALF_ROPE)
_ROPE_HI = (MLA_NOPE_DIM + HALF_ROPE, MLA_NOPE_DIM + MLA_ROPE_DIM)


def _inproj_kernel(x_ref, pos_ref, sh_ref, sc_ref, nm_ref, win_ref, wkr_ref, wg_ref, qn_ref, kvn_ref, wuq_ref, wuk_ref,
                   wuv_ref, frq_ref, one_ref,
                   qa_ref, ka_ref, va_ref, qm_ref, km_ref, vm_ref, ga_ref, gb_ref, st_ref):
    tm = x_ref.shape[0]
    x = x_ref[...]
    h = (_rms(x) * nm_ref[...] * (1.0 + sc_ref[0]) + sh_ref[0]).astype(BF16)

    def proj(c):
        return jnp.dot(h, win_ref[:, c[0]:c[1]], preferred_element_type=F32)

    lane = lax.broadcasted_iota(jnp.int32, (tm, LANE), 1)
    first = lane < SWA_HEAD_DIM

    def twice(t):
        r = pltpu.roll(t, SWA_HEAD_DIM, axis=1)
        return jnp.concatenate([jnp.where(first, t, r), jnp.where(first, r, t)], axis=1)

    kv = proj(_C_KV)
    ka_ref[...] = twice(kv[:, :LANE]).astype(BF16)
    va_ref[...] = twice(kv[:, LANE:]).astype(BF16)

    ang = pos_ref[...].astype(F32) * frq_ref[...]
    cs = jnp.cos(ang)
    sn = jnp.sin(ang)
    sn_lo = jnp.where((lane >= _ROPE_LO[0]) & (lane < _ROPE_LO[1]), -sn, 0.0)
    sn_hi = jnp.where((lane >= _ROPE_HI[0]) & (lane < _ROPE_HI[1]), sn, 0.0)

    def rotary(t):
        return (t * cs + pltpu.roll(t, LANE - HALF_ROPE, axis=1) * sn_lo
                + pltpu.roll(t, HALF_ROPE, axis=1) * sn_hi)

    lat = proj(_C_LAT)
    cqn = (_rms(lat[:, 0:MLA_Q_RANK]) * qn_ref[...]).astype(BF16)
    ckvn = (_rms(lat[:, MLA_Q_RANK:MLA_Q_RANK + MLA_KV_RANK]) * kvn_ref[...]).astype(BF16)
    krr = rotary(jnp.dot(h, wkr_ref[...], preferred_element_type=F32))
    q = jnp.dot(cqn, wuq_ref[...], preferred_element_type=F32)
    kn = jnp.dot(ckvn, wuk_ref[...], preferred_element_type=F32)

    def max_sq_norm(t):
        tf = t.astype(F32)
        return jnp.max(jnp.sum(tf * tf, axis=1, keepdims=True), axis=0, keepdims=True)

    rid = lax.broadcasted_iota(jnp.int32, (8, LANE), 0)
    lid = lax.broadcasted_iota(jnp.int32, (8, LANE), 1)
    stats = jnp.zeros((8, LANE), F32)
    for hh in range(MLA_HEADS):
        sl = slice(hh * LANE, (hh + 1) * LANE)
        qb = (rotary(q[:, sl]) * MLA_QSCALE).astype(BF16)
        kb = (kn[:, sl] + krr).astype(BF16)
        qm_ref[:, sl] = qb
        km_ref[:, sl] = kb
        stats = jnp.where((rid == hh) & (lid == 0), max_sq_norm(qb), stats)
        stats = jnp.where((rid == hh) & (lid == 1), max_sq_norm(kb), stats)
    st_ref[...] = stats
    vm_ref[...] = (jnp.dot(ckvn, wuv_ref[...], preferred_element_type=F32) + one_ref[...]).astype(BF16)
    qa_ref[...] = (proj(_C_QA) * SWA_QSCALE).astype(BF16)
    ga_ref[...] = jax.nn.sigmoid(jnp.dot(h, wg_ref[:, 0:D_MODEL], preferred_element_type=F32)).astype(BF16)
    gb_ref[...] = jax.nn.sigmoid(jnp.dot(h, wg_ref[:, D_MODEL:], preferred_element_type=F32)).astype(BF16)


def _inproj(x2, pos2, sh1, sc1, nm, w_mix, w_kr, w_gates, qn, kvn, wuq_p, wuk_p, wuv_p, frq, one_row, seq):
    n = x2.shape[0]
    tm = TM_PROJ
    per_b = seq // tm
    hw = MLA_HEADS * LANE
    row = lambda i: (i, 0)
    fix = lambda i: (0, 0)
    bsel = lambda i: (i // per_b, 0, 0)
    widths = (512, 256, 256, hw, hw, hw, D_MODEL, D_MODEL)
    return pl.pallas_call(
        _inproj_kernel,
        grid=(n // tm,),
        in_specs=[pl.BlockSpec((tm, D_MODEL), row),
                  pl.BlockSpec((tm, 1), row),
                  pl.BlockSpec((1, 1, D_MODEL), bsel),
                  pl.BlockSpec((1, 1, D_MODEL), bsel),
                  pl.BlockSpec((1, D_MODEL), fix),
                  pl.BlockSpec((D_MODEL, _W_MIX_COLS), fix),
                  pl.BlockSpec((D_MODEL, LANE), fix),
                  pl.BlockSpec((D_MODEL, 2 * D_MODEL), fix),
                  pl.BlockSpec((1, MLA_Q_RANK), fix),
                  pl.BlockSpec((1, MLA_KV_RANK), fix),
                  pl.BlockSpec((MLA_Q_RANK, hw), fix),
                  pl.BlockSpec((MLA_KV_RANK, hw), fix),
                  pl.BlockSpec((MLA_KV_RANK, hw), fix),
                  pl.BlockSpec((1, LANE), fix),
                  pl.BlockSpec((1, hw), fix)],
        out_specs=[pl.BlockSpec((tm, w), row) for w in widths] + [pl.BlockSpec((8, LANE), row)],
        out_shape=[jax.ShapeDtypeStruct((n, w), BF16) for w in widths]
        + [jax.ShapeDtypeStruct((n // tm * 8, LANE), F32)],
        compiler_params=_params(("arbitrary",)),
        name="inproj",
    )(x2, pos2, sh1, sc1, nm, w_mix, w_kr, w_gates, qn, kvn, wuq_p, wuk_p, wuv_p, frq, one_row)


def _mla_kernel(plain_ref, q_ref, k_ref, v_ref, o_ref, m_scr, acc_scr):
    t = TQ_MLA
    nq = q_ref.shape[0] // t

    def scores(i, j, nblk, diag_at):
        q = q_ref[pl.ds(pl.multiple_of(i * t, t), t), :]
        start = pl.multiple_of(j * t, t)
        k = k_ref[pl.ds(start, nblk * t), :]
        v = v_ref[pl.ds(start, nblk * t), :]
        s = lax.dot_general(q, k, (((1,), (1,)), ((), ())), preferred_element_type=F32)
        if diag_at is not None:
            qi = lax.broadcasted_iota(jnp.int32, s.shape, 0)
            kj = lax.broadcasted_iota(jnp.int32, s.shape, 1)
            s = jnp.where(kj <= qi + diag_at * t, s, NEG)
        return s, v

    def plain_step(i, j, nblk, diag_at):
        s, v = scores(i, j, nblk, diag_at)
        acc_scr[...] += jnp.dot(jnp.exp2(s).astype(BF16), v, preferred_element_type=F32)

    def online_step(i, j, nblk, diag_at):
        s, v = scores(i, j, nblk, diag_at)
        m_old = m_scr[...]
        m_new = jnp.maximum(m_old, jnp.max(s, axis=1, keepdims=True))
        p = jnp.exp2(s - m_new)
        alpha = jnp.exp2(m_old - m_new)
        acc_scr[...] = acc_scr[...] * alpha + jnp.dot(p.astype(BF16), v, preferred_element_type=F32)
        m_scr[...] = m_new

    def run(step, wide, online):
        def q_tile(i, carry):
            acc_scr[...] = jnp.zeros(acc_scr.shape, F32)
            if online:
                m_scr[...] = jnp.full(m_scr.shape, NEG, F32)
            nw = lax.shift_right_logical(i, wide.bit_length() - 1)
            left = i - nw * wide

            def wide_body(jj, c):
                step(i, jj * wide, wide, None)
                return c

            lax.fori_loop(0, nw, wide_body, 0)
            for r in range(wide):
                pl.when(left == r)(lambda r=r: step(i, nw * wide, r + 1, r))
            acc = acc_scr[...]
            o_ref[pl.ds(pl.multiple_of(i * t, t), t), :] = (acc / acc[:, MLA_V_DIM:MLA_V_DIM + 1]).astype(BF16)
            return carry

        lax.fori_loop(0, nq, q_tile, 0)

    small = plain_ref[pl.program_id(0) * MLA_HEADS + pl.program_id(1)] != 0
    pl.when(small)(lambda: run(plain_step, MLA_WIDE, False))
    pl.when(jnp.logical_not(small))(lambda: run(online_step, 1, True))


def _mla(plain, qm, km, vm, batch, seq):
    n = qm.shape[0]
    t = TQ_MLA
    kmap = lambda b, h, *_: (b, h)
    return pl.pallas_call(
        _mla_kernel,
        grid_spec=pltpu.PrefetchScalarGridSpec(
            num_scalar_prefetch=1,
            grid=(batch, MLA_HEADS),
            in_specs=[pl.BlockSpec((seq, LANE), kmap),
                      pl.BlockSpec((seq, LANE), kmap),
                      pl.BlockSpec((seq, LANE), kmap)],
            out_specs=pl.BlockSpec((seq, LANE), kmap),
            scratch_shapes=[pltpu.VMEM((t, 1), F32), pltpu.VMEM((t, LANE), F32)]),
        out_shape=jax.ShapeDtypeStruct((n, MLA_HEADS * LANE), BF16),
        compiler_params=_params(("arbitrary", "arbitrary")),
        name="mla",
    )(plain, qm, km, vm)


def _swa_kernel(sink_ref, slope_ref, q_ref, kc_ref, kp_ref, vc_ref, vp_ref, o_ref, bias_scr):
    b = pl.program_id(0)
    i = pl.program_id(1)
    blk = ATTN_BLOCK

    @pl.when((b == 0) & (i == 0))
    def _():
        qi = lax.broadcasted_iota(jnp.int32, (blk, 2 * blk), 0)
        kj = lax.broadcasted_iota(jnp.int32, (blk, 2 * blk), 1)
        dist = qi - kj + blk
        valid = (dist >= 0) & (dist < blk)
        distf = dist.astype(F32)
        for hd in range(SWA_HEADS):
            bias_scr[hd] = jnp.where(valid, -(slope_ref[hd] * LOG2E) * distf, NEG)

    lane = lax.broadcasted_iota(jnp.int32, (2 * blk, LANE), 1)
    lo = lane < SWA_HEAD_DIM
    kcol = lax.broadcasted_iota(jnp.int32, (blk, 2 * blk), 1)
    first_pen = jnp.where(kcol < blk, jnp.where(i == 0, NEG, 0.0), 0.0)
    zero = jnp.zeros((2 * blk, LANE), BF16)

    for qb in range(SWA_QB):
        if qb == 0:
            kprev, vprev = kp_ref[...], vp_ref[...]
        else:
            kprev = kc_ref[(qb - 1) * blk:qb * blk, :]
            vprev = vc_ref[(qb - 1) * blk:qb * blk, :]
        kcat = jnp.concatenate([kprev, kc_ref[qb * blk:(qb + 1) * blk, :]], axis=0)
        vcat = jnp.concatenate([vprev, vc_ref[qb * blk:(qb + 1) * blk, :]], axis=0)
        for g in range(SWA_KV_HEADS):
            kg = kcat[:, g * LANE:(g + 1) * LANE]
            vg = vcat[:, g * LANE:(g + 1) * LANE]
            halves = ((jnp.where(lo, kg, zero), jnp.where(lo, vg, zero)),
                      (jnp.where(lo, zero, kg), jnp.where(lo, zero, vg)))
            for pp in range(2):
                pr = g * 2 + pp
                qpair = q_ref[qb * blk:(qb + 1) * blk, pr * LANE:(pr + 1) * LANE]
                out = None
                for half in range(2):
                    hd = pr * 2 + half
                    kx, vx = halves[half]
                    s = lax.dot_general(qpair, kx, (((1,), (1,)), ((), ())), preferred_element_type=F32)
                    s = s + bias_scr[hd]
                    if qb == 0:
                        s = s + first_pen
                    sink = sink_ref[hd] * LOG2E
                    m = jnp.maximum(jnp.max(s, axis=1, keepdims=True), sink)
                    p = jnp.exp2(s - m)
                    denom = jnp.sum(p, axis=1, keepdims=True) + jnp.exp2(sink - m)
                    o = jnp.dot(p.astype(BF16), vx, preferred_element_type=F32) * (1.0 / denom)
                    out = o if out is None else out + o
                o_ref[qb * blk:(qb + 1) * blk, pr * LANE:(pr + 1) * LANE] = out.astype(BF16)


def _swa(sinks, slopes, qa, ka2, va2, batch, seq):
    n = qa.shape[0]
    blk = ATTN_BLOCK
    rows = SWA_QB * blk
    steps = seq // rows
    cur = lambda b, i: (b * steps + i, 0)
    prev = lambda b, i: (b * (seq // blk) + jnp.maximum(i * SWA_QB - 1, 0), 0)
    smem = pl.BlockSpec(memory_space=pltpu.SMEM)
    return pl.pallas_call(
        _swa_kernel,
        grid=(batch, steps),
        in_specs=[smem, smem,
                  pl.BlockSpec((rows, 512), cur),
                  pl.BlockSpec((rows, 256), cur),
                  pl.BlockSpec((blk, 256), prev),
                  pl.BlockSpec((rows, 256), cur),
                  pl.BlockSpec((blk, 256), prev)],
        out_specs=pl.BlockSpec((rows, 512), cur),
        out_shape=jax.ShapeDtypeStruct((n, 512), BF16),
        scratch_shapes=[pltpu.VMEM((SWA_HEADS, blk, 2 * blk), F32)],
        compiler_params=_params(("arbitrary", "arbitrary")),
        name="swa",
    )(sinks, slopes, qa, ka2, ka2, va2, va2)


def _outproj_kernel(ya_ref, yb_ref, ga_ref, gb_ref, x_ref, g1_ref, sh_ref, sc_ref, nm_ref, wa_ref, wb_ref,
                    wo_ref, wrh_ref, wrl_ref, br_ref,
                    x1_ref, h2_ref, wext_ref, route_ref, routet_ref, g8_ref, lg_scr):
    t = pl.program_id(0)
    tm = x_ref.shape[0]
    slot = lax.rem(t, 2)

    @pl.when(t == 0)
    def _():
        lg_scr[...] = jnp.zeros(lg_scr.shape, F32)

    pj = {}

    def stage_a():
        pj['a'] = jnp.dot(ya_ref[...], wa_ref[...], preferred_element_type=F32)

    def stage_b():
        pj['bm'] = jnp.dot(yb_ref[...], wb_ref[...], preferred_element_type=F32)

    def stage_out():
        mixed = (ga_ref[...].astype(F32) * pj['a'] + gb_ref[...].astype(F32) * pj['bm']).astype(BF16)
        pj['x1'] = x_ref[...] + g1_ref[0] * jnp.dot(mixed, wo_ref[...], preferred_element_type=F32)
        x1_ref[...] = pj['x1']

    def stage_h2():
        h2 = _rms(pj['x1']) * nm_ref[...] * (1.0 + sc_ref[0]) + sh_ref[0]
        hi = h2.astype(BF16)
        h2_ref[...] = hi
        lo = (h2 - hi.astype(F32)).astype(BF16)
        lg_scr[slot] = (jnp.dot(hi, wrh_ref[...], preferred_element_type=F32)
                        + jnp.dot(lo, wrh_ref[...], preferred_element_type=F32)
                        + jnp.dot(hi, wrl_ref[...], preferred_element_type=F32)) + br_ref[...]

    stages = (stage_a, stage_b, stage_out)
    project = stage_h2

    logits = lg_scr[1 - slot]
    lane = lax.broadcasted_iota(jnp.int32, (tm, LANE), 1).astype(F32)
    work = jnp.where(lane < N_EXPERTS, logits, -jnp.inf)
    vals, idxs = [], []
    for rnd in range(TOP_K):
        m = jnp.max(work, axis=1, keepdims=True)
        idx = jnp.min(jnp.where(work == m, lane, float(LANE)), axis=1, keepdims=True)
        vals.append(m)
        idxs.append(idx)
        work = jnp.where(lane == idx, -jnp.inf, work)
        if rnd < len(stages):
            stages[rnd]()
    exps = [jnp.exp(v - vals[0]) for v in vals]
    tot = exps[0] + exps[1] + exps[2] + exps[3]

    onehot = jnp.zeros((tm, LANE), F32)
    for idx in idxs:
        onehot = onehot + (lane == idx).astype(F32)
    r = lax.broadcasted_iota(jnp.int32, (tm, tm), 0)
    c = lax.broadcasted_iota(jnp.int32, (tm, tm), 1)
    prefix = jnp.dot((c < r).astype(BF16), onehot.astype(BF16), preferred_element_type=F32)
    groups = jnp.ceil(jnp.sum(onehot, axis=0, keepdims=True) * (1.0 / GROUP_ALIGN))
    er = lax.broadcasted_iota(jnp.int32, (LANE, LANE), 0)
    ec = lax.broadcasted_iota(jnp.int32, (LANE, LANE), 1)
    before = jnp.dot(jnp.broadcast_to(groups, (8, LANE)).astype(BF16), (er < ec).astype(BF16),
                     preferred_element_type=F32)[0:1, :]
    slot_e = before * GROUP_ALIGN + prefix

    route = jnp.zeros((tm, LANE), F32)
    wext = jnp.zeros((tm, LANE), F32)
    for k in range(TOP_K):
        sk = jnp.sum(jnp.where(lane == idxs[k], slot_e, 0.0), axis=1, keepdims=True)
        sub = jnp.floor(sk * (1.0 / SLOT_SUB))
        route = jnp.where(lane == float(k), sk - sub * SLOT_SUB, route)
        route = jnp.where(lane == float(TOP_K + k), sub, route)
        wk = exps[k] / tot
        wk_hi = wk.astype(BF16).astype(F32)
        wext = jnp.where(lane == idxs[k], wk_hi, wext)
        wext = jnp.where(lane == idxs[k] + float(N_EXPERTS), wk - wk_hi, wext)
    wext_ref[...] = wext.astype(BF16)
    route_ref[...] = route
    routet_ref[...] = route.T[0:8, :]
    g8_ref[...] = jnp.broadcast_to(groups * GROUP_ALIGN, (8, LANE))
    project()


def _outproj(ya, yb, ga, gb, x2, g1, sh2, sc2, nm, wa, wb_p, wo, wr_hi, wr_lo, br_p, seq):
    n = x2.shape[0]
    tm = ROUTE_TILE
    per_b = seq // tm
    hw = MLA_HEADS * LANE
    n_tiles = n // tm
    cur = lambda i: jnp.minimum(i, n_tiles - 1)
    prv = lambda i: jnp.maximum(i - 1, 0)
    row = lambda i: (cur(i), 0)
    rrow = lambda i: (prv(i), 0)
    fix = lambda i: (0, 0)
    bsel = lambda i: (cur(i) // per_b, 0, 0)
    return pl.pallas_call(
        _outproj_kernel,
        grid=(n_tiles + 1,),
        in_specs=[pl.BlockSpec((tm, 512), row),
                  pl.BlockSpec((tm, hw), row),
                  pl.BlockSpec((tm, D_MODEL), row),
                  pl.BlockSpec((tm, D_MODEL), row),
                  pl.BlockSpec((tm, D_MODEL), row),
                  pl.BlockSpec((1, 1, D_MODEL), bsel),
                  pl.BlockSpec((1, 1, D_MODEL), bsel),
                  pl.BlockSpec((1, 1, D_MODEL), bsel),
                  pl.BlockSpec((1, D_MODEL), fix),
                  pl.BlockSpec((512, D_MODEL), fix),
                  pl.BlockSpec((hw, D_MODEL), fix),
                  pl.BlockSpec((D_MODEL, D_MODEL), fix),
                  pl.BlockSpec((D_MODEL, LANE), fix),
                  pl.BlockSpec((D_MODEL, LANE), fix),
                  pl.BlockSpec((1, LANE), fix)],
        out_specs=[pl.BlockSpec((tm, D_MODEL), row),
                   pl.BlockSpec((tm, D_MODEL), row),
                   pl.BlockSpec((tm, LANE), rrow),
                   pl.BlockSpec((tm, LANE), rrow),
                   pl.BlockSpec((8, tm), lambda i: (0, prv(i))),
                   pl.BlockSpec((8, LANE), rrow)],
        out_shape=[jax.ShapeDtypeStruct((n, D_MODEL), F32),
                   jax.ShapeDtypeStruct((n, D_MODEL), BF16),
                   jax.ShapeDtypeStruct((n, LANE), BF16),
                   jax.ShapeDtypeStruct((n, LANE), F32),
                   jax.ShapeDtypeStruct((8, n), F32),
                   jax.ShapeDtypeStruct((n_tiles * 8, LANE), F32)],
        scratch_shapes=[pltpu.VMEM((2, tm, LANE), F32)],
        compiler_params=_params(("arbitrary",)),
        name="outproj",
    )(ya, yb, ga, gb, x2, g1, sh2, sc2, nm, wa, wb_p, wo, wr_hi, wr_lo, br_p)


def _start_runs(tile, g8_ref, loff_ref, goff_ref, make_copy):
    for e in range(N_EXPERTS):
        g = g8_ref[tile * N_EXPERTS + e]
        lo = loff_ref[tile * N_EXPERTS + e]
        go = goff_ref[tile * N_EXPERTS + e]
        def go_copy(lo=lo, go=go, g=g):
            make_copy(pl.multiple_of(lo, GROUP_ALIGN), pl.multiple_of(go, GROUP_ALIGN),
                      pl.multiple_of(g, GROUP_ALIGN)).start()

        pl.when(g > 0)(go_copy)


def _dispatch_kernel(g8_ref, loff_ref, goff_ref, tot_ref, ends_ref, h_ref, w_ref, rt_ref, xs_ref, buf, zbuf, sem,
                     zsem):
    t = pl.program_id(0)
    nt = pl.num_programs(0)
    tm = h_ref.shape[0]
    slot = lax.rem(t, 2)

    @pl.when(t == 0)
    def _():
        zbuf[...] = jnp.zeros(zbuf.shape, zbuf.dtype)

        def fill(start, rows):
            return pltpu.make_async_copy(zbuf.at[pl.ds(0, rows), :], xs_ref.at[pl.ds(start, rows), :], zsem)

        def pads(e):
            real_end = pl.multiple_of(ends_ref[N_EXPERTS + e], GROUP_ALIGN)
            return real_end, pl.multiple_of(ends_ref[e] - real_end, GROUP_ALIGN)

        used = lax.shift_right_logical(ends_ref[N_EXPERTS - 1], MOE_ROWS.bit_length() - 1)
        n_blocks = xs_ref.shape[0] // MOE_ROWS
        tail = lambda b: fill(pl.multiple_of(b * MOE_ROWS, MOE_ROWS), MOE_ROWS)
        for e in range(N_EXPERTS):
            pl.when(pads(e)[1] > 0)(lambda e=e: fill(*pads(e)).start())
        lax.fori_loop(used, n_blocks, lambda b, c: (tail(b).start(), c)[1], 0)
        for e in range(N_EXPERTS):
            pl.when(pads(e)[1] > 0)(lambda e=e: fill(*pads(e)).wait())
        lax.fori_loop(used, n_blocks, lambda b, c: (tail(b).wait(), c)[1], 0)

    hb = jnp.concatenate([h_ref[...], w_ref[...]], axis=1)
    low = [rt_ref[k:k + 1, :] for k in range(TOP_K)]
    sub = [rt_ref[TOP_K + k:TOP_K + k + 1, :] for k in range(TOP_K)]
    srow = lax.broadcasted_iota(jnp.int32, (SLOT_SUB, tm), 0).astype(F32).astype(BF16)
    one = jnp.ones((SLOT_SUB, tm), BF16)

    for ci in range(SLOTS_PER_TILE // SLOT_CHUNK):
        parts = []
        for b in range(SLOT_CHUNK // SLOT_SUB):
            blk = float(ci * (SLOT_CHUNK // SLOT_SUB) + b)
            own = jnp.zeros((SLOT_SUB, tm), BF16)
            for k in range(TOP_K):
                here = jnp.where(sub[k] == blk, low[k], -1.0).astype(BF16)
                own = jnp.where(srow == here, one, own)
            parts.append(own)
        own = jnp.concatenate(parts, axis=0)
        buf[slot, ci * SLOT_CHUNK:(ci + 1) * SLOT_CHUNK, :] = jnp.dot(own, hb, preferred_element_type=F32)

    def copy(s, lo, go, rows):
        return pltpu.make_async_copy(buf.at[s, pl.ds(lo, rows), :], xs_ref.at[pl.ds(go, rows), :], sem.at[s])

    _start_runs(t, g8_ref, loff_ref, goff_ref, functools.partial(copy, slot))

    def drain(tile, s):
        rows = pl.multiple_of(tot_ref[tile], GROUP_ALIGN)
        copy(s, 0, 0, rows).wait()

    pl.when(t > 0)(lambda: drain(t - 1, 1 - slot))
    pl.when(t == nt - 1)(lambda: drain(t, slot))


def _dispatch(g8f, lofff, gofff, totf, ends, h2, wext, routet, p_rows):
    n = h2.shape[0]
    tm = ROUTE_TILE
    return pl.pallas_call(
        _dispatch_kernel,
        grid_spec=pltpu.PrefetchScalarGridSpec(
            num_scalar_prefetch=5,
            grid=(n // tm,),
            in_specs=[pl.BlockSpec((tm, D_MODEL), lambda i, *_: (i, 0)),
                      pl.BlockSpec((tm, LANE), lambda i, *_: (i, 0)),
                      pl.BlockSpec((8, tm), lambda i, *_: (0, i))],
            out_specs=pl.BlockSpec(memory_space=pl.ANY),
            scratch_shapes=[pltpu.VMEM((2, SLOTS_PER_TILE, XS_COLS), F32),
                            pltpu.VMEM((MOE_ROWS, XS_COLS), F32),
                            pltpu.SemaphoreType.DMA((2,)), pltpu.SemaphoreType.DMA(())]),
        out_shape=jax.ShapeDtypeStruct((p_rows, XS_COLS), F32),
        compiler_params=_params(("arbitrary",)),
        name="dispatch",
    )(g8f, lofff, gofff, totf, ends, h2, wext, routet)


def _experts_kernel(be_ref, nu_ref, bv_ref, nx_ref, xs_ref, w1_hbm, b1_ref, w2_hbm, b2_ref, ys_ref,
                    w1buf, w2buf, wsem, ord_ref):
    del nu_ref
    i = pl.program_id(0)
    half = MOE_ROWS // 2
    used = bv_ref[i]
    e = be_ref[i]

    def fetch(ex, s):
        return (pltpu.make_async_copy(w1_hbm.at[ex], w1buf.at[s], wsem.at[0, s]),
                pltpu.make_async_copy(w2_hbm.at[ex], w2buf.at[s], wsem.at[1, s]))

    @pl.when(i == 0)
    def _():
        ord_ref[0] = 0
        for c in fetch(e, 0):
            c.start()

    @pl.when((used > 0) & ((i == 0) | (e != be_ref[jnp.maximum(i - 1, 0)])))
    def _():
        @pl.when(i > 0)
        def _():
            ord_ref[0] = ord_ref[0] + 1

        s = lax.rem(ord_ref[0], 2)
        for c in fetch(e, s):
            c.wait()
        nxt = nx_ref[i]

        @pl.when(nxt >= 0)
        def _():
            for c in fetch(nxt, 1 - s):
                c.start()

    slot = lax.rem(ord_ref[0], 2)

    def mlp(n_rows):
        xb = xs_ref[0:n_rows, 0:D_MODEL].astype(BF16)
        wext = xs_ref[0:n_rows, D_MODEL:XS_COLS]
        lane = lax.broadcasted_iota(jnp.int32, wext.shape, 1)
        wcol = jnp.sum(jnp.where((lane == e) | (lane == e + N_EXPERTS), wext, 0.0), axis=1, keepdims=True)
        hcat = jnp.dot(xb, w1buf[slot].astype(BF16), preferred_element_type=F32) + b1_ref[0]
        x_glu = jnp.minimum(hcat[:, :D_EXPERT], SWIGLU_LIMIT)
        x_lin = jnp.clip(hcat[:, D_EXPERT:], -SWIGLU_LIMIT, SWIGLU_LIMIT)
        act = (x_glu * jax.nn.sigmoid(SWIGLU_ALPHA * x_glu) * (x_lin + 1.0)).astype(BF16)
        ys_ref[0:n_rows, :] = (jnp.dot(act, w2buf[slot].astype(BF16), preferred_element_type=F32) + b2_ref[0]) * wcol

    pl.when(used > half)(lambda: mlp(MOE_ROWS))

    @pl.when((used > 0) & (used <= half))
    def _():
        mlp(half)
        ys_ref[half:, :] = jnp.zeros((MOE_ROWS - half, D_MODEL), F32)

    @pl.when(used == 0)
    def _():
        ys_ref[...] = jnp.zeros(ys_ref.shape, F32)


def _experts(block_e, n_used, block_used, next_e, xs, w1, b1, w2, b2):
    p = xs.shape[0]
    mb = MOE_ROWS
    last = lambda i, nu: jnp.minimum(i, nu[0] - 1)
    rowmap = lambda i, be, nu, bv, nx: (last(i, nu), 0)
    wmap = lambda i, be, nu, bv, nx: (be[last(i, nu)], 0, 0)
    hbm = pl.BlockSpec(memory_space=pl.ANY)
    return pl.pallas_call(
        _experts_kernel,
        grid_spec=pltpu.PrefetchScalarGridSpec(
            num_scalar_prefetch=4,
            grid=(p // mb,),
            in_specs=[pl.BlockSpec((mb, XS_COLS), rowmap),
                      hbm,
                      pl.BlockSpec((1, 1, 2 * D_EXPERT), wmap),
                      hbm,
                      pl.BlockSpec((1, 1, D_MODEL), wmap)],
            out_specs=pl.BlockSpec((mb, D_MODEL), lambda i, be, nu, bv, nx: (i, 0)),
            scratch_shapes=[pltpu.VMEM((2, D_MODEL, 2 * D_EXPERT), F32),
                            pltpu.VMEM((2, D_EXPERT, D_MODEL), F32),
                            pltpu.SemaphoreType.DMA((2, 2)),
                            pltpu.SMEM((1,), jnp.int32)]),
        out_shape=jax.ShapeDtypeStruct((p, D_MODEL), F32),
        compiler_params=_params(("arbitrary",)),
        name="experts",
    )(block_e, n_used, block_used, next_e, xs, w1, b1, w2, b2)


def _combine_kernel(g8_ref, loff_ref, goff_ref, tot_ref, ys_ref, route_ref, x1_ref, g2_ref, fn_ref, o_ref, stg, sem):
    t = pl.program_id(0)
    nt = pl.num_programs(0)
    tm = x1_ref.shape[0]
    slot = lax.rem(t, 2)

    def copy(s, lo, go, rows):
        return pltpu.make_async_copy(ys_ref.at[pl.ds(go, rows), :], stg.at[s, pl.ds(lo, rows), :], sem.at[s])

    @pl.when(t == 0)
    def _():
        stg[...] = jnp.zeros(stg.shape, F32)
        _start_runs(t, g8_ref, loff_ref, goff_ref, functools.partial(copy, 0))

    pl.when(t + 1 < nt)(lambda: _start_runs(t + 1, g8_ref, loff_ref, goff_ref, functools.partial(copy, 1 - slot)))
    copy(slot, 0, 0, pl.multiple_of(tot_ref[t], GROUP_ALIGN)).wait()

    route = route_ref[...]
    low = [jnp.broadcast_to(route[:, k:k + 1], (tm, SLOT_SUB)).astype(BF16) for k in range(TOP_K)]
    sub = [jnp.broadcast_to(route[:, TOP_K + k:TOP_K + k + 1], (tm, SLOT_SUB)).astype(BF16) for k in range(TOP_K)]
    scol = lax.broadcasted_iota(jnp.int32, (tm, SLOT_SUB), 1).astype(F32).astype(BF16)
    one = jnp.ones((tm, SLOT_SUB), BF16)
    never = jnp.full((tm, SLOT_SUB), -1.0, BF16)

    y = None
    for ci in range(SLOTS_PER_TILE // SLOT_CHUNK):
        parts = []
        for b in range(SLOT_CHUNK // SLOT_SUB):
            blk = float(ci * (SLOT_CHUNK // SLOT_SUB) + b)
            own = jnp.zeros((tm, SLOT_SUB), BF16)
            for k in range(TOP_K):
                own = jnp.where(scol == jnp.where(sub[k] == blk, low[k], never), one, own)
            parts.append(own)
        own = jnp.concatenate(parts, axis=1)
        rows = stg[slot, ci * SLOT_CHUNK:(ci + 1) * SLOT_CHUNK, :].astype(BF16)
        part = jnp.dot(own, rows, preferred_element_type=F32)
        y = part if y is None else y + part
    x = x1_ref[...] + g2_ref[0] * y
    o_ref[...] = _rms(x) * fn_ref[...]


def _combine(g8f, lofff, gofff, totf, ys, route, x1, g2, fn, seq):
    n = x1.shape[0]
    tm = ROUTE_TILE
    per_b = seq // tm
    return pl.pallas_call(
        _combine_kernel,
        grid_spec=pltpu.PrefetchScalarGridSpec(
            num_scalar_prefetch=4,
            grid=(n // tm,),
            in_specs=[pl.BlockSpec(memory_space=pl.ANY),
                      pl.BlockSpec((tm, LANE), lambda i, *_: (i, 0)),
                      pl.BlockSpec((tm, D_MODEL), lambda i, *_: (i, 0)),
                      pl.BlockSpec((1, 1, D_MODEL), lambda i, *_: (i // per_b, 0, 0)),
                      pl.BlockSpec((1, D_MODEL), lambda i, *_: (0, 0))],
            out_specs=pl.BlockSpec((tm, D_MODEL), lambda i, *_: (i, 0)),
            scratch_shapes=[pltpu.VMEM((2, SLOTS_PER_TILE, D_MODEL), F32), pltpu.SemaphoreType.DMA((2,))]),
        out_shape=jax.ShapeDtypeStruct((n, D_MODEL), F32),
        compiler_params=_params(("arbitrary",)),
        name="combine",
    )(g8f, lofff, gofff, totf, ys, route, x1, g2, fn)


def _split_w_in(w_in):
    o_kr = _W_MIX_COLS
    w_kr = jnp.pad(w_in[:, o_kr:o_kr + MLA_ROPE_DIM],
                   ((0, 0), (MLA_NOPE_DIM, LANE - MLA_NOPE_DIM - MLA_ROPE_DIM)))
    return w_in[:, :o_kr].astype(BF16), w_kr.astype(BF16), w_in[:, o_kr + MLA_ROPE_DIM:].astype(BF16)


def _pack_heads(w, width, pieces):
    rows = w.shape[0]
    w3 = w.reshape(rows, MLA_HEADS, width)
    cols = [w3[:, :, a:b] for a, b in pieces]
    used = sum(b - a for a, b in pieces)
    cols.append(jnp.zeros((rows, MLA_HEADS, LANE - used), w.dtype))
    return jnp.concatenate(cols, axis=2).reshape(rows, MLA_HEADS * LANE)


def kernel(x, c, positions, w_ada, b_ada, norm_mix, norm_ffn, w_in, sinks, q_norm, kv_norm, w_uq, w_uk, w_uv,
           w_branch_a, w_branch_b, w_out, w_router, b_router, w_moe1, b_moe1, w_moe2, b_moe2, final_norm):
    batch, seq, d = x.shape
    n = batch * seq
    assert d == D_MODEL and w_ada.shape[0] == 1
    assert seq % TQ_MLA == 0 and seq % (SWA_QB * ATTN_BLOCK) == 0 and seq % TM_PROJ == 0 and seq % ROUTE_TILE == 0
    hw = MLA_HEADS * LANE
    qk = MLA_NOPE_DIM + MLA_ROPE_DIM

    w_mix, w_kr, w_gates = _split_w_in(w_in[0])
    wuq_p = _pack_heads(w_uq[0], qk, [(0, qk)]).astype(BF16)
    wuk_p = _pack_heads(w_uk[0], MLA_NOPE_DIM, [(0, MLA_NOPE_DIM)]).astype(BF16)
    wuv_p = _pack_heads(w_uv[0], MLA_V_DIM, [(0, MLA_V_DIM)]).astype(BF16)
    one_row = jnp.tile((jnp.arange(LANE) == MLA_V_DIM).astype(F32), MLA_HEADS)[None, :]
    freqs = ROPE_THETA ** (-jnp.arange(0, MLA_ROPE_DIM, 2, dtype=F32) / MLA_ROPE_DIM)
    frq = jnp.concatenate([jnp.zeros((MLA_NOPE_DIM,), F32), freqs, freqs,
                           jnp.zeros((LANE - qk,), F32)])[None, :]
    wb3 = w_branch_b[0].reshape(MLA_HEADS, MLA_V_DIM, D_MODEL)
    wb_p = jnp.concatenate([wb3, jnp.zeros((MLA_HEADS, LANE - MLA_V_DIM, D_MODEL), F32)],
                           axis=1).reshape(hw, D_MODEL).astype(BF16)
    wa = w_branch_a[0].astype(BF16)
    wo = w_out[0].astype(BF16)
    wr = jnp.pad(w_router[0], ((0, 0), (0, LANE - N_EXPERTS)))
    wr_hi = wr.astype(BF16)
    wr_lo = (wr - wr_hi.astype(F32)).astype(BF16)
    br_p = jnp.pad(b_router[0], (0, LANE - N_EXPERTS))[None, :]
    b1 = b_moe1[0][:, None, :]
    b2 = b_moe2[0][:, None, :]
    slopes = jnp.asarray(np.exp2(-8.0 * np.arange(1, SWA_HEADS + 1) / SWA_HEADS), dtype=F32)

    c8 = jnp.pad(c, ((0, 8 - batch), (0, 0)))
    mod = _ada(c8, w_ada[0], b_ada[0][None, :])[:batch]
    sh1, sc1, g1, sh2, sc2, g2 = [m[:, None, :] for m in jnp.split(mod, 6, axis=-1)]

    x2 = x.reshape(n, D_MODEL)
    pos2 = positions.reshape(n, 1).astype(jnp.int32)
    qa, ka2, va2, qm, km, vm, ga, gb, stats = _inproj(
        x2, pos2, sh1, sc1, norm_mix, w_mix, w_kr, w_gates, q_norm, kv_norm, wuq_p, wuk_p, wuv_p, frq, one_row, seq)
    st = stats.reshape(batch, seq // TM_PROJ, 8, LANE)
    bound_sq = jnp.max(st[:, :, :, 0], axis=1) * jnp.max(st[:, :, :, 1], axis=1)
    plain = (bound_sq <= MLA_PLAIN_BOUND ** 2).astype(jnp.int32).reshape(-1)
    yb = _mla(plain, qm, km, vm, batch, seq)
    ya = _swa(sinks[0], slopes, qa, ka2, va2, batch, seq)

    x1, h2, wext, route, routet, g8 = _outproj(
        ya, yb, ga, gb, x2, g1, sh2, sc2, norm_ffn, wa, wb_p, wo, wr_hi, wr_lo, br_p, seq)

    n_tiles = n // ROUTE_TILE
    g8t = g8.reshape(n_tiles, 8, LANE)[:, 0, :N_EXPERTS].astype(jnp.int32)
    loff = jnp.cumsum(g8t, axis=1) - g8t
    padded = ((jnp.sum(g8t, axis=0) + MOE_ROWS - 1) // MOE_ROWS) * MOE_ROWS
    pends = jnp.cumsum(padded)
    goff = (pends - padded)[None, :] + jnp.cumsum(g8t, axis=0) - g8t
    tot = jnp.sum(g8t, axis=1).astype(jnp.int32)
    p_rows = n * TOP_K + n_tiles * N_EXPERTS * (GROUP_ALIGN - 1) + N_EXPERTS * (MOE_ROWS - 1)
    p_rows = -(-p_rows // MOE_ROWS) * MOE_ROWS
    n_blocks = p_rows // MOE_ROWS
    block_start = jnp.arange(n_blocks, dtype=jnp.int32) * MOE_ROWS
    block_e = jnp.minimum(jnp.sum((pends[None, :] <= block_start[:, None]).astype(jnp.int32), axis=1),
                          N_EXPERTS - 1).astype(jnp.int32)
    n_used = (pends[-1:] // MOE_ROWS).astype(jnp.int32)
    eids = jnp.arange(N_EXPERTS, dtype=jnp.int32)
    of_block = block_e[:, None] == eids[None, :]
    pick = lambda table: jnp.sum(jnp.where(of_block, table[None, :], 0), axis=1).astype(jnp.int32)
    real_end = pends - padded + jnp.sum(g8t, axis=0)
    block_used = jnp.clip(pick(real_end) - block_start, 0, MOE_ROWS).astype(jnp.int32)
    later = (eids[None, :] > eids[:, None]) & (padded[None, :] > 0)
    after = jnp.min(jnp.where(later, eids[None, :], N_EXPERTS), axis=1)
    next_e = pick(jnp.where(after == N_EXPERTS, -1, after))
    tabs = (g8t.reshape(-1), loff.reshape(-1).astype(jnp.int32), goff.reshape(-1).astype(jnp.int32), tot)

    xs = _dispatch(*tabs, jnp.concatenate([pends, real_end]).astype(jnp.int32), h2, wext, routet, p_rows)
    ys = _experts(block_e, n_used, block_used, next_e, xs, w_moe1[0], b1, w_moe2[0], b2)
    out = _combine(*tabs, ys, route, x1, g2, final_norm[None, :], seq)
    return out.reshape(batch, seq, D_MODEL)
```

```python
import functools

import numpy as np
import jax
import jax.numpy as jnp
from jax import lax
from jax.experimental import pallas as pl
from jax.experimental.pallas import tpu as pltpu

D_MODEL = 1024
SWA_HEADS = 8
SWA_KV_HEADS = 2
SWA_HEAD_DIM = 64
ATTN_BLOCK = 128
MLA_HEADS = 8
MLA_Q_RANK = 384
MLA_KV_RANK = 256
MLA_NOPE_DIM = 64
MLA_ROPE_DIM = 32
MLA_V_DIM = 64
ROPE_THETA = 10000.0
N_EXPERTS = 32
TOP_K = 4
D_EXPERT = 1024
SWIGLU_LIMIT = 7.0
SWIGLU_ALPHA = 1.702
NORM_EPS = 1e-6

LANE = 128
LOG2E = 1.4426950408889634
NEG = -1e30
SWA_QSCALE = SWA_HEAD_DIM ** -0.5 * LOG2E
MLA_QSCALE = (MLA_NOPE_DIM + MLA_ROPE_DIM) ** -0.5 * LOG2E
HALF_ROPE = MLA_ROPE_DIM // 2
MLA_PLAIN_BOUND = 60.0
MLA_WIDE = 8

BF16 = jnp.bfloat16
F32 = jnp.float32

TM_PROJ = 512
TQ_MLA = 512
SWA_QB = 16
MOE_ROWS = 512
ROUTE_TILE = 512
GROUP_ALIGN = 8
SLOTS_PER_TILE = ROUTE_TILE * TOP_K + N_EXPERTS * GROUP_ALIGN
SLOT_SUB = 256
SLOT_CHUNK = 3 * SLOT_SUB
XS_COLS = D_MODEL + LANE
VMEM_LIMIT = 56 * 1024 * 1024


def _params(sem, vmem=VMEM_LIMIT):
    return pltpu.CompilerParams(dimension_semantics=sem, vmem_limit_bytes=vmem)


def _rms(x):
    return x * lax.rsqrt(jnp.mean(x * x, axis=-1, keepdims=True) + NORM_EPS)


def _ada_kernel(c_ref, w_ref, b_ref, o_ref):
    c = c_ref[...]
    a = (c * jax.nn.sigmoid(c)).astype(BF16)
    o_ref[...] = jnp.dot(a, w_ref[...].astype(BF16), preferred_element_type=F32) + b_ref[...]


def _ada(c8, w_ada, b_ada):
    n_out = w_ada.shape[1]
    return pl.pallas_call(
        _ada_kernel,
        grid=(n_out // D_MODEL,),
        in_specs=[pl.BlockSpec((8, D_MODEL), lambda j: (0, 0)),
                  pl.BlockSpec((D_MODEL, D_MODEL), lambda j: (0, j)),
                  pl.BlockSpec((1, D_MODEL), lambda j: (0, j))],
        out_specs=pl.BlockSpec((8, D_MODEL), lambda j: (0, j)),
        out_shape=jax.ShapeDtypeStruct((8, n_out), F32),
        compiler_params=_params(("arbitrary",)),
        name="ada",
    )(c8, w_ada, b_ada)


_C_QA = (0, 512)
_C_KV = (512, 768)
_C_LAT = (768, 1408)
_W_MIX_COLS = 1408
_ROPE_LO = (MLA_NOPE_DIM, MLA_NOPE_DIM + HALF_ROPE)
_ROPE_HI = (MLA_NOPE_DIM + HALF_ROPE, MLA_NOPE_DIM + MLA_ROPE_DIM)


def _inproj_kernel(x_ref, pos_ref, sh_ref, sc_ref, nm_ref, win_ref, wkr_ref, wg_ref, qn_ref, kvn_ref, wuq_ref, wuk_ref,
                   wuv_ref, frq_ref, one_ref,
                   qa_ref, ka_ref, va_ref, qm_ref, km_ref, vm_ref, ga_ref, gb_ref, st_ref):
    tm = x_ref.shape[0]
    x = x_ref[...]
    h = (_rms(x) * nm_ref[...] * (1.0 + sc_ref[0]) + sh_ref[0]).astype(BF16)

    def proj(c):
        return jnp.dot(h, win_ref[:, c[0]:c[1]], preferred_element_type=F32)

    lane = lax.broadcasted_iota(jnp.int32, (tm, LANE), 1)
    first = lane < SWA_HEAD_DIM

    def twice(t):
        r = pltpu.roll(t, SWA_HEAD_DIM, axis=1)
        return jnp.concatenate([jnp.where(first, t, r), jnp.where(first, r, t)], axis=1)

    kv = proj(_C_KV)
    ka_ref[...] = twice(kv[:, :LANE]).astype(BF16)
    va_ref[...] = twice(kv[:, LANE:]).astype(BF16)

    ang = pos_ref[...].astype(F32) * frq_ref[...]
    cs = jnp.cos(ang)
    sn = jnp.sin(ang)
    sn_lo = jnp.where((lane >= _ROPE_LO[0]) & (lane < _ROPE_LO[1]), -sn, 0.0)
    sn_hi = jnp.where((lane >= _ROPE_HI[0]) & (lane < _ROPE_HI[1]), sn, 0.0)

    def rotary(t):
        return (t * cs + pltpu.roll(t, LANE - HALF_ROPE, axis=1) * sn_lo
                + pltpu.roll(t, HALF_ROPE, axis=1) * sn_hi)

    lat = proj(_C_LAT)
    cqn = (_rms(lat[:, 0:MLA_Q_RANK]) * qn_ref[...]).astype(BF16)
    ckvn = (_rms(lat[:, MLA_Q_RANK:MLA_Q_RANK + MLA_KV_RANK]) * kvn_ref[...]).astype(BF16)
    krr = rotary(jnp.dot(h, wkr_ref[...], preferred_element_type=F32))
    q = jnp.dot(cqn, wuq_ref[...], preferred_element_type=F32)
    kn = jnp.dot(ckvn, wuk_ref[...], preferred_element_type=F32)

    def max_sq_norm(t):
        tf = t.astype(F32)
        return jnp.max(jnp.sum(tf * tf, axis=1, keepdims=True), axis=0, keepdims=True)

    rid = lax.broadcasted_iota(jnp.int32, (8, LANE), 0)
    lid = lax.broadcasted_iota(jnp.int32, (8, LANE), 1)
    stats = jnp.zeros((8, LANE), F32)
    for hh in range(MLA_HEADS):
        sl = slice(hh * LANE, (hh + 1) * LANE)
        qb = (rotary(q[:, sl]) * MLA_QSCALE).astype(BF16)
        kb = (kn[:, sl] + krr).astype(BF16)
        qm_ref[:, sl] = qb
        km_ref[:, sl] = kb
        stats = jnp.where((rid == hh) & (lid == 0), max_sq_norm(qb), stats)
        stats = jnp.where((rid == hh) & (lid == 1), max_sq_norm(kb), stats)
    st_ref[...] = stats
    vm_ref[...] = (jnp.dot(ckvn, wuv_ref[...], preferred_element_type=F32) + one_ref[...]).astype(BF16)
    qa_ref[...] = (proj(_C_QA) * SWA_QSCALE).astype(BF16)
    ga_ref[...] = jax.nn.sigmoid(jnp.dot(h, wg_ref[:, 0:D_MODEL], preferred_element_type=F32)).astype(BF16)
    gb_ref[...] = jax.nn.sigmoid(jnp.dot(h, wg_ref[:, D_MODEL:], preferred_element_type=F32)).astype(BF16)


def _inproj(x2, pos2, sh1, sc1, nm, w_mix, w_kr, w_gates, qn, kvn, wuq_p, wuk_p, wuv_p, frq, one_row, seq):
    n = x2.shape[0]
    tm = TM_PROJ
    per_b = seq // tm
    hw = MLA_HEADS * LANE
    row = lambda i: (i, 0)
    fix = lambda i: (0, 0)
    bsel = lambda i: (i // per_b, 0, 0)
    widths = (512, 256, 256, hw, hw, hw, D_MODEL, D_MODEL)
    return pl.pallas_call(
        _inproj_kernel,
        grid=(n // tm,),
        in_specs=[pl.BlockSpec((tm, D_MODEL), row),
                  pl.BlockSpec((tm, 1), row),
                  pl.BlockSpec((1, 1, D_MODEL), bsel),
                  pl.BlockSpec((1, 1, D_MODEL), bsel),
                  pl.BlockSpec((1, D_MODEL), fix),
                  pl.BlockSpec((D_MODEL, _W_MIX_COLS), fix),
                  pl.BlockSpec((D_MODEL, LANE), fix),
                  pl.BlockSpec((D_MODEL, 2 * D_MODEL), fix),
                  pl.BlockSpec((1, MLA_Q_RANK), fix),
                  pl.BlockSpec((1, MLA_KV_RANK), fix),
                  pl.BlockSpec((MLA_Q_RANK, hw), fix),
                  pl.BlockSpec((MLA_KV_RANK, hw), fix),
                  pl.BlockSpec((MLA_KV_RANK, hw), fix),
                  pl.BlockSpec((1, LANE), fix),
                  pl.BlockSpec((1, hw), fix)],
        out_specs=[pl.BlockSpec((tm, w), row) for w in widths] + [pl.BlockSpec((8, LANE), row)],
        out_shape=[jax.ShapeDtypeStruct((n, w), BF16) for w in widths]
        + [jax.ShapeDtypeStruct((n // tm * 8, LANE), F32)],
        compiler_params=_params(("arbitrary",)),
        name="inproj",
    )(x2, pos2, sh1, sc1, nm, w_mix, w_kr, w_gates, qn, kvn, wuq_p, wuk_p, wuv_p, frq, one_row)


def _mla_kernel(plain_ref, q_ref, k_ref, v_ref, o_ref, m_scr, acc_scr):
    t = TQ_MLA
    nq = q_ref.shape[0] // t

    def scores(i, j, nblk, diag_at):
        q = q_ref[pl.ds(pl.multiple_of(i * t, t), t), :]
        start = pl.multiple_of(j * t, t)
        k = k_ref[pl.ds(start, nblk * t), :]
        v = v_ref[pl.ds(start, nblk * t), :]
        s = lax.dot_general(q, k, (((1,), (1,)), ((), ())), preferred_element_type=F32)
        if diag_at is not None:
            qi = lax.broadcasted_iota(jnp.int32, s.shape, 0)
            kj = lax.broadcasted_iota(jnp.int32, s.shape, 1)
            s = jnp.where(kj <= qi + diag_at * t, s, NEG)
        return s, v

    def plain_step(i, j, nblk, diag_at):
        s, v = scores(i, j, nblk, diag_at)
        acc_scr[...] += jnp.dot(jnp.exp2(s).astype(BF16), v, preferred_element_type=F32)

    def online_step(i, j, nblk, diag_at):
        s, v = scores(i, j, nblk, diag_at)
        m_old = m_scr[...]
        m_new = jnp.maximum(m_old, jnp.max(s, axis=1, keepdims=True))
        p = jnp.exp2(s - m_new)
        alpha = jnp.exp2(m_old - m_new)
        acc_scr[...] = acc_scr[...] * alpha + jnp.dot(p.astype(BF16), v, preferred_element_type=F32)
        m_scr[...] = m_new

    def run(step, wide, online):
        def q_tile(i, carry):
            acc_scr[...] = jnp.zeros(acc_scr.shape, F32)
            if online:
                m_scr[...] = jnp.full(m_scr.shape, NEG, F32)
            nw = lax.shift_right_logical(i, wide.bit_length() - 1)
            left = i - nw * wide

            def wide_body(jj, c):
                step(i, jj * wide, wide, None)
                return c

            lax.fori_loop(0, nw, wide_body, 0)
            for r in range(wide):
                pl.when(left == r)(lambda r=r: step(i, nw * wide, r + 1, r))
            acc = acc_scr[...]
            o_ref[pl.ds(pl.multiple_of(i * t, t), t), :] = (acc / acc[:, MLA_V_DIM:MLA_V_DIM + 1]).astype(BF16)
            return carry

        lax.fori_loop(0, nq, q_tile, 0)

    small = plain_ref[pl.program_id(0) * MLA_HEADS + pl.program_id(1)] != 0
    pl.when(small)(lambda: run(plain_step, MLA_WIDE, False))
    pl.when(jnp.logical_not(small))(lambda: run(online_step, 1, True))


def _mla(plain, qm, km, vm, batch, seq):
    n = qm.shape[0]
    t = TQ_MLA
    kmap = lambda b, h, *_: (b, h)
    return pl.pallas_call(
        _mla_kernel,
        grid_spec=pltpu.PrefetchScalarGridSpec(
            num_scalar_prefetch=1,
            grid=(batch, MLA_HEADS),
            in_specs=[pl.BlockSpec((seq, LANE), kmap),
                      pl.BlockSpec((seq, LANE), kmap),
                      pl.BlockSpec((seq, LANE), kmap)],
            out_specs=pl.BlockSpec((seq, LANE), kmap),
            scratch_shapes=[pltpu.VMEM((t, 1), F32), pltpu.VMEM((t, LANE), F32)]),
        out_shape=jax.ShapeDtypeStruct((n, MLA_HEADS * LANE), BF16),
        compiler_params=_params(("arbitrary", "arbitrary")),
        name="mla",
    )(plain, qm, km, vm)


def _swa_kernel(sink_ref, slope_ref, q_ref, kc_ref, kp_ref, vc_ref, vp_ref, o_ref, bias_scr):
    b = pl.program_id(0)
    i = pl.program_id(1)
    blk = ATTN_BLOCK

    @pl.when((b == 0) & (i == 0))
    def _():
        qi = lax.broadcasted_iota(jnp.int32, (blk, 2 * blk), 0)
        kj = lax.broadcasted_iota(jnp.int32, (blk, 2 * blk), 1)
        dist = qi - kj + blk
        valid = (dist >= 0) & (dist < blk)
        distf = dist.astype(F32)
        for hd in range(SWA_HEADS):
            bias_scr[hd] = jnp.where(valid, -(slope_ref[hd] * LOG2E) * distf, NEG)

    lane = lax.broadcasted_iota(jnp.int32, (2 * blk, LANE), 1)
    lo = lane < SWA_HEAD_DIM
    kcol = lax.broadcasted_iota(jnp.int32, (blk, 2 * blk), 1)
    first_pen = jnp.where(kcol < blk, jnp.where(i == 0, NEG, 0.0), 0.0)
    zero = jnp.zeros((2 * blk, LANE), BF16)

    for qb in range(SWA_QB):
        if qb == 0:
            kprev, vprev = kp_ref[...], vp_ref[...]
        else:
            kprev = kc_ref[(qb - 1) * blk:qb * blk, :]
            vprev = vc_ref[(qb - 1) * blk:qb * blk, :]
        kcat = jnp.concatenate([kprev, kc_ref[qb * blk:(qb + 1) * blk, :]], axis=0)
        vcat = jnp.concatenate([vprev, vc_ref[qb * blk:(qb + 1) * blk, :]], axis=0)
        for g in range(SWA_KV_HEADS):
            kg = kcat[:, g * LANE:(g + 1) * LANE]
            vg = vcat[:, g * LANE:(g + 1) * LANE]
            halves = ((jnp.where(lo, kg, zero), jnp.where(lo, vg, zero)),
                      (jnp.where(lo, zero, kg), jnp.where(lo, zero, vg)))
            for pp in range(2):
                pr = g * 2 + pp
                qpair = q_ref[qb * blk:(qb + 1) * blk, pr * LANE:(pr + 1) * LANE]
                out = None
                for half in range(2):
                    hd = pr * 2 + half
                    kx, vx = halves[half]
                    s = lax.dot_general(qpair, kx, (((1,), (1,)), ((), ())), preferred_element_type=F32)
                    s = s + bias_scr[hd]
                    if qb == 0:
                        s = s + first_pen
                    sink = sink_ref[hd] * LOG2E
                    m = jnp.maximum(jnp.max(s, axis=1, keepdims=True), sink)
                    p = jnp.exp2(s - m)
                    denom = jnp.sum(p, axis=1, keepdims=True) + jnp.exp2(sink - m)
                    o = jnp.dot(p.astype(BF16), vx, preferred_element_type=F32) * (1.0 / denom)
                    out = o if out is None else out + o
                o_ref[qb * blk:(qb + 1) * blk, pr * LANE:(pr + 1) * LANE] = out.astype(BF16)


def _swa(sinks, slopes, qa, ka2, va2, batch, seq):
    n = qa.shape[0]
    blk = ATTN_BLOCK
    rows = SWA_QB * blk
    steps = seq // rows
    cur = lambda b, i: (b * steps + i, 0)
    prev = lambda b, i: (b * (seq // blk) + jnp.maximum(i * SWA_QB - 1, 0), 0)
    smem = pl.BlockSpec(memory_space=pltpu.SMEM)
    return pl.pallas_call(
        _swa_kernel,
        grid=(batch, steps),
        in_specs=[smem, smem,
                  pl.BlockSpec((rows, 512), cur),
                  pl.BlockSpec((rows, 256), cur),
                  pl.BlockSpec((blk, 256), prev),
                  pl.BlockSpec((rows, 256), cur),
                  pl.BlockSpec((blk, 256), prev)],
        out_specs=pl.BlockSpec((rows, 512), cur),
        out_shape=jax.ShapeDtypeStruct((n, 512), BF16),
        scratch_shapes=[pltpu.VMEM((SWA_HEADS, blk, 2 * blk), F32)],
        compiler_params=_params(("arbitrary", "arbitrary")),
        name="swa",
    )(sinks, slopes, qa, ka2, ka2, va2, va2)


def _outproj_kernel(ya_ref, yb_ref, ga_ref, gb_ref, x_ref, g1_ref, sh_ref, sc_ref, nm_ref, wa_ref, wb_ref,
                    wo_ref, wrh_ref, wrl_ref, br_ref,
                    x1_ref, h2_ref, wext_ref, route_ref, routet_ref, g8_ref, lg_scr):
    t = pl.program_id(0)
    tm = x_ref.shape[0]
    slot = lax.rem(t, 2)

    @pl.when(t == 0)
    def _():
        lg_scr[...] = jnp.zeros(lg_scr.shape, F32)

    pj = {}

    def stage_a():
        pj['a'] = jnp.dot(ya_ref[...], wa_ref[...], preferred_element_type=F32)

    def stage_b():
        pj['bm'] = jnp.dot(yb_ref[...], wb_ref[...], preferred_element_type=F32)

    def stage_out():
        mixed = (ga_ref[...].astype(F32) * pj['a'] + gb_ref[...].astype(F32) * pj['bm']).astype(BF16)
        pj['x1'] = x_ref[...] + g1_ref[0] * jnp.dot(mixed, wo_ref[...], preferred_element_type=F32)
        x1_ref[...] = pj['x1']

    def stage_h2():
        h2 = _rms(pj['x1']) * nm_ref[...] * (1.0 + sc_ref[0]) + sh_ref[0]
        hi = h2.astype(BF16)
        h2_ref[...] = hi
        lo = (h2 - hi.astype(F32)).astype(BF16)
        lg_scr[slot] = (jnp.dot(hi, wrh_ref[...], preferred_element_type=F32)
                        + jnp.dot(lo, wrh_ref[...], preferred_element_type=F32)
                        + jnp.dot(hi, wrl_ref[...], preferred_element_type=F32)) + br_ref[...]

    stages = (stage_a, stage_b, stage_out)
    project = stage_h2

    logits = lg_scr[1 - slot]
    lane = lax.broadcasted_iota(jnp.int32, (tm, LANE), 1).astype(F32)
    work = jnp.where(lane < N_EXPERTS, logits, -jnp.inf)
    vals, idxs = [], []
    for rnd in range(TOP_K):
        m = jnp.max(work, axis=1, keepdims=True)
        idx = jnp.min(jnp.where(work == m, lane, float(LANE)), axis=1, keepdims=True)
        vals.append(m)
        idxs.append(idx)
        work = jnp.where(lane == idx, -jnp.inf, work)
        if rnd < len(stages):
            stages[rnd]()
    exps = [jnp.exp(v - vals[0]) for v in vals]
    tot = exps[0] + exps[1] + exps[2] + exps[3]

    onehot = jnp.zeros((tm, LANE), F32)
    for idx in idxs:
        onehot = onehot + (lane == idx).astype(F32)
    r = lax.broadcasted_iota(jnp.int32, (tm, tm), 0)
    c = lax.broadcasted_iota(jnp.int32, (tm, tm), 1)
    prefix = jnp.dot((c < r).astype(BF16), onehot.astype(BF16), preferred_element_type=F32)
    groups = jnp.ceil(jnp.sum(onehot, axis=0, keepdims=True) * (1.0 / GROUP_ALIGN))
    er = lax.broadcasted_iota(jnp.int32, (LANE, LANE), 0)
    ec = lax.broadcasted_iota(jnp.int32, (LANE, LANE), 1)
    before = jnp.dot(jnp.broadcast_to(groups, (8, LANE)).astype(BF16), (er < ec).astype(BF16),
                     preferred_element_type=F32)[0:1, :]
    slot_e = before * GROUP_ALIGN + prefix

    route = jnp.zeros((tm, LANE), F32)
    wext = jnp.zeros((tm, LANE), F32)
    for k in range(TOP_K):
        sk = jnp.sum(jnp.where(lane == idxs[k], slot_e, 0.0), axis=1, keepdims=True)
        sub = jnp.floor(sk * (1.0 / SLOT_SUB))
        route = jnp.where(lane == float(k), sk - sub * SLOT_SUB, route)
        route = jnp.where(lane == float(TOP_K + k), sub, route)
        wk = exps[k] / tot
        wk_hi = wk.astype(BF16).astype(F32)
        wext = jnp.where(lane == idxs[k], wk_hi, wext)
        wext = jnp.where(lane == idxs[k] + float(N_EXPERTS), wk - wk_hi, wext)
    wext_ref[...] = wext.astype(BF16)
    route_ref[...] = route
    routet_ref[...] = route.T[0:8, :]
    g8_ref[...] = jnp.broadcast_to(groups * GROUP_ALIGN, (8, LANE))
    project()


def _outproj(ya, yb, ga, gb, x2, g1, sh2, sc2, nm, wa, wb_p, wo, wr_hi, wr_lo, br_p, seq):
    n = x2.shape[0]
    tm = ROUTE_TILE
    per_b = seq // tm
    hw = MLA_HEADS * LANE
    n_tiles = n // tm
    cur = lambda i: jnp.minimum(i, n_tiles - 1)
    prv = lambda i: jnp.maximum(i - 1, 0)
    row = lambda i: (cur(i), 0)
    rrow = lambda i: (prv(i), 0)
    fix = lambda i: (0, 0)
    bsel = lambda i: (cur(i) // per_b, 0, 0)
    return pl.pallas_call(
        _outproj_kernel,
        grid=(n_tiles + 1,),
        in_specs=[pl.BlockSpec((tm, 512), row),
                  pl.BlockSpec((tm, hw), row),
                  pl.BlockSpec((tm, D_MODEL), row),
                  pl.BlockSpec((tm, D_MODEL), row),
                  pl.BlockSpec((tm, D_MODEL), row),
                  pl.BlockSpec((1, 1, D_MODEL), bsel),
                  pl.BlockSpec((1, 1, D_MODEL), bsel),
                  pl.BlockSpec((1, 1, D_MODEL), bsel),
                  pl.BlockSpec((1, D_MODEL), fix),
                  pl.BlockSpec((512, D_MODEL), fix),
                  pl.BlockSpec((hw, D_MODEL), fix),
                  pl.BlockSpec((D_MODEL, D_MODEL), fix),
                  pl.BlockSpec((D_MODEL, LANE), fix),
                  pl.BlockSpec((D_MODEL, LANE), fix),
                  pl.BlockSpec((1, LANE), fix)],
        out_specs=[pl.BlockSpec((tm, D_MODEL), row),
                   pl.BlockSpec((tm, D_MODEL), row),
                   pl.BlockSpec((tm, LANE), rrow),
                   pl.BlockSpec((tm, LANE), rrow),
                   pl.BlockSpec((8, tm), lambda i: (0, prv(i))),
                   pl.BlockSpec((8, LANE), rrow)],
        out_shape=[jax.ShapeDtypeStruct((n, D_MODEL), F32),
                   jax.ShapeDtypeStruct((n, D_MODEL), BF16),
                   jax.ShapeDtypeStruct((n, LANE), BF16),
                   jax.ShapeDtypeStruct((n, LANE), F32),
                   jax.ShapeDtypeStruct((8, n), F32),
                   jax.ShapeDtypeStruct((n_tiles * 8, LANE), F32)],
        scratch_shapes=[pltpu.VMEM((2, tm, LANE), F32)],
        compiler_params=_params(("arbitrary",)),
        name="outproj",
    )(ya, yb, ga, gb, x2, g1, sh2, sc2, nm, wa, wb_p, wo, wr_hi, wr_lo, br_p)


def _start_runs(tile, g8_ref, loff_ref, goff_ref, make_copy):
    for e in range(N_EXPERTS):
        g = g8_ref[tile * N_EXPERTS + e]
        lo = loff_ref[tile * N_EXPERTS + e]
        go = goff_ref[tile * N_EXPERTS + e]
        def go_copy(lo=lo, go=go, g=g):
            make_copy(pl.multiple_of(lo, GROUP_ALIGN), pl.multiple_of(go, GROUP_ALIGN),
                      pl.multiple_of(g, GROUP_ALIGN)).start()

        pl.when(g > 0)(go_copy)


def _dispatch_kernel(g8_ref, loff_ref, goff_ref, tot_ref, ends_ref, h_ref, w_ref, rt_ref, xs_ref, buf, zbuf, sem,
                     zsem):
    t = pl.program_id(0)
    nt = pl.num_programs(0)
    tm = h_ref.shape[0]
    slot = lax.rem(t, 2)

    @pl.when(t == 0)
    def _():
        zbuf[...] = jnp.zeros(zbuf.shape, zbuf.dtype)

        def fill(start, rows):
            return pltpu.make_async_copy(zbuf.at[pl.ds(0, rows), :], xs_ref.at[pl.ds(start, rows), :], zsem)

        def pads(e):
            real_end = pl.multiple_of(ends_ref[N_EXPERTS + e], GROUP_ALIGN)
            return real_end, pl.multiple_of(ends_ref[e] - real_end, GROUP_ALIGN)

        used = lax.shift_right_logical(ends_ref[N_EXPERTS - 1], MOE_ROWS.bit_length() - 1)
        n_blocks = xs_ref.shape[0] // MOE_ROWS
        tail = lambda b: fill(pl.multiple_of(b * MOE_ROWS, MOE_ROWS), MOE_ROWS)
        for e in range(N_EXPERTS):
            pl.when(pads(e)[1] > 0)(lambda e=e: fill(*pads(e)).start())
        lax.fori_loop(used, n_blocks, lambda b, c: (tail(b).start(), c)[1], 0)
        for e in range(N_EXPERTS):
            pl.when(pads(e)[1] > 0)(lambda e=e: fill(*pads(e)).wait())
        lax.fori_loop(used, n_blocks, lambda b, c: (tail(b).wait(), c)[1], 0)

    hb = jnp.concatenate([h_ref[...], w_ref[...]], axis=1)
    low = [rt_ref[k:k + 1, :] for k in range(TOP_K)]
    sub = [rt_ref[TOP_K + k:TOP_K + k + 1, :] for k in range(TOP_K)]
    srow = lax.broadcasted_iota(jnp.int32, (SLOT_SUB, tm), 0).astype(F32).astype(BF16)
    one = jnp.ones((SLOT_SUB, tm), BF16)

    for ci in range(SLOTS_PER_TILE // SLOT_CHUNK):
        parts = []
        for b in range(SLOT_CHUNK // SLOT_SUB):
            blk = float(ci * (SLOT_CHUNK // SLOT_SUB) + b)
            own = jnp.zeros((SLOT_SUB, tm), BF16)
            for k in range(TOP_K):
                here = jnp.where(sub[k] == blk, low[k], -1.0).astype(BF16)
                own = jnp.where(srow == here, one, own)
            parts.append(own)
        own = jnp.concatenate(parts, axis=0)
        buf[slot, ci * SLOT_CHUNK:(ci + 1) * SLOT_CHUNK, :] = jnp.dot(own, hb, preferred_element_type=F32)

    def copy(s, lo, go, rows):
        return pltpu.make_async_copy(buf.at[s, pl.ds(lo, rows), :], xs_ref.at[pl.ds(go, rows), :], sem.at[s])

    _start_runs(t, g8_ref, loff_ref, goff_ref, functools.partial(copy, slot))

    def drain(tile, s):
        rows = pl.multiple_of(tot_ref[tile], GROUP_ALIGN)
        copy(s, 0, 0, rows).wait()

    pl.when(t > 0)(lambda: drain(t - 1, 1 - slot))
    pl.when(t == nt - 1)(lambda: drain(t, slot))


def _dispatch(g8f, lofff, gofff, totf, ends, h2, wext, routet, p_rows):
    n = h2.shape[0]
    tm = ROUTE_TILE
    return pl.pallas_call(
        _dispatch_kernel,
        grid_spec=pltpu.PrefetchScalarGridSpec(
            num_scalar_prefetch=5,
            grid=(n // tm,),
            in_specs=[pl.BlockSpec((tm, D_MODEL), lambda i, *_: (i, 0)),
                      pl.BlockSpec((tm, LANE), lambda i, *_: (i, 0)),
                      pl.BlockSpec((8, tm), lambda i, *_: (0, i))],
            out_specs=pl.BlockSpec(memory_space=pl.ANY),
            scratch_shapes=[pltpu.VMEM((2, SLOTS_PER_TILE, XS_COLS), F32),
                            pltpu.VMEM((MOE_ROWS, XS_COLS), F32),
                            pltpu.SemaphoreType.DMA((2,)), pltpu.SemaphoreType.DMA(())]),
        out_shape=jax.ShapeDtypeStruct((p_rows, XS_COLS), F32),
        compiler_params=_params(("arbitrary",)),
        name="dispatch",
    )(g8f, lofff, gofff, totf, ends, h2, wext, routet)


def _experts_kernel(be_ref, nu_ref, bv_ref, nx_ref, xs_ref, w1_hbm, b1_ref, w2_hbm, b2_ref, ys_ref,
                    w1buf, w2buf, wsem, ord_ref):
    del nu_ref
    i = pl.program_id(0)
    half = MOE_ROWS // 2
    used = bv_ref[i]
    e = be_ref[i]

    def fetch(ex, s):
        return (pltpu.make_async_copy(w1_hbm.at[ex], w1buf.at[s], wsem.at[0, s]),
                pltpu.make_async_copy(w2_hbm.at[ex], w2buf.at[s], wsem.at[1, s]))

    @pl.when(i == 0)
    def _():
        ord_ref[0] = 0
        for c in fetch(e, 0):
            c.start()

    @pl.when((used > 0) & ((i == 0) | (e != be_ref[jnp.maximum(i - 1, 0)])))
    def _():
        @pl.when(i > 0)
        def _():
            ord_ref[0] = ord_ref[0] + 1

        s = lax.rem(ord_ref[0], 2)
        for c in fetch(e, s):
            c.wait()
        nxt = nx_ref[i]

        @pl.when(nxt >= 0)
        def _():
            for c in fetch(nxt, 1 - s):
                c.start(priority=1)

    slot = lax.rem(ord_ref[0], 2)

    def mlp(n_rows):
        xb = xs_ref[0:n_rows, 0:D_MODEL].astype(BF16)
        wext = xs_ref[0:n_rows, D_MODEL:XS_COLS]
        lane = lax.broadcasted_iota(jnp.int32, wext.shape, 1)
        wcol = jnp.sum(jnp.where((lane == e) | (lane == e + N_EXPERTS), wext, 0.0), axis=1, keepdims=True)
        hcat = jnp.dot(xb, w1buf[slot].astype(BF16), preferred_element_type=F32) + b1_ref[0]
        x_glu = jnp.minimum(hcat[:, :D_EXPERT], SWIGLU_LIMIT)
        x_lin = jnp.clip(hcat[:, D_EXPERT:], -SWIGLU_LIMIT, SWIGLU_LIMIT)
        act = (x_glu * jax.nn.sigmoid(SWIGLU_ALPHA * x_glu) * (x_lin + 1.0)).astype(BF16)
        ys_ref[0:n_rows, :] = (jnp.dot(act, w2buf[slot].astype(BF16), preferred_element_type=F32) + b2_ref[0]) * wcol

    pl.when(used > half)(lambda: mlp(MOE_ROWS))

    @pl.when((used > 0) & (used <= half))
    def _():
        mlp(half)
        ys_ref[half:, :] = jnp.zeros((MOE_ROWS - half, D_MODEL), F32)

    @pl.when(used == 0)
    def _():
        ys_ref[...] = jnp.zeros(ys_ref.shape, F32)


def _experts(block_e, n_used, block_used, next_e, xs, w1, b1, w2, b2):
    p = xs.shape[0]
    mb = MOE_ROWS
    last = lambda i, nu: jnp.minimum(i, nu[0] - 1)
    rowmap = lambda i, be, nu, bv, nx: (last(i, nu), 0)
    wmap = lambda i, be, nu, bv, nx: (be[last(i, nu)], 0, 0)
    hbm = pl.BlockSpec(memory_space=pl.ANY)
    return pl.pallas_call(
        _experts_kernel,
        grid_spec=pltpu.PrefetchScalarGridSpec(
            num_scalar_prefetch=4,
            grid=(p // mb,),
            in_specs=[pl.BlockSpec((mb, XS_COLS), rowmap),
                      hbm,
                      pl.BlockSpec((1, 1, 2 * D_EXPERT), wmap),
                      hbm,
                      pl.BlockSpec((1, 1, D_MODEL), wmap)],
            out_specs=pl.BlockSpec((mb, D_MODEL), lambda i, be, nu, bv, nx: (i, 0)),
            scratch_shapes=[pltpu.VMEM((2, D_MODEL, 2 * D_EXPERT), F32),
                            pltpu.VMEM((2, D_EXPERT, D_MODEL), F32),
                            pltpu.SemaphoreType.DMA((2, 2)),
                            pltpu.SMEM((1,), jnp.int32)]),
        out_shape=jax.ShapeDtypeStruct((p, D_MODEL), F32),
        compiler_params=_params(("arbitrary",)),
        name="experts",
    )(block_e, n_used, block_used, next_e, xs, w1, b1, w2, b2)


def _combine_kernel(g8_ref, loff_ref, goff_ref, tot_ref, ys_ref, route_ref, x1_ref, g2_ref, fn_ref, o_ref, stg, sem):
    t = pl.program_id(0)
    nt = pl.num_programs(0)
    tm = x1_ref.shape[0]
    slot = lax.rem(t, 2)

    def copy(s, lo, go, rows):
        return pltpu.make_async_copy(ys_ref.at[pl.ds(go, rows), :], stg.at[s, pl.ds(lo, rows), :], sem.at[s])

    @pl.when(t == 0)
    def _():
        stg[...] = jnp.zeros(stg.shape, F32)
        _start_runs(t, g8_ref, loff_ref, goff_ref, functools.partial(copy, 0))

    pl.when(t + 1 < nt)(lambda: _start_runs(t + 1, g8_ref, loff_ref, goff_ref, functools.partial(copy, 1 - slot)))
    copy(slot, 0, 0, pl.multiple_of(tot_ref[t], GROUP_ALIGN)).wait()

    route = route_ref[...]
    low = [jnp.broadcast_to(route[:, k:k + 1], (tm, SLOT_SUB)).astype(BF16) for k in range(TOP_K)]
    sub = [jnp.broadcast_to(route[:, TOP_K + k:TOP_K + k + 1], (tm, SLOT_SUB)).astype(BF16) for k in range(TOP_K)]
    scol = lax.broadcasted_iota(jnp.int32, (tm, SLOT_SUB), 1).astype(F32).astype(BF16)
    one = jnp.ones((tm, SLOT_SUB), BF16)
    never = jnp.full((tm, SLOT_SUB), -1.0, BF16)

    y = None
    for ci in range(SLOTS_PER_TILE // SLOT_CHUNK):
        parts = []
        for b in range(SLOT_CHUNK // SLOT_SUB):
            blk = float(ci * (SLOT_CHUNK // SLOT_SUB) + b)
            own = jnp.zeros((tm, SLOT_SUB), BF16)
            for k in range(TOP_K):
                own = jnp.where(scol == jnp.where(sub[k] == blk, low[k], never), one, own)
            parts.append(own)
        own = jnp.concatenate(parts, axis=1)
        rows = stg[slot, ci * SLOT_CHUNK:(ci + 1) * SLOT_CHUNK, :].astype(BF16)
        part = jnp.dot(own, rows, preferred_element_type=F32)
        y = part if y is None else y + part
    x = x1_ref[...] + g2_ref[0] * y
    o_ref[...] = _rms(x) * fn_ref[...]


def _combine(g8f, lofff, gofff, totf, ys, route, x1, g2, fn, seq):
    n = x1.shape[0]
    tm = ROUTE_TILE
    per_b = seq // tm
    return pl.pallas_call(
        _combine_kernel,
        grid_spec=pltpu.PrefetchScalarGridSpec(
            num_scalar_prefetch=4,
            grid=(n // tm,),
            in_specs=[pl.BlockSpec(memory_space=pl.ANY),
                      pl.BlockSpec((tm, LANE), lambda i, *_: (i, 0)),
                      pl.BlockSpec((tm, D_MODEL), lambda i, *_: (i, 0)),
                      pl.BlockSpec((1, 1, D_MODEL), lambda i, *_: (i // per_b, 0, 0)),
                      pl.BlockSpec((1, D_MODEL), lambda i, *_: (0, 0))],
            out_specs=pl.BlockSpec((tm, D_MODEL), lambda i, *_: (i, 0)),
            scratch_shapes=[pltpu.VMEM((2, SLOTS_PER_TILE, D_MODEL), F32), pltpu.SemaphoreType.DMA((2,))]),
        out_shape=jax.ShapeDtypeStruct((n, D_MODEL), F32),
        compiler_params=_params(("arbitrary",)),
        name="combine",
    )(g8f, lofff, gofff, totf, ys, route, x1, g2, fn)


def _split_w_in(w_in):
    o_kr = _W_MIX_COLS
    w_kr = jnp.pad(w_in[:, o_kr:o_kr + MLA_ROPE_DIM],
                   ((0, 0), (MLA_NOPE_DIM, LANE - MLA_NOPE_DIM - MLA_ROPE_DIM)))
    return w_in[:, :o_kr].astype(BF16), w_kr.astype(BF16), w_in[:, o_kr + MLA_ROPE_DIM:].astype(BF16)


def _pack_heads(w, width, pieces):
    rows = w.shape[0]
    w3 = w.reshape(rows, MLA_HEADS, width)
    cols = [w3[:, :, a:b] for a, b in pieces]
    used = sum(b - a for a, b in pieces)
    cols.append(jnp.zeros((rows, MLA_HEADS, LANE - used), w.dtype))
    return jnp.concatenate(cols, axis=2).reshape(rows, MLA_HEADS * LANE)


def kernel(x, c, positions, w_ada, b_ada, norm_mix, norm_ffn, w_in, sinks, q_norm, kv_norm, w_uq, w_uk, w_uv,
           w_branch_a, w_branch_b, w_out, w_router, b_router, w_moe1, b_moe1, w_moe2, b_moe2, final_norm):
    batch, seq, d = x.shape
    n = batch * seq
    assert d == D_MODEL and w_ada.shape[0] == 1
    assert seq % TQ_MLA == 0 and seq % (SWA_QB * ATTN_BLOCK) == 0 and seq % TM_PROJ == 0 and seq % ROUTE_TILE == 0
    hw = MLA_HEADS * LANE
    qk = MLA_NOPE_DIM + MLA_ROPE_DIM

    w_mix, w_kr, w_gates = _split_w_in(w_in[0])
    wuq_p = _pack_heads(w_uq[0], qk, [(0, qk)]).astype(BF16)
    wuk_p = _pack_heads(w_uk[0], MLA_NOPE_DIM, [(0, MLA_NOPE_DIM)]).astype(BF16)
    wuv_p = _pack_heads(w_uv[0], MLA_V_DIM, [(0, MLA_V_DIM)]).astype(BF16)
    one_row = jnp.tile((jnp.arange(LANE) == MLA_V_DIM).astype(F32), MLA_HEADS)[None, :]
    freqs = ROPE_THETA ** (-jnp.arange(0, MLA_ROPE_DIM, 2, dtype=F32) / MLA_ROPE_DIM)
    frq = jnp.concatenate([jnp.zeros((MLA_NOPE_DIM,), F32), freqs, freqs,
                           jnp.zeros((LANE - qk,), F32)])[None, :]
    wb3 = w_branch_b[0].reshape(MLA_HEADS, MLA_V_DIM, D_MODEL)
    wb_p = jnp.concatenate([wb3, jnp.zeros((MLA_HEADS, LANE - MLA_V_DIM, D_MODEL), F32)],
                           axis=1).reshape(hw, D_MODEL).astype(BF16)
    wa = w_branch_a[0].astype(BF16)
    wo = w_out[0].astype(BF16)
    wr = jnp.pad(w_router[0], ((0, 0), (0, LANE - N_EXPERTS)))
    wr_hi = wr.astype(BF16)
    wr_lo = (wr - wr_hi.astype(F32)).astype(BF16)
    br_p = jnp.pad(b_router[0], (0, LANE - N_EXPERTS))[None, :]
    b1 = b_moe1[0][:, None, :]
    b2 = b_moe2[0][:, None, :]
    slopes = jnp.asarray(np.exp2(-8.0 * np.arange(1, SWA_HEADS + 1) / SWA_HEADS), dtype=F32)

    c8 = jnp.pad(c, ((0, 8 - batch), (0, 0)))
    mod = _ada(c8, w_ada[0], b_ada[0][None, :])[:batch]
    sh1, sc1, g1, sh2, sc2, g2 = [m[:, None, :] for m in jnp.split(mod, 6, axis=-1)]

    x2 = x.reshape(n, D_MODEL)
    pos2 = positions.reshape(n, 1).astype(jnp.int32)
    qa, ka2, va2, qm, km, vm, ga, gb, stats = _inproj(
        x2, pos2, sh1, sc1, norm_mix, w_mix, w_kr, w_gates, q_norm, kv_norm, wuq_p, wuk_p, wuv_p, frq, one_row, seq)
    st = stats.reshape(batch, seq // TM_PROJ, 8, LANE)
    bound_sq = jnp.max(st[:, :, :, 0], axis=1) * jnp.max(st[:, :, :, 1], axis=1)
    plain = (bound_sq <= MLA_PLAIN_BOUND ** 2).astype(jnp.int32).reshape(-1)
    yb = _mla(plain, qm, km, vm, batch, seq)
    ya = _swa(sinks[0], slopes, qa, ka2, va2, batch, seq)

    x1, h2, wext, route, routet, g8 = _outproj(
        ya, yb, ga, gb, x2, g1, sh2, sc2, norm_ffn, wa, wb_p, wo, wr_hi, wr_lo, br_p, seq)

    n_tiles = n // ROUTE_TILE
    g8t = g8.reshape(n_tiles, 8, LANE)[:, 0, :N_EXPERTS].astype(jnp.int32)
    loff = jnp.cumsum(g8t, axis=1) - g8t
    padded = ((jnp.sum(g8t, axis=0) + MOE_ROWS - 1) // MOE_ROWS) * MOE_ROWS
    pends = jnp.cumsum(padded)
    goff = (pends - padded)[None, :] + jnp.cumsum(g8t, axis=0) - g8t
    tot = jnp.sum(g8t, axis=1).astype(jnp.int32)
    p_rows = n * TOP_K + n_tiles * N_EXPERTS * (GROUP_ALIGN - 1) + N_EXPERTS * (MOE_ROWS - 1)
    p_rows = -(-p_rows // MOE_ROWS) * MOE_ROWS
    n_blocks = p_rows // MOE_ROWS
    block_start = jnp.arange(n_blocks, dtype=jnp.int32) * MOE_ROWS
    block_e = jnp.minimum(jnp.sum((pends[None, :] <= block_start[:, None]).astype(jnp.int32), axis=1),
                          N_EXPERTS - 1).astype(jnp.int32)
    n_used = (pends[-1:] // MOE_ROWS).astype(jnp.int32)
    eids = jnp.arange(N_EXPERTS, dtype=jnp.int32)
    of_block = block_e[:, None] == eids[None, :]
    pick = lambda table: jnp.sum(jnp.where(of_block, table[None, :], 0), axis=1).astype(jnp.int32)
    real_end = pends - padded + jnp.sum(g8t, axis=0)
    block_used = jnp.clip(pick(real_end) - block_start, 0, MOE_ROWS).astype(jnp.int32)
    later = (eids[None, :] > eids[:, None]) & (padded[None, :] > 0)
    after = jnp.min(jnp.where(later, eids[None, :], N_EXPERTS), axis=1)
    next_e = pick(jnp.where(after == N_EXPERTS, -1, after))
    tabs = (g8t.reshape(-1), loff.reshape(-1).astype(jnp.int32), goff.reshape(-1).astype(jnp.int32), tot)

    xs = _dispatch(*tabs, jnp.concatenate([pends, real_end]).astype(jnp.int32), h2, wext, routet, p_rows)
    ys = _experts(block_e, n_used, block_used, next_e, xs, w_moe1[0], b1, w_moe2[0], b2)
    out = _combine(*tabs, ys, route, x1, g2, final_norm[None, :], seq)
    return out.reshape(batch, seq, D_MODEL)
```
